```python
import math
import jax, jax.numpy as jnp
from jax import lax
import numpy as np

D_MODEL = 4096
BATCH = 8
SEQ = 4096
DEPTH = 1

CHUNK = 64
EPS = 1e-6
GDN_HEADS = 16
GDN_DK = 128
GDN_DV = 128
CONV_K = 4
GDN_QK_W = GDN_HEADS * GDN_DK
GDN_V_W = GDN_HEADS * GDN_DV
GDN_CONV_W = 2 * GDN_QK_W + GDN_V_W
FOX_HEADS = 16
FOX_DH = 128
FOX_W = FOX_HEADS * FOX_DH
Q_BLOCK = 128
D_FF = -(-8 * D_MODEL // (3 * 256)) * 256
IN_SPLITS = (GDN_QK_W, GDN_QK_W, GDN_V_W, GDN_V_W, GDN_HEADS, GDN_HEADS,
             FOX_W, FOX_W, FOX_W, FOX_HEADS, D_MODEL, D_MODEL)
N_IN = sum(IN_SPLITS)

kernel_name = "hybrid_gdn_fox_gated_merge_swiglu"


def rmsnorm(x, w):
    xf = x.astype(jnp.float32)
    y = xf * lax.rsqrt(jnp.mean(xf * xf, axis=-1, keepdims=True) + EPS) * w.astype(jnp.float32)
    return y.astype(x.dtype)


def l2norm(x):
    return x * lax.rsqrt(jnp.sum(x * x, axis=-1, keepdims=True) + EPS)


def split_cols(p):
    idx, acc = [], 0
    for s in IN_SPLITS[:-1]:
        acc += s
        idx.append(acc)
    return jnp.split(p, idx, axis=-1)


def causal_conv(x, w):
    T = x.shape[1]
    xp = jnp.pad(x, ((0, 0), (CONV_K - 1, 0), (0, 0)))
    y = xp[:, 0:T] * w[0]
    for i in range(1, CONV_K):
        y = y + xp[:, i:i + T] * w[i]
    return y


def gated_delta_rule(q, k, v, g, beta):
    f32 = jnp.float32
    q, k, v, g, beta = (a.astype(f32) for a in (q, k, v, g, beta))
    Bn, T, H, DK = q.shape
    DV = v.shape[-1]
    N = T // CHUNK

    def to_chunks(a):
        a = jnp.moveaxis(a, 2, 1)
        return a.reshape(a.shape[:2] + (N, CHUNK) + a.shape[3:])

    qc = to_chunks(l2norm(q) * DK ** -0.5)
    kc = to_chunks(l2norm(k))
    vc = to_chunks(v)
    bc = to_chunks(beta)
    gc = jnp.cumsum(to_chunks(g), axis=-1)

    tril_incl = jnp.tril(jnp.ones((CHUNK, CHUNK), bool))
    tril_strict = jnp.tril(jnp.ones((CHUNK, CHUNK), bool), k=-1)
    decay = jnp.exp(jnp.where(tril_incl, gc[..., :, None] - gc[..., None, :], -jnp.inf))

    a_strict = jnp.where(tril_strict,
                         jnp.einsum('bhncd,bhnmd->bhncm', kc, kc) * decay * bc[..., :, None], 0.0)
    eye = jnp.eye(CHUNK, dtype=f32)
    rhs = jnp.concatenate([vc * bc[..., None], kc * (bc * jnp.exp(gc))[..., None]], axis=-1)
    uw = lax.linalg.triangular_solve(a_strict + eye, rhs, left_side=True, lower=True)
    u, w = uw[..., :DV], uw[..., DV:]
    qk = jnp.einsum('bhncd,bhnmd->bhncm', qc, kc) * decay

    def step(S, inp):
        q_i, k_i, u_i, w_i, g_i, qk_i = inp
        v_new = u_i - jnp.einsum('bhcd,bhdv->bhcv', w_i, S)
        o = (jnp.einsum('bhcd,bhdv->bhcv', q_i * jnp.exp(g_i)[..., None], S)
             + jnp.einsum('bhcm,bhmv->bhcv', qk_i, v_new))
        g_last = g_i[..., -1]
        S = (S * jnp.exp(g_last)[..., None, None]
             + jnp.einsum('bhcd,bhcv->bhdv', k_i * jnp.exp(g_last[..., None] - g_i)[..., None], v_new))
        return S, o

    xs = tuple(jnp.moveaxis(a, 2, 0) for a in (qc, kc, u, w, gc, qk))
    S0 = jnp.zeros((Bn, H, DK, DV), f32)
    _, o = lax.scan(step, S0, xs)
    o = jnp.transpose(o, (1, 0, 3, 2, 4))
    return o.reshape(Bn, T, H, DV)


def forgetting_attention(q, k, v, f_logit):
    T = q.shape[1]
    scale = FOX_DH ** -0.5
    c = jnp.cumsum(jax.nn.log_sigmoid(f_logit.astype(jnp.float32)), axis=1)
    q, k, v = (jnp.swapaxes(a, 1, 2) for a in (q, k, v))
    c = jnp.swapaxes(c, 1, 2)
    outs = []
    for i in range(T // Q_BLOCK):
        lo, hi = i * Q_BLOCK, (i + 1) * Q_BLOCK
        s = jnp.einsum('bhqd,bhkd->bhqk', q[:, :, lo:hi], k[:, :, :hi]).astype(jnp.float32) * scale
        s = s + c[:, :, lo:hi, None] - c[:, :, None, :hi]
        causal = jnp.arange(lo, hi)[:, None] >= jnp.arange(hi)[None, :]
        p = jax.nn.softmax(jnp.where(causal, s, -jnp.inf), axis=-1)
        outs.append(jnp.einsum('bhqk,bhkd->bhqd', p.astype(v.dtype), v[:, :, :hi]))
    o = jnp.concatenate(outs, axis=2)
    return jnp.swapaxes(o, 1, 2)


def hybrid_layer(x, norm_mix_w, w_in, conv_w, a_log, dt_bias, gdn_norm_w, fox_b_f,
                 fox_q_norm_w, fox_k_norm_w, w_branch_a, w_branch_b, w_out,
                 norm_ffn_w, w_ffn_gate, w_ffn_up, w_ffn_down):
    Bn, T, _ = x.shape
    xn = rmsnorm(x, norm_mix_w)
    (qa, ka, va, za, beta_logit, alpha_logit, qb, kb, vb, f_logit,
     gate_a_logit, gate_b_logit) = split_cols(xn @ w_in)

    qkv_a = jax.nn.silu(causal_conv(jnp.concatenate([qa, ka, va], axis=-1), conv_w))
    qa, ka, va = jnp.split(qkv_a, [GDN_QK_W, 2 * GDN_QK_W], axis=-1)
    beta = jax.nn.sigmoid(beta_logit.astype(jnp.float32))
    g = -jnp.exp(a_log.astype(jnp.float32)) * jax.nn.softplus(
        alpha_logit.astype(jnp.float32) + dt_bias.astype(jnp.float32))
    o_a = gated_delta_rule(qa.reshape(Bn, T, GDN_HEADS, GDN_DK), ka.reshape(Bn, T, GDN_HEADS, GDN_DK),
                           va.reshape(Bn, T, GDN_HEADS, GDN_DV), g, beta)
    o_a = rmsnorm(o_a, gdn_norm_w) * jax.nn.silu(za.reshape(Bn, T, GDN_HEADS, GDN_DV).astype(jnp.float32))
    y_a = o_a.astype(x.dtype).reshape(Bn, T, GDN_V_W) @ w_branch_a

    qb = rmsnorm(qb.reshape(Bn, T, FOX_HEADS, FOX_DH), fox_q_norm_w)
    kb = rmsnorm(kb.reshape(Bn, T, FOX_HEADS, FOX_DH), fox_k_norm_w)
    o_b = forgetting_attention(qb, kb, vb.reshape(Bn, T, FOX_HEADS, FOX_DH), f_logit + fox_b_f)
    y_b = o_b.reshape(Bn, T, FOX_W) @ w_branch_b

    merged = jax.nn.sigmoid(gate_a_logit) * y_a + jax.nn.sigmoid(gate_b_logit) * y_b
    h = x + merged @ w_out

    hn = rmsnorm(h, norm_ffn_w)
    return h + (jax.nn.silu(hn @ w_ffn_gate) * (hn @ w_ffn_up)) @ w_ffn_down


def _fwd_setup_inputs(seed: int = 0) -> dict:
    key = jax.random.key(seed)
    ks = jax.random.split(key, 18)
    L = DEPTH
    f32 = jnp.float32

    def nrm(k, shape, fan_in):
        return jax.random.normal(k, shape, f32) * fan_in ** -0.5

    def gain(k, n):
        return 1.0 + 0.02 * jax.random.normal(k, (L, n), f32)

    dt = jnp.exp(jax.random.uniform(ks[6], (L, GDN_HEADS), f32, math.log(1e-3), math.log(1e-1)))
    return {
        "x": jax.random.normal(ks[0], (BATCH, SEQ, D_MODEL), f32),
        "norm_mix_w": gain(ks[1], D_MODEL),
        "w_in": nrm(ks[2], (L, D_MODEL, N_IN), D_MODEL),
        "conv_w": nrm(ks[3], (L, CONV_K, GDN_CONV_W), CONV_K),
        "a_log": jnp.log(jax.random.uniform(ks[4], (L, GDN_HEADS), f32, 1.0, 16.0)),
        "dt_bias": dt + jnp.log(-jnp.expm1(-dt)),
        "gdn_norm_w": gain(ks[5], GDN_DV),
        "fox_b_f": jax.random.uniform(ks[7], (L, FOX_HEADS), f32, 1.0, 4.0),
        "fox_q_norm_w": gain(ks[8], FOX_DH),
        "fox_k_norm_w": gain(ks[9], FOX_DH),
        "w_branch_a": nrm(ks[10], (L, GDN_V_W, D_MODEL), GDN_V_W),
        "w_branch_b": nrm(ks[11], (L, FOX_W, D_MODEL), FOX_W),
        "w_out": nrm(ks[12], (L, D_MODEL, D_MODEL), D_MODEL),
        "norm_ffn_w": gain(ks[13], D_MODEL),
        "w_ffn_gate": nrm(ks[14], (L, D_MODEL, D_FF), D_MODEL),
        "w_ffn_up": nrm(ks[15], (L, D_MODEL, D_FF), D_MODEL),
        "w_ffn_down": nrm(ks[16], (L, D_FF, D_MODEL), D_FF),
    }


def _fwd_reference(x, norm_mix_w, w_in, conv_w, a_log, dt_bias, gdn_norm_w, fox_b_f,
              fox_q_norm_w, fox_k_norm_w, w_branch_a, w_branch_b, w_out,
              norm_ffn_w, w_ffn_gate, w_ffn_up, w_ffn_down):
    h = x
    for l in range(DEPTH):
        h = hybrid_layer(h, norm_mix_w[l], w_in[l], conv_w[l], a_log[l], dt_bias[l], gdn_norm_w[l],
                         fox_b_f[l], fox_q_norm_w[l], fox_k_norm_w[l], w_branch_a[l], w_branch_b[l],
                         w_out[l], norm_ffn_w[l], w_ffn_gate[l], w_ffn_up[l], w_ffn_down[l])
    return h


import jax as _jax
import jax.numpy as _jnp

TWIN_FORMAT = 'train_step'
FWD_PARAMS = ['x', 'norm_mix_w', 'w_in', 'conv_w', 'a_log', 'dt_bias', 'gdn_norm_w', 'fox_b_f', 'fox_q_norm_w', 'fox_k_norm_w', 'w_branch_a', 'w_branch_b', 'w_out', 'norm_ffn_w', 'w_ffn_gate', 'w_ffn_up', 'w_ffn_down']
TWIN_WEIGHTS = ['norm_mix_w', 'w_in', 'conv_w', 'a_log', 'dt_bias', 'gdn_norm_w', 'fox_b_f', 'fox_q_norm_w', 'fox_k_norm_w', 'w_branch_a', 'w_branch_b', 'w_out', 'norm_ffn_w', 'w_ffn_gate', 'w_ffn_up', 'w_ffn_down']
TWIN_DIFF_INPUT = 'x'
TWIN_INPUTS = ['x', 'norm_mix_w', 'w_in', 'conv_w', 'a_log', 'dt_bias', 'gdn_norm_w', 'fox_b_f', 'fox_q_norm_w', 'fox_k_norm_w', 'w_branch_a', 'w_branch_b', 'w_out', 'norm_ffn_w', 'w_ffn_gate', 'w_ffn_up', 'w_ffn_down', 'loss_target', 'm_norm_mix_w', 'm_w_in', 'm_conv_w', 'm_a_log', 'm_dt_bias', 'm_gdn_norm_w', 'm_fox_b_f', 'm_fox_q_norm_w', 'm_fox_k_norm_w', 'm_w_branch_a', 'm_w_branch_b', 'm_w_out', 'm_norm_ffn_w', 'm_w_ffn_gate', 'm_w_ffn_up', 'm_w_ffn_down', 'v_norm_mix_w', 'v_w_in', 'v_conv_w', 'v_a_log', 'v_dt_bias', 'v_gdn_norm_w', 'v_fox_b_f', 'v_fox_q_norm_w', 'v_fox_k_norm_w', 'v_w_branch_a', 'v_w_branch_b', 'v_w_out', 'v_norm_ffn_w', 'v_w_ffn_gate', 'v_w_ffn_up', 'v_w_ffn_down']
TWIN_OUTPUTS = ['loss', 'grad_x', 'grad_norm_mix_w', 'grad_w_in', 'grad_conv_w', 'grad_a_log', 'grad_dt_bias', 'grad_gdn_norm_w', 'grad_fox_b_f', 'grad_fox_q_norm_w', 'grad_fox_k_norm_w', 'grad_w_branch_a', 'grad_w_branch_b', 'grad_w_out', 'grad_norm_ffn_w', 'grad_w_ffn_gate', 'grad_w_ffn_up', 'grad_w_ffn_down', 'delta_norm_mix_w', 'delta_w_in', 'delta_conv_w', 'delta_a_log', 'delta_dt_bias', 'delta_gdn_norm_w', 'delta_fox_b_f', 'delta_fox_q_norm_w', 'delta_fox_k_norm_w', 'delta_w_branch_a', 'delta_w_branch_b', 'delta_w_out', 'delta_norm_ffn_w', 'delta_w_ffn_gate', 'delta_w_ffn_up', 'delta_w_ffn_down', 'new_m_norm_mix_w', 'new_m_w_in', 'new_m_conv_w', 'new_m_a_log', 'new_m_dt_bias', 'new_m_gdn_norm_w', 'new_m_fox_b_f', 'new_m_fox_q_norm_w', 'new_m_fox_k_norm_w', 'new_m_w_branch_a', 'new_m_w_branch_b', 'new_m_w_out', 'new_m_norm_ffn_w', 'new_m_w_ffn_gate', 'new_m_w_ffn_up', 'new_m_w_ffn_down', 'new_v_norm_mix_w', 'new_v_w_in', 'new_v_conv_w', 'new_v_a_log', 'new_v_dt_bias', 'new_v_gdn_norm_w', 'new_v_fox_b_f', 'new_v_fox_q_norm_w', 'new_v_fox_k_norm_w', 'new_v_w_branch_a', 'new_v_w_branch_b', 'new_v_w_out', 'new_v_norm_ffn_w', 'new_v_w_ffn_gate', 'new_v_w_ffn_up', 'new_v_w_ffn_down']
TWIN_LEAF_KINDS = {'loss': 'loss', 'grad_x': 'grad_x', 'grad_norm_mix_w': 'grad_w', 'grad_w_in': 'grad_w', 'grad_conv_w': 'grad_w', 'grad_a_log': 'grad_w', 'grad_dt_bias': 'grad_w', 'grad_gdn_norm_w': 'grad_w', 'grad_fox_b_f': 'grad_w', 'grad_fox_q_norm_w': 'grad_w', 'grad_fox_k_norm_w': 'grad_w', 'grad_w_branch_a': 'grad_w', 'grad_w_branch_b': 'grad_w', 'grad_w_out': 'grad_w', 'grad_norm_ffn_w': 'grad_w', 'grad_w_ffn_gate': 'grad_w', 'grad_w_ffn_up': 'grad_w', 'grad_w_ffn_down': 'grad_w', 'delta_norm_mix_w': 'delta_w', 'delta_w_in': 'delta_w', 'delta_conv_w': 'delta_w', 'delta_a_log': 'delta_w', 'delta_dt_bias': 'delta_w', 'delta_gdn_norm_w': 'delta_w', 'delta_fox_b_f': 'delta_w', 'delta_fox_q_norm_w': 'delta_w', 'delta_fox_k_norm_w': 'delta_w', 'delta_w_branch_a': 'delta_w', 'delta_w_branch_b': 'delta_w', 'delta_w_out': 'delta_w', 'delta_norm_ffn_w': 'delta_w', 'delta_w_ffn_gate': 'delta_w', 'delta_w_ffn_up': 'delta_w', 'delta_w_ffn_down': 'delta_w', 'new_m_norm_mix_w': 'new_m', 'new_m_w_in': 'new_m', 'new_m_conv_w': 'new_m', 'new_m_a_log': 'new_m', 'new_m_dt_bias': 'new_m', 'new_m_gdn_norm_w': 'new_m', 'new_m_fox_b_f': 'new_m', 'new_m_fox_q_norm_w': 'new_m', 'new_m_fox_k_norm_w': 'new_m', 'new_m_w_branch_a': 'new_m', 'new_m_w_branch_b': 'new_m', 'new_m_w_out': 'new_m', 'new_m_norm_ffn_w': 'new_m', 'new_m_w_ffn_gate': 'new_m', 'new_m_w_ffn_up': 'new_m', 'new_m_w_ffn_down': 'new_m', 'new_v_norm_mix_w': 'new_v', 'new_v_w_in': 'new_v', 'new_v_conv_w': 'new_v', 'new_v_a_log': 'new_v', 'new_v_dt_bias': 'new_v', 'new_v_gdn_norm_w': 'new_v', 'new_v_fox_b_f': 'new_v', 'new_v_fox_q_norm_w': 'new_v', 'new_v_fox_k_norm_w': 'new_v', 'new_v_w_branch_a': 'new_v', 'new_v_w_branch_b': 'new_v', 'new_v_w_out': 'new_v', 'new_v_norm_ffn_w': 'new_v', 'new_v_w_ffn_gate': 'new_v', 'new_v_w_ffn_up': 'new_v', 'new_v_w_ffn_down': 'new_v'}


def _forward(args):
    return _fwd_reference(*[args[k] for k in FWD_PARAMS])


def _output_shape():
    out = _jax.eval_shape(lambda: _forward(_fwd_setup_inputs(0)))
    return out.shape, out.dtype

N_MICROBATCH = 1
ADAM_LR = 0.001
ADAM_B1 = 0.9
ADAM_B2 = 0.999
ADAM_EPS = 1e-08
ADAM_WD = 0.01
ADAM_STEP = 10
PER_EXAMPLE_BATCH_AXIS = {'x': 0, 'loss_target': 0}
SHARED_INPUTS = []
_WEIGHT_DTYPES = {'norm_mix_w': _jnp.float32, 'w_in': _jnp.float32, 'conv_w': _jnp.float32, 'a_log': _jnp.float32, 'dt_bias': _jnp.float32, 'gdn_norm_w': _jnp.float32, 'fox_b_f': _jnp.float32, 'fox_q_norm_w': _jnp.float32, 'fox_k_norm_w': _jnp.float32, 'w_branch_a': _jnp.float32, 'w_branch_b': _jnp.float32, 'w_out': _jnp.float32, 'norm_ffn_w': _jnp.float32, 'w_ffn_gate': _jnp.float32, 'w_ffn_up': _jnp.float32, 'w_ffn_down': _jnp.float32}
MOMENT_SCALE = {'norm_mix_w': 1.442855e+00, 'w_in': 3.503238e-02, 'conv_w': 6.280976e-02, 'a_log': 5.147686e+00, 'dt_bias': 4.817463e+00, 'gdn_norm_w': 2.597479e+01, 'fox_b_f': 2.776456e+01, 'fox_q_norm_w': 3.904078e+00, 'fox_k_norm_w': 3.909377e+00, 'w_branch_a': 7.107949e-02, 'w_branch_b': 2.213110e-02, 'w_out': 6.607312e-02, 'norm_ffn_w': 6.182074e+00, 'w_ffn_gate': 4.229027e-02, 'w_ffn_up': 3.287023e-02, 'w_ffn_down': 4.699062e-02}


def _to_microbatches(a, axis):
    t = _jnp.moveaxis(a, axis, 0)
    t = t.reshape((N_MICROBATCH, t.shape[0] // N_MICROBATCH) + t.shape[1:])
    return _jnp.moveaxis(t, 1, axis + 1)


def setup_inputs(seed: int = 0) -> dict:
    inp = _fwd_setup_inputs(seed)
    key = _jax.random.fold_in(_jax.random.key(seed), 7919)
    shape, _ = _output_shape()
    out = dict(inp)
    out["loss_target"] = _jax.random.normal(_jax.random.fold_in(key, 0), shape, _jnp.float32)
    for i, name in enumerate(TWIN_WEIGHTS):
        w = inp[name].astype(_jnp.float32)
        if MOMENT_SCALE is None:
            s = _jnp.sqrt(_jnp.mean(_jnp.square(w)) + 1e-30)
        else:
            s = MOMENT_SCALE[name]
        km, kv = _jax.random.split(_jax.random.fold_in(key, i + 1))
        out[name] = w
        out["m_" + name] = s * _jax.random.normal(km, w.shape, _jnp.float32)
        out["v_" + name] = (s * s) * _jax.random.uniform(kv, w.shape, _jnp.float32, 0.5, 1.5)
    if N_MICROBATCH > 1:
        for name, axis in PER_EXAMPLE_BATCH_AXIS.items():
            out[name] = _to_microbatches(out[name], axis)
    return {'x': out['x'], 'norm_mix_w': out['norm_mix_w'], 'w_in': out['w_in'], 'conv_w': out['conv_w'], 'a_log': out['a_log'], 'dt_bias': out['dt_bias'], 'gdn_norm_w': out['gdn_norm_w'], 'fox_b_f': out['fox_b_f'], 'fox_q_norm_w': out['fox_q_norm_w'], 'fox_k_norm_w': out['fox_k_norm_w'], 'w_branch_a': out['w_branch_a'], 'w_branch_b': out['w_branch_b'], 'w_out': out['w_out'], 'norm_ffn_w': out['norm_ffn_w'], 'w_ffn_gate': out['w_ffn_gate'], 'w_ffn_up': out['w_ffn_up'], 'w_ffn_down': out['w_ffn_down'], 'loss_target': out['loss_target'], 'm_norm_mix_w': out['m_norm_mix_w'], 'm_w_in': out['m_w_in'], 'm_conv_w': out['m_conv_w'], 'm_a_log': out['m_a_log'], 'm_dt_bias': out['m_dt_bias'], 'm_gdn_norm_w': out['m_gdn_norm_w'], 'm_fox_b_f': out['m_fox_b_f'], 'm_fox_q_norm_w': out['m_fox_q_norm_w'], 'm_fox_k_norm_w': out['m_fox_k_norm_w'], 'm_w_branch_a': out['m_w_branch_a'], 'm_w_branch_b': out['m_w_branch_b'], 'm_w_out': out['m_w_out'], 'm_norm_ffn_w': out['m_norm_ffn_w'], 'm_w_ffn_gate': out['m_w_ffn_gate'], 'm_w_ffn_up': out['m_w_ffn_up'], 'm_w_ffn_down': out['m_w_ffn_down'], 'v_norm_mix_w': out['v_norm_mix_w'], 'v_w_in': out['v_w_in'], 'v_conv_w': out['v_conv_w'], 'v_a_log': out['v_a_log'], 'v_dt_bias': out['v_dt_bias'], 'v_gdn_norm_w': out['v_gdn_norm_w'], 'v_fox_b_f': out['v_fox_b_f'], 'v_fox_q_norm_w': out['v_fox_q_norm_w'], 'v_fox_k_norm_w': out['v_fox_k_norm_w'], 'v_w_branch_a': out['v_w_branch_a'], 'v_w_branch_b': out['v_w_branch_b'], 'v_w_out': out['v_w_out'], 'v_norm_ffn_w': out['v_norm_ffn_w'], 'v_w_ffn_gate': out['v_w_ffn_gate'], 'v_w_ffn_up': out['v_w_ffn_up'], 'v_w_ffn_down': out['v_w_ffn_down']}


def _loss(weights, diff, rest, loss_target):
    with _jax.named_scope("forward"):
        args = {**rest, TWIN_DIFF_INPUT: diff, **{k: w.astype(_WEIGHT_DTYPES[k]) for k, w in weights.items()}}
        y = _forward(args)
    with _jax.named_scope("loss_head"):
        err = _jnp.square(y.astype(_jnp.float32) - loss_target)
        return 0.5 * _jnp.sum(_jnp.mean(err, axis=-1)) if err.ndim else 0.5 * err


def _adamw(w, g, m, v):
    m = ADAM_B1 * m + (1.0 - ADAM_B1) * g
    v = ADAM_B2 * v + (1.0 - ADAM_B2) * _jnp.square(g)
    m_hat = m / (1.0 - ADAM_B1 ** ADAM_STEP)
    v_hat = v / (1.0 - ADAM_B2 ** ADAM_STEP)
    delta = -ADAM_LR * (m_hat / (_jnp.sqrt(v_hat) + ADAM_EPS) + ADAM_WD * w)
    return delta, m, v


def reference(x, norm_mix_w, w_in, conv_w, a_log, dt_bias, gdn_norm_w, fox_b_f, fox_q_norm_w, fox_k_norm_w, w_branch_a, w_branch_b, w_out, norm_ffn_w, w_ffn_gate, w_ffn_up, w_ffn_down, loss_target, m_norm_mix_w, m_w_in, m_conv_w, m_a_log, m_dt_bias, m_gdn_norm_w, m_fox_b_f, m_fox_q_norm_w, m_fox_k_norm_w, m_w_branch_a, m_w_branch_b, m_w_out, m_norm_ffn_w, m_w_ffn_gate, m_w_ffn_up, m_w_ffn_down, v_norm_mix_w, v_w_in, v_conv_w, v_a_log, v_dt_bias, v_gdn_norm_w, v_fox_b_f, v_fox_q_norm_w, v_fox_k_norm_w, v_w_branch_a, v_w_branch_b, v_w_out, v_norm_ffn_w, v_w_ffn_gate, v_w_ffn_up, v_w_ffn_down):
    given = dict(x=x, norm_mix_w=norm_mix_w, w_in=w_in, conv_w=conv_w, a_log=a_log, dt_bias=dt_bias, gdn_norm_w=gdn_norm_w, fox_b_f=fox_b_f, fox_q_norm_w=fox_q_norm_w, fox_k_norm_w=fox_k_norm_w, w_branch_a=w_branch_a, w_branch_b=w_branch_b, w_out=w_out, norm_ffn_w=norm_ffn_w, w_ffn_gate=w_ffn_gate, w_ffn_up=w_ffn_up, w_ffn_down=w_ffn_down, loss_target=loss_target, m_norm_mix_w=m_norm_mix_w, m_w_in=m_w_in, m_conv_w=m_conv_w, m_a_log=m_a_log, m_dt_bias=m_dt_bias, m_gdn_norm_w=m_gdn_norm_w, m_fox_b_f=m_fox_b_f, m_fox_q_norm_w=m_fox_q_norm_w, m_fox_k_norm_w=m_fox_k_norm_w, m_w_branch_a=m_w_branch_a, m_w_branch_b=m_w_branch_b, m_w_out=m_w_out, m_norm_ffn_w=m_norm_ffn_w, m_w_ffn_gate=m_w_ffn_gate, m_w_ffn_up=m_w_ffn_up, m_w_ffn_down=m_w_ffn_down, v_norm_mix_w=v_norm_mix_w, v_w_in=v_w_in, v_conv_w=v_conv_w, v_a_log=v_a_log, v_dt_bias=v_dt_bias, v_gdn_norm_w=v_gdn_norm_w, v_fox_b_f=v_fox_b_f, v_fox_q_norm_w=v_fox_q_norm_w, v_fox_k_norm_w=v_fox_k_norm_w, v_w_branch_a=v_w_branch_a, v_w_branch_b=v_w_branch_b, v_w_out=v_w_out, v_norm_ffn_w=v_norm_ffn_w, v_w_ffn_gate=v_w_ffn_gate, v_w_ffn_up=v_w_ffn_up, v_w_ffn_down=v_w_ffn_down)
    weights = {n: given[n] for n in TWIN_WEIGHTS}
    shared = {n: given[n] for n in SHARED_INPUTS}
    per_example = {n: given[n] for n in ['x']}
    grad_fn = _jax.value_and_grad(_loss, argnums=(0, 1))

    def one_microbatch(ex, loss_target):
        ex = dict(ex)
        diff = ex.pop(TWIN_DIFF_INPUT)
        return grad_fn(weights, diff, {**shared, **ex}, loss_target)

    if N_MICROBATCH == 1:
        loss, (grad_w, grad_x) = one_microbatch(per_example, given["loss_target"])
    else:
        def body(carry, xs):
            loss_sum, grad_sum = carry
            l_k, (gw_k, gx_k) = one_microbatch(xs[0], xs[1])
            with _jax.named_scope("update"):
                return (loss_sum + l_k, _jax.tree.map(_jnp.add, grad_sum, gw_k)), gx_k

        init = (_jnp.zeros((), _jnp.float32), _jax.tree.map(_jnp.zeros_like, weights))
        (loss, grad_w), grad_x = _jax.lax.scan(body, init, (per_example, given["loss_target"]))
    with _jax.named_scope("update"):
        delta_w, new_m, new_v = {}, {}, {}
        for n in TWIN_WEIGHTS:
            delta_w[n], new_m[n], new_v[n] = _adamw(weights[n], grad_w[n], given["m_" + n], given["v_" + n])
    return (loss, grad_x, *[grad_w[n] for n in TWIN_WEIGHTS], *[delta_w[n] for n in TWIN_WEIGHTS],
            *[new_m[n] for n in TWIN_WEIGHTS], *[new_v[n] for n in TWIN_WEIGHTS])
```

```python
import functools
import math

import jax
import jax.numpy as jnp
from jax import lax
from jax.experimental import pallas as pl
from jax.experimental.pallas import tpu as pltpu

F32 = jnp.float32
BF16 = jnp.bfloat16
HI = lax.Precision.HIGHEST
MESH = pl.DeviceIdType.MESH

EPS = 1e-6
HEADS = 16
DH = 128
HW = HEADS * DH
CHUNK = 64
CONV_K = 4
N_DEV = 8
LANES = 128
VMEM_LIMIT = 52 * 1024 * 1024

ADAM_LR = 0.001
ADAM_B1 = 0.9
ADAM_B2 = 0.999
ADAM_EPS = 1e-08
ADAM_WD = 0.01
ADAM_STEP = 10

OFF_QA, OFF_KA, OFF_VA, OFF_ZA, OFF_QB, OFF_KB, OFF_VB, OFF_GA = 0, HW, 2 * HW, 3 * HW, 4 * HW, 5 * HW, 6 * HW, 7 * HW


def _cparams(sem=None, vmem=VMEM_LIMIT):
    return pltpu.CompilerParams(dimension_semantics=sem, vmem_limit_bytes=vmem)


def _mm(a, b, *, mode, m, n, k, tm, tn, tk, out_dtype, name, a_off=(0, 0), b_off=(0, 0), add=None):
    tm, tn, tk = min(tm, m), min(tn, n), min(tk, k)
    assert m % tm == 0 and n % tn == 0 and k % tk == 0, (name, m, n, k, tm, tn, tk)
    nk = k // tk
    if mode == "nn":
        a_blk, b_blk = (tm, tk), (tk, tn)
        ao, bo = (a_off[0] // tm, a_off[1] // tk), (b_off[0] // tk, b_off[1] // tn)
        a_map = lambda i, j, kk: (i + ao[0], kk + ao[1])
        b_map = lambda i, j, kk: (kk + bo[0], j + bo[1])
        dims = (((1,), (0,)), ((), ()))
    elif mode == "nt":
        a_blk, b_blk = (tm, tk), (tn, tk)
        ao, bo = (a_off[0] // tm, a_off[1] // tk), (b_off[0] // tn, b_off[1] // tk)
        a_map = lambda i, j, kk: (i + ao[0], kk + ao[1])
        b_map = lambda i, j, kk: (j + bo[0], kk + bo[1])
        dims = (((1,), (1,)), ((), ()))
    else:
        a_blk, b_blk = (tk, tm), (tk, tn)
        ao, bo = (a_off[0] // tk, a_off[1] // tm), (b_off[0] // tk, b_off[1] // tn)
        a_map = lambda i, j, kk: (kk + ao[0], i + ao[1])
        b_map = lambda i, j, kk: (kk + bo[0], j + bo[1])
        dims = (((0,), (0,)), ((), ()))
    for off, blk in ((a_off, a_blk), (b_off, b_blk)):
        assert off[0] % blk[0] == 0 and off[1] % blk[1] == 0, (name, off, blk)
    has_add = add is not None

    def body(*refs):
        if has_add:
            a_ref, b_ref, c_ref, o_ref, acc = refs
        else:
            a_ref, b_ref, o_ref, acc = refs
            c_ref = None
        p = lax.dot_general(a_ref[...].astype(BF16), b_ref[...].astype(BF16), dims, preferred_element_type=F32)
        if nk == 1:
            if has_add:
                p = p + c_ref[...].astype(F32)
            o_ref[...] = p.astype(o_ref.dtype)
        else:
            kk = pl.program_id(2)

            @pl.when(kk == 0)
            def _():
                acc[...] = p + c_ref[...].astype(F32) if has_add else p

            @pl.when(kk > 0)
            def _():
                acc[...] += p

            @pl.when(kk == nk - 1)
            def _():
                o_ref[...] = acc[...].astype(o_ref.dtype)

    in_specs = [pl.BlockSpec(a_blk, a_map), pl.BlockSpec(b_blk, b_map)]
    args = [a, b]
    if has_add:
        in_specs.append(pl.BlockSpec((tm, tn), lambda i, j, kk: (i, j)))
        args.append(add)
    acc_shape = (tm, tn) if nk > 1 else (8, LANES)
    return pl.pallas_call(
        body, name=name, grid=(m // tm, n // tn, nk), in_specs=in_specs,
        out_specs=pl.BlockSpec((tm, tn), lambda i, j, kk: (i, j)),
        out_shape=jax.ShapeDtypeStruct((m, n), out_dtype),
        scratch_shapes=[pltpu.VMEM(acc_shape, F32)],
        compiler_params=_cparams(("parallel", "parallel", "arbitrary")),
    )(*args)


def _row_specs(rows, tm, ncol):
    specs = []
    for arr, off, width in rows:
        bw = width // ncol
        assert width % ncol == 0 and off % bw == 0, (off, width, ncol)
        ob = off // bw
        specs.append(pl.BlockSpec((tm, bw), lambda i, j, ob=ob: (i, j + ob)))
    return specs


def _rowwise_fwd(fn, rows, params, outs, *, tm, ncol=1, name):
    t = rows[0][0].shape[0]
    tm = min(tm, t)
    nr, npar = len(rows), len(params)

    def body(*refs):
        ins = [r[...].astype(F32) for r in refs[:nr + npar]]
        res = fn(*ins)
        for o_ref, val in zip(refs[nr + npar:], res):
            o_ref[...] = val.astype(o_ref.dtype)

    in_specs = _row_specs(rows, tm, ncol) + [pl.BlockSpec(p.shape, lambda i, j: (0, 0)) for p in params]
    out_specs = [pl.BlockSpec((tm, w // ncol), lambda i, j: (i, j)) for w, _ in outs]
    out_shape = [jax.ShapeDtypeStruct((t, w), dt) for w, dt in outs]
    return pl.pallas_call(
        body, name=name, grid=(t // tm, ncol), in_specs=in_specs, out_specs=out_specs, out_shape=out_shape,
        compiler_params=_cparams(("parallel", "parallel")),
    )(*[r[0] for r in rows], *params)


def _rowwise_bwd(fn, rows, params, cts, grad_dtypes, *, tm, ncol=1, name, adds=None):
    t = rows[0][0].shape[0]
    tm = min(tm, t)
    nr, npar, nct = len(rows), len(params), len(cts)
    adds = adds or [None] * nr
    add_idx = [i for i, a in enumerate(adds) if a is not None]

    def body(*refs):
        ins = [r[...].astype(F32) for r in refs[:nr + npar]]
        ct = tuple(r[...].astype(F32) for r in refs[nr + npar:nr + npar + nct])
        add_refs = refs[nr + npar + nct:nr + npar + nct + len(add_idx)]
        outs = refs[nr + npar + nct + len(add_idx):]
        _, vjp = jax.vjp(lambda *a: tuple(fn(*a)), *ins)
        grads = vjp(ct)
        extra = dict(zip(add_idx, add_refs))
        for i in range(nr):
            g = grads[i]
            if i in extra:
                g = g + extra[i][...].astype(F32)
            outs[i][...] = g.astype(outs[i].dtype)
        first = jnp.logical_and(pl.program_id(0) == 0, pl.program_id(1) == 0)
        for pi in range(npar):
            o_ref = outs[nr + pi]
            g = grads[nr + pi]

            @pl.when(first)
            def _(o_ref=o_ref, g=g):
                o_ref[...] = g

            @pl.when(jnp.logical_not(first))
            def _(o_ref=o_ref, g=g):
                o_ref[...] += g

    in_specs = (_row_specs(rows, tm, ncol)
                + [pl.BlockSpec(p.shape, lambda i, j: (0, 0)) for p in params]
                + [pl.BlockSpec((tm, c.shape[1] // ncol), lambda i, j: (i, j)) for c in cts]
                + [pl.BlockSpec((tm, adds[i].shape[1] // ncol), lambda i, j: (i, j)) for i in add_idx])
    out_specs = ([pl.BlockSpec((tm, w // ncol), lambda i, j: (i, j)) for _, _, w in rows]
                 + [pl.BlockSpec(p.shape, lambda i, j: (0, 0)) for p in params])
    out_shape = ([jax.ShapeDtypeStruct((t, w), dt) for (_, _, w), dt in zip(rows, grad_dtypes)]
                 + [jax.ShapeDtypeStruct(p.shape, F32) for p in params])
    return pl.pallas_call(
        body, name=name, grid=(t // tm, ncol), in_specs=in_specs, out_specs=out_specs, out_shape=out_shape,
        compiler_params=_cparams(("arbitrary", "arbitrary")),
    )(*[r[0] for r in rows], *params, *cts, *[adds[i] for i in add_idx])


def _rms(x, w):
    return x * lax.rsqrt(jnp.mean(x * x, axis=-1, keepdims=True) + EPS) * w


def _fn_norm(x, w):
    return (_rms(x, w),)


def _fn_gates(z, a_row, b_row):
    lane = lax.broadcasted_iota(jnp.int32, z.shape, 1)
    beta = jax.nn.sigmoid(z)
    g = -jnp.exp(a_row) * jax.nn.softplus(z + b_row)
    logf = jax.nn.log_sigmoid(z + b_row)
    return (jnp.where(lane < HEADS, beta, jnp.where(lane < 2 * HEADS, g, jnp.where(lane < 3 * HEADS, logf, 0.0))),)


def _fn_qknorm(q, k, qw, kw):
    return _rms(q, qw), _rms(k, kw)


def _fn_gdn_out(o, z, w):
    return (_rms(o, w) * jax.nn.silu(z),)


def _fn_merge(ga, gb, ya, yb):
    return (jax.nn.sigmoid(ga) * ya + jax.nn.sigmoid(gb) * yb,)


def _fn_swiglu(g, u):
    return (jax.nn.silu(g) * u,)


def _loss_head(y, target, *, tm, name):
    t, d = y.shape
    tm = min(tm, t)

    def body(y_ref, t_ref, dyf_ref, dyb_ref, loss_ref):
        err = y_ref[...] - t_ref[...]
        dy = err * (1.0 / d)
        dyf_ref[...] = dy
        dyb_ref[...] = dy.astype(BF16)
        part = jnp.sum(err * err) * (0.5 / d)

        @pl.when(pl.program_id(0) == 0)
        def _():
            loss_ref[...] = jnp.zeros_like(loss_ref)

        loss_ref[...] += part

    blk = pl.BlockSpec((tm, d), lambda i: (i, 0))
    return pl.pallas_call(
        body, name=name, grid=(t // tm,), in_specs=[blk, blk],
        out_specs=[blk, blk, pl.BlockSpec((1, LANES), lambda i: (0, 0))],
        out_shape=[jax.ShapeDtypeStruct((t, d), F32), jax.ShapeDtypeStruct((t, d), BF16),
                   jax.ShapeDtypeStruct((1, LANES), F32)],
        compiler_params=_cparams(("arbitrary",)),
    )(y, target)


def _shift_down(x, s):
    if s == 0:
        return x
    row = lax.broadcasted_iota(jnp.int32, x.shape, 0)
    return jnp.where(row >= s, pltpu.roll(x, s, 0), 0.0)


def _shift_up(x, s):
    if s == 0:
        return x
    t = x.shape[0]
    row = lax.broadcasted_iota(jnp.int32, x.shape, 0)
    return jnp.where(row < t - s, pltpu.roll(x, t - s, 0), 0.0)


def _conv_pre(x, w):
    y = x * w[CONV_K - 1:CONV_K, :]
    for i in range(CONV_K - 1):
        y = y + _shift_down(x, CONV_K - 1 - i) * w[i:i + 1, :]
    return y


def _conv_fwd(p_main, conv_w, *, width, name):
    t = p_main.shape[0]
    tc = LANES

    def body(x_ref, w_ref, o_ref):
        y = _conv_pre(x_ref[...].astype(F32), w_ref[...])
        o_ref[...] = y * jax.nn.sigmoid(y)

    return pl.pallas_call(
        body, name=name, grid=(width // tc,),
        in_specs=[pl.BlockSpec((t, tc), lambda j: (0, j)), pl.BlockSpec((CONV_K, tc), lambda j: (0, j))],
        out_specs=pl.BlockSpec((t, tc), lambda j: (0, j)),
        out_shape=jax.ShapeDtypeStruct((t, width), F32),
        compiler_params=_cparams(("parallel",)),
    )(p_main, conv_w)


def _conv_bwd(p_main, conv_w, dy, *, width, name):
    t = p_main.shape[0]
    tc = LANES

    def body(x_ref, w_ref, dy_ref, dx_ref, dw_ref):
        x = x_ref[...].astype(F32)
        w = w_ref[...]
        pre = _conv_pre(x, w)
        sg = jax.nn.sigmoid(pre)
        dpre = dy_ref[...] * (sg * (1.0 + pre * (1.0 - sg)))
        dx = dpre * w[CONV_K - 1:CONV_K, :]
        dws = []
        for i in range(CONV_K - 1):
            s = CONV_K - 1 - i
            dx = dx + _shift_up(dpre, s) * w[i:i + 1, :]
            dws.append(jnp.sum(_shift_down(x, s) * dpre, axis=0, keepdims=True))
        dws.append(jnp.sum(x * dpre, axis=0, keepdims=True))
        dx_ref[...] = dx.astype(dx_ref.dtype)
        dw_ref[...] = jnp.concatenate(dws, axis=0)

    return pl.pallas_call(
        body, name=name, grid=(width // tc,),
        in_specs=[pl.BlockSpec((t, tc), lambda j: (0, j)), pl.BlockSpec((CONV_K, tc), lambda j: (0, j)),
                  pl.BlockSpec((t, tc), lambda j: (0, j))],
        out_specs=[pl.BlockSpec((t, tc), lambda j: (0, j)), pl.BlockSpec((CONV_K, tc), lambda j: (0, j))],
        out_shape=[jax.ShapeDtypeStruct((t, width), BF16), jax.ShapeDtypeStruct((CONV_K, width), F32)],
        compiler_params=_cparams(("parallel",)),
    )(p_main, conv_w, dy)


def _bmm(a, b, spec, precision=None):
    return jnp.einsum(spec, a, b, preferred_element_type=F32, precision=precision)


def _iota2(shape, dim):
    return lax.broadcasted_iota(jnp.int32, shape, dim)


def _tri_inverse(a):
    c = a.shape[-1]
    r, m = _iota2((c, c), 0), _iota2((c, c), 1)
    eye = (r == m).astype(F32)
    inv = None
    b = 1
    while b < c:
        mask = jnp.logical_and(r // (2 * b) == m // (2 * b), jnp.logical_and(r % (2 * b) >= b, m % (2 * b) < b))
        off = jnp.where(mask[None], a, 0.0)
        if inv is None:
            inv = eye[None] - off
        else:
            inv = inv - _bmm(_bmm(inv, off, "hij,hjk->hik", HI), inv, "hij,hjk->hik", HI)
        b *= 2
    return inv


def _gdn_chunk(s, q3, k3, v3, b3, gc3):
    c = q3.shape[1]
    r, m = _iota2((c, c), 0), _iota2((c, c), 1)
    tril_incl = (r >= m)[None]
    tril_strict = (r > m)[None]
    eye = (r == m).astype(F32)[None]
    qn = q3 * lax.rsqrt(jnp.sum(q3 * q3, axis=-1, keepdims=True) + EPS) * (DH ** -0.5)
    kn = k3 * lax.rsqrt(jnp.sum(k3 * k3, axis=-1, keepdims=True) + EPS)
    ones = jnp.ones((q3.shape[0], c, c), F32)
    gc_row = _bmm(ones, gc3 * eye, "hij,hjk->hik", HI)
    decay = jnp.where(tril_incl, jnp.exp(jnp.where(tril_incl, gc3 - gc_row, 0.0)), 0.0)
    a = jnp.where(tril_strict, _bmm(kn, kn, "hcd,hmd->hcm") * decay * b3, 0.0)
    tinv = _tri_inverse(a)
    egc = jnp.exp(gc3)
    u = _bmm(tinv, v3 * b3, "hij,hjk->hik", HI)
    w = _bmm(tinv, kn * (b3 * egc), "hij,hjk->hik", HI)
    qk = _bmm(qn, kn, "hcd,hmd->hcm") * decay
    v_new = u - _bmm(w, s, "hcd,hdv->hcv")
    o = _bmm(qn * egc, s, "hcd,hdv->hcv") + _bmm(qk, v_new, "hcm,hmv->hcv")
    row = _iota2((c, 1), 0)[None]
    g_last = jnp.sum(jnp.where(row == c - 1, gc3, 0.0), axis=1, keepdims=True)
    s_new = s * jnp.exp(g_last) + _bmm(kn * jnp.exp(g_last - gc3), v_new, "hcd,hcv->hdv")
    return s_new, o


GDN_HEAD_GROUP = 8


def _split_heads(ref, off, h0):
    return jnp.stack([ref[:, off + h * DH:off + (h + 1) * DH].astype(F32)
                      for h in range(h0, h0 + GDN_HEAD_GROUP)], axis=0)


def _store_heads(ref, x3, off, h0):
    for i in range(GDN_HEAD_GROUP):
        h = h0 + i
        ref[:, off + h * DH:off + (h + 1) * DH] = x3[i].astype(ref.dtype)


def _lane_cols(tile, lane0):
    lane = _iota2(tile.shape, 1)
    return jnp.stack([jnp.sum(jnp.where(lane == lane0 + i, tile, 0.0), axis=1, keepdims=True)
                      for i in range(GDN_HEAD_GROUP)], axis=0)


def _cols_to_lanes(cols3, lane0, shape):
    lane = _iota2(shape, 1)
    out = jnp.zeros(shape, F32)
    for i in range(GDN_HEAD_GROUP):
        out = out + jnp.where(lane == lane0 + i, cols3[i], 0.0)
    return out


def _chunk_cumsum_matrix():
    r, m = _iota2((CHUNK, CHUNK), 0), _iota2((CHUNK, CHUNK), 1)
    return (r >= m).astype(F32)


def _gdn_inputs(qkv_ref, gt, gcum, h0):
    return (_split_heads(qkv_ref, 0, h0), _split_heads(qkv_ref, HW, h0), _split_heads(qkv_ref, 2 * HW, h0),
            _lane_cols(gt, h0), _lane_cols(gcum, HEADS + h0))


def _gdn_fwd(qkv, gates, *, name):
    t = qkv.shape[0]
    n = t // CHUNK

    def body(qkv_ref, gt_ref, o_ref, sall_ref, s_scr):
        @pl.when(pl.program_id(0) == 0)
        def _():
            s_scr[...] = jnp.zeros_like(s_scr)

        gt = gt_ref[...]
        gcum = jnp.dot(_chunk_cumsum_matrix(), gt, preferred_element_type=F32, precision=HI)
        for h0 in range(0, HEADS, GDN_HEAD_GROUP):
            grp = pl.ds(h0, GDN_HEAD_GROUP)
            s = s_scr[grp]
            sall_ref[0, grp] = s
            s_new, o3 = _gdn_chunk(s, *_gdn_inputs(qkv_ref, gt, gcum, h0))
            s_scr[grp] = s_new
            _store_heads(o_ref, o3, 0, h0)

    return pl.pallas_call(
        body, name=name, grid=(n,),
        in_specs=[pl.BlockSpec((CHUNK, 3 * HW), lambda i: (i, 0)), pl.BlockSpec((CHUNK, LANES), lambda i: (i, 0))],
        out_specs=[pl.BlockSpec((CHUNK, HW), lambda i: (i, 0)),
                   pl.BlockSpec((1, HEADS, DH, DH), lambda i: (i, 0, 0, 0))],
        out_shape=[jax.ShapeDtypeStruct((t, HW), F32), jax.ShapeDtypeStruct((n, HEADS, DH, DH), F32)],
        scratch_shapes=[pltpu.VMEM((HEADS, DH, DH), F32)],
        compiler_params=_cparams(("arbitrary",)),
    )(qkv, gates)


def _gdn_bwd(qkv, gates, s_all, do, *, name):
    t = qkv.shape[0]
    n = t // CHUNK

    def body(qkv_ref, gt_ref, sall_ref, do_ref, dqkv_ref, dgt_ref, ds_scr):
        @pl.when(pl.program_id(0) == 0)
        def _():
            ds_scr[...] = jnp.zeros_like(ds_scr)

        gt = gt_ref[...]
        cum = _chunk_cumsum_matrix()
        gcum = jnp.dot(cum, gt, preferred_element_type=F32, precision=HI)
        shape = (CHUNK, LANES)
        dbeta = jnp.zeros(shape, F32)
        dgcum = jnp.zeros(shape, F32)
        for h0 in range(0, HEADS, GDN_HEAD_GROUP):
            grp = pl.ds(h0, GDN_HEAD_GROUP)
            _, vjp = jax.vjp(_gdn_chunk, sall_ref[0, grp], *_gdn_inputs(qkv_ref, gt, gcum, h0))
            ds, dq3, dk3, dv3, db3, dgc3 = vjp((ds_scr[grp], _split_heads(do_ref, 0, h0)))
            ds_scr[grp] = ds
            _store_heads(dqkv_ref, dq3, 0, h0)
            _store_heads(dqkv_ref, dk3, HW, h0)
            _store_heads(dqkv_ref, dv3, 2 * HW, h0)
            dbeta = dbeta + _cols_to_lanes(db3, h0, shape)
            dgcum = dgcum + _cols_to_lanes(dgc3, HEADS + h0, shape)
        dg = lax.dot_general(cum, dgcum, (((0,), (0,)), ((), ())), preferred_element_type=F32, precision=HI)
        dgt_ref[...] = dbeta + dg

    rev = lambda i: n - 1 - i
    return pl.pallas_call(
        body, name=name, grid=(n,),
        in_specs=[pl.BlockSpec((CHUNK, 3 * HW), lambda i: (rev(i), 0)), pl.BlockSpec((CHUNK, LANES), lambda i: (rev(i), 0)),
                  pl.BlockSpec((1, HEADS, DH, DH), lambda i: (rev(i), 0, 0, 0)),
                  pl.BlockSpec((CHUNK, HW), lambda i: (rev(i), 0))],
        out_specs=[pl.BlockSpec((CHUNK, 3 * HW), lambda i: (rev(i), 0)), pl.BlockSpec((CHUNK, LANES), lambda i: (rev(i), 0))],
        out_shape=[jax.ShapeDtypeStruct((t, 3 * HW), F32), jax.ShapeDtypeStruct((t, LANES), F32)],
        scratch_shapes=[pltpu.VMEM((HEADS, DH, DH), F32)],
        compiler_params=_cparams(("arbitrary",)),
    )(qkv, gates, s_all, do)


FOX_BLK = 512
NEG = -1e30


def _fox_cumsum(gates, *, name):
    t = gates.shape[0]
    blk = min(FOX_BLK, t)

    def body(g_ref, c_ref):
        r, m = _iota2((blk, blk), 0), _iota2((blk, blk), 1)
        upper = (r <= m).astype(F32)
        carry = jnp.zeros((HEADS, 1), F32)
        for b in range(t // blk):
            lf = g_ref[b * blk:(b + 1) * blk, :].T[2 * HEADS:3 * HEADS, :]
            c_ref[:, b * blk:(b + 1) * blk] = jnp.dot(lf, upper, preferred_element_type=F32, precision=HI) + carry
            carry = carry + jnp.sum(lf, axis=1, keepdims=True)

    return pl.pallas_call(body, name=name, out_shape=jax.ShapeDtypeStruct((HEADS, t), F32),
                          compiler_params=_cparams())(gates)


def _fox_cumsum_bwd(dc, dgates_gdn, *, name):
    t = dc.shape[1]
    blk = min(FOX_BLK, t)

    def body(dc_ref, dg_ref, o_ref):
        r, m = _iota2((blk, blk), 0), _iota2((blk, blk), 1)
        lower = (r >= m).astype(F32)
        carry = jnp.zeros((HEADS, 1), F32)
        for b in reversed(range(t // blk)):
            d = dc_ref[:, b * blk:(b + 1) * blk]
            dlf = jnp.dot(d, lower, preferred_element_type=F32, precision=HI) + carry
            carry = carry + jnp.sum(d, axis=1, keepdims=True)
            tile = jnp.concatenate([jnp.zeros((2 * HEADS, blk), F32), dlf,
                                    jnp.zeros((LANES - 3 * HEADS, blk), F32)], axis=0)
            o_ref[b * blk:(b + 1) * blk, :] = tile.T + dg_ref[b * blk:(b + 1) * blk, :]

    return pl.pallas_call(body, name=name, out_shape=jax.ShapeDtypeStruct((t, LANES), F32),
                          compiler_params=_cparams())(dc, dgates_gdn)


def _fox_logits(q, k, c_row, qi, kj, blk):
    s = lax.dot_general(q, k, (((1,), (1,)), ((), ())), preferred_element_type=F32) * (DH ** -0.5) - c_row
    rows = qi * blk + _iota2((blk, blk), 0)
    cols = kj * blk + _iota2((blk, blk), 1)
    return jnp.where(rows >= cols, s, NEG)


def _fox_fwd(qn, kn, p_main, c4, *, v_off, name):
    t = qn.shape[0]
    blk = min(FOX_BLK, t)
    nb = t // blk
    vb = v_off // DH

    def body(q_ref, k_ref, v_ref, c_ref, o_ref, lse_ref):
        qi = pl.program_id(1)
        q = q_ref[...]

        def step(j, carry):
            m, l, acc = carry
            rows = pl.ds(pl.multiple_of(j * blk, blk), blk)
            s = _fox_logits(q, k_ref[rows, :], c_ref[0, j], qi, j, blk)
            m_new = jnp.maximum(m, jnp.max(s, axis=1, keepdims=True))
            p = jnp.exp(s - m_new)
            scale = jnp.exp(m - m_new)
            l = scale * l + jnp.sum(p, axis=1, keepdims=True)
            acc = scale * acc + jnp.dot(p.astype(BF16), v_ref[rows, :], preferred_element_type=F32)
            return m_new, l, acc

        init = (jnp.full((blk, 1), NEG, F32), jnp.zeros((blk, 1), F32), jnp.zeros((blk, DH), F32))
        m, l, acc = lax.fori_loop(0, qi + 1, step, init)
        o_ref[...] = (acc / l).astype(o_ref.dtype)
        lse_ref[0] = m + jnp.log(l)

    return pl.pallas_call(
        body, name=name, grid=(HEADS, nb),
        in_specs=[pl.BlockSpec((blk, DH), lambda h, i: (i, h)), pl.BlockSpec((t, DH), lambda h, i: (0, h)),
                  pl.BlockSpec((t, DH), lambda h, i: (0, vb + h)), pl.BlockSpec((1, nb, 1, blk), lambda h, i: (h, 0, 0, 0))],
        out_specs=[pl.BlockSpec((blk, DH), lambda h, i: (i, h)), pl.BlockSpec((1, blk, 1), lambda h, i: (h, i, 0))],
        out_shape=[jax.ShapeDtypeStruct((t, HW), BF16), jax.ShapeDtypeStruct((HEADS, t, 1), F32)],
        compiler_params=_cparams(("parallel", "arbitrary")),
    )(qn, kn, p_main, c4)


def _fox_delta(qn, kn, p_main, c4, do, lse, *, v_off, name):
    t = qn.shape[0]
    blk = min(FOX_BLK, t)
    nb = t // blk
    vb = v_off // DH

    def body(q_ref, k_ref, v_ref, c_ref, do_ref, lse_ref, delta_ref):
        qi = pl.program_id(1)
        q = q_ref[...]
        dob = do_ref[...]
        lse = lse_ref[0]

        def step(j, delta):
            rows = pl.ds(pl.multiple_of(j * blk, blk), blk)
            p = jnp.exp(_fox_logits(q, k_ref[rows, :], c_ref[0, j], qi, j, blk) - lse)
            dp = lax.dot_general(dob, v_ref[rows, :], (((1,), (1,)), ((), ())), preferred_element_type=F32)
            return delta + jnp.sum(p * dp, axis=1, keepdims=True)

        delta_ref[0] = lax.fori_loop(0, qi + 1, step, jnp.zeros((blk, 1), F32))

    qblk = lambda h, i: (i, h)
    return pl.pallas_call(
        body, name=name, grid=(HEADS, nb),
        in_specs=[pl.BlockSpec((blk, DH), qblk), pl.BlockSpec((t, DH), lambda h, i: (0, h)),
                  pl.BlockSpec((t, DH), lambda h, i: (0, vb + h)), pl.BlockSpec((1, nb, 1, blk), lambda h, i: (h, 0, 0, 0)),
                  pl.BlockSpec((blk, DH), qblk), pl.BlockSpec((1, blk, 1), lambda h, i: (h, i, 0))],
        out_specs=pl.BlockSpec((1, blk, 1), lambda h, i: (h, i, 0)),
        out_shape=jax.ShapeDtypeStruct((HEADS, t, 1), F32),
        compiler_params=_cparams(("parallel", "arbitrary")),
    )(qn, kn, p_main, c4, do, lse)


def _fox_bwd(qn, kn, p_main, c4, delta, do, lse, *, v_off, name):
    t = qn.shape[0]
    blk = min(FOX_BLK, t)
    nb = t // blk
    vb = v_off // DH
    tn_dims = (((0,), (0,)), ((), ()))
    nt_dims = (((1,), (1,)), ((), ()))

    def body(q_ref, k_ref, v_ref, c_ref, delta_ref, do_ref, lse_ref, dq_ref, dk_ref, dv_ref, dc_ref):
        kj = pl.program_id(1)

        @pl.when(kj == 0)
        def _():
            dq_ref[...] = jnp.zeros_like(dq_ref)

        k = k_ref[...]
        v = v_ref[...]
        c_row = c_ref[0, 0]

        def step(i, carry):
            dk, dv, dc = carry
            rows = pl.ds(pl.multiple_of(i * blk, blk), blk)
            q = q_ref[rows, :]
            dob = do_ref[rows, :]
            p = jnp.exp(_fox_logits(q, k, c_row, i, kj, blk) - lse_ref[0, rows, :])
            pb = p.astype(BF16)
            dv = dv + lax.dot_general(pb, dob, tn_dims, preferred_element_type=F32)
            dp = lax.dot_general(dob, v, nt_dims, preferred_element_type=F32)
            ds = p * (dp - delta_ref[0, rows, :])
            dsb = ds.astype(BF16)
            dq_ref[rows, :] += jnp.dot(dsb, k, preferred_element_type=F32) * (DH ** -0.5)
            dk = dk + lax.dot_general(dsb, q, tn_dims, preferred_element_type=F32) * (DH ** -0.5)
            dc = dc - jnp.sum(ds, axis=0, keepdims=True)
            return dk, dv, dc

        init = (jnp.zeros((blk, DH), F32), jnp.zeros((blk, DH), F32), jnp.zeros((1, blk), F32))
        dk, dv, dc = lax.fori_loop(kj, nb, step, init)
        dk_ref[...] = dk
        dv_ref[...] = dv.astype(dv_ref.dtype)
        dc_ref[0, 0] = dc

    full = lambda h, j: (0, h)
    kvb = lambda h, j: (j, h)
    return pl.pallas_call(
        body, name=name, grid=(HEADS, nb),
        in_specs=[pl.BlockSpec((t, DH), full), pl.BlockSpec((blk, DH), kvb),
                  pl.BlockSpec((blk, DH), lambda h, j: (j, vb + h)), pl.BlockSpec((1, 1, 1, blk), lambda h, j: (h, j, 0, 0)),
                  pl.BlockSpec((1, t, 1), lambda h, j: (h, 0, 0)), pl.BlockSpec((t, DH), full),
                  pl.BlockSpec((1, t, 1), lambda h, j: (h, 0, 0))],
        out_specs=[pl.BlockSpec((t, DH), full), pl.BlockSpec((blk, DH), kvb), pl.BlockSpec((blk, DH), kvb),
                   pl.BlockSpec((1, 1, 1, blk), lambda h, j: (h, j, 0, 0))],
        out_shape=[jax.ShapeDtypeStruct((t, HW), F32), jax.ShapeDtypeStruct((t, HW), F32),
                   jax.ShapeDtypeStruct((t, HW), BF16), jax.ShapeDtypeStruct((HEADS, nb, 1, blk), F32)],
        compiler_params=_cparams(("parallel", "arbitrary")),
    )(qn, kn, p_main, c4, delta, do, lse)


ANY = pl.BlockSpec(memory_space=pl.ANY)


def _mesh_pos():
    return lax.axis_index("x"), lax.axis_index("y"), lax.axis_index("c")


def _all_gather(blocks, *, name):
    n = len(blocks)

    def body(*refs):
        ins, outs = refs[:n], refs[n:2 * n]
        send, recv, local = refs[2 * n:]
        x, y, c = _mesh_pos()
        me, sibling = (x, y, c), (x, y, 1 - c)
        chips = [(1 - x, y), (x, 1 - y), (1 - x, 1 - y)]

        def copy(t, k, block, to, src=None):
            dst = outs[t].at[4 * block[0] + 2 * block[1] + block[2]]
            return pltpu.make_async_remote_copy(
                src_ref=dst if src is None else src, dst_ref=dst, send_sem=send.at[7 * t + k],
                recv_sem=recv.at[7 * t + k], device_id=to, device_id_type=MESH)

        mine = [pltpu.make_async_copy(ins[t], outs[t].at[4 * x + 2 * y + c], local.at[t]) for t in range(n)]
        for cp in mine:
            cp.start()
        first = []
        for t in range(n):
            first.append(copy(t, 0, me, sibling, src=ins[t]))
            first += [copy(t, 1 + j, me, (*chip, c), src=ins[t]) for j, chip in enumerate(chips)]
        for cp in first:
            cp.start()
        passed = []
        for j, chip in enumerate(chips):
            for t in range(n):
                copy(t, 1 + j, (*chip, c), me).wait_recv()
                fwd = copy(t, 4 + j, (*chip, c), sibling)
                fwd.start()
                passed.append(fwd)
        for t in range(n):
            copy(t, 0, sibling, me).wait_recv()
            for j, chip in enumerate(chips):
                copy(t, 4 + j, (*chip, 1 - c), me).wait_recv()
        for cp in first + passed:
            cp.wait_send()
        for cp in mine:
            cp.wait()

    return pl.pallas_call(
        body, name=name, in_specs=[ANY] * n, out_specs=[ANY] * n,
        out_shape=[jax.ShapeDtypeStruct((N_DEV,) + b.shape, b.dtype) for b in blocks],
        scratch_shapes=[pltpu.SemaphoreType.DMA((7 * n,)), pltpu.SemaphoreType.DMA((7 * n,)),
                        pltpu.SemaphoreType.DMA((n,))],
    )(*blocks)


def _rs_sibling(grads, *, name):
    n = len(grads)

    def body(*refs):
        ins, outs = refs[:n], refs[n:2 * n]
        send, recv = refs[2 * n:]
        x, y, c = _mesh_pos()
        copies = []
        for t in range(n):
            for q in range(4):
                copies.append(pltpu.make_async_remote_copy(
                    src_ref=ins[t].at[2 * q + (1 - c)], dst_ref=outs[t].at[q], send_sem=send.at[4 * t + q],
                    recv_sem=recv.at[4 * t + q], device_id=(x, y, 1 - c), device_id_type=MESH))
        for cp in copies:
            cp.start()
        for cp in copies:
            cp.wait_recv()
        for cp in copies:
            cp.wait_send()

    return pl.pallas_call(
        body, name=name, in_specs=[ANY] * n, out_specs=[ANY] * n,
        out_shape=[jax.ShapeDtypeStruct((4,) + g.shape[1:], g.dtype) for g in grads],
        scratch_shapes=[pltpu.SemaphoreType.DMA((4 * n,)), pltpu.SemaphoreType.DMA((4 * n,))],
    )(*grads)


def _rs_chips(parts, *, name):
    n = len(parts)

    def body(*refs):
        ins, outs = refs[:n], refs[n:2 * n]
        send, recv, local = refs[2 * n:]
        x, y, c = _mesh_pos()
        my_chip = 2 * x + y
        chips = [(1 - x, y), (x, 1 - y), (1 - x, 1 - y)]
        mine = [pltpu.make_async_copy(ins[t].at[my_chip], outs[t].at[my_chip], local.at[t]) for t in range(n)]
        for cp in mine:
            cp.start()
        sends, lands = [], []
        for t in range(n):
            for k, (px, py) in enumerate(chips):
                sends.append(pltpu.make_async_remote_copy(
                    src_ref=ins[t].at[2 * px + py], dst_ref=outs[t].at[my_chip], send_sem=send.at[3 * t + k],
                    recv_sem=recv.at[3 * t + k], device_id=(px, py, c), device_id_type=MESH))
                lands.append(pltpu.make_async_remote_copy(
                    src_ref=outs[t].at[2 * px + py], dst_ref=outs[t].at[2 * px + py], send_sem=send.at[3 * t + k],
                    recv_sem=recv.at[3 * t + k], device_id=(px, py, c), device_id_type=MESH))
        for cp in sends:
            cp.start()
        for cp in lands:
            cp.wait_recv()
        for cp in sends:
            cp.wait_send()
        for cp in mine:
            cp.wait()

    return pl.pallas_call(
        body, name=name, in_specs=[ANY] * n, out_specs=[ANY] * n,
        out_shape=[jax.ShapeDtypeStruct(p.shape, p.dtype) for p in parts],
        scratch_shapes=[pltpu.SemaphoreType.DMA((3 * n,)), pltpu.SemaphoreType.DMA((3 * n,)),
                        pltpu.SemaphoreType.DMA((n,))],
    )(*parts)


def _row_tile(r, c, itemsize, budget=3 * 1024 * 1024):
    best = None
    for tr in range(16, r + 1, 16):
        if r % tr == 0 and tr * c * itemsize <= budget:
            best = tr
    return best or r


def _pair_sum(grad, land, *, name):
    _, r, c = grad.shape
    tr = _row_tile(r, c, 2)

    def body(g_ref, l_ref, o_ref):
        o_ref[...] = (g_ref[...].astype(F32) + l_ref[...].astype(F32)).astype(o_ref.dtype)

    return pl.pallas_call(
        body, name=name, grid=(4, r // tr),
        in_specs=[pl.BlockSpec((1, tr, c), lambda q, i: (2 * q + lax.axis_index("c"), i, 0)),
                  pl.BlockSpec((1, tr, c), lambda q, i: (q, i, 0))],
        out_specs=pl.BlockSpec((1, tr, c), lambda q, i: (q, i, 0)),
        out_shape=jax.ShapeDtypeStruct((4, r, c), grad.dtype),
        compiler_params=_cparams(("parallel", "parallel")),
    )(grad, land)


def _adamw_math(w, g, m, v):
    m = ADAM_B1 * m + (1.0 - ADAM_B1) * g
    v = ADAM_B2 * v + (1.0 - ADAM_B2) * jnp.square(g)
    m_hat = m / (1.0 - ADAM_B1 ** ADAM_STEP)
    v_hat = v / (1.0 - ADAM_B2 ** ADAM_STEP)
    delta = -ADAM_LR * (m_hat / (jnp.sqrt(v_hat) + ADAM_EPS) + ADAM_WD * w)
    return delta, m, v


def _adamw(parts, w, m, v, *, name):
    s, r, c = parts.shape
    tr = _row_tile(r, c, 4, budget=1024 * 1024)

    def body(p_ref, w_ref, m_ref, v_ref, g_ref, d_ref, nm_ref, nv_ref):
        g = p_ref[0].astype(F32)
        for i in range(1, s):
            g = g + p_ref[i].astype(F32)
        delta, nm, nv = _adamw_math(w_ref[...], g, m_ref[...], v_ref[...])
        g_ref[...] = g
        d_ref[...] = delta
        nm_ref[...] = nm
        nv_ref[...] = nv

    blk = pl.BlockSpec((tr, c), lambda i: (i, 0))
    return pl.pallas_call(
        body, name=name, grid=(r // tr,),
        in_specs=[pl.BlockSpec((s, tr, c), lambda i: (0, i, 0)), blk, blk, blk],
        out_specs=[blk] * 4, out_shape=[jax.ShapeDtypeStruct((r, c), F32)] * 4,
        compiler_params=_cparams(("parallel",)),
    )(parts, w, m, v)


def _pad_cols(a, n):
    return a if a.shape[1] == n else jnp.concatenate([a, jnp.zeros((a.shape[0], n - a.shape[1]), a.dtype)], axis=1)


def _pad_rows(a, n):
    return a if a.shape[0] == n else jnp.concatenate([a, jnp.zeros((n - a.shape[0], a.shape[1]), a.dtype)], axis=0)


def _local_step(x, target, w_main, w_small, conv_w, wa, wb, wout, wg, wu, wd,
                norm_mix_w, norm_ffn_w, gdn_norm_w, fox_q_w, fox_k_w, a_row, b_row):
    t, d = x.shape
    fp = wg.shape[1]
    n_main = w_main.shape[1]
    off_gb = OFF_GA + d
    tm = 1024
    rt = 128
    fcol = fp // 1024 if fp % 1024 == 0 else max(fp // 512, 1)

    (xn,) = _rowwise_fwd(_fn_norm, [(x, 0, d)], [norm_mix_w], [(d, BF16)], tm=rt, name="mix_norm")
    p_main = _mm(xn, w_main, mode="nn", m=t, n=n_main, k=d, tm=tm, tn=512, tk=d, out_dtype=BF16, name="in_proj")
    p_small = _mm(xn, w_small, mode="nn", m=t, n=LANES, k=d, tm=tm, tn=LANES, tk=d, out_dtype=F32, name="in_proj_small")
    (gates,) = _rowwise_fwd(_fn_gates, [(p_small, 0, LANES)], [a_row, b_row], [(LANES, F32)], tm=512, name="gates")
    qkv = _conv_fwd(p_main, conv_w, width=3 * HW, name="conv_fwd")
    o_gdn, s_all = _gdn_fwd(qkv, gates, name="gdn_fwd")
    gdn_rows = [(o_gdn, 0, HW), (p_main, OFF_ZA, HW)]
    (oa,) = _rowwise_fwd(_fn_gdn_out, gdn_rows, [gdn_norm_w], [(HW, BF16)], tm=512, ncol=HEADS, name="gdn_out")
    ya = _mm(oa, wa, mode="nn", m=t, n=d, k=HW, tm=tm, tn=1024, tk=HW, out_dtype=BF16, name="branch_a")
    qk_rows = [(p_main, OFF_QB, HW), (p_main, OFF_KB, HW)]
    qn, kn = _rowwise_fwd(_fn_qknorm, qk_rows, [fox_q_w, fox_k_w], [(HW, BF16), (HW, BF16)], tm=512, ncol=HEADS,
                          name="fox_qk_norm")
    blk = min(FOX_BLK, t)
    c4 = _fox_cumsum(gates, name="fox_cumsum").reshape(HEADS, t // blk, 1, blk)
    ob, lse = _fox_fwd(qn, kn, p_main, c4, v_off=OFF_VB, name="fox_fwd")
    yb = _mm(ob, wb, mode="nn", m=t, n=d, k=HW, tm=tm, tn=1024, tk=HW, out_dtype=BF16, name="branch_b")
    mcol = 2 if d >= 2 * HW else 1
    merge_rows = [(p_main, OFF_GA, d), (p_main, off_gb, d), (ya, 0, d), (yb, 0, d)]
    (merged,) = _rowwise_fwd(_fn_merge, merge_rows, [], [(d, BF16)], tm=256, ncol=mcol, name="merge")
    h = _mm(merged, wout, mode="nn", m=t, n=d, k=d, tm=tm, tn=512, tk=d, out_dtype=F32, add=x, name="out_proj")
    (hn,) = _rowwise_fwd(_fn_norm, [(h, 0, d)], [norm_ffn_w], [(d, BF16)], tm=rt, name="ffn_norm")
    gate = _mm(hn, wg, mode="nn", m=t, n=fp, k=d, tm=tm, tn=512, tk=d, out_dtype=BF16, name="ffn_gate")
    up = _mm(hn, wu, mode="nn", m=t, n=fp, k=d, tm=tm, tn=512, tk=d, out_dtype=BF16, name="ffn_up")
    (act,) = _rowwise_fwd(_fn_swiglu, [(gate, 0, fp), (up, 0, fp)], [], [(fp, BF16)], tm=512, ncol=fcol, name="swiglu")
    y = _mm(act, wd, mode="nn", m=t, n=d, k=fp, tm=tm, tn=1024, tk=1024, out_dtype=F32, add=h, name="ffn_down")
    dy, dyb, loss_row = _loss_head(y, target, tm=rt, name="loss_head")

    gw = {}
    dact = _mm(dyb, wd, mode="nt", m=t, n=fp, k=d, tm=tm, tn=512, tk=d, out_dtype=BF16, name="d_act")
    gw["wd"] = _mm(act, dyb, mode="tn", m=fp, n=d, k=t, tm=1024, tn=1024, tk=1024, out_dtype=BF16, name="dw_ffn_down")
    dgate, dup = _rowwise_bwd(_fn_swiglu, [(gate, 0, fp), (up, 0, fp)], [], [dact], [BF16, BF16], tm=512, ncol=fcol,
                              name="d_swiglu")
    dhn = _mm(dgate, wg, mode="nt", m=t, n=d, k=fp, tm=tm, tn=1024, tk=1024, out_dtype=F32, name="d_hn_gate")
    dhn = _mm(dup, wu, mode="nt", m=t, n=d, k=fp, tm=tm, tn=1024, tk=1024, out_dtype=F32, add=dhn, name="d_hn_up")
    gw["wg"] = _mm(hn, dgate, mode="tn", m=d, n=fp, k=t, tm=1024, tn=1024, tk=1024, out_dtype=BF16, name="dw_ffn_gate")
    gw["wu"] = _mm(hn, dup, mode="tn", m=d, n=fp, k=t, tm=1024, tn=1024, tk=1024, out_dtype=BF16, name="dw_ffn_up")
    dh, d_norm_ffn = _rowwise_bwd(_fn_norm, [(h, 0, d)], [norm_ffn_w], [dhn], [F32], tm=rt, name="d_ffn_norm", adds=[dy])
    dmerged = _mm(dh, wout, mode="nt", m=t, n=d, k=d, tm=512, tn=512, tk=d, out_dtype=BF16, name="d_merged")
    gw["wout"] = _mm(merged, dh, mode="tn", m=d, n=d, k=t, tm=1024, tn=1024, tk=512, out_dtype=BF16, name="dw_out")
    dga, dgb, dya, dyb2 = _rowwise_bwd(_fn_merge, merge_rows, [], [dmerged], [BF16] * 4, tm=256, ncol=mcol, name="d_merge")
    doa = _mm(dya, wa, mode="nt", m=t, n=HW, k=d, tm=tm, tn=512, tk=d, out_dtype=BF16, name="d_oa")
    gw["wa"] = _mm(oa, dya, mode="tn", m=HW, n=d, k=t, tm=1024, tn=1024, tk=1024, out_dtype=BF16, name="dw_branch_a")
    dob = _mm(dyb2, wb, mode="nt", m=t, n=HW, k=d, tm=tm, tn=512, tk=d, out_dtype=BF16, name="d_ob")
    gw["wb"] = _mm(ob, dyb2, mode="tn", m=HW, n=d, k=t, tm=1024, tn=1024, tk=1024, out_dtype=BF16, name="dw_branch_b")
    do_gdn, dza, d_gdn_norm = _rowwise_bwd(_fn_gdn_out, gdn_rows, [gdn_norm_w], [doa], [F32, BF16], tm=512,
                                           ncol=HEADS, name="d_gdn_out")
    dqkv, dgates_gdn = _gdn_bwd(qkv, gates, s_all, do_gdn, name="gdn_bwd")
    dp_qkv, gw["conv"] = _conv_bwd(p_main, conv_w, dqkv, width=3 * HW, name="conv_bwd")
    delta = _fox_delta(qn, kn, p_main, c4, dob, lse, v_off=OFF_VB, name="fox_delta")
    dqn, dkn, dvb, dc4 = _fox_bwd(qn, kn, p_main, c4, delta, dob, lse, v_off=OFF_VB, name="fox_bwd")
    dqb, dkb, d_fox_q, d_fox_k = _rowwise_bwd(_fn_qknorm, qk_rows, [fox_q_w, fox_k_w], [dqn, dkn], [BF16, BF16],
                                              tm=512, ncol=HEADS, name="d_fox_qk_norm")
    dgates = _fox_cumsum_bwd(dc4.reshape(HEADS, t), dgates_gdn, name="fox_cumsum_bwd")
    dsmall, d_a_row, d_b_row = _rowwise_bwd(_fn_gates, [(p_small, 0, LANES)], [a_row, b_row], [dgates], [F32],
                                            tm=512, name="d_gates")
    segs = [(dp_qkv, 0, 3 * HW), (dza, OFF_ZA, HW), (dqb, OFF_QB, HW), (dkb, OFF_KB, HW), (dvb, OFF_VB, HW),
            (dga, OFF_GA, d), (dgb, off_gb, d)]
    dxn = _mm(dsmall, w_small, mode="nt", m=t, n=d, k=LANES, tm=tm, tn=1024, tk=LANES, out_dtype=F32, name="d_xn_small")
    gw_main = []
    for i, (dp, off, width) in enumerate(segs):
        dxn = _mm(dp, w_main, mode="nt", m=t, n=d, k=width, tm=tm, tn=1024, tk=1024, out_dtype=F32, add=dxn,
                  b_off=(0, off), name=f"d_xn_{i}")
        gw_main.append(_mm(xn, dp, mode="tn", m=d, n=width, k=t, tm=1024, tn=1024, tk=1024, out_dtype=BF16,
                           name=f"dw_in_{i}"))
    gw["w_main"] = gw_main
    gw["w_small"] = _mm(xn, dsmall, mode="tn", m=d, n=LANES, k=t, tm=1024, tn=LANES, tk=1024, out_dtype=BF16,
                        name="dw_in_small")
    grad_x, d_norm_mix = _rowwise_bwd(_fn_norm, [(x, 0, d)], [norm_mix_w], [dxn], [F32], tm=rt, name="d_mix_norm",
                                      adds=[dh])
    small = dict(norm_mix=d_norm_mix, norm_ffn=d_norm_ffn, gdn_norm=d_gdn_norm, fox_q=d_fox_q, fox_k=d_fox_k,
                 a_row=d_a_row, b_row=d_b_row)
    return loss_row[0, 0], grad_x, gw, small


def _lane_row(pieces):
    row = jnp.zeros((1, LANES), F32)
    for off, p in pieces:
        row = lax.dynamic_update_slice(row, p.astype(F32), (0, off))
    return row


def _pack_small(norm_mix, norm_ffn, gdn_norm, fox_q, fox_k, a_log, dt_bias, b_f):
    rows = [norm_mix.reshape(-1, LANES), norm_ffn.reshape(-1, LANES), gdn_norm, fox_q, fox_k,
            _lane_row([(HEADS, a_log)]), _lane_row([(HEADS, dt_bias), (2 * HEADS, b_f)])]
    packed = jnp.concatenate(rows, axis=0)
    return _pad_rows(packed, -(-packed.shape[0] // 8) * 8)


def _unpack_small(p, d):
    nd = d // LANES
    r = 2 * nd
    return (p[0:nd].reshape(1, d), p[r + 3:r + 4, HEADS:2 * HEADS], p[r + 4:r + 5, HEADS:2 * HEADS], p[r:r + 1],
            p[r + 4:r + 5, 2 * HEADS:3 * HEADS], p[r + 1:r + 2], p[r + 2:r + 3], p[nd:r].reshape(1, d))


def _blocks_of_cols(a):
    r, c8 = a.shape
    return a.reshape(r, N_DEV, c8 // N_DEV).transpose(1, 0, 2)


def _cols_of_blocks(g):
    _, r, c = g.shape
    return g.transpose(1, 0, 2).reshape(r, N_DEV * c)


def kernel(x, norm_mix_w, w_in, conv_w, a_log, dt_bias, gdn_norm_w, fox_b_f, fox_q_norm_w, fox_k_norm_w, w_branch_a, w_branch_b, w_out, norm_ffn_w, w_ffn_gate, w_ffn_up, w_ffn_down, loss_target, m_norm_mix_w, m_w_in, m_conv_w, m_a_log, m_dt_bias, m_gdn_norm_w, m_fox_b_f, m_fox_q_norm_w, m_fox_k_norm_w, m_w_branch_a, m_w_branch_b, m_w_out, m_norm_ffn_w, m_w_ffn_gate, m_w_ffn_up, m_w_ffn_down, v_norm_mix_w, v_w_in, v_conv_w, v_a_log, v_dt_bias, v_gdn_norm_w, v_fox_b_f, v_fox_q_norm_w, v_fox_k_norm_w, v_w_branch_a, v_w_branch_b, v_w_out, v_norm_ffn_w, v_w_ffn_gate, v_w_ffn_up, v_w_ffn_down):
    d = x.shape[-1]
    dff = w_ffn_down.shape[1] * N_DEV
    fp = -(-dff // 512) * 512

    shards = [w_in[0].astype(BF16), w_branch_a[0].astype(BF16), w_branch_b[0].astype(BF16), w_out[0].astype(BF16),
              w_ffn_gate[0].astype(BF16), w_ffn_up[0].astype(BF16), w_ffn_down[0].astype(BF16), conv_w[0]]
    g_in, g_a, g_b, g_out, g_gate, g_up, g_down, g_conv = _all_gather(shards, name="weights_all_gather")
    w_full = _cols_of_blocks(g_in)
    o_qb, o_f, o_ga = 4 * HW + 2 * HEADS, 7 * HW + 2 * HEADS, 7 * HW + 3 * HEADS
    w_main = jnp.concatenate([w_full[:, :4 * HW], w_full[:, o_qb:o_qb + 3 * HW], w_full[:, o_ga:]], axis=1)
    w_small = _pad_cols(jnp.concatenate([w_full[:, 4 * HW:o_qb], w_full[:, o_f:o_ga]], axis=1), LANES)
    wa, wb = _cols_of_blocks(g_a), _cols_of_blocks(g_b)
    wout = g_out.reshape(d, d)
    wg, wu = _pad_cols(_cols_of_blocks(g_gate), fp), _pad_cols(_cols_of_blocks(g_up), fp)
    wd = _pad_rows(g_down.reshape(dff, d), fp)
    conv_full = _cols_of_blocks(g_conv)
    a_row = _lane_row([(HEADS, a_log)])
    b_row = _lane_row([(HEADS, dt_bias), (2 * HEADS, fox_b_f)])

    loss_part, grad_x, gw, gs = _local_step(
        x[0], loss_target[0], w_main, w_small, conv_full, wa, wb, wout, wg, wu, wd,
        norm_mix_w, norm_ffn_w, gdn_norm_w, fox_q_norm_w, fox_k_norm_w, a_row, b_row)
    loss = lax.psum(loss_part, ("x", "y", "c"))

    dwm = jnp.concatenate(gw["w_main"], axis=1)
    dws = gw["w_small"]
    dw_in = jnp.concatenate([dwm[:, :4 * HW], dws[:, :2 * HEADS], dwm[:, 4 * HW:7 * HW], dws[:, 2 * HEADS:3 * HEADS],
                             dwm[:, 7 * HW:]], axis=1)
    full_grads = [_blocks_of_cols(dw_in), _blocks_of_cols(gw["wa"]), _blocks_of_cols(gw["wb"]),
                  gw["wout"].reshape(N_DEV, d // N_DEV, d), _blocks_of_cols(gw["wg"][:, :dff]),
                  _blocks_of_cols(gw["wu"][:, :dff]), gw["wd"][:dff].reshape(N_DEV, dff // N_DEV, d),
                  _blocks_of_cols(gw["conv"].astype(BF16))]
    lands = _rs_sibling(full_grads, name="grads_to_sibling")
    names = ["w_in", "w_branch_a", "w_branch_b", "w_out", "w_ffn_gate", "w_ffn_up", "w_ffn_down", "conv_w"]
    parts = [_pair_sum(g, l, name=f"pair_sum_{nm}") for g, l, nm in zip(full_grads, lands, names)]
    slots = _rs_chips(parts, name="grads_to_chips")
    big = dict(w_in=(w_in, m_w_in, v_w_in), w_branch_a=(w_branch_a, m_w_branch_a, v_w_branch_a),
               w_branch_b=(w_branch_b, m_w_branch_b, v_w_branch_b), w_out=(w_out, m_w_out, v_w_out),
               w_ffn_gate=(w_ffn_gate, m_w_ffn_gate, v_w_ffn_gate), w_ffn_up=(w_ffn_up, m_w_ffn_up, v_w_ffn_up),
               w_ffn_down=(w_ffn_down, m_w_ffn_down, v_w_ffn_down), conv_w=(conv_w, m_conv_w, v_conv_w))
    res = {}
    for nm, slot in zip(names, slots):
        w, m, v = big[nm]
        res[nm] = [o[None] for o in _adamw(slot, w[0], m[0], v[0], name=f"adamw_{nm}")]

    g_small = _pack_small(gs["norm_mix"], gs["norm_ffn"], gs["gdn_norm"], gs["fox_q"], gs["fox_k"],
                          gs["a_row"][:, HEADS:2 * HEADS], gs["b_row"][:, HEADS:2 * HEADS],
                          gs["b_row"][:, 2 * HEADS:3 * HEADS])
    (g_small_all,) = _all_gather([g_small], name="small_grads_all_gather")
    w_small_p = _pack_small(norm_mix_w, norm_ffn_w, gdn_norm_w, fox_q_norm_w, fox_k_norm_w, a_log, dt_bias, fox_b_f)
    m_small_p = _pack_small(m_norm_mix_w, m_norm_ffn_w, m_gdn_norm_w, m_fox_q_norm_w, m_fox_k_norm_w, m_a_log,
                            m_dt_bias, m_fox_b_f)
    v_small_p = _pack_small(v_norm_mix_w, v_norm_ffn_w, v_gdn_norm_w, v_fox_q_norm_w, v_fox_k_norm_w, v_a_log,
                            v_dt_bias, v_fox_b_f)
    small_res = [_unpack_small(o, d) for o in _adamw(g_small_all, w_small_p, m_small_p, v_small_p, name="adamw_small")]

    def group(k):
        s = small_res[k]
        return [s[0], res["w_in"][k], res["conv_w"][k], s[1], s[2], s[3], s[4], s[5], s[6], res["w_branch_a"][k],
                res["w_branch_b"][k], res["w_out"][k], s[7], res["w_ffn_gate"][k], res["w_ffn_up"][k],
                res["w_ffn_down"][k]]

    return (loss, grad_x[None], *group(0), *group(1), *group(2), *group(3))
```

```python
import functools
import math

import jax
import jax.numpy as jnp
from jax import lax
from jax.experimental import pallas as pl
from jax.experimental.pallas import tpu as pltpu

F32 = jnp.float32
BF16 = jnp.bfloat16
HI = lax.Precision.HIGHEST
MESH = pl.DeviceIdType.MESH

EPS = 1e-6
HEADS = 16
DH = 128
HW = HEADS * DH
CHUNK = 64
CONV_K = 4
N_DEV = 8
LANES = 128
VMEM_LIMIT = 52 * 1024 * 1024

ADAM_LR = 0.001
ADAM_B1 = 0.9
ADAM_B2 = 0.999
ADAM_EPS = 1e-08
ADAM_WD = 0.01
ADAM_STEP = 10

OFF_QA, OFF_KA, OFF_VA, OFF_ZA, OFF_QB, OFF_KB, OFF_VB, OFF_GA = 0, HW, 2 * HW, 3 * HW, 4 * HW, 5 * HW, 6 * HW, 7 * HW


def _cparams(sem=None, vmem=VMEM_LIMIT):
    return pltpu.CompilerParams(dimension_semantics=sem, vmem_limit_bytes=vmem)


class _Comm:
    def __init__(self, ins, out_shapes, sems, start, finish, aliases=None):
        self.ins, self.out_shapes, self.sems = list(ins), list(out_shapes), list(sems)
        self.start, self.finish, self.aliases = start, finish, dict(aliases or {})


def _pcall(body, *, name, grid, in_specs, out_specs, out_shape, args, sem, scratch_shapes=(), comm=None):
    multi = isinstance(out_shape, (list, tuple))
    out_specs = list(out_specs) if multi else [out_specs]
    out_shape = list(out_shape) if multi else [out_shape]
    scratch_shapes = list(scratch_shapes)
    if comm is None:
        res = pl.pallas_call(body, name=name, grid=grid, in_specs=list(in_specs), out_specs=out_specs,
                             out_shape=out_shape, scratch_shapes=scratch_shapes, compiler_params=_cparams(sem))(*args)
        return res if multi else res[0]
    ni, no, ns = len(in_specs), len(out_specs), len(scratch_shapes)
    ci, co = len(comm.ins), len(comm.out_shapes)

    def wrapped(*refs):
        cin = refs[ni:ni + ci]
        outs = refs[ni + ci:ni + ci + no]
        cout = refs[ni + ci + no:ni + ci + no + co]
        scr = refs[ni + ci + no + co:ni + ci + no + co + ns]
        csem = refs[ni + ci + no + co + ns:]
        ids = [pl.program_id(ax) for ax in range(len(grid))]
        first = functools.reduce(jnp.logical_and, [i == 0 for i in ids])
        last = functools.reduce(jnp.logical_and, [i == g - 1 for i, g in zip(ids, grid)])

        @pl.when(first)
        def _():
            comm.start(cin, cout, csem)

        body(*refs[:ni], *outs, *scr)

        @pl.when(last)
        def _():
            comm.finish(cin, cout, csem)

    any_spec = pl.BlockSpec(memory_space=pl.ANY)
    res = pl.pallas_call(
        wrapped, name=name, grid=grid, in_specs=list(in_specs) + [any_spec] * ci,
        out_specs=out_specs + [any_spec] * co, out_shape=out_shape + comm.out_shapes,
        scratch_shapes=scratch_shapes + comm.sems,
        input_output_aliases={ni + i: no + o for i, o in comm.aliases.items()},
        compiler_params=_cparams(("arbitrary",) * len(grid)))(*args, *comm.ins)
    return (res[:no] if multi else res[0]), res[no:]


def _mm(a, b, *, mode, m, n, k, tm, tn, tk, out_dtype, name, a_off=(0, 0), b_off=(0, 0), add=None,
        b_blocked=False, out_blocked=False, comm=None):
    tm, tn, tk = min(tm, m), min(tn, n), min(tk, k)
    assert m % tm == 0 and n % tn == 0 and k % tk == 0, (name, m, n, k, tm, tn, tk)
    nk = k // tk
    if mode == "nn":
        a_blk, b_blk = (tm, tk), (tk, tn)
        ao, bo = (a_off[0] // tm, a_off[1] // tk), (b_off[0] // tk, b_off[1] // tn)
        a_map = lambda i, j, kk: (i + ao[0], kk + ao[1])
        b_map = lambda i, j, kk: (kk + bo[0], j + bo[1])
        dims = (((1,), (0,)), ((), ()))
        if b_blocked:
            assert b.shape == (n // tn, k, tn) and b_off == (0, 0), (name, b.shape)
            b_blk, b_map = (None, tk, tn), lambda i, j, kk: (j, kk, 0)
    elif mode == "nt":
        a_blk, b_blk = (tm, tk), (tn, tk)
        ao, bo = (a_off[0] // tm, a_off[1] // tk), (b_off[0] // tn, b_off[1] // tk)
        a_map = lambda i, j, kk: (i + ao[0], kk + ao[1])
        b_map = lambda i, j, kk: (j + bo[0], kk + bo[1])
        dims = (((1,), (1,)), ((), ()))
        if b_blocked:
            assert b.shape == (nk, n, tk) and b_off == (0, 0), (name, b.shape)
            b_blk, b_map = (None, tn, tk), lambda i, j, kk: (kk, j, 0)
    else:
        assert not b_blocked
        a_blk, b_blk = (tk, tm), (tk, tn)
        ao, bo = (a_off[0] // tk, a_off[1] // tm), (b_off[0] // tk, b_off[1] // tn)
        a_map = lambda i, j, kk: (kk + ao[0], i + ao[1])
        b_map = lambda i, j, kk: (kk + bo[0], j + bo[1])
        dims = (((0,), (0,)), ((), ()))
    if not b_blocked:
        for off, blk in ((a_off, a_blk), (b_off, b_blk)):
            assert off[0] % blk[0] == 0 and off[1] % blk[1] == 0, (name, off, blk)
    has_add = add is not None

    def body(*refs):
        if has_add:
            a_ref, b_ref, c_ref, o_ref, acc = refs
        else:
            a_ref, b_ref, o_ref, acc = refs
            c_ref = None
        p = lax.dot_general(a_ref[...].astype(BF16), b_ref[...].astype(BF16), dims, preferred_element_type=F32)
        if nk == 1:
            if has_add:
                p = p + c_ref[...].astype(F32)
            o_ref[...] = p.astype(o_ref.dtype)
        else:
            kk = pl.program_id(2)

            @pl.when(kk == 0)
            def _():
                acc[...] = p + c_ref[...].astype(F32) if has_add else p

            @pl.when(kk > 0)
            def _():
                acc[...] += p

            @pl.when(kk == nk - 1)
            def _():
                o_ref[...] = acc[...].astype(o_ref.dtype)

    in_specs = [pl.BlockSpec(a_blk, a_map), pl.BlockSpec(b_blk, b_map)]
    args = [a, b]
    if has_add:
        in_specs.append(pl.BlockSpec((tm, tn), lambda i, j, kk: (i, j)))
        args.append(add)
    acc_shape = (tm, tn) if nk > 1 else (8, LANES)
    if out_blocked:
        out_spec = pl.BlockSpec((None, tm, tn), lambda i, j, kk: (j, i, 0))
        out_shape = jax.ShapeDtypeStruct((n // tn, m, tn), out_dtype)
    else:
        out_spec = pl.BlockSpec((tm, tn), lambda i, j, kk: (i, j))
        out_shape = jax.ShapeDtypeStruct((m, n), out_dtype)
    return _pcall(body, name=name, grid=(m // tm, n // tn, nk), in_specs=in_specs, out_specs=out_spec,
                  out_shape=out_shape, scratch_shapes=[pltpu.VMEM(acc_shape, F32)], args=args,
                  sem=("parallel", "parallel", "arbitrary"), comm=comm)


def _row_specs(rows, tm, ncol):
    specs = []
    for arr, off, width in rows:
        bw = width // ncol
        assert width % ncol == 0 and off % bw == 0, (off, width, ncol)
        ob = off // bw
        specs.append(pl.BlockSpec((tm, bw), lambda i, j, ob=ob: (i, j + ob)))
    return specs


def _rowwise_fwd(fn, rows, params, outs, *, tm, ncol=1, name):
    t = rows[0][0].shape[0]
    tm = min(tm, t)
    nr, npar = len(rows), len(params)

    def body(*refs):
        ins = [r[...].astype(F32) for r in refs[:nr + npar]]
        res = fn(*ins)
        for o_ref, val in zip(refs[nr + npar:], res):
            o_ref[...] = val.astype(o_ref.dtype)

    in_specs = _row_specs(rows, tm, ncol) + [pl.BlockSpec(p.shape, lambda i, j: (0, 0)) for p in params]
    out_specs = [pl.BlockSpec((tm, w // ncol), lambda i, j: (i, j)) for w, _ in outs]
    out_shape = [jax.ShapeDtypeStruct((t, w), dt) for w, dt in outs]
    return pl.pallas_call(
        body, name=name, grid=(t // tm, ncol), in_specs=in_specs, out_specs=out_specs, out_shape=out_shape,
        compiler_params=_cparams(("parallel", "parallel")),
    )(*[r[0] for r in rows], *params)


def _rowwise_bwd(fn, rows, params, cts, grad_dtypes, *, tm, ncol=1, name, adds=None):
    t = rows[0][0].shape[0]
    tm = min(tm, t)
    nr, npar, nct = len(rows), len(params), len(cts)
    adds = adds or [None] * nr
    add_idx = [i for i, a in enumerate(adds) if a is not None]

    def body(*refs):
        ins = [r[...].astype(F32) for r in refs[:nr + npar]]
        ct = tuple(r[...].astype(F32) for r in refs[nr + npar:nr + npar + nct])
        add_refs = refs[nr + npar + nct:nr + npar + nct + len(add_idx)]
        outs = refs[nr + npar + nct + len(add_idx):]
        _, vjp = jax.vjp(lambda *a: tuple(fn(*a)), *ins)
        grads = vjp(ct)
        extra = dict(zip(add_idx, add_refs))
        for i in range(nr):
            g = grads[i]
            if i in extra:
                g = g + extra[i][...].astype(F32)
            outs[i][...] = g.astype(outs[i].dtype)
        first = jnp.logical_and(pl.program_id(0) == 0, pl.program_id(1) == 0)
        for pi in range(npar):
            o_ref = outs[nr + pi]
            g = grads[nr + pi]

            @pl.when(first)
            def _(o_ref=o_ref, g=g):
                o_ref[...] = g

            @pl.when(jnp.logical_not(first))
            def _(o_ref=o_ref, g=g):
                o_ref[...] += g

    in_specs = (_row_specs(rows, tm, ncol)
                + [pl.BlockSpec(p.shape, lambda i, j: (0, 0)) for p in params]
                + [pl.BlockSpec((tm, c.shape[1] // ncol), lambda i, j: (i, j)) for c in cts]
                + [pl.BlockSpec((tm, adds[i].shape[1] // ncol), lambda i, j: (i, j)) for i in add_idx])
    out_specs = ([pl.BlockSpec((tm, w // ncol), lambda i, j: (i, j)) for _, _, w in rows]
                 + [pl.BlockSpec(p.shape, lambda i, j: (0, 0)) for p in params])
    out_shape = ([jax.ShapeDtypeStruct((t, w), dt) for (_, _, w), dt in zip(rows, grad_dtypes)]
                 + [jax.ShapeDtypeStruct(p.shape, F32) for p in params])
    return pl.pallas_call(
        body, name=name, grid=(t // tm, ncol), in_specs=in_specs, out_specs=out_specs, out_shape=out_shape,
        compiler_params=_cparams(("arbitrary", "arbitrary")),
    )(*[r[0] for r in rows], *params, *cts, *[adds[i] for i in add_idx])


def _rms(x, w):
    return x * lax.rsqrt(jnp.mean(x * x, axis=-1, keepdims=True) + EPS) * w


def _fn_norm(x, w):
    return (_rms(x, w),)


def _fn_gates(z, a_row, b_row):
    lane = lax.broadcasted_iota(jnp.int32, z.shape, 1)
    beta = jax.nn.sigmoid(z)
    g = -jnp.exp(a_row) * jax.nn.softplus(z + b_row)
    logf = jax.nn.log_sigmoid(z + b_row)
    return (jnp.where(lane < HEADS, beta, jnp.where(lane < 2 * HEADS, g, jnp.where(lane < 3 * HEADS, logf, 0.0))),)


def _fn_qknorm(q, k, qw, kw):
    return _rms(q, qw), _rms(k, kw)


def _fn_gdn_out(o, z, w):
    return (_rms(o, w) * jax.nn.silu(z),)


def _fn_merge(ga, gb, ya, yb):
    return (jax.nn.sigmoid(ga) * ya + jax.nn.sigmoid(gb) * yb,)


def _fn_swiglu(g, u):
    return (jax.nn.silu(g) * u,)


def _loss_head(y, target, *, tm, name):
    t, d = y.shape
    tm = min(tm, t)

    def body(y_ref, t_ref, dyf_ref, dyb_ref, loss_ref):
        err = y_ref[...] - t_ref[...]
        dy = err * (1.0 / d)
        dyf_ref[...] = dy
        dyb_ref[...] = dy.astype(BF16)
        part = jnp.sum(err * err) * (0.5 / d)

        @pl.when(pl.program_id(0) == 0)
        def _():
            loss_ref[...] = jnp.zeros_like(loss_ref)

        loss_ref[...] += part

    blk = pl.BlockSpec((tm, d), lambda i: (i, 0))
    return pl.pallas_call(
        body, name=name, grid=(t // tm,), in_specs=[blk, blk],
        out_specs=[blk, blk, pl.BlockSpec((1, LANES), lambda i: (0, 0))],
        out_shape=[jax.ShapeDtypeStruct((t, d), F32), jax.ShapeDtypeStruct((t, d), BF16),
                   jax.ShapeDtypeStruct((1, LANES), F32)],
        compiler_params=_cparams(("arbitrary",)),
    )(y, target)


def _shift_down(x, s):
    if s == 0:
        return x
    row = lax.broadcasted_iota(jnp.int32, x.shape, 0)
    return jnp.where(row >= s, pltpu.roll(x, s, 0), 0.0)


def _shift_up(x, s):
    if s == 0:
        return x
    t = x.shape[0]
    row = lax.broadcasted_iota(jnp.int32, x.shape, 0)
    return jnp.where(row < t - s, pltpu.roll(x, t - s, 0), 0.0)


def _conv_pre(x, w):
    y = x * w[CONV_K - 1:CONV_K, :]
    for i in range(CONV_K - 1):
        y = y + _shift_down(x, CONV_K - 1 - i) * w[i:i + 1, :]
    return y


def _conv_fwd(p_main, conv_w, *, width, name):
    t = p_main.shape[0]
    tc = LANES

    def body(x_ref, w_ref, o_ref):
        y = _conv_pre(x_ref[...].astype(F32), w_ref[...])
        o_ref[...] = y * jax.nn.sigmoid(y)

    return pl.pallas_call(
        body, name=name, grid=(width // tc,),
        in_specs=[pl.BlockSpec((t, tc), lambda j: (0, j)), pl.BlockSpec((CONV_K, tc), lambda j: (0, j))],
        out_specs=pl.BlockSpec((t, tc), lambda j: (0, j)),
        out_shape=jax.ShapeDtypeStruct((t, width), F32),
        compiler_params=_cparams(("parallel",)),
    )(p_main, conv_w)


def _conv_bwd(p_main, conv_w, dy, *, width, name):
    t = p_main.shape[0]
    tc = LANES

    def body(x_ref, w_ref, dy_ref, dx_ref, dw_ref):
        x = x_ref[...].astype(F32)
        w = w_ref[...]
        pre = _conv_pre(x, w)
        sg = jax.nn.sigmoid(pre)
        dpre = dy_ref[...] * (sg * (1.0 + pre * (1.0 - sg)))
        dx = dpre * w[CONV_K - 1:CONV_K, :]
        dws = []
        for i in range(CONV_K - 1):
            s = CONV_K - 1 - i
            dx = dx + _shift_up(dpre, s) * w[i:i + 1, :]
            dws.append(jnp.sum(_shift_down(x, s) * dpre, axis=0, keepdims=True))
        dws.append(jnp.sum(x * dpre, axis=0, keepdims=True))
        dx_ref[...] = dx.astype(dx_ref.dtype)
        dw_ref[...] = jnp.concatenate(dws, axis=0)

    return pl.pallas_call(
        body, name=name, grid=(width // tc,),
        in_specs=[pl.BlockSpec((t, tc), lambda j: (0, j)), pl.BlockSpec((CONV_K, tc), lambda j: (0, j)),
                  pl.BlockSpec((t, tc), lambda j: (0, j))],
        out_specs=[pl.BlockSpec((t, tc), lambda j: (0, j)), pl.BlockSpec((CONV_K, tc), lambda j: (0, j))],
        out_shape=[jax.ShapeDtypeStruct((t, width), BF16), jax.ShapeDtypeStruct((CONV_K, width), F32)],
        compiler_params=_cparams(("parallel",)),
    )(p_main, conv_w, dy)


def _bmm(a, b, spec, precision=None):
    return jnp.einsum(spec, a, b, preferred_element_type=F32, precision=precision)


def _iota2(shape, dim):
    return lax.broadcasted_iota(jnp.int32, shape, dim)


def _tri_inverse(a):
    c = a.shape[-1]
    r, m = _iota2((c, c), 0), _iota2((c, c), 1)
    eye = (r == m).astype(F32)
    inv = None
    b = 1
    while b < c:
        mask = jnp.logical_and(r // (2 * b) == m // (2 * b), jnp.logical_and(r % (2 * b) >= b, m % (2 * b) < b))
        off = jnp.where(mask[None], a, 0.0)
        if inv is None:
            inv = eye[None] - off
        else:
            inv = inv - _bmm(_bmm(inv, off, "hij,hjk->hik", HI), inv, "hij,hjk->hik", HI)
        b *= 2
    return inv


def _gdn_chunk(s, q3, k3, v3, b3, gc3):
    c = q3.shape[1]
    r, m = _iota2((c, c), 0), _iota2((c, c), 1)
    tril_incl = (r >= m)[None]
    tril_strict = (r > m)[None]
    eye = (r == m).astype(F32)[None]
    qn = q3 * lax.rsqrt(jnp.sum(q3 * q3, axis=-1, keepdims=True) + EPS) * (DH ** -0.5)
    kn = k3 * lax.rsqrt(jnp.sum(k3 * k3, axis=-1, keepdims=True) + EPS)
    ones = jnp.ones((q3.shape[0], c, c), F32)
    gc_row = _bmm(ones, gc3 * eye, "hij,hjk->hik", HI)
    decay = jnp.where(tril_incl, jnp.exp(jnp.where(tril_incl, gc3 - gc_row, 0.0)), 0.0)
    a = jnp.where(tril_strict, _bmm(kn, kn, "hcd,hmd->hcm") * decay * b3, 0.0)
    tinv = _tri_inverse(a)
    egc = jnp.exp(gc3)
    u = _bmm(tinv, v3 * b3, "hij,hjk->hik", HI)
    w = _bmm(tinv, kn * (b3 * egc), "hij,hjk->hik", HI)
    qk = _bmm(qn, kn, "hcd,hmd->hcm") * decay
    v_new = u - _bmm(w, s, "hcd,hdv->hcv")
    o = _bmm(qn * egc, s, "hcd,hdv->hcv") + _bmm(qk, v_new, "hcm,hmv->hcv")
    row = _iota2((c, 1), 0)[None]
    g_last = jnp.sum(jnp.where(row == c - 1, gc3, 0.0), axis=1, keepdims=True)
    s_new = s * jnp.exp(g_last) + _bmm(kn * jnp.exp(g_last - gc3), v_new, "hcd,hcv->hdv")
    return s_new, o


GDN_HEAD_GROUP = 8


def _split_heads(ref, off, h0):
    return jnp.stack([ref[:, off + h * DH:off + (h + 1) * DH].astype(F32)
                      for h in range(h0, h0 + GDN_HEAD_GROUP)], axis=0)


def _store_heads(ref, x3, off, h0):
    for i in range(GDN_HEAD_GROUP):
        h = h0 + i
        ref[:, off + h * DH:off + (h + 1) * DH] = x3[i].astype(ref.dtype)


def _lane_cols(tile, lane0):
    lane = _iota2(tile.shape, 1)
    return jnp.stack([jnp.sum(jnp.where(lane == lane0 + i, tile, 0.0), axis=1, keepdims=True)
                      for i in range(GDN_HEAD_GROUP)], axis=0)


def _cols_to_lanes(cols3, lane0, shape):
    lane = _iota2(shape, 1)
    out = jnp.zeros(shape, F32)
    for i in range(GDN_HEAD_GROUP):
        out = out + jnp.where(lane == lane0 + i, cols3[i], 0.0)
    return out


def _chunk_cumsum_matrix():
    r, m = _iota2((CHUNK, CHUNK), 0), _iota2((CHUNK, CHUNK), 1)
    return (r >= m).astype(F32)


def _gdn_inputs(qkv_ref, gt, gcum, h0):
    return (_split_heads(qkv_ref, 0, h0), _split_heads(qkv_ref, HW, h0), _split_heads(qkv_ref, 2 * HW, h0),
            _lane_cols(gt, h0), _lane_cols(gcum, HEADS + h0))


def _gdn_fwd(qkv, gates, *, name, comm=None):
    t = qkv.shape[0]
    n = t // CHUNK

    def body(qkv_ref, gt_ref, o_ref, sall_ref, s_scr):
        @pl.when(pl.program_id(0) == 0)
        def _():
            s_scr[...] = jnp.zeros_like(s_scr)

        gt = gt_ref[...]
        gcum = jnp.dot(_chunk_cumsum_matrix(), gt, preferred_element_type=F32, precision=HI)
        for h0 in range(0, HEADS, GDN_HEAD_GROUP):
            grp = pl.ds(h0, GDN_HEAD_GROUP)
            s = s_scr[grp]
            sall_ref[0, grp] = s
            s_new, o3 = _gdn_chunk(s, *_gdn_inputs(qkv_ref, gt, gcum, h0))
            s_scr[grp] = s_new
            _store_heads(o_ref, o3, 0, h0)

    return _pcall(
        body, name=name, grid=(n,),
        in_specs=[pl.BlockSpec((CHUNK, 3 * HW), lambda i: (i, 0)), pl.BlockSpec((CHUNK, LANES), lambda i: (i, 0))],
        out_specs=[pl.BlockSpec((CHUNK, HW), lambda i: (i, 0)),
                   pl.BlockSpec((1, HEADS, DH, DH), lambda i: (i, 0, 0, 0))],
        out_shape=[jax.ShapeDtypeStruct((t, HW), F32), jax.ShapeDtypeStruct((n, HEADS, DH, DH), F32)],
        scratch_shapes=[pltpu.VMEM((HEADS, DH, DH), F32)], sem=("arbitrary",), args=(qkv, gates), comm=comm)


def _gdn_bwd(qkv, gates, s_all, do, *, name, comm=None):
    t = qkv.shape[0]
    n = t // CHUNK

    def body(qkv_ref, gt_ref, sall_ref, do_ref, dqkv_ref, dgt_ref, ds_scr):
        @pl.when(pl.program_id(0) == 0)
        def _():
            ds_scr[...] = jnp.zeros_like(ds_scr)

        gt = gt_ref[...]
        cum = _chunk_cumsum_matrix()
        gcum = jnp.dot(cum, gt, preferred_element_type=F32, precision=HI)
        shape = (CHUNK, LANES)
        dbeta = jnp.zeros(shape, F32)
        dgcum = jnp.zeros(shape, F32)
        for h0 in range(0, HEADS, GDN_HEAD_GROUP):
            grp = pl.ds(h0, GDN_HEAD_GROUP)
            _, vjp = jax.vjp(_gdn_chunk, sall_ref[0, grp], *_gdn_inputs(qkv_ref, gt, gcum, h0))
            ds, dq3, dk3, dv3, db3, dgc3 = vjp((ds_scr[grp], _split_heads(do_ref, 0, h0)))
            ds_scr[grp] = ds
            _store_heads(dqkv_ref, dq3, 0, h0)
            _store_heads(dqkv_ref, dk3, HW, h0)
            _store_heads(dqkv_ref, dv3, 2 * HW, h0)
            dbeta = dbeta + _cols_to_lanes(db3, h0, shape)
            dgcum = dgcum + _cols_to_lanes(dgc3, HEADS + h0, shape)
        dg = lax.dot_general(cum, dgcum, (((0,), (0,)), ((), ())), preferred_element_type=F32, precision=HI)
        dgt_ref[...] = dbeta + dg

    rev = lambda i: n - 1 - i
    return _pcall(
        body, name=name, grid=(n,),
        in_specs=[pl.BlockSpec((CHUNK, 3 * HW), lambda i: (rev(i), 0)), pl.BlockSpec((CHUNK, LANES), lambda i: (rev(i), 0)),
                  pl.BlockSpec((1, HEADS, DH, DH), lambda i: (rev(i), 0, 0, 0)),
                  pl.BlockSpec((CHUNK, HW), lambda i: (rev(i), 0))],
        out_specs=[pl.BlockSpec((CHUNK, 3 * HW), lambda i: (rev(i), 0)), pl.BlockSpec((CHUNK, LANES), lambda i: (rev(i), 0))],
        out_shape=[jax.ShapeDtypeStruct((t, 3 * HW), F32), jax.ShapeDtypeStruct((t, LANES), F32)],
        scratch_shapes=[pltpu.VMEM((HEADS, DH, DH), F32)], sem=("arbitrary",), args=(qkv, gates, s_all, do), comm=comm)


FOX_BLK = 512
NEG = -1e30


def _fox_cumsum(gates, *, name):
    t = gates.shape[0]
    blk = min(FOX_BLK, t)

    def body(g_ref, c_ref):
        r, m = _iota2((blk, blk), 0), _iota2((blk, blk), 1)
        upper = (r <= m).astype(F32)
        carry = jnp.zeros((HEADS, 1), F32)
        for b in range(t // blk):
            lf = g_ref[b * blk:(b + 1) * blk, :].T[2 * HEADS:3 * HEADS, :]
            c_ref[:, b * blk:(b + 1) * blk] = jnp.dot(lf, upper, preferred_element_type=F32, precision=HI) + carry
            carry = carry + jnp.sum(lf, axis=1, keepdims=True)

    return pl.pallas_call(body, name=name, out_shape=jax.ShapeDtypeStruct((HEADS, t), F32),
                          compiler_params=_cparams())(gates)


def _fox_cumsum_bwd(dc, dgates_gdn, *, name):
    t = dc.shape[1]
    blk = min(FOX_BLK, t)

    def body(dc_ref, dg_ref, o_ref):
        r, m = _iota2((blk, blk), 0), _iota2((blk, blk), 1)
        lower = (r >= m).astype(F32)
        carry = jnp.zeros((HEADS, 1), F32)
        for b in reversed(range(t // blk)):
            d = dc_ref[:, b * blk:(b + 1) * blk]
            dlf = jnp.dot(d, lower, preferred_element_type=F32, precision=HI) + carry
            carry = carry + jnp.sum(d, axis=1, keepdims=True)
            tile = jnp.concatenate([jnp.zeros((2 * HEADS, blk), F32), dlf,
                                    jnp.zeros((LANES - 3 * HEADS, blk), F32)], axis=0)
            o_ref[b * blk:(b + 1) * blk, :] = tile.T + dg_ref[b * blk:(b + 1) * blk, :]

    return pl.pallas_call(body, name=name, out_shape=jax.ShapeDtypeStruct((t, LANES), F32),
                          compiler_params=_cparams())(dc, dgates_gdn)


def _fox_logits(q, k, c_row, qi, kj, blk):
    s = lax.dot_general(q, k, (((1,), (1,)), ((), ())), preferred_element_type=F32) * (DH ** -0.5) - c_row
    rows = qi * blk + _iota2((blk, blk), 0)
    cols = kj * blk + _iota2((blk, blk), 1)
    return jnp.where(rows >= cols, s, NEG)


def _fox_fwd(qn, kn, p_main, c4, *, v_off, name, comm=None):
    t = qn.shape[0]
    blk = min(FOX_BLK, t)
    nb = t // blk
    vb = v_off // DH

    def body(q_ref, k_ref, v_ref, c_ref, o_ref, lse_ref):
        qi = pl.program_id(1)
        q = q_ref[...]

        def step(j, carry):
            m, l, acc = carry
            rows = pl.ds(pl.multiple_of(j * blk, blk), blk)
            s = _fox_logits(q, k_ref[rows, :], c_ref[0, j], qi, j, blk)
            m_new = jnp.maximum(m, jnp.max(s, axis=1, keepdims=True))
            p = jnp.exp(s - m_new)
            scale = jnp.exp(m - m_new)
            l = scale * l + jnp.sum(p, axis=1, keepdims=True)
            acc = scale * acc + jnp.dot(p.astype(BF16), v_ref[rows, :], preferred_element_type=F32)
            return m_new, l, acc

        init = (jnp.full((blk, 1), NEG, F32), jnp.zeros((blk, 1), F32), jnp.zeros((blk, DH), F32))
        m, l, acc = lax.fori_loop(0, qi + 1, step, init)
        o_ref[...] = (acc / l).astype(o_ref.dtype)
        lse_ref[0] = m + jnp.log(l)

    return _pcall(
        body, name=name, grid=(HEADS, nb),
        in_specs=[pl.BlockSpec((blk, DH), lambda h, i: (i, h)), pl.BlockSpec((t, DH), lambda h, i: (0, h)),
                  pl.BlockSpec((t, DH), lambda h, i: (0, vb + h)), pl.BlockSpec((1, nb, 1, blk), lambda h, i: (h, 0, 0, 0))],
        out_specs=[pl.BlockSpec((blk, DH), lambda h, i: (i, h)), pl.BlockSpec((1, blk, 1), lambda h, i: (h, i, 0))],
        out_shape=[jax.ShapeDtypeStruct((t, HW), BF16), jax.ShapeDtypeStruct((HEADS, t, 1), F32)],
        sem=("parallel", "arbitrary"), args=(qn, kn, p_main, c4), comm=comm)


def _fox_delta(qn, kn, p_main, c4, do, lse, *, v_off, name):
    t = qn.shape[0]
    blk = min(FOX_BLK, t)
    nb = t // blk
    vb = v_off // DH

    def body(q_ref, k_ref, v_ref, c_ref, do_ref, lse_ref, delta_ref):
        qi = pl.program_id(1)
        q = q_ref[...]
        dob = do_ref[...]
        lse = lse_ref[0]

        def step(j, delta):
            rows = pl.ds(pl.multiple_of(j * blk, blk), blk)
            p = jnp.exp(_fox_logits(q, k_ref[rows, :], c_ref[0, j], qi, j, blk) - lse)
            dp = lax.dot_general(dob, v_ref[rows, :], (((1,), (1,)), ((), ())), preferred_element_type=F32)
            return delta + jnp.sum(p * dp, axis=1, keepdims=True)

        delta_ref[0] = lax.fori_loop(0, qi + 1, step, jnp.zeros((blk, 1), F32))

    qblk = lambda h, i: (i, h)
    return pl.pallas_call(
        body, name=name, grid=(HEADS, nb),
        in_specs=[pl.BlockSpec((blk, DH), qblk), pl.BlockSpec((t, DH), lambda h, i: (0, h)),
                  pl.BlockSpec((t, DH), lambda h, i: (0, vb + h)), pl.BlockSpec((1, nb, 1, blk), lambda h, i: (h, 0, 0, 0)),
                  pl.BlockSpec((blk, DH), qblk), pl.BlockSpec((1, blk, 1), lambda h, i: (h, i, 0))],
        out_specs=pl.BlockSpec((1, blk, 1), lambda h, i: (h, i, 0)),
        out_shape=jax.ShapeDtypeStruct((HEADS, t, 1), F32),
        compiler_params=_cparams(("parallel", "arbitrary")),
    )(qn, kn, p_main, c4, do, lse)


def _fox_bwd(qn, kn, p_main, c4, delta, do, lse, *, v_off, name, comm=None):
    t = qn.shape[0]
    blk = min(FOX_BLK, t)
    nb = t // blk
    vb = v_off // DH
    tn_dims = (((0,), (0,)), ((), ()))
    nt_dims = (((1,), (1,)), ((), ()))

    def body(q_ref, k_ref, v_ref, c_ref, delta_ref, do_ref, lse_ref, dq_ref, dk_ref, dv_ref, dc_ref):
        kj = pl.program_id(1)

        @pl.when(kj == 0)
        def _():
            dq_ref[...] = jnp.zeros_like(dq_ref)

        k = k_ref[...]
        v = v_ref[...]
        c_row = c_ref[0, 0]

        def step(i, carry):
            dk, dv, dc = carry
            rows = pl.ds(pl.multiple_of(i * blk, blk), blk)
            q = q_ref[rows, :]
            dob = do_ref[rows, :]
            p = jnp.exp(_fox_logits(q, k, c_row, i, kj, blk) - lse_ref[0, rows, :])
            pb = p.astype(BF16)
            dv = dv + lax.dot_general(pb, dob, tn_dims, preferred_element_type=F32)
            dp = lax.dot_general(dob, v, nt_dims, preferred_element_type=F32)
            ds = p * (dp - delta_ref[0, rows, :])
            dsb = ds.astype(BF16)
            dq_ref[rows, :] += jnp.dot(dsb, k, preferred_element_type=F32) * (DH ** -0.5)
            dk = dk + lax.dot_general(dsb, q, tn_dims, preferred_element_type=F32) * (DH ** -0.5)
            dc = dc - jnp.sum(ds, axis=0, keepdims=True)
            return dk, dv, dc

        init = (jnp.zeros((blk, DH), F32), jnp.zeros((blk, DH), F32), jnp.zeros((1, blk), F32))
        dk, dv, dc = lax.fori_loop(kj, nb, step, init)
        dk_ref[...] = dk
        dv_ref[...] = dv.astype(dv_ref.dtype)
        dc_ref[0, 0] = dc

    full = lambda h, j: (0, h)
    kvb = lambda h, j: (j, h)
    return _pcall(
        body, name=name, grid=(HEADS, nb), sem=("parallel", "arbitrary"), comm=comm,
        args=(qn, kn, p_main, c4, delta, do, lse),
        in_specs=[pl.BlockSpec((t, DH), full), pl.BlockSpec((blk, DH), kvb),
                  pl.BlockSpec((blk, DH), lambda h, j: (j, vb + h)), pl.BlockSpec((1, 1, 1, blk), lambda h, j: (h, j, 0, 0)),
                  pl.BlockSpec((1, t, 1), lambda h, j: (h, 0, 0)), pl.BlockSpec((t, DH), full),
                  pl.BlockSpec((1, t, 1), lambda h, j: (h, 0, 0))],
        out_specs=[pl.BlockSpec((t, DH), full), pl.BlockSpec((blk, DH), kvb), pl.BlockSpec((blk, DH), kvb),
                   pl.BlockSpec((1, 1, 1, blk), lambda h, j: (h, j, 0, 0))],
        out_shape=[jax.ShapeDtypeStruct((t, HW), F32), jax.ShapeDtypeStruct((t, HW), F32),
                   jax.ShapeDtypeStruct((t, HW), BF16), jax.ShapeDtypeStruct((HEADS, nb, 1, blk), F32)])


ANY = pl.BlockSpec(memory_space=pl.ANY)


def _mesh_pos():
    return lax.axis_index("x"), lax.axis_index("y"), lax.axis_index("c")


def _all_gather(blocks, *, name):
    n = len(blocks)

    def body(*refs):
        ins, outs = refs[:n], refs[n:2 * n]
        send, recv, local = refs[2 * n:]
        x, y, c = _mesh_pos()
        me, sibling = (x, y, c), (x, y, 1 - c)
        chips = [(1 - x, y), (x, 1 - y), (1 - x, 1 - y)]

        def copy(t, k, block, to, src=None):
            dst = outs[t].at[4 * block[0] + 2 * block[1] + block[2]]
            return pltpu.make_async_remote_copy(
                src_ref=dst if src is None else src, dst_ref=dst, send_sem=send.at[7 * t + k],
                recv_sem=recv.at[7 * t + k], device_id=to, device_id_type=MESH)

        mine = [pltpu.make_async_copy(ins[t], outs[t].at[4 * x + 2 * y + c], local.at[t]) for t in range(n)]
        for cp in mine:
            cp.start()
        first = []
        for t in range(n):
            first.append(copy(t, 0, me, sibling, src=ins[t]))
            first += [copy(t, 1 + j, me, (*chip, c), src=ins[t]) for j, chip in enumerate(chips)]
        for cp in first:
            cp.start()
        passed = []
        for j, chip in enumerate(chips):
            for t in range(n):
                copy(t, 1 + j, (*chip, c), me).wait_recv()
                fwd = copy(t, 4 + j, (*chip, c), sibling)
                fwd.start()
                passed.append(fwd)
        for t in range(n):
            copy(t, 0, sibling, me).wait_recv()
            for j, chip in enumerate(chips):
                copy(t, 4 + j, (*chip, 1 - c), me).wait_recv()
        for cp in first + passed:
            cp.wait_send()
        for cp in mine:
            cp.wait()

    return pl.pallas_call(
        body, name=name, in_specs=[ANY] * n, out_specs=[ANY] * n,
        out_shape=[jax.ShapeDtypeStruct((N_DEV,) + b.shape, b.dtype) for b in blocks],
        scratch_shapes=[pltpu.SemaphoreType.DMA((7 * n,)), pltpu.SemaphoreType.DMA((7 * n,)),
                        pltpu.SemaphoreType.DMA((n,))],
    )(*blocks)


def _comm_call(comm, *, name):
    ci, co = len(comm.ins), len(comm.out_shapes)

    def body(*refs):
        comm.start(refs[:ci], refs[ci:ci + co], refs[ci + co:])
        comm.finish(refs[:ci], refs[ci:ci + co], refs[ci + co:])

    return pl.pallas_call(body, name=name, in_specs=[ANY] * ci, out_specs=[ANY] * co, out_shape=comm.out_shapes,
                          scratch_shapes=comm.sems, input_output_aliases=comm.aliases)(*comm.ins)


def _ag_first_comm(shards):
    n = len(shards)

    def copies(cin, cout, sems):
        send, recv, local = sems
        x, y, c = _mesh_pos()
        peers = [(x, y, 1 - c), (1 - x, y, c), (x, 1 - y, c), (1 - x, 1 - y, c)]
        slot = lambda p: 4 * p[0] + 2 * p[1] + p[2]
        mine, out, inc = [], [], []
        for t in range(n):
            mine.append(pltpu.make_async_copy(cin[t], cout[t].at[slot((x, y, c))], local.at[t]))
            for k, peer in enumerate(peers):
                sems_k = dict(send_sem=send.at[4 * t + k], recv_sem=recv.at[4 * t + k], device_id=peer,
                              device_id_type=MESH)
                out.append(pltpu.make_async_remote_copy(src_ref=cin[t], dst_ref=cout[t].at[slot((x, y, c))], **sems_k))
                inc.append(pltpu.make_async_remote_copy(src_ref=cout[t].at[slot(peer)], dst_ref=cout[t].at[slot(peer)],
                                                        **sems_k))
        return mine, out, inc

    def start(cin, cout, sems):
        mine, out, _ = copies(cin, cout, sems)
        for cp in mine + out:
            cp.start()

    def finish(cin, cout, sems):
        mine, out, inc = copies(cin, cout, sems)
        for cp in inc:
            cp.wait_recv()
        for cp in out:
            cp.wait_send()
        for cp in mine:
            cp.wait()

    return _Comm(shards, [jax.ShapeDtypeStruct((N_DEV,) + s.shape, s.dtype) for s in shards],
                 [pltpu.SemaphoreType.DMA((4 * n,)), pltpu.SemaphoreType.DMA((4 * n,)), pltpu.SemaphoreType.DMA((n,))],
                 start, finish)


def _ag_forward(gathered, *, name):
    n = len(gathered)

    def body(*refs):
        outs = refs[n:2 * n]
        send, recv = refs[2 * n:]
        x, y, c = _mesh_pos()
        chips = [(1 - x, y), (x, 1 - y), (1 - x, 1 - y)]
        fwd, inc = [], []
        for t in range(n):
            for j, (px, py) in enumerate(chips):
                sems_j = dict(send_sem=send.at[3 * t + j], recv_sem=recv.at[3 * t + j], device_id=(x, y, 1 - c),
                              device_id_type=MESH)
                mine, theirs = outs[t].at[4 * px + 2 * py + c], outs[t].at[4 * px + 2 * py + 1 - c]
                fwd.append(pltpu.make_async_remote_copy(src_ref=mine, dst_ref=mine, **sems_j))
                inc.append(pltpu.make_async_remote_copy(src_ref=theirs, dst_ref=theirs, **sems_j))
        for cp in fwd:
            cp.start()
        for cp in inc:
            cp.wait_recv()
        for cp in fwd:
            cp.wait_send()

    return pl.pallas_call(
        body, name=name, in_specs=[ANY] * n, out_specs=[ANY] * n,
        out_shape=[jax.ShapeDtypeStruct(g.shape, g.dtype) for g in gathered],
        scratch_shapes=[pltpu.SemaphoreType.DMA((3 * n,)), pltpu.SemaphoreType.DMA((3 * n,))],
        input_output_aliases={t: t for t in range(n)},
    )(*gathered)


def _rs_sibling(grads, *, name):
    n = len(grads)

    def body(*refs):
        ins, outs = refs[:n], refs[n:2 * n]
        send, recv = refs[2 * n:]
        x, y, c = _mesh_pos()
        copies = []
        for t in range(n):
            for q in range(4):
                copies.append(pltpu.make_async_remote_copy(
                    src_ref=ins[t].at[2 * q + (1 - c)], dst_ref=outs[t].at[q], send_sem=send.at[4 * t + q],
                    recv_sem=recv.at[4 * t + q], device_id=(x, y, 1 - c), device_id_type=MESH))
        for cp in copies:
            cp.start()
        for cp in copies:
            cp.wait_recv()
        for cp in copies:
            cp.wait_send()

    return pl.pallas_call(
        body, name=name, in_specs=[ANY] * n, out_specs=[ANY] * n,
        out_shape=[jax.ShapeDtypeStruct((4,) + g.shape[1:], g.dtype) for g in grads],
        scratch_shapes=[pltpu.SemaphoreType.DMA((4 * n,)), pltpu.SemaphoreType.DMA((4 * n,))],
    )(*grads)


def _rs_chips_comm(parts):
    n = len(parts)

    def copies(cin, cout, sems):
        send, recv, local = sems
        x, y, c = _mesh_pos()
        my_chip = 2 * x + y
        mine = [pltpu.make_async_copy(cin[t].at[my_chip], cout[t].at[my_chip], local.at[t]) for t in range(n)]
        sends, lands = [], []
        for t in range(n):
            for k, (px, py) in enumerate([(1 - x, y), (x, 1 - y), (1 - x, 1 - y)]):
                sems_k = dict(send_sem=send.at[3 * t + k], recv_sem=recv.at[3 * t + k], device_id=(px, py, c),
                              device_id_type=MESH)
                sends.append(pltpu.make_async_remote_copy(src_ref=cin[t].at[2 * px + py], dst_ref=cout[t].at[my_chip],
                                                          **sems_k))
                lands.append(pltpu.make_async_remote_copy(src_ref=cout[t].at[2 * px + py],
                                                          dst_ref=cout[t].at[2 * px + py], **sems_k))
        return mine, sends, lands

    def start(cin, cout, sems):
        mine, sends, _ = copies(cin, cout, sems)
        for cp in mine + sends:
            cp.start()

    def finish(cin, cout, sems):
        mine, sends, lands = copies(cin, cout, sems)
        for cp in lands:
            cp.wait_recv()
        for cp in sends:
            cp.wait_send()
        for cp in mine:
            cp.wait()

    return _Comm(parts, [jax.ShapeDtypeStruct(p.shape, p.dtype) for p in parts],
                 [pltpu.SemaphoreType.DMA((3 * n,)), pltpu.SemaphoreType.DMA((3 * n,)), pltpu.SemaphoreType.DMA((n,))],
                 start, finish)


def _row_tile(r, c, itemsize, budget=3 * 1024 * 1024):
    best = None
    for tr in range(16, r + 1, 16):
        if r % tr == 0 and tr * c * itemsize <= budget:
            best = tr
    return best or r


def _pair_sum(grad, land, *, name):
    _, r, c = grad.shape
    tr = _row_tile(r, c, 2)

    def body(g_ref, l_ref, o_ref):
        o_ref[...] = (g_ref[...].astype(F32) + l_ref[...].astype(F32)).astype(o_ref.dtype)

    return pl.pallas_call(
        body, name=name, grid=(4, r // tr),
        in_specs=[pl.BlockSpec((1, tr, c), lambda q, i: (2 * q + lax.axis_index("c"), i, 0)),
                  pl.BlockSpec((1, tr, c), lambda q, i: (q, i, 0))],
        out_specs=pl.BlockSpec((1, tr, c), lambda q, i: (q, i, 0)),
        out_shape=jax.ShapeDtypeStruct((4, r, c), grad.dtype),
        compiler_params=_cparams(("parallel", "parallel")),
    )(grad, land)


def _adamw_math(w, g, m, v):
    m = ADAM_B1 * m + (1.0 - ADAM_B1) * g
    v = ADAM_B2 * v + (1.0 - ADAM_B2) * jnp.square(g)
    m_hat = m / (1.0 - ADAM_B1 ** ADAM_STEP)
    v_hat = v / (1.0 - ADAM_B2 ** ADAM_STEP)
    delta = -ADAM_LR * (m_hat / (jnp.sqrt(v_hat) + ADAM_EPS) + ADAM_WD * w)
    return delta, m, v


def _adamw(parts, w, m, v, *, name):
    s, _, cp = parts.shape
    r, c = w.shape
    tr = _row_tile(r, cp, 4, budget=1024 * 1024)

    def body(p_ref, w_ref, m_ref, v_ref, g_ref, d_ref, nm_ref, nv_ref):
        g = p_ref[0].astype(F32)
        for i in range(1, s):
            g = g + p_ref[i].astype(F32)
        g = g[:, :c]
        delta, nm, nv = _adamw_math(w_ref[...], g, m_ref[...], v_ref[...])
        g_ref[...] = g
        d_ref[...] = delta
        nm_ref[...] = nm
        nv_ref[...] = nv

    blk = pl.BlockSpec((tr, c), lambda i: (i, 0))
    return pl.pallas_call(
        body, name=name, grid=(r // tr,),
        in_specs=[pl.BlockSpec((s, tr, cp), lambda i: (0, i, 0)), blk, blk, blk],
        out_specs=[blk] * 4, out_shape=[jax.ShapeDtypeStruct((r, c), F32)] * 4,
        compiler_params=_cparams(("parallel",)),
    )(parts, w, m, v)


def _w_in_pieces(d, nb, sources):
    segs = [(0, 4 * HW, False, 0), (4 * HW, 4 * HW + 2 * HEADS, True, 0),
            (4 * HW + 2 * HEADS, 7 * HW + 2 * HEADS, False, 4 * HW),
            (7 * HW + 2 * HEADS, 7 * HW + 3 * HEADS, True, 2 * HEADS),
            (7 * HW + 3 * HEADS, 7 * HW + 3 * HEADS + 2 * d, False, 7 * HW)]
    out = []
    for dev in range(N_DEV):
        lo, hi = dev * nb, (dev + 1) * nb
        for s0, s1, is_small, a0 in segs:
            p, q = max(lo, s0), min(hi, s1)
            if p >= q:
                continue
            a, b = a0 + p - s0, a0 + q - s0
            if is_small:
                out.append((dev, p - lo, q - lo, len(sources), a, b))
                continue
            for si, (start, width) in enumerate(sources):
                u, v = max(a, start), min(b, start + width)
                if u < v:
                    out.append((dev, p - lo + (u - a), p - lo + (v - a), si, u - start, v - start))
    return out


def _w_in_to_aligned(g_in, *, name):
    _, d, nb = g_in.shape
    n_main = 7 * HW + 2 * d
    tr = min(128, d)
    pieces = _w_in_pieces(d, nb, [(0, n_main)])

    def body(g_ref, main_ref, small_ref):
        small_ref[...] = jnp.zeros_like(small_ref)
        for dev, s, e, src, a, b in pieces:
            dst = main_ref if src == 0 else small_ref
            dst[:, a:b] = g_ref[dev, :, s:e]

    return pl.pallas_call(
        body, name=name, grid=(d // tr,), in_specs=[pl.BlockSpec((N_DEV, tr, nb), lambda i: (0, i, 0))],
        out_specs=[pl.BlockSpec((tr, n_main), lambda i: (i, 0)), pl.BlockSpec((tr, LANES), lambda i: (i, 0))],
        out_shape=[jax.ShapeDtypeStruct((d, n_main), g_in.dtype), jax.ShapeDtypeStruct((d, LANES), g_in.dtype)],
        compiler_params=_cparams(("parallel",)),
    )(g_in)


def _w_in_grad_blocks(seg_grads, small_grad, sources, nb, *, name):
    d = small_grad.shape[0]
    tr = min(128, d)
    pieces = _w_in_pieces(d, nb, sources)
    ns = len(seg_grads)

    def body(*refs):
        o_ref = refs[ns + 1]
        for dev, s, e, src, a, b in pieces:
            o_ref[dev, :, s:e] = refs[src][:, a:b]

    return pl.pallas_call(
        body, name=name, grid=(d // tr,),
        in_specs=[pl.BlockSpec((tr, g.shape[1]), lambda i: (i, 0)) for g in seg_grads + [small_grad]],
        out_specs=pl.BlockSpec((N_DEV, tr, nb), lambda i: (0, i, 0)),
        out_shape=jax.ShapeDtypeStruct((N_DEV, d, nb), small_grad.dtype),
        compiler_params=_cparams(("parallel",)),
    )(*seg_grads, small_grad)


def _pad_cols(a, n):
    return a if a.shape[1] == n else jnp.concatenate([a, jnp.zeros((a.shape[0], n - a.shape[1]), a.dtype)], axis=1)


def _pad_rows(a, n):
    return a if a.shape[0] == n else jnp.concatenate([a, jnp.zeros((n - a.shape[0], a.shape[1]), a.dtype)], axis=0)


class _StaticPlan:
    def __init__(self, weights, cp):
        self.w, self.cp, self.grads = weights, cp, {}

    def comm_for(self, key):
        return None

    def done(self, key, res):
        pass

    def weight(self, name):
        return self.w[name]

    def grad(self, name, g):
        self.grads[name] = g


class _FsdpPlan:
    AG_RIDES = {"in_proj": ("wa", "wb", "wout", "wg"), "gdn_fwd": ("wu",), "fox_fwd": ("wd",)}
    RS_RIDES = ("d_hn_gate", "gdn_bwd")

    def __init__(self, shards, d, cp):
        self.shards, self.d, self.cp = shards, d, cp
        self.first, self.full = {}, {}
        self.queue, self.flying, self.slots = [], [], {}

    def comm_for(self, key):
        if key in self.AG_RIDES:
            return _ag_first_comm([self.shards[n] for n in self.AG_RIDES[key]])
        if key in self.RS_RIDES and self.queue:
            self.flying, self.queue = self.queue, []
            return _rs_chips_comm([p for _, p in self.flying])
        return None

    def done(self, key, res):
        if key in self.AG_RIDES:
            self.first[key] = list(res)
        else:
            self.slots.update((n, s) for (n, _), s in zip(self.flying, res))
            self.flying = []

    def weight(self, name):
        if name not in self.full:
            key = next(k for k, names in self.AG_RIDES.items() if name in names)
            outs = _ag_forward(self.first[key], name=f"all_gather_forward_{key}")
            self.full.update(zip(self.AG_RIDES[key], outs))
        g = self.full[name]
        if name in ("wa", "wb"):
            return _cols_of_blocks(g)
        if name == "wout":
            return g.reshape(self.d, self.d)
        if name == "wd":
            return g.reshape(N_DEV * self.cp, self.d)
        return g

    def grad(self, name, g):
        if name == "wout":
            g = g.reshape(N_DEV, self.d // N_DEV, self.d)
        if name == "wd":
            g = g.reshape(N_DEV, self.cp, self.d)
        self.reduce(name, g)

    def reduce(self, name, blocks):
        (land,) = _rs_sibling([blocks], name=f"grads_to_sibling_{name}")
        self.queue.append((name, _pair_sum(blocks, land, name=f"pair_sum_{name}")))

    def flush(self):
        if self.queue:
            outs = _comm_call(_rs_chips_comm([p for _, p in self.queue]), name="grads_to_chips_tail")
            self.slots.update((n, s) for (n, _), s in zip(self.queue, outs))
            self.queue = []


def _carried(plan, key, fn, *args, **kw):
    comm = plan.comm_for(key)
    if comm is None:
        return fn(*args, **kw)
    res, comm_res = fn(*args, comm=comm, **kw)
    plan.done(key, comm_res)
    return res


def _local_step(x, target, w_main, w_small, conv_w, plan,
                norm_mix_w, norm_ffn_w, gdn_norm_w, fox_q_w, fox_k_w, a_row, b_row):
    t, d = x.shape
    cp = plan.cp
    fp = N_DEV * cp
    n_main = w_main.shape[1]
    off_gb = OFF_GA + d
    tm = 1024
    rt = 128
    fcol = fp // 1024 if fp % 1024 == 0 else max(fp // 512, 1)

    (xn,) = _rowwise_fwd(_fn_norm, [(x, 0, d)], [norm_mix_w], [(d, BF16)], tm=rt, name="mix_norm")
    p_main = _carried(plan, "in_proj", _mm, xn, w_main, mode="nn", m=t, n=n_main, k=d, tm=tm, tn=512, tk=d,
                      out_dtype=BF16, name="in_proj")
    p_small = _mm(xn, w_small, mode="nn", m=t, n=LANES, k=d, tm=tm, tn=LANES, tk=d, out_dtype=F32, name="in_proj_small")
    (gates,) = _rowwise_fwd(_fn_gates, [(p_small, 0, LANES)], [a_row, b_row], [(LANES, F32)], tm=512, name="gates")
    qkv = _conv_fwd(p_main, conv_w, width=3 * HW, name="conv_fwd")
    o_gdn, s_all = _carried(plan, "gdn_fwd", _gdn_fwd, qkv, gates, name="gdn_fwd")
    gdn_rows = [(o_gdn, 0, HW), (p_main, OFF_ZA, HW)]
    (oa,) = _rowwise_fwd(_fn_gdn_out, gdn_rows, [gdn_norm_w], [(HW, BF16)], tm=512, ncol=HEADS, name="gdn_out")
    wa = plan.weight("wa")
    ya = _mm(oa, wa, mode="nn", m=t, n=d, k=HW, tm=tm, tn=1024, tk=HW, out_dtype=BF16, name="branch_a")
    qk_rows = [(p_main, OFF_QB, HW), (p_main, OFF_KB, HW)]
    qn, kn = _rowwise_fwd(_fn_qknorm, qk_rows, [fox_q_w, fox_k_w], [(HW, BF16), (HW, BF16)], tm=512, ncol=HEADS,
                          name="fox_qk_norm")
    blk = min(FOX_BLK, t)
    c4 = _fox_cumsum(gates, name="fox_cumsum").reshape(HEADS, t // blk, 1, blk)
    ob, lse = _carried(plan, "fox_fwd", _fox_fwd, qn, kn, p_main, c4, v_off=OFF_VB, name="fox_fwd")
    wb = plan.weight("wb")
    yb = _mm(ob, wb, mode="nn", m=t, n=d, k=HW, tm=tm, tn=1024, tk=HW, out_dtype=BF16, name="branch_b")
    mcol = 2 if d >= 2 * HW else 1
    merge_rows = [(p_main, OFF_GA, d), (p_main, off_gb, d), (ya, 0, d), (yb, 0, d)]
    (merged,) = _rowwise_fwd(_fn_merge, merge_rows, [], [(d, BF16)], tm=256, ncol=mcol, name="merge")
    wout = plan.weight("wout")
    h = _mm(merged, wout, mode="nn", m=t, n=d, k=d, tm=tm, tn=512, tk=d, out_dtype=F32, add=x, name="out_proj")
    (hn,) = _rowwise_fwd(_fn_norm, [(h, 0, d)], [norm_ffn_w], [(d, BF16)], tm=rt, name="ffn_norm")
    wg, wu = plan.weight("wg"), plan.weight("wu")
    gate = _mm(hn, wg, mode="nn", m=t, n=fp, k=d, tm=512, tn=cp, tk=d, out_dtype=BF16, b_blocked=True, name="ffn_gate")
    up = _mm(hn, wu, mode="nn", m=t, n=fp, k=d, tm=512, tn=cp, tk=d, out_dtype=BF16, b_blocked=True, name="ffn_up")
    (act,) = _rowwise_fwd(_fn_swiglu, [(gate, 0, fp), (up, 0, fp)], [], [(fp, BF16)], tm=512, ncol=fcol, name="swiglu")
    wd = plan.weight("wd")
    y = _mm(act, wd, mode="nn", m=t, n=d, k=fp, tm=512, tn=256, tk=fp, out_dtype=F32, add=h, name="ffn_down")
    dy, dyb, loss_row = _loss_head(y, target, tm=rt, name="loss_head")

    gw = {}
    dact = _mm(dyb, wd, mode="nt", m=t, n=fp, k=d, tm=tm, tn=512, tk=d, out_dtype=BF16, name="d_act")
    plan.grad("wd", _mm(act, dyb, mode="tn", m=fp, n=d, k=t, tm=512, tn=1024, tk=t, out_dtype=BF16, name="dw_ffn_down"))
    dgate, dup = _rowwise_bwd(_fn_swiglu, [(gate, 0, fp), (up, 0, fp)], [], [dact], [BF16, BF16], tm=512, ncol=fcol,
                              name="d_swiglu")
    dhn = _carried(plan, "d_hn_gate", _mm, dgate, wg, mode="nt", m=t, n=d, k=fp, tm=tm, tn=1024, tk=cp, out_dtype=F32,
                   b_blocked=True, name="d_hn_gate")
    dhn = _mm(dup, wu, mode="nt", m=t, n=d, k=fp, tm=tm, tn=1024, tk=cp, out_dtype=F32, add=dhn, b_blocked=True,
              name="d_hn_up")
    plan.grad("wg", _mm(hn, dgate, mode="tn", m=d, n=fp, k=t, tm=512, tn=cp, tk=t, out_dtype=BF16, out_blocked=True,
                        name="dw_ffn_gate"))
    plan.grad("wu", _mm(hn, dup, mode="tn", m=d, n=fp, k=t, tm=512, tn=cp, tk=t, out_dtype=BF16, out_blocked=True,
                        name="dw_ffn_up"))
    dh, d_norm_ffn = _rowwise_bwd(_fn_norm, [(h, 0, d)], [norm_ffn_w], [dhn], [F32], tm=rt, name="d_ffn_norm", adds=[dy])
    dmerged = _mm(dh, wout, mode="nt", m=t, n=d, k=d, tm=512, tn=512, tk=d, out_dtype=BF16, name="d_merged")
    plan.grad("wout", _mm(merged, dh, mode="tn", m=d, n=d, k=t, tm=512, tn=512, tk=t, out_dtype=BF16, name="dw_out"))
    dga, dgb, dya, dyb2 = _rowwise_bwd(_fn_merge, merge_rows, [], [dmerged], [BF16] * 4, tm=256, ncol=mcol, name="d_merge")
    doa = _mm(dya, wa, mode="nt", m=t, n=HW, k=d, tm=tm, tn=512, tk=d, out_dtype=BF16, name="d_oa")
    plan.grad("wa", _mm(oa, dya, mode="tn", m=HW, n=d, k=t, tm=1024, tn=d // N_DEV, tk=t, out_dtype=BF16,
                        out_blocked=True, name="dw_branch_a"))
    dob = _mm(dyb2, wb, mode="nt", m=t, n=HW, k=d, tm=tm, tn=512, tk=d, out_dtype=BF16, name="d_ob")
    plan.grad("wb", _mm(ob, dyb2, mode="tn", m=HW, n=d, k=t, tm=1024, tn=d // N_DEV, tk=t, out_dtype=BF16,
                        out_blocked=True, name="dw_branch_b"))
    do_gdn, dza, d_gdn_norm = _rowwise_bwd(_fn_gdn_out, gdn_rows, [gdn_norm_w], [doa], [F32, BF16], tm=512,
                                           ncol=HEADS, name="d_gdn_out")
    dqkv, dgates_gdn = _carried(plan, "gdn_bwd", _gdn_bwd, qkv, gates, s_all, do_gdn, name="gdn_bwd")
    dp_qkv, gw["conv"] = _conv_bwd(p_main, conv_w, dqkv, width=3 * HW, name="conv_bwd")
    delta = _fox_delta(qn, kn, p_main, c4, dob, lse, v_off=OFF_VB, name="fox_delta")
    dqn, dkn, dvb, dc4 = _fox_bwd(qn, kn, p_main, c4, delta, dob, lse, v_off=OFF_VB, name="fox_bwd")
    dqb, dkb, d_fox_q, d_fox_k = _rowwise_bwd(_fn_qknorm, qk_rows, [fox_q_w, fox_k_w], [dqn, dkn], [BF16, BF16],
                                              tm=512, ncol=HEADS, name="d_fox_qk_norm")
    dgates = _fox_cumsum_bwd(dc4.reshape(HEADS, t), dgates_gdn, name="fox_cumsum_bwd")
    dsmall, d_a_row, d_b_row = _rowwise_bwd(_fn_gates, [(p_small, 0, LANES)], [a_row, b_row], [dgates], [F32],
                                            tm=512, name="d_gates")
    segs = [(dp_qkv, 0, 3 * HW), (dza, OFF_ZA, HW), (dqb, OFF_QB, HW), (dkb, OFF_KB, HW), (dvb, OFF_VB, HW),
            (dga, OFF_GA, d), (dgb, off_gb, d)]
    dxn = _mm(dsmall, w_small, mode="nt", m=t, n=d, k=LANES, tm=tm, tn=1024, tk=LANES, out_dtype=F32, name="d_xn_small")
    gw_main = []
    for i, (dp, off, width) in enumerate(segs):
        dxn = _mm(dp, w_main, mode="nt", m=t, n=d, k=width, tm=512, tn=512, tk=math.gcd(width, off) if off else width,
                  out_dtype=F32, add=dxn,
                  b_off=(0, off), name=f"d_xn_{i}")
        gw_main.append(_mm(xn, dp, mode="tn", m=d, n=width, k=t, tm=1024, tn=1024, tk=t, out_dtype=BF16,
                           name=f"dw_in_{i}"))
    gw["w_main"] = gw_main
    gw["w_main_sources"] = [(off, width) for _, off, width in segs]
    gw["w_small"] = _mm(xn, dsmall, mode="tn", m=d, n=LANES, k=t, tm=1024, tn=LANES, tk=t, out_dtype=BF16,
                        name="dw_in_small")
    grad_x, d_norm_mix = _rowwise_bwd(_fn_norm, [(x, 0, d)], [norm_mix_w], [dxn], [F32], tm=rt, name="d_mix_norm",
                                      adds=[dh])
    small = dict(norm_mix=d_norm_mix, norm_ffn=d_norm_ffn, gdn_norm=d_gdn_norm, fox_q=d_fox_q, fox_k=d_fox_k,
                 a_row=d_a_row, b_row=d_b_row)
    return loss_row[0, 0], grad_x, gw, small


def _lane_row(pieces):
    row = jnp.zeros((1, LANES), F32)
    for off, p in pieces:
        row = lax.dynamic_update_slice(row, p.astype(F32), (0, off))
    return row


def _pack_small(norm_mix, norm_ffn, gdn_norm, fox_q, fox_k, a_log, dt_bias, b_f):
    rows = [norm_mix.reshape(-1, LANES), norm_ffn.reshape(-1, LANES), gdn_norm, fox_q, fox_k,
            _lane_row([(HEADS, a_log)]), _lane_row([(HEADS, dt_bias), (2 * HEADS, b_f)])]
    packed = jnp.concatenate(rows, axis=0)
    return _pad_rows(packed, -(-packed.shape[0] // 8) * 8)


def _unpack_small(p, d):
    nd = d // LANES
    r = 2 * nd
    return (p[0:nd].reshape(1, d), p[r + 3:r + 4, HEADS:2 * HEADS], p[r + 4:r + 5, HEADS:2 * HEADS], p[r:r + 1],
            p[r + 4:r + 5, 2 * HEADS:3 * HEADS], p[r + 1:r + 2], p[r + 2:r + 3], p[nd:r].reshape(1, d))


def _blocks_of_cols(a):
    r, c8 = a.shape
    return a.reshape(r, N_DEV, c8 // N_DEV).transpose(1, 0, 2)


def _cols_of_blocks(g):
    _, r, c = g.shape
    return g.transpose(1, 0, 2).reshape(r, N_DEV * c)


def kernel(x, norm_mix_w, w_in, conv_w, a_log, dt_bias, gdn_norm_w, fox_b_f, fox_q_norm_w, fox_k_norm_w, w_branch_a, w_branch_b, w_out, norm_ffn_w, w_ffn_gate, w_ffn_up, w_ffn_down, loss_target, m_norm_mix_w, m_w_in, m_conv_w, m_a_log, m_dt_bias, m_gdn_norm_w, m_fox_b_f, m_fox_q_norm_w, m_fox_k_norm_w, m_w_branch_a, m_w_branch_b, m_w_out, m_norm_ffn_w, m_w_ffn_gate, m_w_ffn_up, m_w_ffn_down, v_norm_mix_w, v_w_in, v_conv_w, v_a_log, v_dt_bias, v_gdn_norm_w, v_fox_b_f, v_fox_q_norm_w, v_fox_k_norm_w, v_w_branch_a, v_w_branch_b, v_w_out, v_norm_ffn_w, v_w_ffn_gate, v_w_ffn_up, v_w_ffn_down):
    d = x.shape[-1]
    cp = -(-w_ffn_down.shape[1] // LANES) * LANES
    nb = w_in.shape[2]

    g_in, g_conv = _all_gather([w_in[0].astype(BF16), conv_w[0]], name="w_in_all_gather")
    w_main, w_small = _w_in_to_aligned(g_in, name="w_in_to_aligned")
    conv_full = _cols_of_blocks(g_conv)
    plan = _FsdpPlan(dict(wa=w_branch_a[0].astype(BF16), wb=w_branch_b[0].astype(BF16), wout=w_out[0].astype(BF16),
                          wg=_pad_cols(w_ffn_gate[0].astype(BF16), cp), wu=_pad_cols(w_ffn_up[0].astype(BF16), cp),
                          wd=_pad_rows(w_ffn_down[0].astype(BF16), cp)), d, cp)
    a_row = _lane_row([(HEADS, a_log)])
    b_row = _lane_row([(HEADS, dt_bias), (2 * HEADS, fox_b_f)])

    loss_part, grad_x, gw, gs = _local_step(
        x[0], loss_target[0], w_main, w_small, conv_full, plan,
        norm_mix_w, norm_ffn_w, gdn_norm_w, fox_q_norm_w, fox_k_norm_w, a_row, b_row)
    loss = lax.psum(loss_part, ("x", "y", "c"))

    plan.reduce("w_in", _w_in_grad_blocks(gw["w_main"], gw["w_small"], gw["w_main_sources"], nb, name="w_in_grad_blocks"))
    plan.reduce("conv", _blocks_of_cols(gw["conv"].astype(BF16)))
    plan.flush()
    big = dict(w_in=("w_in", w_in, m_w_in, v_w_in), w_branch_a=("wa", w_branch_a, m_w_branch_a, v_w_branch_a),
               w_branch_b=("wb", w_branch_b, m_w_branch_b, v_w_branch_b), w_out=("wout", w_out, m_w_out, v_w_out),
               w_ffn_gate=("wg", w_ffn_gate, m_w_ffn_gate, v_w_ffn_gate), w_ffn_up=("wu", w_ffn_up, m_w_ffn_up, v_w_ffn_up),
               w_ffn_down=("wd", w_ffn_down, m_w_ffn_down, v_w_ffn_down), conv_w=("conv", conv_w, m_conv_w, v_conv_w))
    res = {}
    for nm, (key, w, m, v) in big.items():
        res[nm] = [o[None] for o in _adamw(plan.slots[key], w[0], m[0], v[0], name=f"adamw_{nm}")]

    g_small = _pack_small(gs["norm_mix"], gs["norm_ffn"], gs["gdn_norm"], gs["fox_q"], gs["fox_k"],
                          gs["a_row"][:, HEADS:2 * HEADS], gs["b_row"][:, HEADS:2 * HEADS],
                          gs["b_row"][:, 2 * HEADS:3 * HEADS])
    (g_small_all,) = _all_gather([g_small], name="small_grads_all_gather")
    w_small_p = _pack_small(norm_mix_w, norm_ffn_w, gdn_norm_w, fox_q_norm_w, fox_k_norm_w, a_log, dt_bias, fox_b_f)
    m_small_p = _pack_small(m_norm_mix_w, m_norm_ffn_w, m_gdn_norm_w, m_fox_q_norm_w, m_fox_k_norm_w, m_a_log,
                            m_dt_bias, m_fox_b_f)
    v_small_p = _pack_small(v_norm_mix_w, v_norm_ffn_w, v_gdn_norm_w, v_fox_q_norm_w, v_fox_k_norm_w, v_a_log,
                            v_dt_bias, v_fox_b_f)
    small_res = [_unpack_small(o, d) for o in _adamw(g_small_all, w_small_p, m_small_p, v_small_p, name="adamw_small")]

    def group(k):
        s = small_res[k]
        return [s[0], res["w_in"][k], res["conv_w"][k], s[1], s[2], s[3], s[4], s[5], s[6], res["w_branch_a"][k],
                res["w_branch_b"][k], res["w_out"][k], s[7], res["w_ffn_gate"][k], res["w_ffn_up"][k],
                res["w_ffn_down"][k]]

    return (loss, grad_x[None], *group(0), *group(1), *group(2), *group(3))
```

```python
import functools
import math

import jax
import jax.numpy as jnp
from jax import lax
from jax.experimental import pallas as pl
from jax.experimental.pallas import tpu as pltpu

F32 = jnp.float32
BF16 = jnp.bfloat16
HI = lax.Precision.HIGHEST
SOLVE_PRECISION = lax.Precision.HIGH
MESH = pl.DeviceIdType.MESH

EPS = 1e-6
HEADS = 16
DH = 128
HW = HEADS * DH
CHUNK = 64
CONV_K = 4
N_DEV = 8
LANES = 128
VMEM_LIMIT = 52 * 1024 * 1024

ADAM_LR = 0.001
ADAM_B1 = 0.9
ADAM_B2 = 0.999
ADAM_EPS = 1e-08
ADAM_WD = 0.01
ADAM_STEP = 10

OFF_QA, OFF_KA, OFF_VA, OFF_ZA, OFF_QB, OFF_KB, OFF_VB, OFF_GA = 0, HW, 2 * HW, 3 * HW, 4 * HW, 5 * HW, 6 * HW, 7 * HW


def _cparams(sem=None, vmem=VMEM_LIMIT):
    return pltpu.CompilerParams(dimension_semantics=sem, vmem_limit_bytes=vmem)


class _Comm:
    def __init__(self, ins, out_shapes, sems, start, finish, aliases=None):
        self.ins, self.out_shapes, self.sems = list(ins), list(out_shapes), list(sems)
        self.start, self.finish, self.aliases = start, finish, dict(aliases or {})


def _pcall(body, *, name, grid, in_specs, out_specs, out_shape, args, sem, scratch_shapes=(), comm=None):
    multi = isinstance(out_shape, (list, tuple))
    out_specs = list(out_specs) if multi else [out_specs]
    out_shape = list(out_shape) if multi else [out_shape]
    scratch_shapes = list(scratch_shapes)
    if comm is None:
        res = pl.pallas_call(body, name=name, grid=grid, in_specs=list(in_specs), out_specs=out_specs,
                             out_shape=out_shape, scratch_shapes=scratch_shapes, compiler_params=_cparams(sem))(*args)
        return res if multi else res[0]
    ni, no, ns = len(in_specs), len(out_specs), len(scratch_shapes)
    ci, co = len(comm.ins), len(comm.out_shapes)

    def wrapped(*refs):
        cin = refs[ni:ni + ci]
        outs = refs[ni + ci:ni + ci + no]
        cout = refs[ni + ci + no:ni + ci + no + co]
        scr = refs[ni + ci + no + co:ni + ci + no + co + ns]
        csem = refs[ni + ci + no + co + ns:]
        ids = [pl.program_id(ax) for ax in range(len(grid))]
        first = functools.reduce(jnp.logical_and, [i == 0 for i in ids])
        last = functools.reduce(jnp.logical_and, [i == g - 1 for i, g in zip(ids, grid)])

        @pl.when(first)
        def _():
            comm.start(cin, cout, csem)

        body(*refs[:ni], *outs, *scr)

        @pl.when(last)
        def _():
            comm.finish(cin, cout, csem)

    any_spec = pl.BlockSpec(memory_space=pl.ANY)
    res = pl.pallas_call(
        wrapped, name=name, grid=grid, in_specs=list(in_specs) + [any_spec] * ci,
        out_specs=out_specs + [any_spec] * co, out_shape=out_shape + comm.out_shapes,
        scratch_shapes=scratch_shapes + comm.sems,
        input_output_aliases={ni + i: no + o for i, o in comm.aliases.items()},
        compiler_params=_cparams(("arbitrary",) * len(grid)))(*args, *comm.ins)
    return (res[:no] if multi else res[0]), res[no:]


def _mm(a, b, *, mode, m, n, k, tm, tn, tk, out_dtype, name, a_off=(0, 0), b_off=(0, 0), add=None,
        b_blocked=False, out_blocked=False, comm=None):
    tm, tn, tk = min(tm, m), min(tn, n), min(tk, k)
    assert m % tm == 0 and n % tn == 0 and k % tk == 0, (name, m, n, k, tm, tn, tk)
    nk = k // tk
    if mode == "nn":
        a_blk, b_blk = (tm, tk), (tk, tn)
        ao, bo = (a_off[0] // tm, a_off[1] // tk), (b_off[0] // tk, b_off[1] // tn)
        a_map = lambda i, j, kk: (i + ao[0], kk + ao[1])
        b_map = lambda i, j, kk: (kk + bo[0], j + bo[1])
        dims = (((1,), (0,)), ((), ()))
        if b_blocked:
            assert b.shape == (n // tn, k, tn) and b_off == (0, 0), (name, b.shape)
            b_blk, b_map = (None, tk, tn), lambda i, j, kk: (j, kk, 0)
    elif mode == "nt":
        a_blk, b_blk = (tm, tk), (tn, tk)
        ao, bo = (a_off[0] // tm, a_off[1] // tk), (b_off[0] // tn, b_off[1] // tk)
        a_map = lambda i, j, kk: (i + ao[0], kk + ao[1])
        b_map = lambda i, j, kk: (j + bo[0], kk + bo[1])
        dims = (((1,), (1,)), ((), ()))
        if b_blocked:
            assert b.shape == (nk, n, tk) and b_off == (0, 0), (name, b.shape)
            b_blk, b_map = (None, tn, tk), lambda i, j, kk: (kk, j, 0)
    else:
        assert not b_blocked
        a_blk, b_blk = (tk, tm), (tk, tn)
        ao, bo = (a_off[0] // tk, a_off[1] // tm), (b_off[0] // tk, b_off[1] // tn)
        a_map = lambda i, j, kk: (kk + ao[0], i + ao[1])
        b_map = lambda i, j, kk: (kk + bo[0], j + bo[1])
        dims = (((0,), (0,)), ((), ()))
    if not b_blocked:
        for off, blk in ((a_off, a_blk), (b_off, b_blk)):
            assert off[0] % blk[0] == 0 and off[1] % blk[1] == 0, (name, off, blk)
    has_add = add is not None

    def body(*refs):
        if has_add:
            a_ref, b_ref, c_ref, o_ref, acc = refs
        else:
            a_ref, b_ref, o_ref, acc = refs
            c_ref = None
        p = lax.dot_general(a_ref[...].astype(BF16), b_ref[...].astype(BF16), dims, preferred_element_type=F32)
        if nk == 1:
            if has_add:
                p = p + c_ref[...].astype(F32)
            o_ref[...] = p.astype(o_ref.dtype)
        else:
            kk = pl.program_id(2)

            @pl.when(kk == 0)
            def _():
                acc[...] = p + c_ref[...].astype(F32) if has_add else p

            @pl.when(kk > 0)
            def _():
                acc[...] += p

            @pl.when(kk == nk - 1)
            def _():
                o_ref[...] = acc[...].astype(o_ref.dtype)

    in_specs = [pl.BlockSpec(a_blk, a_map), pl.BlockSpec(b_blk, b_map)]
    args = [a, b]
    if has_add:
        in_specs.append(pl.BlockSpec((tm, tn), lambda i, j, kk: (i, j)))
        args.append(add)
    acc_shape = (tm, tn) if nk > 1 else (8, LANES)
    if out_blocked:
        out_spec = pl.BlockSpec((None, tm, tn), lambda i, j, kk: (j, i, 0))
        out_shape = jax.ShapeDtypeStruct((n // tn, m, tn), out_dtype)
    else:
        out_spec = pl.BlockSpec((tm, tn), lambda i, j, kk: (i, j))
        out_shape = jax.ShapeDtypeStruct((m, n), out_dtype)
    return _pcall(body, name=name, grid=(m // tm, n // tn, nk), in_specs=in_specs, out_specs=out_spec,
                  out_shape=out_shape, scratch_shapes=[pltpu.VMEM(acc_shape, F32)], args=args,
                  sem=("parallel", "parallel", "arbitrary"), comm=comm)


def _row_specs(rows, tm, ncol):
    specs = []
    for arr, off, width in rows:
        bw = width // ncol
        assert width % ncol == 0 and off % bw == 0, (off, width, ncol)
        ob = off // bw
        specs.append(pl.BlockSpec((tm, bw), lambda i, j, ob=ob: (i, j + ob)))
    return specs


def _rowwise_fwd(fn, rows, params, outs, *, tm, ncol=1, name):
    t = rows[0][0].shape[0]
    tm = min(tm, t)
    nr, npar = len(rows), len(params)

    def body(*refs):
        ins = [r[...].astype(F32) for r in refs[:nr + npar]]
        res = fn(*ins)
        for o_ref, val in zip(refs[nr + npar:], res):
            o_ref[...] = val.astype(o_ref.dtype)

    in_specs = _row_specs(rows, tm, ncol) + [pl.BlockSpec(p.shape, lambda i, j: (0, 0)) for p in params]
    out_specs = [pl.BlockSpec((tm, w // ncol), lambda i, j: (i, j)) for w, _ in outs]
    out_shape = [jax.ShapeDtypeStruct((t, w), dt) for w, dt in outs]
    return pl.pallas_call(
        body, name=name, grid=(t // tm, ncol), in_specs=in_specs, out_specs=out_specs, out_shape=out_shape,
        compiler_params=_cparams(("parallel", "parallel")),
    )(*[r[0] for r in rows], *params)


def _rowwise_bwd(fn, rows, params, cts, grad_dtypes, *, tm, ncol=1, name, adds=None):
    t = rows[0][0].shape[0]
    tm = min(tm, t)
    nr, npar, nct = len(rows), len(params), len(cts)
    adds = adds or [None] * nr
    add_idx = [i for i, a in enumerate(adds) if a is not None]

    def body(*refs):
        ins = [r[...].astype(F32) for r in refs[:nr + npar]]
        ct = tuple(r[...].astype(F32) for r in refs[nr + npar:nr + npar + nct])
        add_refs = refs[nr + npar + nct:nr + npar + nct + len(add_idx)]
        outs = refs[nr + npar + nct + len(add_idx):]
        _, vjp = jax.vjp(lambda *a: tuple(fn(*a)), *ins)
        grads = vjp(ct)
        extra = dict(zip(add_idx, add_refs))
        for i in range(nr):
            g = grads[i]
            if i in extra:
                g = g + extra[i][...].astype(F32)
            outs[i][...] = g.astype(outs[i].dtype)
        first = jnp.logical_and(pl.program_id(0) == 0, pl.program_id(1) == 0)
        for pi in range(npar):
            o_ref = outs[nr + pi]
            g = grads[nr + pi]

            @pl.when(first)
            def _(o_ref=o_ref, g=g):
                o_ref[...] = g

            @pl.when(jnp.logical_not(first))
            def _(o_ref=o_ref, g=g):
                o_ref[...] += g

    in_specs = (_row_specs(rows, tm, ncol)
                + [pl.BlockSpec(p.shape, lambda i, j: (0, 0)) for p in params]
                + [pl.BlockSpec((tm, c.shape[1] // ncol), lambda i, j: (i, j)) for c in cts]
                + [pl.BlockSpec((tm, adds[i].shape[1] // ncol), lambda i, j: (i, j)) for i in add_idx])
    out_specs = ([pl.BlockSpec((tm, w // ncol), lambda i, j: (i, j)) for _, _, w in rows]
                 + [pl.BlockSpec(p.shape, lambda i, j: (0, 0)) for p in params])
    out_shape = ([jax.ShapeDtypeStruct((t, w), dt) for (_, _, w), dt in zip(rows, grad_dtypes)]
                 + [jax.ShapeDtypeStruct(p.shape, F32) for p in params])
    return pl.pallas_call(
        body, name=name, grid=(t // tm, ncol), in_specs=in_specs, out_specs=out_specs, out_shape=out_shape,
        compiler_params=_cparams(("arbitrary", "arbitrary")),
    )(*[r[0] for r in rows], *params, *cts, *[adds[i] for i in add_idx])


def _rms(x, w):
    return x * lax.rsqrt(jnp.mean(x * x, axis=-1, keepdims=True) + EPS) * w


def _fn_norm(x, w):
    return (_rms(x, w),)


def _fn_gates(z, a_row, b_row):
    lane = lax.broadcasted_iota(jnp.int32, z.shape, 1)
    beta = jax.nn.sigmoid(z)
    g = -jnp.exp(a_row) * jax.nn.softplus(z + b_row)
    logf = jax.nn.log_sigmoid(z + b_row)
    return (jnp.where(lane < HEADS, beta, jnp.where(lane < 2 * HEADS, g, jnp.where(lane < 3 * HEADS, logf, 0.0))),)


def _fn_qknorm(q, k, qw, kw):
    return _rms(q, qw), _rms(k, kw)


def _fn_gdn_out(o, z, w):
    return (_rms(o, w) * jax.nn.silu(z),)


def _fn_merge(ga, gb, ya, yb):
    return (jax.nn.sigmoid(ga) * ya + jax.nn.sigmoid(gb) * yb,)


def _fn_swiglu(g, u):
    return (jax.nn.silu(g) * u,)


def _loss_head(y, target, *, tm, name):
    t, d = y.shape
    tm = min(tm, t)

    def body(y_ref, t_ref, dyf_ref, dyb_ref, loss_ref):
        err = y_ref[...] - t_ref[...]
        dy = err * (1.0 / d)
        dyf_ref[...] = dy
        dyb_ref[...] = dy.astype(BF16)
        part = jnp.sum(err * err) * (0.5 / d)

        @pl.when(pl.program_id(0) == 0)
        def _():
            loss_ref[...] = jnp.zeros_like(loss_ref)

        loss_ref[...] += part

    blk = pl.BlockSpec((tm, d), lambda i: (i, 0))
    return pl.pallas_call(
        body, name=name, grid=(t // tm,), in_specs=[blk, blk],
        out_specs=[blk, blk, pl.BlockSpec((1, LANES), lambda i: (0, 0))],
        out_shape=[jax.ShapeDtypeStruct((t, d), F32), jax.ShapeDtypeStruct((t, d), BF16),
                   jax.ShapeDtypeStruct((1, LANES), F32)],
        compiler_params=_cparams(("arbitrary",)),
    )(y, target)


def _shift_down(x, s):
    if s == 0:
        return x
    row = lax.broadcasted_iota(jnp.int32, x.shape, 0)
    return jnp.where(row >= s, pltpu.roll(x, s, 0), 0.0)


def _shift_up(x, s):
    if s == 0:
        return x
    t = x.shape[0]
    row = lax.broadcasted_iota(jnp.int32, x.shape, 0)
    return jnp.where(row < t - s, pltpu.roll(x, t - s, 0), 0.0)


def _conv_pre(x, w):
    y = x * w[CONV_K - 1:CONV_K, :]
    for i in range(CONV_K - 1):
        y = y + _shift_down(x, CONV_K - 1 - i) * w[i:i + 1, :]
    return y


def _conv_fwd(p_main, conv_w, *, width, name):
    t = p_main.shape[0]
    tc = LANES

    def body(x_ref, w_ref, o_ref):
        y = _conv_pre(x_ref[...].astype(F32), w_ref[...])
        o_ref[...] = y * jax.nn.sigmoid(y)

    return pl.pallas_call(
        body, name=name, grid=(width // tc,),
        in_specs=[pl.BlockSpec((t, tc), lambda j: (0, j)), pl.BlockSpec((CONV_K, tc), lambda j: (0, j))],
        out_specs=pl.BlockSpec((t, tc), lambda j: (0, j)),
        out_shape=jax.ShapeDtypeStruct((t, width), F32),
        compiler_params=_cparams(("parallel",)),
    )(p_main, conv_w)


def _conv_bwd(p_main, conv_w, dy, *, width, name):
    t = p_main.shape[0]
    tc = LANES

    def body(x_ref, w_ref, dy_ref, dx_ref, dw_ref):
        x = x_ref[...].astype(F32)
        w = w_ref[...]
        pre = _conv_pre(x, w)
        sg = jax.nn.sigmoid(pre)
        dpre = dy_ref[...] * (sg * (1.0 + pre * (1.0 - sg)))
        dx = dpre * w[CONV_K - 1:CONV_K, :]
        dws = []
        for i in range(CONV_K - 1):
            s = CONV_K - 1 - i
            dx = dx + _shift_up(dpre, s) * w[i:i + 1, :]
            dws.append(jnp.sum(_shift_down(x, s) * dpre, axis=0, keepdims=True))
        dws.append(jnp.sum(x * dpre, axis=0, keepdims=True))
        dx_ref[...] = dx.astype(dx_ref.dtype)
        dw_ref[...] = jnp.concatenate(dws, axis=0)

    return pl.pallas_call(
        body, name=name, grid=(width // tc,),
        in_specs=[pl.BlockSpec((t, tc), lambda j: (0, j)), pl.BlockSpec((CONV_K, tc), lambda j: (0, j)),
                  pl.BlockSpec((t, tc), lambda j: (0, j))],
        out_specs=[pl.BlockSpec((t, tc), lambda j: (0, j)), pl.BlockSpec((CONV_K, tc), lambda j: (0, j))],
        out_shape=[jax.ShapeDtypeStruct((t, width), BF16), jax.ShapeDtypeStruct((CONV_K, width), F32)],
        compiler_params=_cparams(("parallel",)),
    )(p_main, conv_w, dy)


def _bmm(a, b, spec, precision=None):
    return jnp.einsum(spec, a, b, preferred_element_type=F32, precision=precision)


def _iota2(shape, dim):
    return lax.broadcasted_iota(jnp.int32, shape, dim)


def _tri_inverse(a):
    c = a.shape[-1]
    r, m = _iota2((c, c), 0), _iota2((c, c), 1)
    eye = (r == m).astype(F32)
    inv = None
    b = 1
    while b < c:
        mask = jnp.logical_and(r // (2 * b) == m // (2 * b), jnp.logical_and(r % (2 * b) >= b, m % (2 * b) < b))
        off = jnp.where(mask[None], a, 0.0)
        if inv is None:
            inv = eye[None] - off
        else:
            inv = inv - _bmm(_bmm(inv, off, "hij,hjk->hik", SOLVE_PRECISION), inv, "hij,hjk->hik", SOLVE_PRECISION)
        b *= 2
    return inv


def _gdn_chunk(s, q3, k3, v3, b3, gc3):
    c = q3.shape[1]
    r, m = _iota2((c, c), 0), _iota2((c, c), 1)
    tril_incl = (r >= m)[None]
    tril_strict = (r > m)[None]
    eye = (r == m).astype(F32)[None]
    qn = q3 * lax.rsqrt(jnp.sum(q3 * q3, axis=-1, keepdims=True) + EPS) * (DH ** -0.5)
    kn = k3 * lax.rsqrt(jnp.sum(k3 * k3, axis=-1, keepdims=True) + EPS)
    ones = jnp.ones((q3.shape[0], c, c), F32)
    gc_row = _bmm(ones, gc3 * eye, "hij,hjk->hik", SOLVE_PRECISION)
    decay = jnp.where(tril_incl, jnp.exp(jnp.where(tril_incl, gc3 - gc_row, 0.0)), 0.0)
    a = jnp.where(tril_strict, _bmm(kn, kn, "hcd,hmd->hcm") * decay * b3, 0.0)
    tinv = _tri_inverse(a)
    egc = jnp.exp(gc3)
    u = _bmm(tinv, v3 * b3, "hij,hjk->hik", SOLVE_PRECISION)
    w = _bmm(tinv, kn * (b3 * egc), "hij,hjk->hik", SOLVE_PRECISION)
    qk = _bmm(qn, kn, "hcd,hmd->hcm") * decay
    v_new = u - _bmm(w, s, "hcd,hdv->hcv")
    o = _bmm(qn * egc, s, "hcd,hdv->hcv") + _bmm(qk, v_new, "hcm,hmv->hcv")
    row = _iota2((c, 1), 0)[None]
    g_last = jnp.sum(jnp.where(row == c - 1, gc3, 0.0), axis=1, keepdims=True)
    s_new = s * jnp.exp(g_last) + _bmm(kn * jnp.exp(g_last - gc3), v_new, "hcd,hcv->hdv")
    return s_new, o


GDN_HEAD_GROUP = 8


def _split_heads(ref, off, h0):
    return jnp.stack([ref[:, off + h * DH:off + (h + 1) * DH].astype(F32)
                      for h in range(h0, h0 + GDN_HEAD_GROUP)], axis=0)


def _store_heads(ref, x3, off, h0):
    for i in range(GDN_HEAD_GROUP):
        h = h0 + i
        ref[:, off + h * DH:off + (h + 1) * DH] = x3[i].astype(ref.dtype)


def _lane_cols(tile, lane0):
    lane = _iota2(tile.shape, 1)
    return jnp.stack([jnp.sum(jnp.where(lane == lane0 + i, tile, 0.0), axis=1, keepdims=True)
                      for i in range(GDN_HEAD_GROUP)], axis=0)


def _cols_to_lanes(cols3, lane0, shape):
    lane = _iota2(shape, 1)
    out = jnp.zeros(shape, F32)
    for i in range(GDN_HEAD_GROUP):
        out = out + jnp.where(lane == lane0 + i, cols3[i], 0.0)
    return out


def _chunk_cumsum_matrix():
    r, m = _iota2((CHUNK, CHUNK), 0), _iota2((CHUNK, CHUNK), 1)
    return (r >= m).astype(F32)


def _gdn_inputs(qkv_ref, gt, gcum, h0):
    return (_split_heads(qkv_ref, 0, h0), _split_heads(qkv_ref, HW, h0), _split_heads(qkv_ref, 2 * HW, h0),
            _lane_cols(gt, h0), _lane_cols(gcum, HEADS + h0))


def _gdn_fwd(qkv, gates, *, name, comm=None):
    t = qkv.shape[0]
    n = t // CHUNK

    def body(qkv_ref, gt_ref, o_ref, sall_ref, s_scr):
        @pl.when(pl.program_id(0) == 0)
        def _():
            s_scr[...] = jnp.zeros_like(s_scr)

        gt = gt_ref[...]
        gcum = jnp.dot(_chunk_cumsum_matrix(), gt, preferred_element_type=F32, precision=HI)
        for h0 in range(0, HEADS, GDN_HEAD_GROUP):
            grp = pl.ds(h0, GDN_HEAD_GROUP)
            s = s_scr[grp]
            sall_ref[0, grp] = s
            s_new, o3 = _gdn_chunk(s, *_gdn_inputs(qkv_ref, gt, gcum, h0))
            s_scr[grp] = s_new
            _store_heads(o_ref, o3, 0, h0)

    return _pcall(
        body, name=name, grid=(n,),
        in_specs=[pl.BlockSpec((CHUNK, 3 * HW), lambda i: (i, 0)), pl.BlockSpec((CHUNK, LANES), lambda i: (i, 0))],
        out_specs=[pl.BlockSpec((CHUNK, HW), lambda i: (i, 0)),
                   pl.BlockSpec((1, HEADS, DH, DH), lambda i: (i, 0, 0, 0))],
        out_shape=[jax.ShapeDtypeStruct((t, HW), F32), jax.ShapeDtypeStruct((n, HEADS, DH, DH), F32)],
        scratch_shapes=[pltpu.VMEM((HEADS, DH, DH), F32)], sem=("arbitrary",), args=(qkv, gates), comm=comm)


def _gdn_bwd(qkv, gates, s_all, do, *, name, comm=None):
    t = qkv.shape[0]
    n = t // CHUNK

    def body(qkv_ref, gt_ref, sall_ref, do_ref, dqkv_ref, dgt_ref, ds_scr):
        @pl.when(pl.program_id(0) == 0)
        def _():
            ds_scr[...] = jnp.zeros_like(ds_scr)

        gt = gt_ref[...]
        cum = _chunk_cumsum_matrix()
        gcum = jnp.dot(cum, gt, preferred_element_type=F32, precision=HI)
        shape = (CHUNK, LANES)
        dbeta = jnp.zeros(shape, F32)
        dgcum = jnp.zeros(shape, F32)
        for h0 in range(0, HEADS, GDN_HEAD_GROUP):
            grp = pl.ds(h0, GDN_HEAD_GROUP)
            _, vjp = jax.vjp(_gdn_chunk, sall_ref[0, grp], *_gdn_inputs(qkv_ref, gt, gcum, h0))
            ds, dq3, dk3, dv3, db3, dgc3 = vjp((ds_scr[grp], _split_heads(do_ref, 0, h0)))
            ds_scr[grp] = ds
            _store_heads(dqkv_ref, dq3, 0, h0)
            _store_heads(dqkv_ref, dk3, HW, h0)
            _store_heads(dqkv_ref, dv3, 2 * HW, h0)
            dbeta = dbeta + _cols_to_lanes(db3, h0, shape)
            dgcum = dgcum + _cols_to_lanes(dgc3, HEADS + h0, shape)
        dg = lax.dot_general(cum, dgcum, (((0,), (0,)), ((), ())), preferred_element_type=F32, precision=HI)
        dgt_ref[...] = dbeta + dg

    rev = lambda i: n - 1 - i
    return _pcall(
        body, name=name, grid=(n,),
        in_specs=[pl.BlockSpec((CHUNK, 3 * HW), lambda i: (rev(i), 0)), pl.BlockSpec((CHUNK, LANES), lambda i: (rev(i), 0)),
                  pl.BlockSpec((1, HEADS, DH, DH), lambda i: (rev(i), 0, 0, 0)),
                  pl.BlockSpec((CHUNK, HW), lambda i: (rev(i), 0))],
        out_specs=[pl.BlockSpec((CHUNK, 3 * HW), lambda i: (rev(i), 0)), pl.BlockSpec((CHUNK, LANES), lambda i: (rev(i), 0))],
        out_shape=[jax.ShapeDtypeStruct((t, 3 * HW), F32), jax.ShapeDtypeStruct((t, LANES), F32)],
        scratch_shapes=[pltpu.VMEM((HEADS, DH, DH), F32)], sem=("arbitrary",), args=(qkv, gates, s_all, do), comm=comm)


FOX_BLK = 512
NEG = -1e30


def _fox_cumsum(gates, *, name):
    t = gates.shape[0]
    blk = min(FOX_BLK, t)

    def body(g_ref, c_ref):
        r, m = _iota2((blk, blk), 0), _iota2((blk, blk), 1)
        upper = (r <= m).astype(F32)
        carry = jnp.zeros((HEADS, 1), F32)
        for b in range(t // blk):
            lf = g_ref[b * blk:(b + 1) * blk, :].T[2 * HEADS:3 * HEADS, :]
            c_ref[:, b * blk:(b + 1) * blk] = jnp.dot(lf, upper, preferred_element_type=F32, precision=HI) + carry
            carry = carry + jnp.sum(lf, axis=1, keepdims=True)

    return pl.pallas_call(body, name=name, out_shape=jax.ShapeDtypeStruct((HEADS, t), F32),
                          compiler_params=_cparams())(gates)


def _fox_cumsum_bwd(dc, dgates_gdn, *, name):
    t = dc.shape[1]
    blk = min(FOX_BLK, t)

    def body(dc_ref, dg_ref, o_ref):
        r, m = _iota2((blk, blk), 0), _iota2((blk, blk), 1)
        lower = (r >= m).astype(F32)
        carry = jnp.zeros((HEADS, 1), F32)
        for b in reversed(range(t // blk)):
            d = dc_ref[:, b * blk:(b + 1) * blk]
            dlf = jnp.dot(d, lower, preferred_element_type=F32, precision=HI) + carry
            carry = carry + jnp.sum(d, axis=1, keepdims=True)
            tile = jnp.concatenate([jnp.zeros((2 * HEADS, blk), F32), dlf,
                                    jnp.zeros((LANES - 3 * HEADS, blk), F32)], axis=0)
            o_ref[b * blk:(b + 1) * blk, :] = tile.T + dg_ref[b * blk:(b + 1) * blk, :]

    return pl.pallas_call(body, name=name, out_shape=jax.ShapeDtypeStruct((t, LANES), F32),
                          compiler_params=_cparams())(dc, dgates_gdn)


def _fox_logits(q, k, c_row, diagonal):
    s = lax.dot_general(q, k, (((1,), (1,)), ((), ())), preferred_element_type=F32) * (DH ** -0.5) - c_row
    if not diagonal:
        return s
    return jnp.where(_iota2(s.shape, 0) >= _iota2(s.shape, 1), s, NEG)


def _fox_fwd(qn, kn, p_main, c4, *, v_off, name, comm=None):
    t = qn.shape[0]
    blk = min(FOX_BLK, t)
    nb = t // blk
    vb = v_off // DH

    def body(q_ref, k_ref, v_ref, c_ref, o_ref, lse_ref):
        qi = pl.program_id(1)
        q = q_ref[...]

        def step(j, carry, diagonal=False):
            m, l, acc = carry
            rows = pl.ds(pl.multiple_of(j * blk, blk), blk)
            s = _fox_logits(q, k_ref[rows, :], c_ref[0, j], diagonal)
            m_new = jnp.maximum(m, jnp.max(s, axis=1, keepdims=True))
            p = jnp.exp(s - m_new)
            scale = jnp.exp(m - m_new)
            l = scale * l + jnp.sum(p, axis=1, keepdims=True)
            acc = scale * acc + jnp.dot(p.astype(BF16), v_ref[rows, :], preferred_element_type=F32)
            return m_new, l, acc

        init = (jnp.full((blk, 1), NEG, F32), jnp.zeros((blk, 1), F32), jnp.zeros((blk, DH), F32))
        m, l, acc = step(qi, lax.fori_loop(0, qi, step, init), diagonal=True)
        o_ref[...] = (acc / l).astype(o_ref.dtype)
        lse_ref[0] = m + jnp.log(l)

    return _pcall(
        body, name=name, grid=(HEADS, nb),
        in_specs=[pl.BlockSpec((blk, DH), lambda h, i: (i, h)), pl.BlockSpec((t, DH), lambda h, i: (0, h)),
                  pl.BlockSpec((t, DH), lambda h, i: (0, vb + h)), pl.BlockSpec((1, nb, 1, blk), lambda h, i: (h, 0, 0, 0))],
        out_specs=[pl.BlockSpec((blk, DH), lambda h, i: (i, h)), pl.BlockSpec((1, blk, 1), lambda h, i: (h, i, 0))],
        out_shape=[jax.ShapeDtypeStruct((t, HW), BF16), jax.ShapeDtypeStruct((HEADS, t, 1), F32)],
        sem=("parallel", "arbitrary"), args=(qn, kn, p_main, c4), comm=comm)


def _fox_delta(qn, kn, p_main, c4, do, lse, *, v_off, name):
    t = qn.shape[0]
    blk = min(FOX_BLK, t)
    nb = t // blk
    vb = v_off // DH

    def body(q_ref, k_ref, v_ref, c_ref, do_ref, lse_ref, delta_ref):
        qi = pl.program_id(1)
        q = q_ref[...]
        dob = do_ref[...]
        lse = lse_ref[0]

        def step(j, delta, diagonal=False):
            rows = pl.ds(pl.multiple_of(j * blk, blk), blk)
            p = jnp.exp(_fox_logits(q, k_ref[rows, :], c_ref[0, j], diagonal) - lse)
            dp = lax.dot_general(dob, v_ref[rows, :], (((1,), (1,)), ((), ())), preferred_element_type=F32)
            return delta + jnp.sum(p * dp, axis=1, keepdims=True)

        delta_ref[0] = step(qi, lax.fori_loop(0, qi, step, jnp.zeros((blk, 1), F32)), diagonal=True)

    qblk = lambda h, i: (i, h)
    return pl.pallas_call(
        body, name=name, grid=(HEADS, nb),
        in_specs=[pl.BlockSpec((blk, DH), qblk), pl.BlockSpec((t, DH), lambda h, i: (0, h)),
                  pl.BlockSpec((t, DH), lambda h, i: (0, vb + h)), pl.BlockSpec((1, nb, 1, blk), lambda h, i: (h, 0, 0, 0)),
                  pl.BlockSpec((blk, DH), qblk), pl.BlockSpec((1, blk, 1), lambda h, i: (h, i, 0))],
        out_specs=pl.BlockSpec((1, blk, 1), lambda h, i: (h, i, 0)),
        out_shape=jax.ShapeDtypeStruct((HEADS, t, 1), F32),
        compiler_params=_cparams(("parallel", "arbitrary")),
    )(qn, kn, p_main, c4, do, lse)


def _fox_bwd(qn, kn, p_main, c4, delta, do, lse, *, v_off, name, comm=None):
    t = qn.shape[0]
    blk = min(FOX_BLK, t)
    nb = t // blk
    vb = v_off // DH
    tn_dims = (((0,), (0,)), ((), ()))
    nt_dims = (((1,), (1,)), ((), ()))

    def body(q_ref, k_ref, v_ref, c_ref, delta_ref, do_ref, lse_ref, dq_ref, dk_ref, dv_ref, dc_ref):
        kj = pl.program_id(1)

        @pl.when(kj == 0)
        def _():
            dq_ref[...] = jnp.zeros_like(dq_ref)

        k = k_ref[...]
        v = v_ref[...]
        c_row = c_ref[0, 0]

        def step(i, carry, diagonal=False):
            dk, dv, dc = carry
            rows = pl.ds(pl.multiple_of(i * blk, blk), blk)
            q = q_ref[rows, :]
            dob = do_ref[rows, :]
            p = jnp.exp(_fox_logits(q, k, c_row, diagonal) - lse_ref[0, rows, :])
            pb = p.astype(BF16)
            dv = dv + lax.dot_general(pb, dob, tn_dims, preferred_element_type=F32)
            dp = lax.dot_general(dob, v, nt_dims, preferred_element_type=F32)
            ds = p * (dp - delta_ref[0, rows, :])
            dsb = ds.astype(BF16)
            dq_ref[rows, :] += jnp.dot(dsb, k, preferred_element_type=F32) * (DH ** -0.5)
            dk = dk + lax.dot_general(dsb, q, tn_dims, preferred_element_type=F32) * (DH ** -0.5)
            dc = dc - jnp.sum(ds, axis=0, keepdims=True)
            return dk, dv, dc

        init = (jnp.zeros((blk, DH), F32), jnp.zeros((blk, DH), F32), jnp.zeros((1, blk), F32))
        dk, dv, dc = lax.fori_loop(kj + 1, nb, step, step(kj, init, diagonal=True))
        dk_ref[...] = dk
        dv_ref[...] = dv.astype(dv_ref.dtype)
        dc_ref[0, 0] = dc

    full = lambda h, j: (0, h)
    kvb = lambda h, j: (j, h)
    return _pcall(
        body, name=name, grid=(HEADS, nb), sem=("parallel", "arbitrary"), comm=comm,
        args=(qn, kn, p_main, c4, delta, do, lse),
        in_specs=[pl.BlockSpec((t, DH), full), pl.BlockSpec((blk, DH), kvb),
                  pl.BlockSpec((blk, DH), lambda h, j: (j, vb + h)), pl.BlockSpec((1, 1, 1, blk), lambda h, j: (h, j, 0, 0)),
                  pl.BlockSpec((1, t, 1), lambda h, j: (h, 0, 0)), pl.BlockSpec((t, DH), full),
                  pl.BlockSpec((1, t, 1), lambda h, j: (h, 0, 0))],
        out_specs=[pl.BlockSpec((t, DH), full), pl.BlockSpec((blk, DH), kvb), pl.BlockSpec((blk, DH), kvb),
                   pl.BlockSpec((1, 1, 1, blk), lambda h, j: (h, j, 0, 0))],
        out_shape=[jax.ShapeDtypeStruct((t, HW), F32), jax.ShapeDtypeStruct((t, HW), F32),
                   jax.ShapeDtypeStruct((t, HW), BF16), jax.ShapeDtypeStruct((HEADS, nb, 1, blk), F32)])


ANY = pl.BlockSpec(memory_space=pl.ANY)


def _mesh_pos():
    return lax.axis_index("x"), lax.axis_index("y"), lax.axis_index("c")


def _all_gather(blocks, *, name):
    n = len(blocks)

    def body(*refs):
        ins, outs = refs[:n], refs[n:2 * n]
        send, recv, local = refs[2 * n:]
        x, y, c = _mesh_pos()
        me, sibling = (x, y, c), (x, y, 1 - c)
        chips = [(1 - x, y), (x, 1 - y), (1 - x, 1 - y)]

        def copy(t, k, block, to, src=None):
            dst = outs[t].at[4 * block[0] + 2 * block[1] + block[2]]
            return pltpu.make_async_remote_copy(
                src_ref=dst if src is None else src, dst_ref=dst, send_sem=send.at[7 * t + k],
                recv_sem=recv.at[7 * t + k], device_id=to, device_id_type=MESH)

        mine = [pltpu.make_async_copy(ins[t], outs[t].at[4 * x + 2 * y + c], local.at[t]) for t in range(n)]
        for cp in mine:
            cp.start()
        first = []
        for t in range(n):
            first.append(copy(t, 0, me, sibling, src=ins[t]))
            first += [copy(t, 1 + j, me, (*chip, c), src=ins[t]) for j, chip in enumerate(chips)]
        for cp in first:
            cp.start()
        passed = []
        for j, chip in enumerate(chips):
            for t in range(n):
                copy(t, 1 + j, (*chip, c), me).wait_recv()
                fwd = copy(t, 4 + j, (*chip, c), sibling)
                fwd.start()
                passed.append(fwd)
        for t in range(n):
            copy(t, 0, sibling, me).wait_recv()
            for j, chip in enumerate(chips):
                copy(t, 4 + j, (*chip, 1 - c), me).wait_recv()
        for cp in first + passed:
            cp.wait_send()
        for cp in mine:
            cp.wait()

    return pl.pallas_call(
        body, name=name, in_specs=[ANY] * n, out_specs=[ANY] * n,
        out_shape=[jax.ShapeDtypeStruct((N_DEV,) + b.shape, b.dtype) for b in blocks],
        scratch_shapes=[pltpu.SemaphoreType.DMA((7 * n,)), pltpu.SemaphoreType.DMA((7 * n,)),
                        pltpu.SemaphoreType.DMA((n,))],
    )(*blocks)


def _comm_call(comm, *, name):
    ci, co = len(comm.ins), len(comm.out_shapes)

    def body(*refs):
        comm.start(refs[:ci], refs[ci:ci + co], refs[ci + co:])
        comm.finish(refs[:ci], refs[ci:ci + co], refs[ci + co:])

    return pl.pallas_call(body, name=name, in_specs=[ANY] * ci, out_specs=[ANY] * co, out_shape=comm.out_shapes,
                          scratch_shapes=comm.sems, input_output_aliases=comm.aliases)(*comm.ins)


def _ag_first_comm(shards):
    n = len(shards)

    def copies(cin, cout, sems):
        send, recv, local = sems
        x, y, c = _mesh_pos()
        peers = [(x, y, 1 - c), (1 - x, y, c), (x, 1 - y, c), (1 - x, 1 - y, c)]
        slot = lambda p: 4 * p[0] + 2 * p[1] + p[2]
        mine, out, inc = [], [], []
        for t in range(n):
            mine.append(pltpu.make_async_copy(cin[t], cout[t].at[slot((x, y, c))], local.at[t]))
            for k, peer in enumerate(peers):
                sems_k = dict(send_sem=send.at[4 * t + k], recv_sem=recv.at[4 * t + k], device_id=peer,
                              device_id_type=MESH)
                out.append(pltpu.make_async_remote_copy(src_ref=cin[t], dst_ref=cout[t].at[slot((x, y, c))], **sems_k))
                inc.append(pltpu.make_async_remote_copy(src_ref=cout[t].at[slot(peer)], dst_ref=cout[t].at[slot(peer)],
                                                        **sems_k))
        return mine, out, inc

    def start(cin, cout, sems):
        mine, out, _ = copies(cin, cout, sems)
        for cp in mine + out:
            cp.start()

    def finish(cin, cout, sems):
        mine, out, inc = copies(cin, cout, sems)
        for cp in inc:
            cp.wait_recv()
        for cp in out:
            cp.wait_send()
        for cp in mine:
            cp.wait()

    return _Comm(shards, [jax.ShapeDtypeStruct((N_DEV,) + s.shape, s.dtype) for s in shards],
                 [pltpu.SemaphoreType.DMA((4 * n,)), pltpu.SemaphoreType.DMA((4 * n,)), pltpu.SemaphoreType.DMA((n,))],
                 start, finish)


def _ag_forward(gathered, *, name):
    n = len(gathered)

    def body(*refs):
        outs = refs[n:2 * n]
        send, recv = refs[2 * n:]
        x, y, c = _mesh_pos()
        chips = [(1 - x, y), (x, 1 - y), (1 - x, 1 - y)]
        fwd, inc = [], []
        for t in range(n):
            for j, (px, py) in enumerate(chips):
                sems_j = dict(send_sem=send.at[3 * t + j], recv_sem=recv.at[3 * t + j], device_id=(x, y, 1 - c),
                              device_id_type=MESH)
                mine, theirs = outs[t].at[4 * px + 2 * py + c], outs[t].at[4 * px + 2 * py + 1 - c]
                fwd.append(pltpu.make_async_remote_copy(src_ref=mine, dst_ref=mine, **sems_j))
                inc.append(pltpu.make_async_remote_copy(src_ref=theirs, dst_ref=theirs, **sems_j))
        for cp in fwd:
            cp.start()
        for cp in inc:
            cp.wait_recv()
        for cp in fwd:
            cp.wait_send()

    return pl.pallas_call(
        body, name=name, in_specs=[ANY] * n, out_specs=[ANY] * n,
        out_shape=[jax.ShapeDtypeStruct(g.shape, g.dtype) for g in gathered],
        scratch_shapes=[pltpu.SemaphoreType.DMA((3 * n,)), pltpu.SemaphoreType.DMA((3 * n,))],
        input_output_aliases={t: t for t in range(n)},
    )(*gathered)


def _rs_sibling(grads, *, name):
    n = len(grads)

    def body(*refs):
        ins, outs = refs[:n], refs[n:2 * n]
        send, recv = refs[2 * n:]
        x, y, c = _mesh_pos()
        copies = []
        for t in range(n):
            for q in range(4):
                copies.append(pltpu.make_async_remote_copy(
                    src_ref=ins[t].at[2 * q + (1 - c)], dst_ref=outs[t].at[q], send_sem=send.at[4 * t + q],
                    recv_sem=recv.at[4 * t + q], device_id=(x, y, 1 - c), device_id_type=MESH))
        for cp in copies:
            cp.start()
        for cp in copies:
            cp.wait_recv()
        for cp in copies:
            cp.wait_send()

    return pl.pallas_call(
        body, name=name, in_specs=[ANY] * n, out_specs=[ANY] * n,
        out_shape=[jax.ShapeDtypeStruct((4,) + g.shape[1:], g.dtype) for g in grads],
        scratch_shapes=[pltpu.SemaphoreType.DMA((4 * n,)), pltpu.SemaphoreType.DMA((4 * n,))],
    )(*grads)


def _rs_chips_comm(parts):
    n = len(parts)

    def copies(cin, cout, sems):
        send, recv, local = sems
        x, y, c = _mesh_pos()
        my_chip = 2 * x + y
        mine = [pltpu.make_async_copy(cin[t].at[my_chip], cout[t].at[my_chip], local.at[t]) for t in range(n)]
        sends, lands = [], []
        for t in range(n):
            for k, (px, py) in enumerate([(1 - x, y), (x, 1 - y), (1 - x, 1 - y)]):
                sems_k = dict(send_sem=send.at[3 * t + k], recv_sem=recv.at[3 * t + k], device_id=(px, py, c),
                              device_id_type=MESH)
                sends.append(pltpu.make_async_remote_copy(src_ref=cin[t].at[2 * px + py], dst_ref=cout[t].at[my_chip],
                                                          **sems_k))
                lands.append(pltpu.make_async_remote_copy(src_ref=cout[t].at[2 * px + py],
                                                          dst_ref=cout[t].at[2 * px + py], **sems_k))
        return mine, sends, lands

    def start(cin, cout, sems):
        mine, sends, _ = copies(cin, cout, sems)
        for cp in mine + sends:
            cp.start()

    def finish(cin, cout, sems):
        mine, sends, lands = copies(cin, cout, sems)
        for cp in lands:
            cp.wait_recv()
        for cp in sends:
            cp.wait_send()
        for cp in mine:
            cp.wait()

    return _Comm(parts, [jax.ShapeDtypeStruct(p.shape, p.dtype) for p in parts],
                 [pltpu.SemaphoreType.DMA((3 * n,)), pltpu.SemaphoreType.DMA((3 * n,)), pltpu.SemaphoreType.DMA((n,))],
                 start, finish)


def _row_tile(r, c, itemsize, budget=3 * 1024 * 1024):
    best = None
    for tr in range(16, r + 1, 16):
        if r % tr == 0 and tr * c * itemsize <= budget:
            best = tr
    return best or r


def _pair_sum(grad, land, *, name):
    _, r, c = grad.shape
    tr = _row_tile(r, c, 2)

    def body(g_ref, l_ref, o_ref):
        o_ref[...] = (g_ref[...].astype(F32) + l_ref[...].astype(F32)).astype(o_ref.dtype)

    return pl.pallas_call(
        body, name=name, grid=(4, r // tr),
        in_specs=[pl.BlockSpec((1, tr, c), lambda q, i: (2 * q + lax.axis_index("c"), i, 0)),
                  pl.BlockSpec((1, tr, c), lambda q, i: (q, i, 0))],
        out_specs=pl.BlockSpec((1, tr, c), lambda q, i: (q, i, 0)),
        out_shape=jax.ShapeDtypeStruct((4, r, c), grad.dtype),
        compiler_params=_cparams(("parallel", "parallel")),
    )(grad, land)


def _adamw_math(w, g, m, v):
    m = ADAM_B1 * m + (1.0 - ADAM_B1) * g
    v = ADAM_B2 * v + (1.0 - ADAM_B2) * jnp.square(g)
    m_hat = m / (1.0 - ADAM_B1 ** ADAM_STEP)
    v_hat = v / (1.0 - ADAM_B2 ** ADAM_STEP)
    delta = -ADAM_LR * (m_hat / (jnp.sqrt(v_hat) + ADAM_EPS) + ADAM_WD * w)
    return delta, m, v


def _adamw(parts, w, m, v, *, name):
    s, _, cp = parts.shape
    r, c = w.shape
    tr = _row_tile(r, cp, 4, budget=1024 * 1024)

    def body(p_ref, w_ref, m_ref, v_ref, g_ref, d_ref, nm_ref, nv_ref):
        g = p_ref[0].astype(F32)
        for i in range(1, s):
            g = g + p_ref[i].astype(F32)
        g = g[:, :c]
        delta, nm, nv = _adamw_math(w_ref[...], g, m_ref[...], v_ref[...])
        g_ref[...] = g
        d_ref[...] = delta
        nm_ref[...] = nm
        nv_ref[...] = nv

    blk = pl.BlockSpec((tr, c), lambda i: (i, 0))
    return pl.pallas_call(
        body, name=name, grid=(r // tr,),
        in_specs=[pl.BlockSpec((s, tr, cp), lambda i: (0, i, 0)), blk, blk, blk],
        out_specs=[blk] * 4, out_shape=[jax.ShapeDtypeStruct((r, c), F32)] * 4,
        compiler_params=_cparams(("parallel",)),
    )(parts, w, m, v)


def _w_in_pieces(d, nb, sources):
    segs = [(0, 4 * HW, False, 0), (4 * HW, 4 * HW + 2 * HEADS, True, 0),
            (4 * HW + 2 * HEADS, 7 * HW + 2 * HEADS, False, 4 * HW),
            (7 * HW + 2 * HEADS, 7 * HW + 3 * HEADS, True, 2 * HEADS),
            (7 * HW + 3 * HEADS, 7 * HW + 3 * HEADS + 2 * d, False, 7 * HW)]
    out = []
    for dev in range(N_DEV):
        lo, hi = dev * nb, (dev + 1) * nb
        for s0, s1, is_small, a0 in segs:
            p, q = max(lo, s0), min(hi, s1)
            if p >= q:
                continue
            a, b = a0 + p - s0, a0 + q - s0
            if is_small:
                out.append((dev, p - lo, q - lo, len(sources), a, b))
                continue
            for si, (start, width) in enumerate(sources):
                u, v = max(a, start), min(b, start + width)
                if u < v:
                    out.append((dev, p - lo + (u - a), p - lo + (v - a), si, u - start, v - start))
    return out


def _concat_cols(parts, *, name):
    t = parts[0].shape[0]
    n = len(parts)
    offs = [sum(p.shape[1] for p in parts[:i]) for i in range(n)]

    def body(*refs):
        o_ref, sems = refs[n], refs[n + 1]
        copies = [pltpu.make_async_copy(refs[i], o_ref.at[:, pl.ds(offs[i], parts[i].shape[1])], sems.at[i])
                  for i in range(n)]
        for cp in copies:
            cp.start()
        for cp in copies:
            cp.wait()

    return pl.pallas_call(
        body, name=name, in_specs=[ANY] * n, out_specs=ANY,
        out_shape=jax.ShapeDtypeStruct((t, offs[-1] + parts[-1].shape[1]), parts[0].dtype),
        scratch_shapes=[pltpu.SemaphoreType.DMA((n,))])(*parts)


def _w_in_to_aligned(g_in, *, name):
    _, d, nb = g_in.shape
    n_main = 7 * HW + 2 * d
    tr = min(128, d)
    pieces = _w_in_pieces(d, nb, [(0, n_main)])

    def body(g_ref, main_ref, small_ref):
        small_ref[...] = jnp.zeros_like(small_ref)
        for dev, s, e, src, a, b in pieces:
            dst = main_ref if src == 0 else small_ref
            dst[:, a:b] = g_ref[dev, :, s:e]

    return pl.pallas_call(
        body, name=name, grid=(d // tr,), in_specs=[pl.BlockSpec((N_DEV, tr, nb), lambda i: (0, i, 0))],
        out_specs=[pl.BlockSpec((tr, n_main), lambda i: (i, 0)), pl.BlockSpec((tr, LANES), lambda i: (i, 0))],
        out_shape=[jax.ShapeDtypeStruct((d, n_main), g_in.dtype), jax.ShapeDtypeStruct((d, LANES), g_in.dtype)],
        compiler_params=_cparams(("parallel",)),
    )(g_in)


def _w_in_grad_blocks(seg_grads, small_grad, sources, nb, *, name):
    d = small_grad.shape[0]
    tr = min(128, d)
    pieces = _w_in_pieces(d, nb, sources)
    ns = len(seg_grads)

    def body(*refs):
        o_ref = refs[ns + 1]
        for dev, s, e, src, a, b in pieces:
            o_ref[dev, :, s:e] = refs[src][:, a:b]

    return pl.pallas_call(
        body, name=name, grid=(d // tr,),
        in_specs=[pl.BlockSpec((tr, g.shape[1]), lambda i: (i, 0)) for g in seg_grads + [small_grad]],
        out_specs=pl.BlockSpec((N_DEV, tr, nb), lambda i: (0, i, 0)),
        out_shape=jax.ShapeDtypeStruct((N_DEV, d, nb), small_grad.dtype),
        compiler_params=_cparams(("parallel",)),
    )(*seg_grads, small_grad)


def _pad_cols(a, n):
    return a if a.shape[1] == n else jnp.concatenate([a, jnp.zeros((a.shape[0], n - a.shape[1]), a.dtype)], axis=1)


def _pad_rows(a, n):
    return a if a.shape[0] == n else jnp.concatenate([a, jnp.zeros((n - a.shape[0], a.shape[1]), a.dtype)], axis=0)


class _StaticPlan:
    def __init__(self, weights, cp):
        self.w, self.cp, self.grads = weights, cp, {}

    def comm_for(self, key):
        return None

    def done(self, key, res):
        pass

    def weight(self, name):
        return self.w[name]

    def grad(self, name, g):
        self.grads[name] = g

    def grad_w_in(self, g_main, g_small):
        self.grads["w_main"], self.grads["w_small"] = g_main, g_small


class _FsdpPlan:
    AG_RIDES = {"in_proj": ("wa", "wb", "wout", "wg"), "gdn_fwd": ("wu",), "fox_fwd": ("wd",)}
    RS_RIDES = ("d_hn_gate", "gdn_bwd", "d_xn")

    def __init__(self, shards, d, cp, nb):
        self.shards, self.d, self.cp, self.nb = shards, d, cp, nb
        self.first, self.full = {}, {}
        self.queue, self.flying, self.slots = [], [], {}

    def comm_for(self, key):
        if key in self.AG_RIDES:
            return _ag_first_comm([self.shards[n] for n in self.AG_RIDES[key]])
        if key in self.RS_RIDES and self.queue:
            self.flying, self.queue = self.queue, []
            return _rs_chips_comm([p for _, p in self.flying])
        return None

    def done(self, key, res):
        if key in self.AG_RIDES:
            self.first[key] = list(res)
        else:
            self.slots.update((n, s) for (n, _), s in zip(self.flying, res))
            self.flying = []

    def weight(self, name):
        if name not in self.full:
            key = next(k for k, names in self.AG_RIDES.items() if name in names)
            outs = _ag_forward(self.first[key], name=f"all_gather_forward_{key}")
            self.full.update(zip(self.AG_RIDES[key], outs))
        g = self.full[name]
        if name in ("wa", "wb"):
            return _cols_of_blocks(g)
        if name == "wout":
            return g.reshape(self.d, self.d)
        if name == "wd":
            return g.reshape(N_DEV * self.cp, self.d)
        return g

    def grad(self, name, g):
        if name == "wout":
            g = g.reshape(N_DEV, self.d // N_DEV, self.d)
        if name == "wd":
            g = g.reshape(N_DEV, self.cp, self.d)
        if name == "conv":
            g = _blocks_of_cols(g.astype(BF16))
        self.reduce(name, g)

    def grad_w_in(self, g_main, g_small):
        self.reduce("w_in", _w_in_grad_blocks([g_main], g_small, [(0, g_main.shape[1])], self.nb, name="w_in_grad_blocks"))

    def reduce(self, name, blocks):
        (land,) = _rs_sibling([blocks], name=f"grads_to_sibling_{name}")
        self.queue.append((name, _pair_sum(blocks, land, name=f"pair_sum_{name}")))

    def flush(self):
        if self.queue:
            outs = _comm_call(_rs_chips_comm([p for _, p in self.queue]), name="grads_to_chips_tail")
            self.slots.update((n, s) for (n, _), s in zip(self.queue, outs))
            self.queue = []


def _carried(plan, key, fn, *args, **kw):
    comm = plan.comm_for(key)
    if comm is None:
        return fn(*args, **kw)
    res, comm_res = fn(*args, comm=comm, **kw)
    plan.done(key, comm_res)
    return res


def _local_step(x, target, w_main, w_small, conv_w, plan,
                norm_mix_w, norm_ffn_w, gdn_norm_w, fox_q_w, fox_k_w, a_row, b_row):
    t, d = x.shape
    cp = plan.cp
    fp = N_DEV * cp
    n_main = w_main.shape[1]
    off_gb = OFF_GA + d
    tm = 1024
    rt = 128
    fcol = fp // 1024 if fp % 1024 == 0 else max(fp // 512, 1)

    (xn,) = _rowwise_fwd(_fn_norm, [(x, 0, d)], [norm_mix_w], [(d, BF16)], tm=rt, name="mix_norm")
    p_main = _carried(plan, "in_proj", _mm, xn, w_main, mode="nn", m=t, n=n_main, k=d, tm=tm, tn=512, tk=d,
                      out_dtype=BF16, name="in_proj")
    p_small = _mm(xn, w_small, mode="nn", m=t, n=LANES, k=d, tm=tm, tn=LANES, tk=d, out_dtype=F32, name="in_proj_small")
    (gates,) = _rowwise_fwd(_fn_gates, [(p_small, 0, LANES)], [a_row, b_row], [(LANES, F32)], tm=512, name="gates")
    qkv = _conv_fwd(p_main, conv_w, width=3 * HW, name="conv_fwd")
    o_gdn, s_all = _carried(plan, "gdn_fwd", _gdn_fwd, qkv, gates, name="gdn_fwd")
    gdn_rows = [(o_gdn, 0, HW), (p_main, OFF_ZA, HW)]
    (oa,) = _rowwise_fwd(_fn_gdn_out, gdn_rows, [gdn_norm_w], [(HW, BF16)], tm=512, ncol=HEADS, name="gdn_out")
    wa = plan.weight("wa")
    ya = _mm(oa, wa, mode="nn", m=t, n=d, k=HW, tm=tm, tn=1024, tk=HW, out_dtype=BF16, name="branch_a")
    qk_rows = [(p_main, OFF_QB, HW), (p_main, OFF_KB, HW)]
    qn, kn = _rowwise_fwd(_fn_qknorm, qk_rows, [fox_q_w, fox_k_w], [(HW, BF16), (HW, BF16)], tm=512, ncol=HEADS,
                          name="fox_qk_norm")
    blk = min(FOX_BLK, t)
    c4 = _fox_cumsum(gates, name="fox_cumsum").reshape(HEADS, t // blk, 1, blk)
    ob, lse = _carried(plan, "fox_fwd", _fox_fwd, qn, kn, p_main, c4, v_off=OFF_VB, name="fox_fwd")
    wb = plan.weight("wb")
    yb = _mm(ob, wb, mode="nn", m=t, n=d, k=HW, tm=tm, tn=1024, tk=HW, out_dtype=BF16, name="branch_b")
    mcol = 2 if d >= 2 * HW else 1
    merge_rows = [(p_main, OFF_GA, d), (p_main, off_gb, d), (ya, 0, d), (yb, 0, d)]
    (merged,) = _rowwise_fwd(_fn_merge, merge_rows, [], [(d, BF16)], tm=256, ncol=mcol, name="merge")
    wout = plan.weight("wout")
    h = _mm(merged, wout, mode="nn", m=t, n=d, k=d, tm=tm, tn=512, tk=d, out_dtype=F32, add=x, name="out_proj")
    (hn,) = _rowwise_fwd(_fn_norm, [(h, 0, d)], [norm_ffn_w], [(d, BF16)], tm=rt, name="ffn_norm")
    wg, wu = plan.weight("wg"), plan.weight("wu")
    gate = _mm(hn, wg, mode="nn", m=t, n=fp, k=d, tm=512, tn=cp, tk=d, out_dtype=BF16, b_blocked=True, name="ffn_gate")
    up = _mm(hn, wu, mode="nn", m=t, n=fp, k=d, tm=512, tn=cp, tk=d, out_dtype=BF16, b_blocked=True, name="ffn_up")
    (act,) = _rowwise_fwd(_fn_swiglu, [(gate, 0, fp), (up, 0, fp)], [], [(fp, BF16)], tm=512, ncol=fcol, name="swiglu")
    wd = plan.weight("wd")
    y = _mm(act, wd, mode="nn", m=t, n=d, k=fp, tm=512, tn=256, tk=fp, out_dtype=F32, add=h, name="ffn_down")
    dy, dyb, loss_row = _loss_head(y, target, tm=rt, name="loss_head")

    dact = _mm(dyb, wd, mode="nt", m=t, n=fp, k=d, tm=tm, tn=512, tk=d, out_dtype=BF16, name="d_act")
    plan.grad("wd", _mm(act, dyb, mode="tn", m=fp, n=d, k=t, tm=512, tn=1024, tk=t, out_dtype=BF16, name="dw_ffn_down"))
    dgate, dup = _rowwise_bwd(_fn_swiglu, [(gate, 0, fp), (up, 0, fp)], [], [dact], [BF16, BF16], tm=512, ncol=fcol,
                              name="d_swiglu")
    dhn = _carried(plan, "d_hn_gate", _mm, dgate, wg, mode="nt", m=t, n=d, k=fp, tm=tm, tn=1024, tk=cp, out_dtype=F32,
                   b_blocked=True, name="d_hn_gate")
    dhn = _mm(dup, wu, mode="nt", m=t, n=d, k=fp, tm=tm, tn=1024, tk=cp, out_dtype=F32, add=dhn, b_blocked=True,
              name="d_hn_up")
    plan.grad("wg", _mm(hn, dgate, mode="tn", m=d, n=fp, k=t, tm=512, tn=cp, tk=t, out_dtype=BF16, out_blocked=True,
                        name="dw_ffn_gate"))
    plan.grad("wu", _mm(hn, dup, mode="tn", m=d, n=fp, k=t, tm=512, tn=cp, tk=t, out_dtype=BF16, out_blocked=True,
                        name="dw_ffn_up"))
    dh, d_norm_ffn = _rowwise_bwd(_fn_norm, [(h, 0, d)], [norm_ffn_w], [dhn], [F32], tm=rt, name="d_ffn_norm", adds=[dy])
    dmerged = _mm(dh, wout, mode="nt", m=t, n=d, k=d, tm=512, tn=512, tk=d, out_dtype=BF16, name="d_merged")
    plan.grad("wout", _mm(merged, dh, mode="tn", m=d, n=d, k=t, tm=512, tn=512, tk=t, out_dtype=BF16, name="dw_out"))
    dga, dgb, dya, dyb2 = _rowwise_bwd(_fn_merge, merge_rows, [], [dmerged], [BF16] * 4, tm=256, ncol=mcol, name="d_merge")
    doa = _mm(dya, wa, mode="nt", m=t, n=HW, k=d, tm=tm, tn=512, tk=d, out_dtype=BF16, name="d_oa")
    plan.grad("wa", _mm(oa, dya, mode="tn", m=HW, n=d, k=t, tm=1024, tn=d // N_DEV, tk=t, out_dtype=BF16,
                        out_blocked=True, name="dw_branch_a"))
    dob = _mm(dyb2, wb, mode="nt", m=t, n=HW, k=d, tm=tm, tn=512, tk=d, out_dtype=BF16, name="d_ob")
    plan.grad("wb", _mm(ob, dyb2, mode="tn", m=HW, n=d, k=t, tm=1024, tn=d // N_DEV, tk=t, out_dtype=BF16,
                        out_blocked=True, name="dw_branch_b"))
    do_gdn, dza, d_gdn_norm = _rowwise_bwd(_fn_gdn_out, gdn_rows, [gdn_norm_w], [doa], [F32, BF16], tm=512,
                                           ncol=HEADS, name="d_gdn_out")
    dqkv, dgates_gdn = _carried(plan, "gdn_bwd", _gdn_bwd, qkv, gates, s_all, do_gdn, name="gdn_bwd")
    dp_qkv, dconv = _conv_bwd(p_main, conv_w, dqkv, width=3 * HW, name="conv_bwd")
    plan.grad("conv", dconv)
    delta = _fox_delta(qn, kn, p_main, c4, dob, lse, v_off=OFF_VB, name="fox_delta")
    dqn, dkn, dvb, dc4 = _fox_bwd(qn, kn, p_main, c4, delta, dob, lse, v_off=OFF_VB, name="fox_bwd")
    dqb, dkb, d_fox_q, d_fox_k = _rowwise_bwd(_fn_qknorm, qk_rows, [fox_q_w, fox_k_w], [dqn, dkn], [BF16, BF16],
                                              tm=512, ncol=HEADS, name="d_fox_qk_norm")
    dgates = _fox_cumsum_bwd(dc4.reshape(HEADS, t), dgates_gdn, name="fox_cumsum_bwd")
    dsmall, d_a_row, d_b_row = _rowwise_bwd(_fn_gates, [(p_small, 0, LANES)], [a_row, b_row], [dgates], [F32],
                                            tm=512, name="d_gates")
    dp_main = _concat_cols([dp_qkv, dza, dqb, dkb, dvb, dga, dgb], name="d_p_main")
    plan.grad_w_in(_mm(xn, dp_main, mode="tn", m=d, n=n_main, k=t, tm=1024, tn=math.gcd(n_main, 1024), tk=t,
                       out_dtype=BF16, name="dw_in"),
                   _mm(xn, dsmall, mode="tn", m=d, n=LANES, k=t, tm=1024, tn=LANES, tk=t, out_dtype=BF16,
                       name="dw_in_small"))
    dxn = _mm(dsmall, w_small, mode="nt", m=t, n=d, k=LANES, tm=tm, tn=1024, tk=LANES, out_dtype=F32, name="d_xn_small")
    dxn = _carried(plan, "d_xn", _mm, dp_main, w_main, mode="nt", m=t, n=d, k=n_main, tm=tm, tn=1024,
                   tk=math.gcd(n_main, 2048),
                   out_dtype=F32, add=dxn, name="d_xn")
    grad_x, d_norm_mix = _rowwise_bwd(_fn_norm, [(x, 0, d)], [norm_mix_w], [dxn], [F32], tm=rt, name="d_mix_norm",
                                      adds=[dh])
    small = dict(norm_mix=d_norm_mix, norm_ffn=d_norm_ffn, gdn_norm=d_gdn_norm, fox_q=d_fox_q, fox_k=d_fox_k,
                 a_row=d_a_row, b_row=d_b_row)
    return loss_row[0, 0], grad_x, small


def _lane_row(pieces):
    row = jnp.zeros((1, LANES), F32)
    for off, p in pieces:
        row = lax.dynamic_update_slice(row, p.astype(F32), (0, off))
    return row


def _pack_small(norm_mix, norm_ffn, gdn_norm, fox_q, fox_k, a_log, dt_bias, b_f):
    rows = [norm_mix.reshape(-1, LANES), norm_ffn.reshape(-1, LANES), gdn_norm, fox_q, fox_k,
            _lane_row([(HEADS, a_log)]), _lane_row([(HEADS, dt_bias), (2 * HEADS, b_f)])]
    packed = jnp.concatenate(rows, axis=0)
    return _pad_rows(packed, -(-packed.shape[0] // 8) * 8)


def _unpack_small(p, d):
    nd = d // LANES
    r = 2 * nd
    return (p[0:nd].reshape(1, d), p[r + 3:r + 4, HEADS:2 * HEADS], p[r + 4:r + 5, HEADS:2 * HEADS], p[r:r + 1],
            p[r + 4:r + 5, 2 * HEADS:3 * HEADS], p[r + 1:r + 2], p[r + 2:r + 3], p[nd:r].reshape(1, d))


def _blocks_of_cols(a):
    r, c8 = a.shape
    return a.reshape(r, N_DEV, c8 // N_DEV).transpose(1, 0, 2)


def _cols_of_blocks(g):
    _, r, c = g.shape
    return g.transpose(1, 0, 2).reshape(r, N_DEV * c)


def kernel(x, norm_mix_w, w_in, conv_w, a_log, dt_bias, gdn_norm_w, fox_b_f, fox_q_norm_w, fox_k_norm_w, w_branch_a, w_branch_b, w_out, norm_ffn_w, w_ffn_gate, w_ffn_up, w_ffn_down, loss_target, m_norm_mix_w, m_w_in, m_conv_w, m_a_log, m_dt_bias, m_gdn_norm_w, m_fox_b_f, m_fox_q_norm_w, m_fox_k_norm_w, m_w_branch_a, m_w_branch_b, m_w_out, m_norm_ffn_w, m_w_ffn_gate, m_w_ffn_up, m_w_ffn_down, v_norm_mix_w, v_w_in, v_conv_w, v_a_log, v_dt_bias, v_gdn_norm_w, v_fox_b_f, v_fox_q_norm_w, v_fox_k_norm_w, v_w_branch_a, v_w_branch_b, v_w_out, v_norm_ffn_w, v_w_ffn_gate, v_w_ffn_up, v_w_ffn_down):
    d = x.shape[-1]
    cp = -(-w_ffn_down.shape[1] // LANES) * LANES
    nb = w_in.shape[2]

    g_in, g_conv = _all_gather([w_in[0].astype(BF16), conv_w[0]], name="w_in_all_gather")
    w_main, w_small = _w_in_to_aligned(g_in, name="w_in_to_aligned")
    conv_full = _cols_of_blocks(g_conv)
    plan = _FsdpPlan(dict(wa=w_branch_a[0].astype(BF16), wb=w_branch_b[0].astype(BF16), wout=w_out[0].astype(BF16),
                          wg=_pad_cols(w_ffn_gate[0].astype(BF16), cp), wu=_pad_cols(w_ffn_up[0].astype(BF16), cp),
                          wd=_pad_rows(w_ffn_down[0].astype(BF16), cp)), d, cp, nb)
    a_row = _lane_row([(HEADS, a_log)])
    b_row = _lane_row([(HEADS, dt_bias), (2 * HEADS, fox_b_f)])

    loss_part, grad_x, gs = _local_step(
        x[0], loss_target[0], w_main, w_small, conv_full, plan,
        norm_mix_w, norm_ffn_w, gdn_norm_w, fox_q_norm_w, fox_k_norm_w, a_row, b_row)
    loss = lax.psum(loss_part, ("x", "y", "c"))

    plan.flush()
    big = dict(w_in=("w_in", w_in, m_w_in, v_w_in), w_branch_a=("wa", w_branch_a, m_w_branch_a, v_w_branch_a),
               w_branch_b=("wb", w_branch_b, m_w_branch_b, v_w_branch_b), w_out=("wout", w_out, m_w_out, v_w_out),
               w_ffn_gate=("wg", w_ffn_gate, m_w_ffn_gate, v_w_ffn_gate), w_ffn_up=("wu", w_ffn_up, m_w_ffn_up, v_w_ffn_up),
               w_ffn_down=("wd", w_ffn_down, m_w_ffn_down, v_w_ffn_down), conv_w=("conv", conv_w, m_conv_w, v_conv_w))
    res = {}
    for nm, (key, w, m, v) in big.items():
        res[nm] = [o[None] for o in _adamw(plan.slots[key], w[0], m[0], v[0], name=f"adamw_{nm}")]

    g_small = _pack_small(gs["norm_mix"], gs["norm_ffn"], gs["gdn_norm"], gs["fox_q"], gs["fox_k"],
                          gs["a_row"][:, HEADS:2 * HEADS], gs["b_row"][:, HEADS:2 * HEADS],
                          gs["b_row"][:, 2 * HEADS:3 * HEADS])
    (g_small_all,) = _all_gather([g_small], name="small_grads_all_gather")
    w_small_p = _pack_small(norm_mix_w, norm_ffn_w, gdn_norm_w, fox_q_norm_w, fox_k_norm_w, a_log, dt_bias, fox_b_f)
    m_small_p = _pack_small(m_norm_mix_w, m_norm_ffn_w, m_gdn_norm_w, m_fox_q_norm_w, m_fox_k_norm_w, m_a_log,
                            m_dt_bias, m_fox_b_f)
    v_small_p = _pack_small(v_norm_mix_w, v_norm_ffn_w, v_gdn_norm_w, v_fox_q_norm_w, v_fox_k_norm_w, v_a_log,
                            v_dt_bias, v_fox_b_f)
    small_res = [_unpack_small(o, d) for o in _adamw(g_small_all, w_small_p, m_small_p, v_small_p, name="adamw_small")]

    def group(k):
        s = small_res[k]
        return [s[0], res["w_in"][k], res["conv_w"][k], s[1], s[2], s[3], s[4], s[5], s[6], res["w_branch_a"][k],
                res["w_branch_b"][k], res["w_out"][k], s[7], res["w_ffn_gate"][k], res["w_ffn_up"][k],
                res["w_ffn_down"][k]]

    return (loss, grad_x[None], *group(0), *group(1), *group(2), *group(3))
```

```python
import functools
import math

import jax
import jax.numpy as jnp
from jax import lax
from jax.experimental import pallas as pl
from jax.experimental.pallas import tpu as pltpu

F32 = jnp.float32
BF16 = jnp.bfloat16
HI = lax.Precision.HIGHEST
SOLVE_PRECISION = lax.Precision.HIGH
MESH = pl.DeviceIdType.MESH

EPS = 1e-6
HEADS = 16
DH = 128
HW = HEADS * DH
CHUNK = 64
CONV_K = 4
N_DEV = 8
LANES = 128
VMEM_LIMIT = 52 * 1024 * 1024

ADAM_LR = 0.001
ADAM_B1 = 0.9
ADAM_B2 = 0.999
ADAM_EPS = 1e-08
ADAM_WD = 0.01
ADAM_STEP = 10

OFF_QA, OFF_KA, OFF_VA, OFF_ZA, OFF_QB, OFF_KB, OFF_VB, OFF_GA = 0, HW, 2 * HW, 3 * HW, 4 * HW, 5 * HW, 6 * HW, 7 * HW


def _cparams(sem=None, vmem=VMEM_LIMIT):
    return pltpu.CompilerParams(dimension_semantics=sem, vmem_limit_bytes=vmem)


class _Comm:
    def __init__(self, ins, out_shapes, sems, start, finish, aliases=None):
        self.ins, self.out_shapes, self.sems = list(ins), list(out_shapes), list(sems)
        self.start, self.finish, self.aliases = start, finish, dict(aliases or {})


def _pcall(body, *, name, grid, in_specs, out_specs, out_shape, args, sem, scratch_shapes=(), comm=None):
    multi = isinstance(out_shape, (list, tuple))
    out_specs = list(out_specs) if multi else [out_specs]
    out_shape = list(out_shape) if multi else [out_shape]
    scratch_shapes = list(scratch_shapes)
    if comm is None:
        res = pl.pallas_call(body, name=name, grid=grid, in_specs=list(in_specs), out_specs=out_specs,
                             out_shape=out_shape, scratch_shapes=scratch_shapes, compiler_params=_cparams(sem))(*args)
        return res if multi else res[0]
    ni, no, ns = len(in_specs), len(out_specs), len(scratch_shapes)
    ci, co = len(comm.ins), len(comm.out_shapes)

    def wrapped(*refs):
        cin = refs[ni:ni + ci]
        outs = refs[ni + ci:ni + ci + no]
        cout = refs[ni + ci + no:ni + ci + no + co]
        scr = refs[ni + ci + no + co:ni + ci + no + co + ns]
        csem = refs[ni + ci + no + co + ns:]
        ids = [pl.program_id(ax) for ax in range(len(grid))]
        first = functools.reduce(jnp.logical_and, [i == 0 for i in ids])
        last = functools.reduce(jnp.logical_and, [i == g - 1 for i, g in zip(ids, grid)])

        @pl.when(first)
        def _():
            comm.start(cin, cout, csem)

        body(*refs[:ni], *outs, *scr)

        @pl.when(last)
        def _():
            comm.finish(cin, cout, csem)

    any_spec = pl.BlockSpec(memory_space=pl.ANY)
    res = pl.pallas_call(
        wrapped, name=name, grid=grid, in_specs=list(in_specs) + [any_spec] * ci,
        out_specs=out_specs + [any_spec] * co, out_shape=out_shape + comm.out_shapes,
        scratch_shapes=scratch_shapes + comm.sems,
        input_output_aliases={ni + i: no + o for i, o in comm.aliases.items()},
        compiler_params=_cparams(("arbitrary",) * len(grid)))(*args, *comm.ins)
    return (res[:no] if multi else res[0]), res[no:]


def _mm(a, b, *, mode, m, n, k, tm, tn, tk, out_dtype, name, a_off=(0, 0), b_off=(0, 0), add=None,
        b_blocked=False, out_blocked=False, comm=None):
    tm, tn, tk = min(tm, m), min(tn, n), min(tk, k)
    assert m % tm == 0 and n % tn == 0 and k % tk == 0, (name, m, n, k, tm, tn, tk)
    nk = k // tk
    if mode == "nn":
        a_blk, b_blk = (tm, tk), (tk, tn)
        ao, bo = (a_off[0] // tm, a_off[1] // tk), (b_off[0] // tk, b_off[1] // tn)
        a_map = lambda i, j, kk: (i + ao[0], kk + ao[1])
        b_map = lambda i, j, kk: (kk + bo[0], j + bo[1])
        dims = (((1,), (0,)), ((), ()))
        if b_blocked:
            assert b.shape == (n // tn, k, tn) and b_off == (0, 0), (name, b.shape)
            b_blk, b_map = (None, tk, tn), lambda i, j, kk: (j, kk, 0)
    elif mode == "nt":
        a_blk, b_blk = (tm, tk), (tn, tk)
        ao, bo = (a_off[0] // tm, a_off[1] // tk), (b_off[0] // tn, b_off[1] // tk)
        a_map = lambda i, j, kk: (i + ao[0], kk + ao[1])
        b_map = lambda i, j, kk: (j + bo[0], kk + bo[1])
        dims = (((1,), (1,)), ((), ()))
        if b_blocked:
            assert b.shape == (nk, n, tk) and b_off == (0, 0), (name, b.shape)
            b_blk, b_map = (None, tn, tk), lambda i, j, kk: (kk, j, 0)
    else:
        assert not b_blocked
        a_blk, b_blk = (tk, tm), (tk, tn)
        ao, bo = (a_off[0] // tk, a_off[1] // tm), (b_off[0] // tk, b_off[1] // tn)
        a_map = lambda i, j, kk: (kk + ao[0], i + ao[1])
        b_map = lambda i, j, kk: (kk + bo[0], j + bo[1])
        dims = (((0,), (0,)), ((), ()))
    if not b_blocked:
        for off, blk in ((a_off, a_blk), (b_off, b_blk)):
            assert off[0] % blk[0] == 0 and off[1] % blk[1] == 0, (name, off, blk)
    has_add = add is not None

    def body(*refs):
        if has_add:
            a_ref, b_ref, c_ref, o_ref, acc = refs
        else:
            a_ref, b_ref, o_ref, acc = refs
            c_ref = None
        p = lax.dot_general(a_ref[...].astype(BF16), b_ref[...].astype(BF16), dims, preferred_element_type=F32)
        if nk == 1:
            if has_add:
                p = p + c_ref[...].astype(F32)
            o_ref[...] = p.astype(o_ref.dtype)
        else:
            kk = pl.program_id(2)

            @pl.when(kk == 0)
            def _():
                acc[...] = p + c_ref[...].astype(F32) if has_add else p

            @pl.when(kk > 0)
            def _():
                acc[...] += p

            @pl.when(kk == nk - 1)
            def _():
                o_ref[...] = acc[...].astype(o_ref.dtype)

    in_specs = [pl.BlockSpec(a_blk, a_map), pl.BlockSpec(b_blk, b_map)]
    args = [a, b]
    if has_add:
        in_specs.append(pl.BlockSpec((tm, tn), lambda i, j, kk: (i, j)))
        args.append(add)
    acc_shape = (tm, tn) if nk > 1 else (8, LANES)
    if out_blocked:
        out_spec = pl.BlockSpec((None, tm, tn), lambda i, j, kk: (j, i, 0))
        out_shape = jax.ShapeDtypeStruct((n // tn, m, tn), out_dtype)
    else:
        out_spec = pl.BlockSpec((tm, tn), lambda i, j, kk: (i, j))
        out_shape = jax.ShapeDtypeStruct((m, n), out_dtype)
    return _pcall(body, name=name, grid=(m // tm, n // tn, nk), in_specs=in_specs, out_specs=out_spec,
                  out_shape=out_shape, scratch_shapes=[pltpu.VMEM(acc_shape, F32)], args=args,
                  sem=("parallel", "parallel", "arbitrary"), comm=comm)


def _row_specs(rows, tm, ncol):
    specs = []
    for arr, off, width in rows:
        bw = width // ncol
        assert width % ncol == 0 and off % bw == 0, (off, width, ncol)
        ob = off // bw
        specs.append(pl.BlockSpec((tm, bw), lambda i, j, ob=ob: (i, j + ob)))
    return specs


def _rowwise_fwd(fn, rows, params, outs, *, tm, ncol=1, name):
    t = rows[0][0].shape[0]
    tm = min(tm, t)
    nr, npar = len(rows), len(params)

    def body(*refs):
        ins = [r[...].astype(F32) for r in refs[:nr + npar]]
        res = fn(*ins)
        for o_ref, val in zip(refs[nr + npar:], res):
            o_ref[...] = val.astype(o_ref.dtype)

    in_specs = _row_specs(rows, tm, ncol) + [pl.BlockSpec(p.shape, lambda i, j: (0, 0)) for p in params]
    out_specs = [pl.BlockSpec((tm, w // ncol), lambda i, j: (i, j)) for w, _ in outs]
    out_shape = [jax.ShapeDtypeStruct((t, w), dt) for w, dt in outs]
    return pl.pallas_call(
        body, name=name, grid=(t // tm, ncol), in_specs=in_specs, out_specs=out_specs, out_shape=out_shape,
        compiler_params=_cparams(("parallel", "parallel")),
    )(*[r[0] for r in rows], *params)


def _rowwise_bwd(fn, rows, params, cts, grad_dtypes, *, tm, ncol=1, name, adds=None):
    t = rows[0][0].shape[0]
    tm = min(tm, t)
    nr, npar, nct = len(rows), len(params), len(cts)
    adds = adds or [None] * nr
    add_idx = [i for i, a in enumerate(adds) if a is not None]

    def body(*refs):
        ins = [r[...].astype(F32) for r in refs[:nr + npar]]
        ct = tuple(r[...].astype(F32) for r in refs[nr + npar:nr + npar + nct])
        add_refs = refs[nr + npar + nct:nr + npar + nct + len(add_idx)]
        outs = refs[nr + npar + nct + len(add_idx):]
        _, vjp = jax.vjp(lambda *a: tuple(fn(*a)), *ins)
        grads = vjp(ct)
        extra = dict(zip(add_idx, add_refs))
        for i in range(nr):
            g = grads[i]
            if i in extra:
                g = g + extra[i][...].astype(F32)
            outs[i][...] = g.astype(outs[i].dtype)
        first = jnp.logical_and(pl.program_id(0) == 0, pl.program_id(1) == 0)
        for pi in range(npar):
            o_ref = outs[nr + pi]
            g = grads[nr + pi]

            @pl.when(first)
            def _(o_ref=o_ref, g=g):
                o_ref[...] = g

            @pl.when(jnp.logical_not(first))
            def _(o_ref=o_ref, g=g):
                o_ref[...] += g

    in_specs = (_row_specs(rows, tm, ncol)
                + [pl.BlockSpec(p.shape, lambda i, j: (0, 0)) for p in params]
                + [pl.BlockSpec((tm, c.shape[1] // ncol), lambda i, j: (i, j)) for c in cts]
                + [pl.BlockSpec((tm, adds[i].shape[1] // ncol), lambda i, j: (i, j)) for i in add_idx])
    out_specs = ([pl.BlockSpec((tm, w // ncol), lambda i, j: (i, j)) for _, _, w in rows]
                 + [pl.BlockSpec(p.shape, lambda i, j: (0, 0)) for p in params])
    out_shape = ([jax.ShapeDtypeStruct((t, w), dt) for (_, _, w), dt in zip(rows, grad_dtypes)]
                 + [jax.ShapeDtypeStruct(p.shape, F32) for p in params])
    return pl.pallas_call(
        body, name=name, grid=(t // tm, ncol), in_specs=in_specs, out_specs=out_specs, out_shape=out_shape,
        compiler_params=_cparams(("arbitrary", "arbitrary")),
    )(*[r[0] for r in rows], *params, *cts, *[adds[i] for i in add_idx])


def _rms(x, w):
    return x * lax.rsqrt(jnp.mean(x * x, axis=-1, keepdims=True) + EPS) * w


def _fn_norm(x, w):
    return (_rms(x, w),)


def _fn_gates(z, a_row, b_row):
    lane = lax.broadcasted_iota(jnp.int32, z.shape, 1)
    beta = jax.nn.sigmoid(z)
    g = -jnp.exp(a_row) * jax.nn.softplus(z + b_row)
    logf = jax.nn.log_sigmoid(z + b_row)
    return (jnp.where(lane < HEADS, beta, jnp.where(lane < 2 * HEADS, g, jnp.where(lane < 3 * HEADS, logf, 0.0))),)


def _fn_qknorm(q, k, qw, kw):
    return _rms(q, qw), _rms(k, kw)


def _fn_gdn_out(o, z, w):
    return (_rms(o, w) * jax.nn.silu(z),)


def _fn_merge(ga, gb, ya, yb):
    return (jax.nn.sigmoid(ga) * ya + jax.nn.sigmoid(gb) * yb,)


def _fn_swiglu(g, u):
    return (jax.nn.silu(g) * u,)


def _loss_head(y, target, *, tm, name):
    t, d = y.shape
    tm = min(tm, t)

    def body(y_ref, t_ref, dyf_ref, dyb_ref, loss_ref):
        err = y_ref[...] - t_ref[...]
        dy = err * (1.0 / d)
        dyf_ref[...] = dy
        dyb_ref[...] = dy.astype(BF16)
        part = jnp.sum(err * err) * (0.5 / d)

        @pl.when(pl.program_id(0) == 0)
        def _():
            loss_ref[...] = jnp.zeros_like(loss_ref)

        loss_ref[...] += part

    blk = pl.BlockSpec((tm, d), lambda i: (i, 0))
    return pl.pallas_call(
        body, name=name, grid=(t // tm,), in_specs=[blk, blk],
        out_specs=[blk, blk, pl.BlockSpec((1, LANES), lambda i: (0, 0))],
        out_shape=[jax.ShapeDtypeStruct((t, d), F32), jax.ShapeDtypeStruct((t, d), BF16),
                   jax.ShapeDtypeStruct((1, LANES), F32)],
        compiler_params=_cparams(("arbitrary",)),
    )(y, target)


def _shift_down(x, s):
    if s == 0:
        return x
    row = lax.broadcasted_iota(jnp.int32, x.shape, 0)
    return jnp.where(row >= s, pltpu.roll(x, s, 0), 0.0)


def _shift_up(x, s):
    if s == 0:
        return x
    t = x.shape[0]
    row = lax.broadcasted_iota(jnp.int32, x.shape, 0)
    return jnp.where(row < t - s, pltpu.roll(x, t - s, 0), 0.0)


def _conv_pre(x, w):
    y = x * w[CONV_K - 1:CONV_K, :]
    for i in range(CONV_K - 1):
        y = y + _shift_down(x, CONV_K - 1 - i) * w[i:i + 1, :]
    return y


def _conv_fwd(p_main, conv_w, *, width, name):
    t = p_main.shape[0]
    tc = LANES

    def body(x_ref, w_ref, o_ref):
        y = _conv_pre(x_ref[...].astype(F32), w_ref[...])
        o_ref[...] = y * jax.nn.sigmoid(y)

    return pl.pallas_call(
        body, name=name, grid=(width // tc,),
        in_specs=[pl.BlockSpec((t, tc), lambda j: (0, j)), pl.BlockSpec((CONV_K, tc), lambda j: (0, j))],
        out_specs=pl.BlockSpec((t, tc), lambda j: (0, j)),
        out_shape=jax.ShapeDtypeStruct((t, width), F32),
        compiler_params=_cparams(("parallel",)),
    )(p_main, conv_w)


def _conv_bwd(p_main, conv_w, dy, *, width, name):
    t = p_main.shape[0]
    tc = LANES

    def body(x_ref, w_ref, dy_ref, dx_ref, dw_ref):
        x = x_ref[...].astype(F32)
        w = w_ref[...]
        pre = _conv_pre(x, w)
        sg = jax.nn.sigmoid(pre)
        dpre = dy_ref[...] * (sg * (1.0 + pre * (1.0 - sg)))
        dx = dpre * w[CONV_K - 1:CONV_K, :]
        dws = []
        for i in range(CONV_K - 1):
            s = CONV_K - 1 - i
            dx = dx + _shift_up(dpre, s) * w[i:i + 1, :]
            dws.append(jnp.sum(_shift_down(x, s) * dpre, axis=0, keepdims=True))
        dws.append(jnp.sum(x * dpre, axis=0, keepdims=True))
        dx_ref[...] = dx.astype(dx_ref.dtype)
        dw_ref[...] = jnp.concatenate(dws, axis=0)

    return pl.pallas_call(
        body, name=name, grid=(width // tc,),
        in_specs=[pl.BlockSpec((t, tc), lambda j: (0, j)), pl.BlockSpec((CONV_K, tc), lambda j: (0, j)),
                  pl.BlockSpec((t, tc), lambda j: (0, j))],
        out_specs=[pl.BlockSpec((t, tc), lambda j: (0, j)), pl.BlockSpec((CONV_K, tc), lambda j: (0, j))],
        out_shape=[jax.ShapeDtypeStruct((t, width), BF16), jax.ShapeDtypeStruct((CONV_K, width), F32)],
        compiler_params=_cparams(("parallel",)),
    )(p_main, conv_w, dy)


def _bmm(a, b, spec, precision=None):
    return jnp.einsum(spec, a, b, preferred_element_type=F32, precision=precision)


def _iota2(shape, dim):
    return lax.broadcasted_iota(jnp.int32, shape, dim)


@jax.custom_vjp
def _tri_inverse(a):
    return _tri_inverse_levels(a)


def _tri_inverse_fwd(a):
    t = _tri_inverse_levels(a)
    return t, t


def _tri_inverse_bwd(t, g):
    x = _bmm(t, g, "hji,hjk->hik", SOLVE_PRECISION)
    return (-_bmm(x, t, "hik,hjk->hij", SOLVE_PRECISION),)


_tri_inverse.defvjp(_tri_inverse_fwd, _tri_inverse_bwd)


def _tri_inverse_levels(a):
    c = a.shape[-1]
    r, m = _iota2((c, c), 0), _iota2((c, c), 1)
    eye = (r == m).astype(F32)
    inv = None
    b = 1
    while b < c:
        mask = jnp.logical_and(r // (2 * b) == m // (2 * b), jnp.logical_and(r % (2 * b) >= b, m % (2 * b) < b))
        off = jnp.where(mask[None], a, 0.0)
        if inv is None:
            inv = eye[None] - off
        else:
            inv = inv - _bmm(_bmm(inv, off, "hij,hjk->hik", SOLVE_PRECISION), inv, "hij,hjk->hik", SOLVE_PRECISION)
        b *= 2
    return inv


def _gdn_chunk(s, q3, k3, v3, b3, gc3):
    c = q3.shape[1]
    r, m = _iota2((c, c), 0), _iota2((c, c), 1)
    tril_incl = (r >= m)[None]
    tril_strict = (r > m)[None]
    eye = (r == m).astype(F32)[None]
    qn = q3 * lax.rsqrt(jnp.sum(q3 * q3, axis=-1, keepdims=True) + EPS) * (DH ** -0.5)
    kn = k3 * lax.rsqrt(jnp.sum(k3 * k3, axis=-1, keepdims=True) + EPS)
    ones = jnp.ones((q3.shape[0], c, c), F32)
    gc_row = _bmm(ones, gc3 * eye, "hij,hjk->hik", SOLVE_PRECISION)
    decay = jnp.where(tril_incl, jnp.exp(jnp.where(tril_incl, gc3 - gc_row, 0.0)), 0.0)
    a = jnp.where(tril_strict, _bmm(kn, kn, "hcd,hmd->hcm") * decay * b3, 0.0)
    tinv = _tri_inverse(a)
    egc = jnp.exp(gc3)
    u = _bmm(tinv, v3 * b3, "hij,hjk->hik", SOLVE_PRECISION)
    w = _bmm(tinv, kn * (b3 * egc), "hij,hjk->hik", SOLVE_PRECISION)
    qk = _bmm(qn, kn, "hcd,hmd->hcm") * decay
    v_new = u - _bmm(w, s, "hcd,hdv->hcv")
    o = _bmm(qn * egc, s, "hcd,hdv->hcv") + _bmm(qk, v_new, "hcm,hmv->hcv")
    row = _iota2((c, 1), 0)[None]
    g_last = jnp.sum(jnp.where(row == c - 1, gc3, 0.0), axis=1, keepdims=True)
    s_new = s * jnp.exp(g_last) + _bmm(kn * jnp.exp(g_last - gc3), v_new, "hcd,hcv->hdv")
    return s_new, o


GDN_HEAD_GROUP = 8


def _split_heads(ref, off, h0):
    return jnp.stack([ref[:, off + h * DH:off + (h + 1) * DH].astype(F32)
                      for h in range(h0, h0 + GDN_HEAD_GROUP)], axis=0)


def _store_heads(ref, x3, off, h0):
    for i in range(GDN_HEAD_GROUP):
        h = h0 + i
        ref[:, off + h * DH:off + (h + 1) * DH] = x3[i].astype(ref.dtype)


def _lane_cols(tile, lane0):
    lane = _iota2(tile.shape, 1)
    return jnp.stack([jnp.sum(jnp.where(lane == lane0 + i, tile, 0.0), axis=1, keepdims=True)
                      for i in range(GDN_HEAD_GROUP)], axis=0)


def _cols_to_lanes(cols3, lane0, shape):
    lane = _iota2(shape, 1)
    out = jnp.zeros(shape, F32)
    for i in range(GDN_HEAD_GROUP):
        out = out + jnp.where(lane == lane0 + i, cols3[i], 0.0)
    return out


def _chunk_cumsum_matrix():
    r, m = _iota2((CHUNK, CHUNK), 0), _iota2((CHUNK, CHUNK), 1)
    return (r >= m).astype(F32)


def _gdn_inputs(qkv_ref, gt, gcum, h0):
    return (_split_heads(qkv_ref, 0, h0), _split_heads(qkv_ref, HW, h0), _split_heads(qkv_ref, 2 * HW, h0),
            _lane_cols(gt, h0), _lane_cols(gcum, HEADS + h0))


def _gdn_fwd(qkv, gates, *, name, comm=None):
    t = qkv.shape[0]
    n = t // CHUNK

    def body(qkv_ref, gt_ref, o_ref, sall_ref, s_scr):
        @pl.when(pl.program_id(0) == 0)
        def _():
            s_scr[...] = jnp.zeros_like(s_scr)

        gt = gt_ref[...]
        gcum = jnp.dot(_chunk_cumsum_matrix(), gt, preferred_element_type=F32, precision=HI)
        for h0 in range(0, HEADS, GDN_HEAD_GROUP):
            grp = pl.ds(h0, GDN_HEAD_GROUP)
            s = s_scr[grp]
            sall_ref[0, grp] = s
            s_new, o3 = _gdn_chunk(s, *_gdn_inputs(qkv_ref, gt, gcum, h0))
            s_scr[grp] = s_new
            _store_heads(o_ref, o3, 0, h0)

    return _pcall(
        body, name=name, grid=(n,),
        in_specs=[pl.BlockSpec((CHUNK, 3 * HW), lambda i: (i, 0)), pl.BlockSpec((CHUNK, LANES), lambda i: (i, 0))],
        out_specs=[pl.BlockSpec((CHUNK, HW), lambda i: (i, 0)),
                   pl.BlockSpec((1, HEADS, DH, DH), lambda i: (i, 0, 0, 0))],
        out_shape=[jax.ShapeDtypeStruct((t, HW), F32), jax.ShapeDtypeStruct((n, HEADS, DH, DH), F32)],
        scratch_shapes=[pltpu.VMEM((HEADS, DH, DH), F32)], sem=("arbitrary",), args=(qkv, gates), comm=comm)


def _gdn_bwd(qkv, gates, s_all, do, *, name, comm=None):
    t = qkv.shape[0]
    n = t // CHUNK

    def body(qkv_ref, gt_ref, sall_ref, do_ref, dqkv_ref, dgt_ref, ds_scr):
        @pl.when(pl.program_id(0) == 0)
        def _():
            ds_scr[...] = jnp.zeros_like(ds_scr)

        gt = gt_ref[...]
        cum = _chunk_cumsum_matrix()
        gcum = jnp.dot(cum, gt, preferred_element_type=F32, precision=HI)
        shape = (CHUNK, LANES)
        dbeta = jnp.zeros(shape, F32)
        dgcum = jnp.zeros(shape, F32)
        for h0 in range(0, HEADS, GDN_HEAD_GROUP):
            grp = pl.ds(h0, GDN_HEAD_GROUP)
            _, vjp = jax.vjp(_gdn_chunk, sall_ref[0, grp], *_gdn_inputs(qkv_ref, gt, gcum, h0))
            ds, dq3, dk3, dv3, db3, dgc3 = vjp((ds_scr[grp], _split_heads(do_ref, 0, h0)))
            ds_scr[grp] = ds
            _store_heads(dqkv_ref, dq3, 0, h0)
            _store_heads(dqkv_ref, dk3, HW, h0)
            _store_heads(dqkv_ref, dv3, 2 * HW, h0)
            dbeta = dbeta + _cols_to_lanes(db3, h0, shape)
            dgcum = dgcum + _cols_to_lanes(dgc3, HEADS + h0, shape)
        dg = lax.dot_general(cum, dgcum, (((0,), (0,)), ((), ())), preferred_element_type=F32, precision=HI)
        dgt_ref[...] = dbeta + dg

    rev = lambda i: n - 1 - i
    return _pcall(
        body, name=name, grid=(n,),
        in_specs=[pl.BlockSpec((CHUNK, 3 * HW), lambda i: (rev(i), 0)), pl.BlockSpec((CHUNK, LANES), lambda i: (rev(i), 0)),
                  pl.BlockSpec((1, HEADS, DH, DH), lambda i: (rev(i), 0, 0, 0)),
                  pl.BlockSpec((CHUNK, HW), lambda i: (rev(i), 0))],
        out_specs=[pl.BlockSpec((CHUNK, 3 * HW), lambda i: (rev(i), 0)), pl.BlockSpec((CHUNK, LANES), lambda i: (rev(i), 0))],
        out_shape=[jax.ShapeDtypeStruct((t, 3 * HW), F32), jax.ShapeDtypeStruct((t, LANES), F32)],
        scratch_shapes=[pltpu.VMEM((HEADS, DH, DH), F32)], sem=("arbitrary",), args=(qkv, gates, s_all, do), comm=comm)


FOX_BLK = 512
NEG = -1e30


def _fox_cumsum(gates, *, name):
    t = gates.shape[0]
    blk = min(FOX_BLK, t)

    def body(g_ref, c_ref):
        r, m = _iota2((blk, blk), 0), _iota2((blk, blk), 1)
        upper = (r <= m).astype(F32)
        carry = jnp.zeros((HEADS, 1), F32)
        for b in range(t // blk):
            lf = g_ref[b * blk:(b + 1) * blk, :].T[2 * HEADS:3 * HEADS, :]
            c_ref[:, b * blk:(b + 1) * blk] = jnp.dot(lf, upper, preferred_element_type=F32, precision=HI) + carry
            carry = carry + jnp.sum(lf, axis=1, keepdims=True)

    return pl.pallas_call(body, name=name, out_shape=jax.ShapeDtypeStruct((HEADS, t), F32),
                          compiler_params=_cparams())(gates)


def _fox_cumsum_bwd(dc, dgates_gdn, *, name):
    t = dc.shape[1]
    blk = min(FOX_BLK, t)

    def body(dc_ref, dg_ref, o_ref):
        r, m = _iota2((blk, blk), 0), _iota2((blk, blk), 1)
        lower = (r >= m).astype(F32)
        carry = jnp.zeros((HEADS, 1), F32)
        for b in reversed(range(t // blk)):
            d = dc_ref[:, b * blk:(b + 1) * blk]
            dlf = jnp.dot(d, lower, preferred_element_type=F32, precision=HI) + carry
            carry = carry + jnp.sum(d, axis=1, keepdims=True)
            tile = jnp.concatenate([jnp.zeros((2 * HEADS, blk), F32), dlf,
                                    jnp.zeros((LANES - 3 * HEADS, blk), F32)], axis=0)
            o_ref[b * blk:(b + 1) * blk, :] = tile.T + dg_ref[b * blk:(b + 1) * blk, :]

    return pl.pallas_call(body, name=name, out_shape=jax.ShapeDtypeStruct((t, LANES), F32),
                          compiler_params=_cparams())(dc, dgates_gdn)


def _fox_logits(q, k, c_row, diagonal):
    s = lax.dot_general(q, k, (((1,), (1,)), ((), ())), preferred_element_type=F32) * (DH ** -0.5) - c_row
    if not diagonal:
        return s
    return jnp.where(_iota2(s.shape, 0) >= _iota2(s.shape, 1), s, NEG)


def _fox_fwd(qn, kn, p_main, c4, *, v_off, name, comm=None):
    t = qn.shape[0]
    blk = min(FOX_BLK, t)
    nb = t // blk
    vb = v_off // DH

    def body(q_ref, k_ref, v_ref, c_ref, o_ref, lse_ref):
        qi = pl.program_id(1)
        q = q_ref[...]

        def step(j, carry, diagonal=False):
            m, l, acc = carry
            rows = pl.ds(pl.multiple_of(j * blk, blk), blk)
            s = _fox_logits(q, k_ref[rows, :], c_ref[0, j], diagonal)
            m_new = jnp.maximum(m, jnp.max(s, axis=1, keepdims=True))
            p = jnp.exp(s - m_new)
            scale = jnp.exp(m - m_new)
            l = scale * l + jnp.sum(p, axis=1, keepdims=True)
            acc = scale * acc + jnp.dot(p.astype(BF16), v_ref[rows, :], preferred_element_type=F32)
            return m_new, l, acc

        init = (jnp.full((blk, 1), NEG, F32), jnp.zeros((blk, 1), F32), jnp.zeros((blk, DH), F32))
        m, l, acc = step(qi, lax.fori_loop(0, qi, step, init), diagonal=True)
        o_ref[...] = (acc / l).astype(o_ref.dtype)
        lse_ref[0] = m + jnp.log(l)

    return _pcall(
        body, name=name, grid=(HEADS, nb),
        in_specs=[pl.BlockSpec((blk, DH), lambda h, i: (i, h)), pl.BlockSpec((t, DH), lambda h, i: (0, h)),
                  pl.BlockSpec((t, DH), lambda h, i: (0, vb + h)), pl.BlockSpec((1, nb, 1, blk), lambda h, i: (h, 0, 0, 0))],
        out_specs=[pl.BlockSpec((blk, DH), lambda h, i: (i, h)), pl.BlockSpec((1, blk, 1), lambda h, i: (h, i, 0))],
        out_shape=[jax.ShapeDtypeStruct((t, HW), BF16), jax.ShapeDtypeStruct((HEADS, t, 1), F32)],
        sem=("parallel", "arbitrary"), args=(qn, kn, p_main, c4), comm=comm)


def _fox_delta(qn, kn, p_main, c4, do, lse, *, v_off, name):
    t = qn.shape[0]
    blk = min(FOX_BLK, t)
    nb = t // blk
    vb = v_off // DH

    def body(q_ref, k_ref, v_ref, c_ref, do_ref, lse_ref, delta_ref):
        qi = pl.program_id(1)
        q = q_ref[...]
        dob = do_ref[...]
        lse = lse_ref[0]

        def step(j, delta, diagonal=False):
            rows = pl.ds(pl.multiple_of(j * blk, blk), blk)
            p = jnp.exp(_fox_logits(q, k_ref[rows, :], c_ref[0, j], diagonal) - lse)
            dp = lax.dot_general(dob, v_ref[rows, :], (((1,), (1,)), ((), ())), preferred_element_type=F32)
            return delta + jnp.sum(p * dp, axis=1, keepdims=True)

        delta_ref[0] = step(qi, lax.fori_loop(0, qi, step, jnp.zeros((blk, 1), F32)), diagonal=True)

    qblk = lambda h, i: (i, h)
    return pl.pallas_call(
        body, name=name, grid=(HEADS, nb),
        in_specs=[pl.BlockSpec((blk, DH), qblk), pl.BlockSpec((t, DH), lambda h, i: (0, h)),
                  pl.BlockSpec((t, DH), lambda h, i: (0, vb + h)), pl.BlockSpec((1, nb, 1, blk), lambda h, i: (h, 0, 0, 0)),
                  pl.BlockSpec((blk, DH), qblk), pl.BlockSpec((1, blk, 1), lambda h, i: (h, i, 0))],
        out_specs=pl.BlockSpec((1, blk, 1), lambda h, i: (h, i, 0)),
        out_shape=jax.ShapeDtypeStruct((HEADS, t, 1), F32),
        compiler_params=_cparams(("parallel", "arbitrary")),
    )(qn, kn, p_main, c4, do, lse)


def _fox_bwd(qn, kn, p_main, c4, delta, do, lse, *, v_off, name, comm=None):
    t = qn.shape[0]
    blk = min(FOX_BLK, t)
    nb = t // blk
    vb = v_off // DH
    tn_dims = (((0,), (0,)), ((), ()))
    nt_dims = (((1,), (1,)), ((), ()))

    def body(q_ref, k_ref, v_ref, c_ref, delta_ref, do_ref, lse_ref, dq_ref, dk_ref, dv_ref, dc_ref):
        kj = pl.program_id(1)

        @pl.when(kj == 0)
        def _():
            dq_ref[...] = jnp.zeros_like(dq_ref)

        k = k_ref[...]
        v = v_ref[...]
        c_row = c_ref[0, 0]

        def step(i, carry, diagonal=False):
            dk, dv, dc = carry
            rows = pl.ds(pl.multiple_of(i * blk, blk), blk)
            q = q_ref[rows, :]
            dob = do_ref[rows, :]
            p = jnp.exp(_fox_logits(q, k, c_row, diagonal) - lse_ref[0, rows, :])
            pb = p.astype(BF16)
            dv = dv + lax.dot_general(pb, dob, tn_dims, preferred_element_type=F32)
            dp = lax.dot_general(dob, v, nt_dims, preferred_element_type=F32)
            ds = p * (dp - delta_ref[0, rows, :])
            dsb = ds.astype(BF16)
            dq_ref[rows, :] += jnp.dot(dsb, k, preferred_element_type=F32) * (DH ** -0.5)
            dk = dk + lax.dot_general(dsb, q, tn_dims, preferred_element_type=F32) * (DH ** -0.5)
            dc = dc - jnp.sum(ds, axis=0, keepdims=True)
            return dk, dv, dc

        init = (jnp.zeros((blk, DH), F32), jnp.zeros((blk, DH), F32), jnp.zeros((1, blk), F32))
        dk, dv, dc = lax.fori_loop(kj + 1, nb, step, step(kj, init, diagonal=True))
        dk_ref[...] = dk
        dv_ref[...] = dv.astype(dv_ref.dtype)
        dc_ref[0, 0] = dc

    full = lambda h, j: (0, h)
    kvb = lambda h, j: (j, h)
    return _pcall(
        body, name=name, grid=(HEADS, nb), sem=("parallel", "arbitrary"), comm=comm,
        args=(qn, kn, p_main, c4, delta, do, lse),
        in_specs=[pl.BlockSpec((t, DH), full), pl.BlockSpec((blk, DH), kvb),
                  pl.BlockSpec((blk, DH), lambda h, j: (j, vb + h)), pl.BlockSpec((1, 1, 1, blk), lambda h, j: (h, j, 0, 0)),
                  pl.BlockSpec((1, t, 1), lambda h, j: (h, 0, 0)), pl.BlockSpec((t, DH), full),
                  pl.BlockSpec((1, t, 1), lambda h, j: (h, 0, 0))],
        out_specs=[pl.BlockSpec((t, DH), full), pl.BlockSpec((blk, DH), kvb), pl.BlockSpec((blk, DH), kvb),
                   pl.BlockSpec((1, 1, 1, blk), lambda h, j: (h, j, 0, 0))],
        out_shape=[jax.ShapeDtypeStruct((t, HW), F32), jax.ShapeDtypeStruct((t, HW), F32),
                   jax.ShapeDtypeStruct((t, HW), BF16), jax.ShapeDtypeStruct((HEADS, nb, 1, blk), F32)])


ANY = pl.BlockSpec(memory_space=pl.ANY)


def _mesh_pos():
    return lax.axis_index("x"), lax.axis_index("y"), lax.axis_index("c")


def _all_gather(blocks, *, name):
    n = len(blocks)

    def body(*refs):
        ins, outs = refs[:n], refs[n:2 * n]
        send, recv, local = refs[2 * n:]
        x, y, c = _mesh_pos()
        me, sibling = (x, y, c), (x, y, 1 - c)
        chips = [(1 - x, y), (x, 1 - y), (1 - x, 1 - y)]

        def copy(t, k, block, to, src=None):
            dst = outs[t].at[4 * block[0] + 2 * block[1] + block[2]]
            return pltpu.make_async_remote_copy(
                src_ref=dst if src is None else src, dst_ref=dst, send_sem=send.at[7 * t + k],
                recv_sem=recv.at[7 * t + k], device_id=to, device_id_type=MESH)

        mine = [pltpu.make_async_copy(ins[t], outs[t].at[4 * x + 2 * y + c], local.at[t]) for t in range(n)]
        for cp in mine:
            cp.start()
        first = []
        for t in range(n):
            first.append(copy(t, 0, me, sibling, src=ins[t]))
            first += [copy(t, 1 + j, me, (*chip, c), src=ins[t]) for j, chip in enumerate(chips)]
        for cp in first:
            cp.start()
        passed = []
        for j, chip in enumerate(chips):
            for t in range(n):
                copy(t, 1 + j, (*chip, c), me).wait_recv()
                fwd = copy(t, 4 + j, (*chip, c), sibling)
                fwd.start()
                passed.append(fwd)
        for t in range(n):
            copy(t, 0, sibling, me).wait_recv()
            for j, chip in enumerate(chips):
                copy(t, 4 + j, (*chip, 1 - c), me).wait_recv()
        for cp in first + passed:
            cp.wait_send()
        for cp in mine:
            cp.wait()

    return pl.pallas_call(
        body, name=name, in_specs=[ANY] * n, out_specs=[ANY] * n,
        out_shape=[jax.ShapeDtypeStruct((N_DEV,) + b.shape, b.dtype) for b in blocks],
        scratch_shapes=[pltpu.SemaphoreType.DMA((7 * n,)), pltpu.SemaphoreType.DMA((7 * n,)),
                        pltpu.SemaphoreType.DMA((n,))],
    )(*blocks)


def _comm_call(comm, *, name):
    ci, co = len(comm.ins), len(comm.out_shapes)

    def body(*refs):
        comm.start(refs[:ci], refs[ci:ci + co], refs[ci + co:])
        comm.finish(refs[:ci], refs[ci:ci + co], refs[ci + co:])

    return pl.pallas_call(body, name=name, in_specs=[ANY] * ci, out_specs=[ANY] * co, out_shape=comm.out_shapes,
                          scratch_shapes=comm.sems, input_output_aliases=comm.aliases)(*comm.ins)


def _ag_first_comm(shards):
    n = len(shards)

    def copies(cin, cout, sems):
        send, recv, local = sems
        x, y, c = _mesh_pos()
        peers = [(x, y, 1 - c), (1 - x, y, c), (x, 1 - y, c), (1 - x, 1 - y, c)]
        slot = lambda p: 4 * p[0] + 2 * p[1] + p[2]
        mine, out, inc = [], [], []
        for t in range(n):
            mine.append(pltpu.make_async_copy(cin[t], cout[t].at[slot((x, y, c))], local.at[t]))
            for k, peer in enumerate(peers):
                sems_k = dict(send_sem=send.at[4 * t + k], recv_sem=recv.at[4 * t + k], device_id=peer,
                              device_id_type=MESH)
                out.append(pltpu.make_async_remote_copy(src_ref=cin[t], dst_ref=cout[t].at[slot((x, y, c))], **sems_k))
                inc.append(pltpu.make_async_remote_copy(src_ref=cout[t].at[slot(peer)], dst_ref=cout[t].at[slot(peer)],
                                                        **sems_k))
        return mine, out, inc

    def start(cin, cout, sems):
        mine, out, _ = copies(cin, cout, sems)
        for cp in mine + out:
            cp.start()

    def finish(cin, cout, sems):
        mine, out, inc = copies(cin, cout, sems)
        for cp in inc:
            cp.wait_recv()
        for cp in out:
            cp.wait_send()
        for cp in mine:
            cp.wait()

    return _Comm(shards, [jax.ShapeDtypeStruct((N_DEV,) + s.shape, s.dtype) for s in shards],
                 [pltpu.SemaphoreType.DMA((4 * n,)), pltpu.SemaphoreType.DMA((4 * n,)), pltpu.SemaphoreType.DMA((n,))],
                 start, finish)


def _ag_forward(gathered, *, name):
    n = len(gathered)

    def body(*refs):
        outs = refs[n:2 * n]
        send, recv = refs[2 * n:]
        x, y, c = _mesh_pos()
        chips = [(1 - x, y), (x, 1 - y), (1 - x, 1 - y)]
        fwd, inc = [], []
        for t in range(n):
            for j, (px, py) in enumerate(chips):
                sems_j = dict(send_sem=send.at[3 * t + j], recv_sem=recv.at[3 * t + j], device_id=(x, y, 1 - c),
                              device_id_type=MESH)
                mine, theirs = outs[t].at[4 * px + 2 * py + c], outs[t].at[4 * px + 2 * py + 1 - c]
                fwd.append(pltpu.make_async_remote_copy(src_ref=mine, dst_ref=mine, **sems_j))
                inc.append(pltpu.make_async_remote_copy(src_ref=theirs, dst_ref=theirs, **sems_j))
        for cp in fwd:
            cp.start()
        for cp in inc:
            cp.wait_recv()
        for cp in fwd:
            cp.wait_send()

    return pl.pallas_call(
        body, name=name, in_specs=[ANY] * n, out_specs=[ANY] * n,
        out_shape=[jax.ShapeDtypeStruct(g.shape, g.dtype) for g in gathered],
        scratch_shapes=[pltpu.SemaphoreType.DMA((3 * n,)), pltpu.SemaphoreType.DMA((3 * n,))],
        input_output_aliases={t: t for t in range(n)},
    )(*gathered)


def _rs_sibling(grads, *, name):
    n = len(grads)

    def body(*refs):
        ins, outs = refs[:n], refs[n:2 * n]
        send, recv = refs[2 * n:]
        x, y, c = _mesh_pos()
        copies = []
        for t in range(n):
            for q in range(4):
                copies.append(pltpu.make_async_remote_copy(
                    src_ref=ins[t].at[2 * q + (1 - c)], dst_ref=outs[t].at[q], send_sem=send.at[4 * t + q],
                    recv_sem=recv.at[4 * t + q], device_id=(x, y, 1 - c), device_id_type=MESH))
        for cp in copies:
            cp.start()
        for cp in copies:
            cp.wait_recv()
        for cp in copies:
            cp.wait_send()

    return pl.pallas_call(
        body, name=name, in_specs=[ANY] * n, out_specs=[ANY] * n,
        out_shape=[jax.ShapeDtypeStruct((4,) + g.shape[1:], g.dtype) for g in grads],
        scratch_shapes=[pltpu.SemaphoreType.DMA((4 * n,)), pltpu.SemaphoreType.DMA((4 * n,))],
    )(*grads)


def _rs_chips_comm(parts):
    n = len(parts)

    def copies(cin, cout, sems):
        send, recv, local = sems
        x, y, c = _mesh_pos()
        my_chip = 2 * x + y
        mine = [pltpu.make_async_copy(cin[t].at[my_chip], cout[t].at[my_chip], local.at[t]) for t in range(n)]
        sends, lands = [], []
        for t in range(n):
            for k, (px, py) in enumerate([(1 - x, y), (x, 1 - y), (1 - x, 1 - y)]):
                sems_k = dict(send_sem=send.at[3 * t + k], recv_sem=recv.at[3 * t + k], device_id=(px, py, c),
                              device_id_type=MESH)
                sends.append(pltpu.make_async_remote_copy(src_ref=cin[t].at[2 * px + py], dst_ref=cout[t].at[my_chip],
                                                          **sems_k))
                lands.append(pltpu.make_async_remote_copy(src_ref=cout[t].at[2 * px + py],
                                                          dst_ref=cout[t].at[2 * px + py], **sems_k))
        return mine, sends, lands

    def start(cin, cout, sems):
        mine, sends, _ = copies(cin, cout, sems)
        for cp in mine + sends:
            cp.start()

    def finish(cin, cout, sems):
        mine, sends, lands = copies(cin, cout, sems)
        for cp in lands:
            cp.wait_recv()
        for cp in sends:
            cp.wait_send()
        for cp in mine:
            cp.wait()

    return _Comm(parts, [jax.ShapeDtypeStruct(p.shape, p.dtype) for p in parts],
                 [pltpu.SemaphoreType.DMA((3 * n,)), pltpu.SemaphoreType.DMA((3 * n,)), pltpu.SemaphoreType.DMA((n,))],
                 start, finish)


def _row_tile(r, c, itemsize, budget=3 * 1024 * 1024):
    best = None
    for tr in range(16, r + 1, 16):
        if r % tr == 0 and tr * c * itemsize <= budget:
            best = tr
    return best or r


def _pair_sum(grad, land, *, name):
    _, r, c = grad.shape
    tr = _row_tile(r, c, 2)

    def body(g_ref, l_ref, o_ref):
        o_ref[...] = (g_ref[...].astype(F32) + l_ref[...].astype(F32)).astype(o_ref.dtype)

    return pl.pallas_call(
        body, name=name, grid=(4, r // tr),
        in_specs=[pl.BlockSpec((1, tr, c), lambda q, i: (2 * q + lax.axis_index("c"), i, 0)),
                  pl.BlockSpec((1, tr, c), lambda q, i: (q, i, 0))],
        out_specs=pl.BlockSpec((1, tr, c), lambda q, i: (q, i, 0)),
        out_shape=jax.ShapeDtypeStruct((4, r, c), grad.dtype),
        compiler_params=_cparams(("parallel", "parallel")),
    )(grad, land)


def _adamw_math(w, g, m, v):
    m = ADAM_B1 * m + (1.0 - ADAM_B1) * g
    v = ADAM_B2 * v + (1.0 - ADAM_B2) * jnp.square(g)
    m_hat = m / (1.0 - ADAM_B1 ** ADAM_STEP)
    v_hat = v / (1.0 - ADAM_B2 ** ADAM_STEP)
    delta = -ADAM_LR * (m_hat / (jnp.sqrt(v_hat) + ADAM_EPS) + ADAM_WD * w)
    return delta, m, v


def _adamw(parts, w, m, v, *, name):
    s, _, cp = parts.shape
    r, c = w.shape
    tr = _row_tile(r, cp, 4, budget=1024 * 1024)

    def body(p_ref, w_ref, m_ref, v_ref, g_ref, d_ref, nm_ref, nv_ref):
        g = p_ref[0].astype(F32)
        for i in range(1, s):
            g = g + p_ref[i].astype(F32)
        g = g[:, :c]
        delta, nm, nv = _adamw_math(w_ref[...], g, m_ref[...], v_ref[...])
        g_ref[...] = g
        d_ref[...] = delta
        nm_ref[...] = nm
        nv_ref[...] = nv

    blk = pl.BlockSpec((tr, c), lambda i: (i, 0))
    return pl.pallas_call(
        body, name=name, grid=(r // tr,),
        in_specs=[pl.BlockSpec((s, tr, cp), lambda i: (0, i, 0)), blk, blk, blk],
        out_specs=[blk] * 4, out_shape=[jax.ShapeDtypeStruct((r, c), F32)] * 4,
        compiler_params=_cparams(("parallel",)),
    )(parts, w, m, v)


def _w_in_pieces(d, nb, sources):
    segs = [(0, 4 * HW, False, 0), (4 * HW, 4 * HW + 2 * HEADS, True, 0),
            (4 * HW + 2 * HEADS, 7 * HW + 2 * HEADS, False, 4 * HW),
            (7 * HW + 2 * HEADS, 7 * HW + 3 * HEADS, True, 2 * HEADS),
            (7 * HW + 3 * HEADS, 7 * HW + 3 * HEADS + 2 * d, False, 7 * HW)]
    out = []
    for dev in range(N_DEV):
        lo, hi = dev * nb, (dev + 1) * nb
        for s0, s1, is_small, a0 in segs:
            p, q = max(lo, s0), min(hi, s1)
            if p >= q:
                continue
            a, b = a0 + p - s0, a0 + q - s0
            if is_small:
                out.append((dev, p - lo, q - lo, len(sources), a, b))
                continue
            for si, (start, width) in enumerate(sources):
                u, v = max(a, start), min(b, start + width)
                if u < v:
                    out.append((dev, p - lo + (u - a), p - lo + (v - a), si, u - start, v - start))
    return out


def _concat_cols(parts, *, name):
    t = parts[0].shape[0]
    n = len(parts)
    offs = [sum(p.shape[1] for p in parts[:i]) for i in range(n)]
    tm = min(128, t)

    def body(*refs):
        for i in range(n):
            refs[n][:, offs[i]:offs[i] + parts[i].shape[1]] = refs[i][...]

    return pl.pallas_call(
        body, name=name, grid=(t // tm,),
        in_specs=[pl.BlockSpec((tm, p.shape[1]), lambda i: (i, 0)) for p in parts],
        out_specs=pl.BlockSpec((tm, offs[-1] + parts[-1].shape[1]), lambda i: (i, 0)),
        out_shape=jax.ShapeDtypeStruct((t, offs[-1] + parts[-1].shape[1]), parts[0].dtype),
        compiler_params=_cparams(("parallel",)))(*parts)


def _w_in_to_aligned(g_in, *, name):
    _, d, nb = g_in.shape
    n_main = 7 * HW + 2 * d
    tr = min(128, d)
    pieces = _w_in_pieces(d, nb, [(0, n_main)])

    def body(g_ref, main_ref, small_ref):
        small_ref[...] = jnp.zeros_like(small_ref)
        for dev, s, e, src, a, b in pieces:
            dst = main_ref if src == 0 else small_ref
            dst[:, a:b] = g_ref[dev, :, s:e]

    return pl.pallas_call(
        body, name=name, grid=(d // tr,), in_specs=[pl.BlockSpec((N_DEV, tr, nb), lambda i: (0, i, 0))],
        out_specs=[pl.BlockSpec((tr, n_main), lambda i: (i, 0)), pl.BlockSpec((tr, LANES), lambda i: (i, 0))],
        out_shape=[jax.ShapeDtypeStruct((d, n_main), g_in.dtype), jax.ShapeDtypeStruct((d, LANES), g_in.dtype)],
        compiler_params=_cparams(("parallel",)),
    )(g_in)


def _w_in_grad_blocks(seg_grads, small_grad, sources, nb, *, name):
    d = small_grad.shape[0]
    tr = min(128, d)
    pieces = _w_in_pieces(d, nb, sources)
    ns = len(seg_grads)

    def body(*refs):
        o_ref = refs[ns + 1]
        for dev, s, e, src, a, b in pieces:
            o_ref[dev, :, s:e] = refs[src][:, a:b]

    return pl.pallas_call(
        body, name=name, grid=(d // tr,),
        in_specs=[pl.BlockSpec((tr, g.shape[1]), lambda i: (i, 0)) for g in seg_grads + [small_grad]],
        out_specs=pl.BlockSpec((N_DEV, tr, nb), lambda i: (0, i, 0)),
        out_shape=jax.ShapeDtypeStruct((N_DEV, d, nb), small_grad.dtype),
        compiler_params=_cparams(("parallel",)),
    )(*seg_grads, small_grad)


def _pad_cols(a, n):
    return a if a.shape[1] == n else jnp.concatenate([a, jnp.zeros((a.shape[0], n - a.shape[1]), a.dtype)], axis=1)


def _pad_rows(a, n):
    return a if a.shape[0] == n else jnp.concatenate([a, jnp.zeros((n - a.shape[0], a.shape[1]), a.dtype)], axis=0)


class _StaticPlan:
    def __init__(self, weights, cp):
        self.w, self.cp, self.grads = weights, cp, {}

    def comm_for(self, key):
        return None

    def done(self, key, res):
        pass

    def weight(self, name):
        return self.w[name]

    def grad(self, name, g):
        self.grads[name] = g

    def grad_w_in(self, g_main, g_small):
        self.grads["w_main"], self.grads["w_small"] = g_main, g_small


class _FsdpPlan:
    AG_RIDES = {"in_proj": ("wa", "wb", "wout", "wg"), "gdn_fwd": ("wu",), "fox_fwd": ("wd",)}
    RS_RIDES = ("d_hn_gate", "gdn_bwd", "d_xn")

    def __init__(self, shards, d, cp, nb):
        self.shards, self.d, self.cp, self.nb = shards, d, cp, nb
        self.first, self.full = {}, {}
        self.queue, self.flying, self.slots = [], [], {}

    def comm_for(self, key):
        if key in self.AG_RIDES:
            return _ag_first_comm([self.shards[n] for n in self.AG_RIDES[key]])
        if key in self.RS_RIDES and self.queue:
            self.flying, self.queue = self.queue, []
            return _rs_chips_comm([p for _, p in self.flying])
        return None

    def done(self, key, res):
        if key in self.AG_RIDES:
            self.first[key] = list(res)
        else:
            self.slots.update((n, s) for (n, _), s in zip(self.flying, res))
            self.flying = []

    def weight(self, name):
        if name not in self.full:
            key = next(k for k, names in self.AG_RIDES.items() if name in names)
            outs = _ag_forward(self.first[key], name=f"all_gather_forward_{key}")
            self.full.update(zip(self.AG_RIDES[key], outs))
        g = self.full[name]
        if name in ("wa", "wb"):
            return _cols_of_blocks(g)
        if name == "wout":
            return g.reshape(self.d, self.d)
        if name == "wd":
            return g.reshape(N_DEV * self.cp, self.d)
        return g

    def grad(self, name, g):
        if name == "wout":
            g = g.reshape(N_DEV, self.d // N_DEV, self.d)
        if name == "wd":
            g = g.reshape(N_DEV, self.cp, self.d)
        if name == "conv":
            g = _blocks_of_cols(g.astype(BF16))
        self.reduce(name, g)

    def grad_w_in(self, g_main, g_small):
        self.reduce("w_in", _w_in_grad_blocks([g_main], g_small, [(0, g_main.shape[1])], self.nb, name="w_in_grad_blocks"))

    def reduce(self, name, blocks):
        (land,) = _rs_sibling([blocks], name=f"grads_to_sibling_{name}")
        self.queue.append((name, _pair_sum(blocks, land, name=f"pair_sum_{name}")))

    def flush(self):
        if self.queue:
            outs = _comm_call(_rs_chips_comm([p for _, p in self.queue]), name="grads_to_chips_tail")
            self.slots.update((n, s) for (n, _), s in zip(self.queue, outs))
            self.queue = []


def _carried(plan, key, fn, *args, **kw):
    comm = plan.comm_for(key)
    if comm is None:
        return fn(*args, **kw)
    res, comm_res = fn(*args, comm=comm, **kw)
    plan.done(key, comm_res)
    return res


def _local_step(x, target, w_main, w_small, conv_w, plan,
                norm_mix_w, norm_ffn_w, gdn_norm_w, fox_q_w, fox_k_w, a_row, b_row):
    t, d = x.shape
    cp = plan.cp
    fp = N_DEV * cp
    n_main = w_main.shape[1]
    off_gb = OFF_GA + d
    tm = 1024
    rt = 128
    fcol = fp // 1024 if fp % 1024 == 0 else max(fp // 512, 1)

    (xn,) = _rowwise_fwd(_fn_norm, [(x, 0, d)], [norm_mix_w], [(d, BF16)], tm=rt, name="mix_norm")
    p_main = _carried(plan, "in_proj", _mm, xn, w_main, mode="nn", m=t, n=n_main, k=d, tm=tm, tn=512, tk=d,
                      out_dtype=BF16, name="in_proj")
    p_small = _mm(xn, w_small, mode="nn", m=t, n=LANES, k=d, tm=tm, tn=LANES, tk=d, out_dtype=F32, name="in_proj_small")
    (gates,) = _rowwise_fwd(_fn_gates, [(p_small, 0, LANES)], [a_row, b_row], [(LANES, F32)], tm=512, name="gates")
    qkv = _conv_fwd(p_main, conv_w, width=3 * HW, name="conv_fwd")
    o_gdn, s_all = _carried(plan, "gdn_fwd", _gdn_fwd, qkv, gates, name="gdn_fwd")
    gdn_rows = [(o_gdn, 0, HW), (p_main, OFF_ZA, HW)]
    (oa,) = _rowwise_fwd(_fn_gdn_out, gdn_rows, [gdn_norm_w], [(HW, BF16)], tm=512, ncol=HEADS, name="gdn_out")
    wa = plan.weight("wa")
    ya = _mm(oa, wa, mode="nn", m=t, n=d, k=HW, tm=tm, tn=1024, tk=HW, out_dtype=BF16, name="branch_a")
    qk_rows = [(p_main, OFF_QB, HW), (p_main, OFF_KB, HW)]
    qn, kn = _rowwise_fwd(_fn_qknorm, qk_rows, [fox_q_w, fox_k_w], [(HW, BF16), (HW, BF16)], tm=512, ncol=HEADS,
                          name="fox_qk_norm")
    blk = min(FOX_BLK, t)
    c4 = _fox_cumsum(gates, name="fox_cumsum").reshape(HEADS, t // blk, 1, blk)
    ob, lse = _carried(plan, "fox_fwd", _fox_fwd, qn, kn, p_main, c4, v_off=OFF_VB, name="fox_fwd")
    wb = plan.weight("wb")
    yb = _mm(ob, wb, mode="nn", m=t, n=d, k=HW, tm=tm, tn=1024, tk=HW, out_dtype=BF16, name="branch_b")
    mcol = 2 if d >= 2 * HW else 1
    merge_rows = [(p_main, OFF_GA, d), (p_main, off_gb, d), (ya, 0, d), (yb, 0, d)]
    (merged,) = _rowwise_fwd(_fn_merge, merge_rows, [], [(d, BF16)], tm=256, ncol=mcol, name="merge")
    wout = plan.weight("wout")
    h = _mm(merged, wout, mode="nn", m=t, n=d, k=d, tm=tm, tn=512, tk=d, out_dtype=F32, add=x, name="out_proj")
    (hn,) = _rowwise_fwd(_fn_norm, [(h, 0, d)], [norm_ffn_w], [(d, BF16)], tm=rt, name="ffn_norm")
    wg, wu = plan.weight("wg"), plan.weight("wu")
    gate = _mm(hn, wg, mode="nn", m=t, n=fp, k=d, tm=512, tn=cp, tk=d, out_dtype=BF16, b_blocked=True, name="ffn_gate")
    up = _mm(hn, wu, mode="nn", m=t, n=fp, k=d, tm=512, tn=cp, tk=d, out_dtype=BF16, b_blocked=True, name="ffn_up")
    (act,) = _rowwise_fwd(_fn_swiglu, [(gate, 0, fp), (up, 0, fp)], [], [(fp, BF16)], tm=512, ncol=fcol, name="swiglu")
    wd = plan.weight("wd")
    y = _mm(act, wd, mode="nn", m=t, n=d, k=fp, tm=512, tn=256, tk=fp, out_dtype=F32, add=h, name="ffn_down")
    dy, dyb, loss_row = _loss_head(y, target, tm=rt, name="loss_head")

    dact = _mm(dyb, wd, mode="nt", m=t, n=fp, k=d, tm=tm, tn=512, tk=d, out_dtype=BF16, name="d_act")
    plan.grad("wd", _mm(act, dyb, mode="tn", m=fp, n=d, k=t, tm=512, tn=1024, tk=t, out_dtype=BF16, name="dw_ffn_down"))
    dgate, dup = _rowwise_bwd(_fn_swiglu, [(gate, 0, fp), (up, 0, fp)], [], [dact], [BF16, BF16], tm=512, ncol=fcol,
                              name="d_swiglu")
    dhn = _carried(plan, "d_hn_gate", _mm, dgate, wg, mode="nt", m=t, n=d, k=fp, tm=tm, tn=1024, tk=cp, out_dtype=F32,
                   b_blocked=True, name="d_hn_gate")
    dhn = _mm(dup, wu, mode="nt", m=t, n=d, k=fp, tm=tm, tn=1024, tk=cp, out_dtype=F32, add=dhn, b_blocked=True,
              name="d_hn_up")
    plan.grad("wg", _mm(hn, dgate, mode="tn", m=d, n=fp, k=t, tm=512, tn=cp, tk=t, out_dtype=BF16, out_blocked=True,
                        name="dw_ffn_gate"))
    plan.grad("wu", _mm(hn, dup, mode="tn", m=d, n=fp, k=t, tm=512, tn=cp, tk=t, out_dtype=BF16, out_blocked=True,
                        name="dw_ffn_up"))
    dh, d_norm_ffn = _rowwise_bwd(_fn_norm, [(h, 0, d)], [norm_ffn_w], [dhn], [F32], tm=rt, name="d_ffn_norm", adds=[dy])
    dmerged = _mm(dh, wout, mode="nt", m=t, n=d, k=d, tm=512, tn=512, tk=d, out_dtype=BF16, name="d_merged")
    plan.grad("wout", _mm(merged, dh, mode="tn", m=d, n=d, k=t, tm=512, tn=512, tk=t, out_dtype=BF16, name="dw_out"))
    dga, dgb, dya, dyb2 = _rowwise_bwd(_fn_merge, merge_rows, [], [dmerged], [BF16] * 4, tm=256, ncol=mcol, name="d_merge")
    doa = _mm(dya, wa, mode="nt", m=t, n=HW, k=d, tm=tm, tn=512, tk=d, out_dtype=BF16, name="d_oa")
    plan.grad("wa", _mm(oa, dya, mode="tn", m=HW, n=d, k=t, tm=1024, tn=d // N_DEV, tk=t, out_dtype=BF16,
                        out_blocked=True, name="dw_branch_a"))
    dob = _mm(dyb2, wb, mode="nt", m=t, n=HW, k=d, tm=tm, tn=512, tk=d, out_dtype=BF16, name="d_ob")
    plan.grad("wb", _mm(ob, dyb2, mode="tn", m=HW, n=d, k=t, tm=1024, tn=d // N_DEV, tk=t, out_dtype=BF16,
                        out_blocked=True, name="dw_branch_b"))
    do_gdn, dza, d_gdn_norm = _rowwise_bwd(_fn_gdn_out, gdn_rows, [gdn_norm_w], [doa], [F32, BF16], tm=512,
                                           ncol=HEADS, name="d_gdn_out")
    dqkv, dgates_gdn = _carried(plan, "gdn_bwd", _gdn_bwd, qkv, gates, s_all, do_gdn, name="gdn_bwd")
    dp_qkv, dconv = _conv_bwd(p_main, conv_w, dqkv, width=3 * HW, name="conv_bwd")
    plan.grad("conv", dconv)
    delta = _fox_delta(qn, kn, p_main, c4, dob, lse, v_off=OFF_VB, name="fox_delta")
    dqn, dkn, dvb, dc4 = _fox_bwd(qn, kn, p_main, c4, delta, dob, lse, v_off=OFF_VB, name="fox_bwd")
    dqb, dkb, d_fox_q, d_fox_k = _rowwise_bwd(_fn_qknorm, qk_rows, [fox_q_w, fox_k_w], [dqn, dkn], [BF16, BF16],
                                              tm=512, ncol=HEADS, name="d_fox_qk_norm")
    dgates = _fox_cumsum_bwd(dc4.reshape(HEADS, t), dgates_gdn, name="fox_cumsum_bwd")
    dsmall, d_a_row, d_b_row = _rowwise_bwd(_fn_gates, [(p_small, 0, LANES)], [a_row, b_row], [dgates], [F32],
                                            tm=512, name="d_gates")
    dp_main = _concat_cols([dp_qkv, dza, dqb, dkb, dvb, dga, dgb], name="d_p_main")
    plan.grad_w_in(_mm(xn, dp_main, mode="tn", m=d, n=n_main, k=t, tm=1024, tn=math.gcd(n_main, 1024), tk=t,
                       out_dtype=BF16, name="dw_in"),
                   _mm(xn, dsmall, mode="tn", m=d, n=LANES, k=t, tm=1024, tn=LANES, tk=t, out_dtype=BF16,
                       name="dw_in_small"))
    dxn = _mm(dsmall, w_small, mode="nt", m=t, n=d, k=LANES, tm=tm, tn=1024, tk=LANES, out_dtype=F32, name="d_xn_small")
    dxn = _carried(plan, "d_xn", _mm, dp_main, w_main, mode="nt", m=t, n=d, k=n_main, tm=tm, tn=1024,
                   tk=math.gcd(n_main, 2048),
                   out_dtype=F32, add=dxn, name="d_xn")
    grad_x, d_norm_mix = _rowwise_bwd(_fn_norm, [(x, 0, d)], [norm_mix_w], [dxn], [F32], tm=rt, name="d_mix_norm",
                                      adds=[dh])
    small = dict(norm_mix=d_norm_mix, norm_ffn=d_norm_ffn, gdn_norm=d_gdn_norm, fox_q=d_fox_q, fox_k=d_fox_k,
                 a_row=d_a_row, b_row=d_b_row)
    return loss_row[0, 0], grad_x, small


def _lane_row(pieces):
    row = jnp.zeros((1, LANES), F32)
    for off, p in pieces:
        row = lax.dynamic_update_slice(row, p.astype(F32), (0, off))
    return row


def _pack_small(norm_mix, norm_ffn, gdn_norm, fox_q, fox_k, a_log, dt_bias, b_f):
    rows = [norm_mix.reshape(-1, LANES), norm_ffn.reshape(-1, LANES), gdn_norm, fox_q, fox_k,
            _lane_row([(HEADS, a_log)]), _lane_row([(HEADS, dt_bias), (2 * HEADS, b_f)])]
    packed = jnp.concatenate(rows, axis=0)
    return _pad_rows(packed, -(-packed.shape[0] // 8) * 8)


def _unpack_small(p, d):
    nd = d // LANES
    r = 2 * nd
    return (p[0:nd].reshape(1, d), p[r + 3:r + 4, HEADS:2 * HEADS], p[r + 4:r + 5, HEADS:2 * HEADS], p[r:r + 1],
            p[r + 4:r + 5, 2 * HEADS:3 * HEADS], p[r + 1:r + 2], p[r + 2:r + 3], p[nd:r].reshape(1, d))


def _blocks_of_cols(a):
    r, c8 = a.shape
    return a.reshape(r, N_DEV, c8 // N_DEV).transpose(1, 0, 2)


def _cols_of_blocks(g):
    _, r, c = g.shape
    return g.transpose(1, 0, 2).reshape(r, N_DEV * c)


def kernel(x, norm_mix_w, w_in, conv_w, a_log, dt_bias, gdn_norm_w, fox_b_f, fox_q_norm_w, fox_k_norm_w, w_branch_a, w_branch_b, w_out, norm_ffn_w, w_ffn_gate, w_ffn_up, w_ffn_down, loss_target, m_norm_mix_w, m_w_in, m_conv_w, m_a_log, m_dt_bias, m_gdn_norm_w, m_fox_b_f, m_fox_q_norm_w, m_fox_k_norm_w, m_w_branch_a, m_w_branch_b, m_w_out, m_norm_ffn_w, m_w_ffn_gate, m_w_ffn_up, m_w_ffn_down, v_norm_mix_w, v_w_in, v_conv_w, v_a_log, v_dt_bias, v_gdn_norm_w, v_fox_b_f, v_fox_q_norm_w, v_fox_k_norm_w, v_w_branch_a, v_w_branch_b, v_w_out, v_norm_ffn_w, v_w_ffn_gate, v_w_ffn_up, v_w_ffn_down):
    d = x.shape[-1]
    cp = -(-w_ffn_down.shape[1] // LANES) * LANES
    nb = w_in.shape[2]

    g_in, g_conv = _all_gather([w_in[0].astype(BF16), conv_w[0]], name="w_in_all_gather")
    w_main, w_small = _w_in_to_aligned(g_in, name="w_in_to_aligned")
    conv_full = _cols_of_blocks(g_conv)
    plan = _FsdpPlan(dict(wa=w_branch_a[0].astype(BF16), wb=w_branch_b[0].astype(BF16), wout=w_out[0].astype(BF16),
                          wg=_pad_cols(w_ffn_gate[0].astype(BF16), cp), wu=_pad_cols(w_ffn_up[0].astype(BF16), cp),
                          wd=_pad_rows(w_ffn_down[0].astype(BF16), cp)), d, cp, nb)
    a_row = _lane_row([(HEADS, a_log)])
    b_row = _lane_row([(HEADS, dt_bias), (2 * HEADS, fox_b_f)])

    loss_part, grad_x, gs = _local_step(
        x[0], loss_target[0], w_main, w_small, conv_full, plan,
        norm_mix_w, norm_ffn_w, gdn_norm_w, fox_q_norm_w, fox_k_norm_w, a_row, b_row)
    loss = lax.psum(loss_part, ("x", "y", "c"))

    plan.flush()
    big = dict(w_in=("w_in", w_in, m_w_in, v_w_in), w_branch_a=("wa", w_branch_a, m_w_branch_a, v_w_branch_a),
               w_branch_b=("wb", w_branch_b, m_w_branch_b, v_w_branch_b), w_out=("wout", w_out, m_w_out, v_w_out),
               w_ffn_gate=("wg", w_ffn_gate, m_w_ffn_gate, v_w_ffn_gate), w_ffn_up=("wu", w_ffn_up, m_w_ffn_up, v_w_ffn_up),
               w_ffn_down=("wd", w_ffn_down, m_w_ffn_down, v_w_ffn_down), conv_w=("conv", conv_w, m_conv_w, v_conv_w))
    res = {}
    for nm, (key, w, m, v) in big.items():
        res[nm] = [o[None] for o in _adamw(plan.slots[key], w[0], m[0], v[0], name=f"adamw_{nm}")]

    g_small = _pack_small(gs["norm_mix"], gs["norm_ffn"], gs["gdn_norm"], gs["fox_q"], gs["fox_k"],
                          gs["a_row"][:, HEADS:2 * HEADS], gs["b_row"][:, HEADS:2 * HEADS],
                          gs["b_row"][:, 2 * HEADS:3 * HEADS])
    (g_small_all,) = _all_gather([g_small], name="small_grads_all_gather")
    w_small_p = _pack_small(norm_mix_w, norm_ffn_w, gdn_norm_w, fox_q_norm_w, fox_k_norm_w, a_log, dt_bias, fox_b_f)
    m_small_p = _pack_small(m_norm_mix_w, m_norm_ffn_w, m_gdn_norm_w, m_fox_q_norm_w, m_fox_k_norm_w, m_a_log,
                            m_dt_bias, m_fox_b_f)
    v_small_p = _pack_small(v_norm_mix_w, v_norm_ffn_w, v_gdn_norm_w, v_fox_q_norm_w, v_fox_k_norm_w, v_a_log,
                            v_dt_bias, v_fox_b_f)
    small_res = [_unpack_small(o, d) for o in _adamw(g_small_all, w_small_p, m_small_p, v_small_p, name="adamw_small")]

    def group(k):
        s = small_res[k]
        return [s[0], res["w_in"][k], res["conv_w"][k], s[1], s[2], s[3], s[4], s[5], s[6], res["w_branch_a"][k],
                res["w_branch_b"][k], res["w_out"][k], s[7], res["w_ffn_gate"][k], res["w_ffn_up"][k],
                res["w_ffn_down"][k]]

    return (loss, grad_x[None], *group(0), *group(1), *group(2), *group(3))
```

```python
import functools
import math

import jax
import jax.numpy as jnp
from jax import lax
from jax.experimental import pallas as pl
from jax.experimental.pallas import tpu as pltpu

F32 = jnp.float32
BF16 = jnp.bfloat16
HI = lax.Precision.HIGHEST
SOLVE_PRECISION = lax.Precision.HIGH
MESH = pl.DeviceIdType.MESH

EPS = 1e-6
HEADS = 16
DH = 128
HW = HEADS * DH
CHUNK = 64
CONV_K = 4
N_DEV = 8
LANES = 128
VMEM_LIMIT = 52 * 1024 * 1024

ADAM_LR = 0.001
ADAM_B1 = 0.9
ADAM_B2 = 0.999
ADAM_EPS = 1e-08
ADAM_WD = 0.01
ADAM_STEP = 10

OFF_QA, OFF_KA, OFF_VA, OFF_ZA, OFF_QB, OFF_KB, OFF_VB, OFF_GA = 0, HW, 2 * HW, 3 * HW, 4 * HW, 5 * HW, 6 * HW, 7 * HW


def _cparams(sem=None, vmem=VMEM_LIMIT):
    return pltpu.CompilerParams(dimension_semantics=sem, vmem_limit_bytes=vmem)


class _Comm:
    def __init__(self, ins, out_shapes, sems, start, finish, aliases=None):
        self.ins, self.out_shapes, self.sems = list(ins), list(out_shapes), list(sems)
        self.start, self.finish, self.aliases = start, finish, dict(aliases or {})


def _pcall(body, *, name, grid, in_specs, out_specs, out_shape, args, sem, scratch_shapes=(), comm=None):
    multi = isinstance(out_shape, (list, tuple))
    out_specs = list(out_specs) if multi else [out_specs]
    out_shape = list(out_shape) if multi else [out_shape]
    scratch_shapes = list(scratch_shapes)
    if comm is None:
        res = pl.pallas_call(body, name=name, grid=grid, in_specs=list(in_specs), out_specs=out_specs,
                             out_shape=out_shape, scratch_shapes=scratch_shapes, compiler_params=_cparams(sem))(*args)
        return res if multi else res[0]
    ni, no, ns = len(in_specs), len(out_specs), len(scratch_shapes)
    ci, co = len(comm.ins), len(comm.out_shapes)

    def wrapped(*refs):
        cin = refs[ni:ni + ci]
        outs = refs[ni + ci:ni + ci + no]
        cout = refs[ni + ci + no:ni + ci + no + co]
        scr = refs[ni + ci + no + co:ni + ci + no + co + ns]
        csem = refs[ni + ci + no + co + ns:]
        ids = [pl.program_id(ax) for ax in range(len(grid))]
        first = functools.reduce(jnp.logical_and, [i == 0 for i in ids])
        last = functools.reduce(jnp.logical_and, [i == g - 1 for i, g in zip(ids, grid)])

        @pl.when(first)
        def _():
            comm.start(cin, cout, csem)

        body(*refs[:ni], *outs, *scr)

        @pl.when(last)
        def _():
            comm.finish(cin, cout, csem)

    any_spec = pl.BlockSpec(memory_space=pl.ANY)
    res = pl.pallas_call(
        wrapped, name=name, grid=grid, in_specs=list(in_specs) + [any_spec] * ci,
        out_specs=out_specs + [any_spec] * co, out_shape=out_shape + comm.out_shapes,
        scratch_shapes=scratch_shapes + comm.sems,
        input_output_aliases={ni + i: no + o for i, o in comm.aliases.items()},
        compiler_params=_cparams(("arbitrary",) * len(grid)))(*args, *comm.ins)
    return (res[:no] if multi else res[0]), res[no:]


def _mm(a, b, *, mode, m, n, k, tm, tn, tk, out_dtype, name, a_off=(0, 0), b_off=(0, 0), add=None,
        b_blocked=False, out_blocked=False, comm=None):
    tm, tn, tk = min(tm, m), min(tn, n), min(tk, k)
    assert m % tm == 0 and n % tn == 0 and k % tk == 0, (name, m, n, k, tm, tn, tk)
    nk = k // tk
    if mode == "nn":
        a_blk, b_blk = (tm, tk), (tk, tn)
        ao, bo = (a_off[0] // tm, a_off[1] // tk), (b_off[0] // tk, b_off[1] // tn)
        a_map = lambda i, j, kk: (i + ao[0], kk + ao[1])
        b_map = lambda i, j, kk: (kk + bo[0], j + bo[1])
        dims = (((1,), (0,)), ((), ()))
        if b_blocked:
            assert b.shape == (n // tn, k, tn) and b_off == (0, 0), (name, b.shape)
            b_blk, b_map = (None, tk, tn), lambda i, j, kk: (j, kk, 0)
    elif mode == "nt":
        a_blk, b_blk = (tm, tk), (tn, tk)
        ao, bo = (a_off[0] // tm, a_off[1] // tk), (b_off[0] // tn, b_off[1] // tk)
        a_map = lambda i, j, kk: (i + ao[0], kk + ao[1])
        b_map = lambda i, j, kk: (j + bo[0], kk + bo[1])
        dims = (((1,), (1,)), ((), ()))
        if b_blocked:
            assert b.shape == (nk, n, tk) and b_off == (0, 0), (name, b.shape)
            b_blk, b_map = (None, tn, tk), lambda i, j, kk: (kk, j, 0)
    else:
        assert not b_blocked
        a_blk, b_blk = (tk, tm), (tk, tn)
        ao, bo = (a_off[0] // tk, a_off[1] // tm), (b_off[0] // tk, b_off[1] // tn)
        a_map = lambda i, j, kk: (kk + ao[0], i + ao[1])
        b_map = lambda i, j, kk: (kk + bo[0], j + bo[1])
        dims = (((0,), (0,)), ((), ()))
    if not b_blocked:
        for off, blk in ((a_off, a_blk), (b_off, b_blk)):
            assert off[0] % blk[0] == 0 and off[1] % blk[1] == 0, (name, off, blk)
    has_add = add is not None

    def body(*refs):
        if has_add:
            a_ref, b_ref, c_ref, o_ref, acc = refs
        else:
            a_ref, b_ref, o_ref, acc = refs
            c_ref = None
        p = lax.dot_general(a_ref[...].astype(BF16), b_ref[...].astype(BF16), dims, preferred_element_type=F32)
        if nk == 1:
            if has_add:
                p = p + c_ref[...].astype(F32)
            o_ref[...] = p.astype(o_ref.dtype)
        else:
            kk = pl.program_id(2)

            @pl.when(kk == 0)
            def _():
                acc[...] = p + c_ref[...].astype(F32) if has_add else p

            @pl.when(kk > 0)
            def _():
                acc[...] += p

            @pl.when(kk == nk - 1)
            def _():
                o_ref[...] = acc[...].astype(o_ref.dtype)

    in_specs = [pl.BlockSpec(a_blk, a_map), pl.BlockSpec(b_blk, b_map)]
    args = [a, b]
    if has_add:
        in_specs.append(pl.BlockSpec((tm, tn), lambda i, j, kk: (i, j)))
        args.append(add)
    acc_shape = (tm, tn) if nk > 1 else (8, LANES)
    if out_blocked:
        out_spec = pl.BlockSpec((None, tm, tn), lambda i, j, kk: (j, i, 0))
        out_shape = jax.ShapeDtypeStruct((n // tn, m, tn), out_dtype)
    else:
        out_spec = pl.BlockSpec((tm, tn), lambda i, j, kk: (i, j))
        out_shape = jax.ShapeDtypeStruct((m, n), out_dtype)
    return _pcall(body, name=name, grid=(m // tm, n // tn, nk), in_specs=in_specs, out_specs=out_spec,
                  out_shape=out_shape, scratch_shapes=[pltpu.VMEM(acc_shape, F32)], args=args,
                  sem=("parallel", "parallel", "arbitrary"), comm=comm)


def _row_specs(rows, tm, ncol):
    specs = []
    for arr, off, width in rows:
        bw = width // ncol
        assert width % ncol == 0 and off % bw == 0, (off, width, ncol)
        ob = off // bw
        specs.append(pl.BlockSpec((tm, bw), lambda i, j, ob=ob: (i, j + ob)))
    return specs


def _rowwise_fwd(fn, rows, params, outs, *, tm, ncol=1, name):
    t = rows[0][0].shape[0]
    tm = min(tm, t)
    nr, npar = len(rows), len(params)

    def body(*refs):
        ins = [r[...].astype(F32) for r in refs[:nr + npar]]
        res = fn(*ins)
        for o_ref, val in zip(refs[nr + npar:], res):
            o_ref[...] = val.astype(o_ref.dtype)

    in_specs = _row_specs(rows, tm, ncol) + [pl.BlockSpec(p.shape, lambda i, j: (0, 0)) for p in params]
    out_specs = [pl.BlockSpec((tm, w // ncol), lambda i, j: (i, j)) for w, _ in outs]
    out_shape = [jax.ShapeDtypeStruct((t, w), dt) for w, dt in outs]
    return pl.pallas_call(
        body, name=name, grid=(t // tm, ncol), in_specs=in_specs, out_specs=out_specs, out_shape=out_shape,
        compiler_params=_cparams(("parallel", "parallel")),
    )(*[r[0] for r in rows], *params)


def _rowwise_bwd(fn, rows, params, cts, grad_dtypes, *, tm, ncol=1, name, adds=None):
    t = rows[0][0].shape[0]
    tm = min(tm, t)
    nr, npar, nct = len(rows), len(params), len(cts)
    adds = adds or [None] * nr
    add_idx = [i for i, a in enumerate(adds) if a is not None]

    def body(*refs):
        ins = [r[...].astype(F32) for r in refs[:nr + npar]]
        ct = tuple(r[...].astype(F32) for r in refs[nr + npar:nr + npar + nct])
        add_refs = refs[nr + npar + nct:nr + npar + nct + len(add_idx)]
        outs = refs[nr + npar + nct + len(add_idx):]
        _, vjp = jax.vjp(lambda *a: tuple(fn(*a)), *ins)
        grads = vjp(ct)
        extra = dict(zip(add_idx, add_refs))
        for i in range(nr):
            g = grads[i]
            if i in extra:
                g = g + extra[i][...].astype(F32)
            outs[i][...] = g.astype(outs[i].dtype)
        first = jnp.logical_and(pl.program_id(0) == 0, pl.program_id(1) == 0)
        for pi in range(npar):
            o_ref = outs[nr + pi]
            g = grads[nr + pi]

            @pl.when(first)
            def _(o_ref=o_ref, g=g):
                o_ref[...] = g

            @pl.when(jnp.logical_not(first))
            def _(o_ref=o_ref, g=g):
                o_ref[...] += g

    in_specs = (_row_specs(rows, tm, ncol)
                + [pl.BlockSpec(p.shape, lambda i, j: (0, 0)) for p in params]
                + [pl.BlockSpec((tm, c.shape[1] // ncol), lambda i, j: (i, j)) for c in cts]
                + [pl.BlockSpec((tm, adds[i].shape[1] // ncol), lambda i, j: (i, j)) for i in add_idx])
    out_specs = ([pl.BlockSpec((tm, w // ncol), lambda i, j: (i, j)) for _, _, w in rows]
                 + [pl.BlockSpec(p.shape, lambda i, j: (0, 0)) for p in params])
    out_shape = ([jax.ShapeDtypeStruct((t, w), dt) for (_, _, w), dt in zip(rows, grad_dtypes)]
                 + [jax.ShapeDtypeStruct(p.shape, F32) for p in params])
    return pl.pallas_call(
        body, name=name, grid=(t // tm, ncol), in_specs=in_specs, out_specs=out_specs, out_shape=out_shape,
        compiler_params=_cparams(("arbitrary", "arbitrary")),
    )(*[r[0] for r in rows], *params, *cts, *[adds[i] for i in add_idx])


def _rms(x, w):
    return x * lax.rsqrt(jnp.mean(x * x, axis=-1, keepdims=True) + EPS) * w


def _fn_norm(x, w):
    return (_rms(x, w),)


def _fn_gates(z, a_row, b_row):
    lane = lax.broadcasted_iota(jnp.int32, z.shape, 1)
    beta = jax.nn.sigmoid(z)
    g = -jnp.exp(a_row) * jax.nn.softplus(z + b_row)
    logf = jax.nn.log_sigmoid(z + b_row)
    return (jnp.where(lane < HEADS, beta, jnp.where(lane < 2 * HEADS, g, jnp.where(lane < 3 * HEADS, logf, 0.0))),)


def _fn_qknorm(q, k, qw, kw):
    return _rms(q, qw), _rms(k, kw)


def _fn_gdn_out(o, z, w):
    return (_rms(o, w) * jax.nn.silu(z),)


def _fn_merge(ga, gb, ya, yb):
    return (jax.nn.sigmoid(ga) * ya + jax.nn.sigmoid(gb) * yb,)


def _fn_swiglu(g, u):
    return (jax.nn.silu(g) * u,)


def _loss_head(y, target, *, tm, name):
    t, d = y.shape
    tm = min(tm, t)

    def body(y_ref, t_ref, dyf_ref, dyb_ref, loss_ref):
        err = y_ref[...] - t_ref[...]
        dy = err * (1.0 / d)
        dyf_ref[...] = dy
        dyb_ref[...] = dy.astype(BF16)
        part = jnp.sum(err * err) * (0.5 / d)

        @pl.when(pl.program_id(0) == 0)
        def _():
            loss_ref[...] = jnp.zeros_like(loss_ref)

        loss_ref[...] += part

    blk = pl.BlockSpec((tm, d), lambda i: (i, 0))
    return pl.pallas_call(
        body, name=name, grid=(t // tm,), in_specs=[blk, blk],
        out_specs=[blk, blk, pl.BlockSpec((1, LANES), lambda i: (0, 0))],
        out_shape=[jax.ShapeDtypeStruct((t, d), F32), jax.ShapeDtypeStruct((t, d), BF16),
                   jax.ShapeDtypeStruct((1, LANES), F32)],
        compiler_params=_cparams(("arbitrary",)),
    )(y, target)


def _shift_down(x, s):
    if s == 0:
        return x
    row = lax.broadcasted_iota(jnp.int32, x.shape, 0)
    return jnp.where(row >= s, pltpu.roll(x, s, 0), 0.0)


def _shift_up(x, s):
    if s == 0:
        return x
    t = x.shape[0]
    row = lax.broadcasted_iota(jnp.int32, x.shape, 0)
    return jnp.where(row < t - s, pltpu.roll(x, t - s, 0), 0.0)


def _conv_pre(x, w):
    y = x * w[CONV_K - 1:CONV_K, :]
    for i in range(CONV_K - 1):
        y = y + _shift_down(x, CONV_K - 1 - i) * w[i:i + 1, :]
    return y


def _conv_fwd(p_main, conv_w, *, width, name):
    t = p_main.shape[0]
    tc = LANES

    def body(x_ref, w_ref, o_ref):
        y = _conv_pre(x_ref[...].astype(F32), w_ref[...])
        o_ref[...] = y * jax.nn.sigmoid(y)

    return pl.pallas_call(
        body, name=name, grid=(width // tc,),
        in_specs=[pl.BlockSpec((t, tc), lambda j: (0, j)), pl.BlockSpec((CONV_K, tc), lambda j: (0, j))],
        out_specs=pl.BlockSpec((t, tc), lambda j: (0, j)),
        out_shape=jax.ShapeDtypeStruct((t, width), F32),
        compiler_params=_cparams(("parallel",)),
    )(p_main, conv_w)


def _conv_bwd(p_main, conv_w, dy, *, width, name):
    t = p_main.shape[0]
    tc = LANES

    def body(x_ref, w_ref, dy_ref, dx_ref, dw_ref):
        x = x_ref[...].astype(F32)
        w = w_ref[...]
        pre = _conv_pre(x, w)
        sg = jax.nn.sigmoid(pre)
        dpre = dy_ref[...] * (sg * (1.0 + pre * (1.0 - sg)))
        dx = dpre * w[CONV_K - 1:CONV_K, :]
        dws = []
        for i in range(CONV_K - 1):
            s = CONV_K - 1 - i
            dx = dx + _shift_up(dpre, s) * w[i:i + 1, :]
            dws.append(jnp.sum(_shift_down(x, s) * dpre, axis=0, keepdims=True))
        dws.append(jnp.sum(x * dpre, axis=0, keepdims=True))
        dx_ref[...] = dx.astype(dx_ref.dtype)
        dw_ref[...] = jnp.concatenate(dws, axis=0)

    return pl.pallas_call(
        body, name=name, grid=(width // tc,),
        in_specs=[pl.BlockSpec((t, tc), lambda j: (0, j)), pl.BlockSpec((CONV_K, tc), lambda j: (0, j)),
                  pl.BlockSpec((t, tc), lambda j: (0, j))],
        out_specs=[pl.BlockSpec((t, tc), lambda j: (0, j)), pl.BlockSpec((CONV_K, tc), lambda j: (0, j))],
        out_shape=[jax.ShapeDtypeStruct((t, width), BF16), jax.ShapeDtypeStruct((CONV_K, width), F32)],
        compiler_params=_cparams(("parallel",)),
    )(p_main, conv_w, dy)


def _bmm(a, b, spec, precision=None):
    return jnp.einsum(spec, a, b, preferred_element_type=F32, precision=precision)


def _iota2(shape, dim):
    return lax.broadcasted_iota(jnp.int32, shape, dim)


@jax.custom_vjp
def _tri_inverse(a):
    return _tri_inverse_levels(a)


def _tri_inverse_fwd(a):
    t = _tri_inverse_levels(a)
    return t, t


def _tri_inverse_bwd(t, g):
    x = _bmm(t, g, "hji,hjk->hik", SOLVE_PRECISION)
    return (-_bmm(x, t, "hik,hjk->hij", SOLVE_PRECISION),)


_tri_inverse.defvjp(_tri_inverse_fwd, _tri_inverse_bwd)


def _tri_inverse_levels(a):
    c = a.shape[-1]
    r, m = _iota2((c, c), 0), _iota2((c, c), 1)
    eye = (r == m).astype(F32)
    inv = None
    b = 1
    while b < c:
        mask = jnp.logical_and(r // (2 * b) == m // (2 * b), jnp.logical_and(r % (2 * b) >= b, m % (2 * b) < b))
        off = jnp.where(mask[None], a, 0.0)
        if inv is None:
            inv = eye[None] - off
        else:
            inv = inv - _bmm(_bmm(inv, off, "hij,hjk->hik", SOLVE_PRECISION), inv, "hij,hjk->hik", SOLVE_PRECISION)
        b *= 2
    return inv


def _gdn_chunk(s, q3, k3, v3, b3, gc3):
    c = q3.shape[1]
    r, m = _iota2((c, c), 0), _iota2((c, c), 1)
    tril_incl = (r >= m)[None]
    tril_strict = (r > m)[None]
    eye = (r == m).astype(F32)[None]
    qn = q3 * lax.rsqrt(jnp.sum(q3 * q3, axis=-1, keepdims=True) + EPS) * (DH ** -0.5)
    kn = k3 * lax.rsqrt(jnp.sum(k3 * k3, axis=-1, keepdims=True) + EPS)
    ones = jnp.ones((q3.shape[0], c, c), F32)
    gc_row = _bmm(ones, gc3 * eye, "hij,hjk->hik", SOLVE_PRECISION)
    decay = jnp.where(tril_incl, jnp.exp(jnp.where(tril_incl, gc3 - gc_row, 0.0)), 0.0)
    a = jnp.where(tril_strict, _bmm(kn, kn, "hcd,hmd->hcm") * decay * b3, 0.0)
    tinv = _tri_inverse(a)
    egc = jnp.exp(gc3)
    u = _bmm(tinv, v3 * b3, "hij,hjk->hik", SOLVE_PRECISION)
    w = _bmm(tinv, kn * (b3 * egc), "hij,hjk->hik", SOLVE_PRECISION)
    qk = _bmm(qn, kn, "hcd,hmd->hcm") * decay
    v_new = u - _bmm(w, s, "hcd,hdv->hcv")
    o = _bmm(qn * egc, s, "hcd,hdv->hcv") + _bmm(qk, v_new, "hcm,hmv->hcv")
    row = _iota2((c, 1), 0)[None]
    g_last = jnp.sum(jnp.where(row == c - 1, gc3, 0.0), axis=1, keepdims=True)
    s_new = s * jnp.exp(g_last) + _bmm(kn * jnp.exp(g_last - gc3), v_new, "hcd,hcv->hdv")
    return s_new, o


GDN_HEAD_GROUP = 8


def _split_heads(ref, off, h0):
    return jnp.stack([ref[:, off + h * DH:off + (h + 1) * DH].astype(F32)
                      for h in range(h0, h0 + GDN_HEAD_GROUP)], axis=0)


def _store_heads(ref, x3, off, h0):
    for i in range(GDN_HEAD_GROUP):
        h = h0 + i
        ref[:, off + h * DH:off + (h + 1) * DH] = x3[i].astype(ref.dtype)


def _lane_cols(tile, lane0):
    lane = _iota2(tile.shape, 1)
    return jnp.stack([jnp.sum(jnp.where(lane == lane0 + i, tile, 0.0), axis=1, keepdims=True)
                      for i in range(GDN_HEAD_GROUP)], axis=0)


def _cols_to_lanes(cols3, lane0, shape):
    lane = _iota2(shape, 1)
    out = jnp.zeros(shape, F32)
    for i in range(GDN_HEAD_GROUP):
        out = out + jnp.where(lane == lane0 + i, cols3[i], 0.0)
    return out


def _chunk_cumsum_matrix():
    r, m = _iota2((CHUNK, CHUNK), 0), _iota2((CHUNK, CHUNK), 1)
    return (r >= m).astype(F32)


def _gdn_inputs(qkv_ref, gt, gcum, h0):
    return (_split_heads(qkv_ref, 0, h0), _split_heads(qkv_ref, HW, h0), _split_heads(qkv_ref, 2 * HW, h0),
            _lane_cols(gt, h0), _lane_cols(gcum, HEADS + h0))


def _gdn_fwd(qkv, gates, *, name, comm=None):
    t = qkv.shape[0]
    n = t // CHUNK

    def body(qkv_ref, gt_ref, o_ref, sall_ref, s_scr):
        @pl.when(pl.program_id(0) == 0)
        def _():
            s_scr[...] = jnp.zeros_like(s_scr)

        gt = gt_ref[...]
        gcum = jnp.dot(_chunk_cumsum_matrix(), gt, preferred_element_type=F32, precision=HI)
        for h0 in range(0, HEADS, GDN_HEAD_GROUP):
            grp = pl.ds(h0, GDN_HEAD_GROUP)
            s = s_scr[grp]
            sall_ref[0, grp] = s
            s_new, o3 = _gdn_chunk(s, *_gdn_inputs(qkv_ref, gt, gcum, h0))
            s_scr[grp] = s_new
            _store_heads(o_ref, o3, 0, h0)

    return _pcall(
        body, name=name, grid=(n,),
        in_specs=[pl.BlockSpec((CHUNK, 3 * HW), lambda i: (i, 0)), pl.BlockSpec((CHUNK, LANES), lambda i: (i, 0))],
        out_specs=[pl.BlockSpec((CHUNK, HW), lambda i: (i, 0)),
                   pl.BlockSpec((1, HEADS, DH, DH), lambda i: (i, 0, 0, 0))],
        out_shape=[jax.ShapeDtypeStruct((t, HW), F32), jax.ShapeDtypeStruct((n, HEADS, DH, DH), F32)],
        scratch_shapes=[pltpu.VMEM((HEADS, DH, DH), F32)], sem=("arbitrary",), args=(qkv, gates), comm=comm)


def _gdn_bwd(qkv, gates, s_all, do, *, name, comm=None):
    t = qkv.shape[0]
    n = t // CHUNK

    def body(qkv_ref, gt_ref, sall_ref, do_ref, dqkv_ref, dgt_ref, ds_scr):
        @pl.when(pl.program_id(0) == 0)
        def _():
            ds_scr[...] = jnp.zeros_like(ds_scr)

        gt = gt_ref[...]
        cum = _chunk_cumsum_matrix()
        gcum = jnp.dot(cum, gt, preferred_element_type=F32, precision=HI)
        shape = (CHUNK, LANES)
        dbeta = jnp.zeros(shape, F32)
        dgcum = jnp.zeros(shape, F32)
        for h0 in range(0, HEADS, GDN_HEAD_GROUP):
            grp = pl.ds(h0, GDN_HEAD_GROUP)
            _, vjp = jax.vjp(_gdn_chunk, sall_ref[0, grp], *_gdn_inputs(qkv_ref, gt, gcum, h0))
            ds, dq3, dk3, dv3, db3, dgc3 = vjp((ds_scr[grp], _split_heads(do_ref, 0, h0)))
            ds_scr[grp] = ds
            _store_heads(dqkv_ref, dq3, 0, h0)
            _store_heads(dqkv_ref, dk3, HW, h0)
            _store_heads(dqkv_ref, dv3, 2 * HW, h0)
            dbeta = dbeta + _cols_to_lanes(db3, h0, shape)
            dgcum = dgcum + _cols_to_lanes(dgc3, HEADS + h0, shape)
        dg = lax.dot_general(cum, dgcum, (((0,), (0,)), ((), ())), preferred_element_type=F32, precision=HI)
        dgt_ref[...] = dbeta + dg

    rev = lambda i: n - 1 - i
    return _pcall(
        body, name=name, grid=(n,),
        in_specs=[pl.BlockSpec((CHUNK, 3 * HW), lambda i: (rev(i), 0)), pl.BlockSpec((CHUNK, LANES), lambda i: (rev(i), 0)),
                  pl.BlockSpec((1, HEADS, DH, DH), lambda i: (rev(i), 0, 0, 0)),
                  pl.BlockSpec((CHUNK, HW), lambda i: (rev(i), 0))],
        out_specs=[pl.BlockSpec((CHUNK, 3 * HW), lambda i: (rev(i), 0)), pl.BlockSpec((CHUNK, LANES), lambda i: (rev(i), 0))],
        out_shape=[jax.ShapeDtypeStruct((t, 3 * HW), F32), jax.ShapeDtypeStruct((t, LANES), F32)],
        scratch_shapes=[pltpu.VMEM((HEADS, DH, DH), F32)], sem=("arbitrary",), args=(qkv, gates, s_all, do), comm=comm)


FOX_BLK = 512
NEG = -1e30


def _fox_cumsum(gates, *, name):
    t = gates.shape[0]
    blk = min(FOX_BLK, t)

    def body(g_ref, c_ref):
        r, m = _iota2((blk, blk), 0), _iota2((blk, blk), 1)
        upper = (r <= m).astype(F32)
        carry = jnp.zeros((HEADS, 1), F32)
        for b in range(t // blk):
            lf = g_ref[b * blk:(b + 1) * blk, :].T[2 * HEADS:3 * HEADS, :]
            c_ref[:, b * blk:(b + 1) * blk] = jnp.dot(lf, upper, preferred_element_type=F32, precision=HI) + carry
            carry = carry + jnp.sum(lf, axis=1, keepdims=True)

    return pl.pallas_call(body, name=name, out_shape=jax.ShapeDtypeStruct((HEADS, t), F32),
                          compiler_params=_cparams())(gates)


def _fox_cumsum_bwd(dc, dgates_gdn, *, name):
    t = dc.shape[1]
    blk = min(FOX_BLK, t)

    def body(dc_ref, dg_ref, o_ref):
        r, m = _iota2((blk, blk), 0), _iota2((blk, blk), 1)
        lower = (r >= m).astype(F32)
        carry = jnp.zeros((HEADS, 1), F32)
        for b in reversed(range(t // blk)):
            d = dc_ref[:, b * blk:(b + 1) * blk]
            dlf = jnp.dot(d, lower, preferred_element_type=F32, precision=HI) + carry
            carry = carry + jnp.sum(d, axis=1, keepdims=True)
            tile = jnp.concatenate([jnp.zeros((2 * HEADS, blk), F32), dlf,
                                    jnp.zeros((LANES - 3 * HEADS, blk), F32)], axis=0)
            o_ref[b * blk:(b + 1) * blk, :] = tile.T + dg_ref[b * blk:(b + 1) * blk, :]

    return pl.pallas_call(body, name=name, out_shape=jax.ShapeDtypeStruct((t, LANES), F32),
                          compiler_params=_cparams())(dc, dgates_gdn)


def _fox_logits(q, k, c_row, diagonal):
    s = lax.dot_general(q, k, (((1,), (1,)), ((), ())), preferred_element_type=F32) * (DH ** -0.5) - c_row
    if not diagonal:
        return s
    return jnp.where(_iota2(s.shape, 0) >= _iota2(s.shape, 1), s, NEG)


def _fox_fwd(qn, kn, p_main, c4, *, v_off, name, comm=None):
    t = qn.shape[0]
    blk = min(FOX_BLK, t)
    nb = t // blk
    vb = v_off // DH

    def body(q_ref, k_ref, v_ref, c_ref, o_ref, lse_ref):
        qi = pl.program_id(1)
        q = q_ref[...]

        def step(j, carry, diagonal=False):
            m, l, acc = carry
            rows = pl.ds(pl.multiple_of(j * blk, blk), blk)
            s = _fox_logits(q, k_ref[rows, :], c_ref[0, j], diagonal)
            m_new = jnp.maximum(m, jnp.max(s, axis=1, keepdims=True))
            p = jnp.exp(s - m_new)
            scale = jnp.exp(m - m_new)
            l = scale * l + jnp.sum(p, axis=1, keepdims=True)
            acc = scale * acc + jnp.dot(p.astype(BF16), v_ref[rows, :], preferred_element_type=F32)
            return m_new, l, acc

        init = (jnp.full((blk, 1), NEG, F32), jnp.zeros((blk, 1), F32), jnp.zeros((blk, DH), F32))
        m, l, acc = step(qi, lax.fori_loop(0, qi, step, init), diagonal=True)
        o_ref[...] = (acc / l).astype(o_ref.dtype)
        lse_ref[0] = m + jnp.log(l)

    return _pcall(
        body, name=name, grid=(HEADS, nb),
        in_specs=[pl.BlockSpec((blk, DH), lambda h, i: (i, h)), pl.BlockSpec((t, DH), lambda h, i: (0, h)),
                  pl.BlockSpec((t, DH), lambda h, i: (0, vb + h)), pl.BlockSpec((1, nb, 1, blk), lambda h, i: (h, 0, 0, 0))],
        out_specs=[pl.BlockSpec((blk, DH), lambda h, i: (i, h)), pl.BlockSpec((1, blk, 1), lambda h, i: (h, i, 0))],
        out_shape=[jax.ShapeDtypeStruct((t, HW), BF16), jax.ShapeDtypeStruct((HEADS, t, 1), F32)],
        sem=("parallel", "arbitrary"), args=(qn, kn, p_main, c4), comm=comm)


def _fox_delta(qn, kn, p_main, c4, do, lse, *, v_off, name):
    t = qn.shape[0]
    blk = min(FOX_BLK, t)
    nb = t // blk
    vb = v_off // DH

    def body(q_ref, k_ref, v_ref, c_ref, do_ref, lse_ref, delta_ref):
        qi = pl.program_id(1)
        q = q_ref[...]
        dob = do_ref[...]
        lse = lse_ref[0]

        def step(j, delta, diagonal=False):
            rows = pl.ds(pl.multiple_of(j * blk, blk), blk)
            p = jnp.exp(_fox_logits(q, k_ref[rows, :], c_ref[0, j], diagonal) - lse)
            dp = lax.dot_general(dob, v_ref[rows, :], (((1,), (1,)), ((), ())), preferred_element_type=F32)
            return delta + jnp.sum(p * dp, axis=1, keepdims=True)

        delta_ref[0] = step(qi, lax.fori_loop(0, qi, step, jnp.zeros((blk, 1), F32)), diagonal=True)

    qblk = lambda h, i: (i, h)
    return pl.pallas_call(
        body, name=name, grid=(HEADS, nb),
        in_specs=[pl.BlockSpec((blk, DH), qblk), pl.BlockSpec((t, DH), lambda h, i: (0, h)),
                  pl.BlockSpec((t, DH), lambda h, i: (0, vb + h)), pl.BlockSpec((1, nb, 1, blk), lambda h, i: (h, 0, 0, 0)),
                  pl.BlockSpec((blk, DH), qblk), pl.BlockSpec((1, blk, 1), lambda h, i: (h, i, 0))],
        out_specs=pl.BlockSpec((1, blk, 1), lambda h, i: (h, i, 0)),
        out_shape=jax.ShapeDtypeStruct((HEADS, t, 1), F32),
        compiler_params=_cparams(("parallel", "arbitrary")),
    )(qn, kn, p_main, c4, do, lse)


def _fox_bwd(qn, kn, p_main, c4, delta, do, lse, *, v_off, name, comm=None):
    t = qn.shape[0]
    blk = min(FOX_BLK, t)
    nb = t // blk
    vb = v_off // DH
    tn_dims = (((0,), (0,)), ((), ()))
    nt_dims = (((1,), (1,)), ((), ()))

    def body(q_ref, k_ref, v_ref, c_ref, delta_ref, do_ref, lse_ref, dq_ref, dk_ref, dv_ref, dc_ref):
        kj = pl.program_id(1)

        @pl.when(kj == 0)
        def _():
            dq_ref[...] = jnp.zeros_like(dq_ref)

        k = k_ref[...]
        v = v_ref[...]
        c_row = c_ref[0, 0]

        def step(i, carry, diagonal=False):
            dk, dv, dc = carry
            rows = pl.ds(pl.multiple_of(i * blk, blk), blk)
            q = q_ref[rows, :]
            dob = do_ref[rows, :]
            p = jnp.exp(_fox_logits(q, k, c_row, diagonal) - lse_ref[0, rows, :])
            pb = p.astype(BF16)
            dv = dv + lax.dot_general(pb, dob, tn_dims, preferred_element_type=F32)
            dp = lax.dot_general(dob, v, nt_dims, preferred_element_type=F32)
            ds = p * (dp - delta_ref[0, rows, :])
            dsb = ds.astype(BF16)
            dq_ref[rows, :] += jnp.dot(dsb, k, preferred_element_type=F32) * (DH ** -0.5)
            dk = dk + lax.dot_general(dsb, q, tn_dims, preferred_element_type=F32) * (DH ** -0.5)
            dc = dc - jnp.sum(ds, axis=0, keepdims=True)
            return dk, dv, dc

        init = (jnp.zeros((blk, DH), F32), jnp.zeros((blk, DH), F32), jnp.zeros((1, blk), F32))
        dk, dv, dc = lax.fori_loop(kj + 1, nb, step, step(kj, init, diagonal=True))
        dk_ref[...] = dk
        dv_ref[...] = dv.astype(dv_ref.dtype)
        dc_ref[0, 0] = dc

    full = lambda h, j: (0, h)
    kvb = lambda h, j: (j, h)
    return _pcall(
        body, name=name, grid=(HEADS, nb), sem=("parallel", "arbitrary"), comm=comm,
        args=(qn, kn, p_main, c4, delta, do, lse),
        in_specs=[pl.BlockSpec((t, DH), full), pl.BlockSpec((blk, DH), kvb),
                  pl.BlockSpec((blk, DH), lambda h, j: (j, vb + h)), pl.BlockSpec((1, 1, 1, blk), lambda h, j: (h, j, 0, 0)),
                  pl.BlockSpec((1, t, 1), lambda h, j: (h, 0, 0)), pl.BlockSpec((t, DH), full),
                  pl.BlockSpec((1, t, 1), lambda h, j: (h, 0, 0))],
        out_specs=[pl.BlockSpec((t, DH), full), pl.BlockSpec((blk, DH), kvb), pl.BlockSpec((blk, DH), kvb),
                   pl.BlockSpec((1, 1, 1, blk), lambda h, j: (h, j, 0, 0))],
        out_shape=[jax.ShapeDtypeStruct((t, HW), F32), jax.ShapeDtypeStruct((t, HW), F32),
                   jax.ShapeDtypeStruct((t, HW), BF16), jax.ShapeDtypeStruct((HEADS, nb, 1, blk), F32)])


ANY = pl.BlockSpec(memory_space=pl.ANY)


def _mesh_pos():
    return lax.axis_index("x"), lax.axis_index("y"), lax.axis_index("c")


def _all_gather(blocks, *, name):
    n = len(blocks)

    def body(*refs):
        ins, outs = refs[:n], refs[n:2 * n]
        send, recv, local = refs[2 * n:]
        x, y, c = _mesh_pos()
        me, sibling = (x, y, c), (x, y, 1 - c)
        chips = [(1 - x, y), (x, 1 - y), (1 - x, 1 - y)]

        def copy(t, k, block, to, src=None):
            dst = outs[t].at[4 * block[0] + 2 * block[1] + block[2]]
            return pltpu.make_async_remote_copy(
                src_ref=dst if src is None else src, dst_ref=dst, send_sem=send.at[7 * t + k],
                recv_sem=recv.at[7 * t + k], device_id=to, device_id_type=MESH)

        mine = [pltpu.make_async_copy(ins[t], outs[t].at[4 * x + 2 * y + c], local.at[t]) for t in range(n)]
        for cp in mine:
            cp.start()
        first = []
        for t in range(n):
            first.append(copy(t, 0, me, sibling, src=ins[t]))
            first += [copy(t, 1 + j, me, (*chip, c), src=ins[t]) for j, chip in enumerate(chips)]
        for cp in first:
            cp.start()
        passed = []
        for j, chip in enumerate(chips):
            for t in range(n):
                copy(t, 1 + j, (*chip, c), me).wait_recv()
                fwd = copy(t, 4 + j, (*chip, c), sibling)
                fwd.start()
                passed.append(fwd)
        for t in range(n):
            copy(t, 0, sibling, me).wait_recv()
            for j, chip in enumerate(chips):
                copy(t, 4 + j, (*chip, 1 - c), me).wait_recv()
        for cp in first + passed:
            cp.wait_send()
        for cp in mine:
            cp.wait()

    return pl.pallas_call(
        body, name=name, in_specs=[ANY] * n, out_specs=[ANY] * n,
        out_shape=[jax.ShapeDtypeStruct((N_DEV,) + b.shape, b.dtype) for b in blocks],
        scratch_shapes=[pltpu.SemaphoreType.DMA((7 * n,)), pltpu.SemaphoreType.DMA((7 * n,)),
                        pltpu.SemaphoreType.DMA((n,))],
    )(*blocks)


def _all_gather_relayed(block, *, name):
    r = block.shape[0]
    half = r // 2
    assert half * 2 == r and half % 16 == 0, block.shape

    def body(in_ref, out_ref, send, recv, local):
        x, y, c = _mesh_pos()
        me, sibling, xn, yn, dg = (x, y, c), (x, y, 1 - c), (1 - x, y, c), (x, 1 - y, c), (1 - x, 1 - y, c)
        slot = lambda p: 4 * p[0] + 2 * p[1] + p[2]
        rows = {"a": pl.ds(0, half), "b": pl.ds(half, half)}

        def copy(k, src, dst, to):
            return pltpu.make_async_remote_copy(src_ref=src, dst_ref=dst, send_sem=send.at[k], recv_sem=recv.at[k],
                                                device_id=to, device_id_type=MESH)

        def part(p, h=None):
            ref = out_ref.at[slot(p)]
            return ref if h is None else ref.at[rows[h]]

        def landed(k, p, h=None):
            copy(k, part(p, h), part(p, h), me).wait_recv()

        mine = pltpu.make_async_copy(in_ref, part(me), local)
        mine.start()
        first = [copy(0, in_ref, part(me), sibling),
                 copy(1, in_ref.at[rows["a"]], part(me, "a"), xn), copy(2, in_ref.at[rows["b"]], part(me, "b"), xn),
                 copy(3, in_ref.at[rows["a"]], part(me, "a"), yn), copy(4, in_ref.at[rows["b"]], part(me, "b"), yn)]
        for cp in first:
            cp.start()
        landed(1, xn, "a")
        relay_a = copy(5, part(xn, "a"), part(xn, "a"), yn)
        relay_a.start()
        landed(4, yn, "b")
        relay_b = copy(6, part(yn, "b"), part(yn, "b"), xn)
        relay_b.start()
        landed(2, xn, "b")
        pass_x = copy(7, part(xn), part(xn), sibling)
        pass_x.start()
        landed(3, yn, "a")
        pass_y = copy(8, part(yn), part(yn), sibling)
        pass_y.start()
        landed(5, dg, "a")
        landed(6, dg, "b")
        pass_d = copy(9, part(dg), part(dg), sibling)
        pass_d.start()
        landed(0, sibling)
        for k, p in ((7, (1 - x, y, 1 - c)), (8, (x, 1 - y, 1 - c)), (9, (1 - x, 1 - y, 1 - c))):
            landed(k, p)
        for cp in first + [relay_a, relay_b, pass_x, pass_y, pass_d]:
            cp.wait_send()
        mine.wait()

    return pl.pallas_call(
        body, name=name, in_specs=[ANY], out_specs=ANY,
        out_shape=jax.ShapeDtypeStruct((N_DEV,) + block.shape, block.dtype),
        scratch_shapes=[pltpu.SemaphoreType.DMA((10,)), pltpu.SemaphoreType.DMA((10,)), pltpu.SemaphoreType.DMA],
    )(block)


def _comm_call(comm, *, name):
    ci, co = len(comm.ins), len(comm.out_shapes)

    def body(*refs):
        comm.start(refs[:ci], refs[ci:ci + co], refs[ci + co:])
        comm.finish(refs[:ci], refs[ci:ci + co], refs[ci + co:])

    return pl.pallas_call(body, name=name, in_specs=[ANY] * ci, out_specs=[ANY] * co, out_shape=comm.out_shapes,
                          scratch_shapes=comm.sems, input_output_aliases=comm.aliases)(*comm.ins)


def _ag_first_comm(shards, rows=None, into=None):
    n = len(shards)
    rows = rows or [None] * n
    into = into or [None] * n
    carried = [t for t in range(n) if into[t] is not None]

    def copies(cin, cout, sems):
        send, recv, local = sems
        x, y, c = _mesh_pos()
        peers = [(x, y, 1 - c), (1 - x, y, c), (x, 1 - y, c), (1 - x, 1 - y, c)]
        slot = lambda p: 4 * p[0] + 2 * p[1] + p[2]
        mine, out, inc = [], [], []
        for t in range(n):
            part = (lambda ref: ref) if rows[t] is None else (lambda ref, r=rows[t]: ref.at[pl.ds(r[0], r[1])])
            own = part(cout[t].at[slot((x, y, c))])
            mine.append(pltpu.make_async_copy(part(cin[t]), own, local.at[t]))
            for k, peer in enumerate(peers):
                sems_k = dict(send_sem=send.at[4 * t + k], recv_sem=recv.at[4 * t + k], device_id=peer,
                              device_id_type=MESH)
                theirs = part(cout[t].at[slot(peer)])
                out.append(pltpu.make_async_remote_copy(src_ref=part(cin[t]), dst_ref=own, **sems_k))
                inc.append(pltpu.make_async_remote_copy(src_ref=theirs, dst_ref=theirs, **sems_k))
        return mine, out, inc

    def start(cin, cout, sems):
        mine, out, _ = copies(cin, cout, sems)
        for cp in mine + out:
            cp.start()

    def finish(cin, cout, sems):
        mine, out, inc = copies(cin, cout, sems)
        for cp in inc:
            cp.wait_recv()
        for cp in out:
            cp.wait_send()
        for cp in mine:
            cp.wait()

    return _Comm(list(shards) + [into[t] for t in carried],
                 [jax.ShapeDtypeStruct((N_DEV,) + s.shape, s.dtype) for s in shards],
                 [pltpu.SemaphoreType.DMA((4 * n,)), pltpu.SemaphoreType.DMA((4 * n,)), pltpu.SemaphoreType.DMA((n,))],
                 start, finish, aliases={n + i: t for i, t in enumerate(carried)})


def _ag_forward(gathered, *, name):
    n = len(gathered)

    def body(*refs):
        outs = refs[n:2 * n]
        send, recv = refs[2 * n:]
        x, y, c = _mesh_pos()
        chips = [(1 - x, y), (x, 1 - y), (1 - x, 1 - y)]
        fwd, inc = [], []
        for t in range(n):
            for j, (px, py) in enumerate(chips):
                sems_j = dict(send_sem=send.at[3 * t + j], recv_sem=recv.at[3 * t + j], device_id=(x, y, 1 - c),
                              device_id_type=MESH)
                mine, theirs = outs[t].at[4 * px + 2 * py + c], outs[t].at[4 * px + 2 * py + 1 - c]
                fwd.append(pltpu.make_async_remote_copy(src_ref=mine, dst_ref=mine, **sems_j))
                inc.append(pltpu.make_async_remote_copy(src_ref=theirs, dst_ref=theirs, **sems_j))
        for cp in fwd:
            cp.start()
        for cp in inc:
            cp.wait_recv()
        for cp in fwd:
            cp.wait_send()

    return pl.pallas_call(
        body, name=name, in_specs=[ANY] * n, out_specs=[ANY] * n,
        out_shape=[jax.ShapeDtypeStruct(g.shape, g.dtype) for g in gathered],
        scratch_shapes=[pltpu.SemaphoreType.DMA((3 * n,)), pltpu.SemaphoreType.DMA((3 * n,))],
        input_output_aliases={t: t for t in range(n)},
    )(*gathered)


def _rs_sibling(grads, *, name):
    n = len(grads)

    def body(*refs):
        ins, outs = refs[:n], refs[n:2 * n]
        send, recv = refs[2 * n:]
        x, y, c = _mesh_pos()
        copies = []
        for t in range(n):
            for q in range(4):
                copies.append(pltpu.make_async_remote_copy(
                    src_ref=ins[t].at[2 * q + (1 - c)], dst_ref=outs[t].at[q], send_sem=send.at[4 * t + q],
                    recv_sem=recv.at[4 * t + q], device_id=(x, y, 1 - c), device_id_type=MESH))
        for cp in copies:
            cp.start()
        for cp in copies:
            cp.wait_recv()
        for cp in copies:
            cp.wait_send()

    return pl.pallas_call(
        body, name=name, in_specs=[ANY] * n, out_specs=[ANY] * n,
        out_shape=[jax.ShapeDtypeStruct((4,) + g.shape[1:], g.dtype) for g in grads],
        scratch_shapes=[pltpu.SemaphoreType.DMA((4 * n,)), pltpu.SemaphoreType.DMA((4 * n,))],
    )(*grads)


def _rs_chips_comm(parts):
    n = len(parts)

    def copies(cin, cout, sems):
        send, recv, local = sems
        x, y, c = _mesh_pos()
        my_chip = 2 * x + y
        mine = [pltpu.make_async_copy(cin[t].at[my_chip], cout[t].at[my_chip], local.at[t]) for t in range(n)]
        sends, lands = [], []
        for t in range(n):
            for k, (px, py) in enumerate([(1 - x, y), (x, 1 - y), (1 - x, 1 - y)]):
                sems_k = dict(send_sem=send.at[3 * t + k], recv_sem=recv.at[3 * t + k], device_id=(px, py, c),
                              device_id_type=MESH)
                sends.append(pltpu.make_async_remote_copy(src_ref=cin[t].at[2 * px + py], dst_ref=cout[t].at[my_chip],
                                                          **sems_k))
                lands.append(pltpu.make_async_remote_copy(src_ref=cout[t].at[2 * px + py],
                                                          dst_ref=cout[t].at[2 * px + py], **sems_k))
        return mine, sends, lands

    def start(cin, cout, sems):
        mine, sends, _ = copies(cin, cout, sems)
        for cp in mine + sends:
            cp.start()

    def finish(cin, cout, sems):
        mine, sends, lands = copies(cin, cout, sems)
        for cp in lands:
            cp.wait_recv()
        for cp in sends:
            cp.wait_send()
        for cp in mine:
            cp.wait()

    return _Comm(parts, [jax.ShapeDtypeStruct(p.shape, p.dtype) for p in parts],
                 [pltpu.SemaphoreType.DMA((3 * n,)), pltpu.SemaphoreType.DMA((3 * n,)), pltpu.SemaphoreType.DMA((n,))],
                 start, finish)


def _row_tile(r, c, itemsize, budget=3 * 1024 * 1024):
    best = None
    for tr in range(16, r + 1, 16):
        if r % tr == 0 and tr * c * itemsize <= budget:
            best = tr
    return best or r


def _pair_sum(grad, land, *, name):
    _, r, c = grad.shape
    tr = _row_tile(r, c, 2)

    def body(g_ref, l_ref, o_ref):
        o_ref[...] = (g_ref[...].astype(F32) + l_ref[...].astype(F32)).astype(o_ref.dtype)

    return pl.pallas_call(
        body, name=name, grid=(4, r // tr),
        in_specs=[pl.BlockSpec((1, tr, c), lambda q, i: (2 * q + lax.axis_index("c"), i, 0)),
                  pl.BlockSpec((1, tr, c), lambda q, i: (q, i, 0))],
        out_specs=pl.BlockSpec((1, tr, c), lambda q, i: (q, i, 0)),
        out_shape=jax.ShapeDtypeStruct((4, r, c), grad.dtype),
        compiler_params=_cparams(("parallel", "parallel")),
    )(grad, land)


def _adamw_math(w, g, m, v):
    m = ADAM_B1 * m + (1.0 - ADAM_B1) * g
    v = ADAM_B2 * v + (1.0 - ADAM_B2) * jnp.square(g)
    m_hat = m / (1.0 - ADAM_B1 ** ADAM_STEP)
    v_hat = v / (1.0 - ADAM_B2 ** ADAM_STEP)
    delta = -ADAM_LR * (m_hat / (jnp.sqrt(v_hat) + ADAM_EPS) + ADAM_WD * w)
    return delta, m, v


def _adamw(parts, w, m, v, *, name):
    s, _, cp = parts.shape
    r, c = w.shape
    tr = _row_tile(r, cp, 4, budget=1024 * 1024)

    def body(p_ref, w_ref, m_ref, v_ref, g_ref, d_ref, nm_ref, nv_ref):
        g = p_ref[0].astype(F32)
        for i in range(1, s):
            g = g + p_ref[i].astype(F32)
        g = g[:, :c]
        delta, nm, nv = _adamw_math(w_ref[...], g, m_ref[...], v_ref[...])
        g_ref[...] = g
        d_ref[...] = delta
        nm_ref[...] = nm
        nv_ref[...] = nv

    blk = pl.BlockSpec((tr, c), lambda i: (i, 0))
    return pl.pallas_call(
        body, name=name, grid=(r // tr,),
        in_specs=[pl.BlockSpec((s, tr, cp), lambda i: (0, i, 0)), blk, blk, blk],
        out_specs=[blk] * 4, out_shape=[jax.ShapeDtypeStruct((r, c), F32)] * 4,
        compiler_params=_cparams(("parallel",)),
    )(parts, w, m, v)


def _w_in_pieces(d, nb, sources):
    segs = [(0, 4 * HW, False, 0), (4 * HW, 4 * HW + 2 * HEADS, True, 0),
            (4 * HW + 2 * HEADS, 7 * HW + 2 * HEADS, False, 4 * HW),
            (7 * HW + 2 * HEADS, 7 * HW + 3 * HEADS, True, 2 * HEADS),
            (7 * HW + 3 * HEADS, 7 * HW + 3 * HEADS + 2 * d, False, 7 * HW)]
    out = []
    for dev in range(N_DEV):
        lo, hi = dev * nb, (dev + 1) * nb
        for s0, s1, is_small, a0 in segs:
            p, q = max(lo, s0), min(hi, s1)
            if p >= q:
                continue
            a, b = a0 + p - s0, a0 + q - s0
            if is_small:
                out.append((dev, p - lo, q - lo, len(sources), a, b))
                continue
            for si, (start, width) in enumerate(sources):
                u, v = max(a, start), min(b, start + width)
                if u < v:
                    out.append((dev, p - lo + (u - a), p - lo + (v - a), si, u - start, v - start))
    return out


def _concat_cols(parts, *, name):
    t = parts[0].shape[0]
    n = len(parts)
    offs = [sum(p.shape[1] for p in parts[:i]) for i in range(n)]
    tm = min(128, t)

    def body(*refs):
        for i in range(n):
            refs[n][:, offs[i]:offs[i] + parts[i].shape[1]] = refs[i][...]

    return pl.pallas_call(
        body, name=name, grid=(t // tm,),
        in_specs=[pl.BlockSpec((tm, p.shape[1]), lambda i: (i, 0)) for p in parts],
        out_specs=pl.BlockSpec((tm, offs[-1] + parts[-1].shape[1]), lambda i: (i, 0)),
        out_shape=jax.ShapeDtypeStruct((t, offs[-1] + parts[-1].shape[1]), parts[0].dtype),
        compiler_params=_cparams(("parallel",)))(*parts)


def _w_in_to_aligned(g_in, *, name):
    _, d, nb = g_in.shape
    n_main = 7 * HW + 2 * d
    tr = min(128, d)
    pieces = _w_in_pieces(d, nb, [(0, n_main)])

    def body(g_ref, main_ref, small_ref):
        small_ref[...] = jnp.zeros_like(small_ref)
        for dev, s, e, src, a, b in pieces:
            dst = main_ref if src == 0 else small_ref
            dst[:, a:b] = g_ref[dev, :, s:e]

    return pl.pallas_call(
        body, name=name, grid=(d // tr,), in_specs=[pl.BlockSpec((N_DEV, tr, nb), lambda i: (0, i, 0))],
        out_specs=[pl.BlockSpec((tr, n_main), lambda i: (i, 0)), pl.BlockSpec((tr, LANES), lambda i: (i, 0))],
        out_shape=[jax.ShapeDtypeStruct((d, n_main), g_in.dtype), jax.ShapeDtypeStruct((d, LANES), g_in.dtype)],
        compiler_params=_cparams(("parallel",)),
    )(g_in)


def _w_in_grad_blocks(seg_grads, small_grad, sources, nb, *, name):
    d = small_grad.shape[0]
    tr = min(128, d)
    pieces = _w_in_pieces(d, nb, sources)
    ns = len(seg_grads)

    def body(*refs):
        o_ref = refs[ns + 1]
        for dev, s, e, src, a, b in pieces:
            o_ref[dev, :, s:e] = refs[src][:, a:b]

    return pl.pallas_call(
        body, name=name, grid=(d // tr,),
        in_specs=[pl.BlockSpec((tr, g.shape[1]), lambda i: (i, 0)) for g in seg_grads + [small_grad]],
        out_specs=pl.BlockSpec((N_DEV, tr, nb), lambda i: (0, i, 0)),
        out_shape=jax.ShapeDtypeStruct((N_DEV, d, nb), small_grad.dtype),
        compiler_params=_cparams(("parallel",)),
    )(*seg_grads, small_grad)


def _pad_cols(a, n):
    return a if a.shape[1] == n else jnp.concatenate([a, jnp.zeros((a.shape[0], n - a.shape[1]), a.dtype)], axis=1)


def _pad_rows(a, n):
    return a if a.shape[0] == n else jnp.concatenate([a, jnp.zeros((n - a.shape[0], a.shape[1]), a.dtype)], axis=0)


class _StaticPlan:
    def __init__(self, weights, cp):
        self.w, self.cp, self.grads = weights, cp, {}

    def comm_for(self, key):
        return None

    def done(self, key, res):
        pass

    def weight(self, name):
        return self.w[name]

    def grad(self, name, g):
        self.grads[name] = g

    def grad_w_in(self, g_main, g_small):
        self.grads["w_main"], self.grads["w_small"] = g_main, g_small


class _FsdpPlan:
    AG_RIDES = {"in_proj": (("conv", None), ("wa", None), ("wb", None), ("wout", None), ("wg", 0), ("wd", 1)),
                "gdn_fwd": (("wg", 1),), "fox_fwd": (("wu", 0),), "ffn_gate": (("wu", 1),), "ffn_up": (("wd", 0),)}
    AG_GROUPS = (("conv",), ("wa", "wb", "wout"), ("wg",), ("wu",), ("wd",))
    RS_RIDES = {"d_hn_gate": ("wd",), "d_hn_up": ("wg",), "gdn_bwd": ("wu", "wout"), "fox_bwd": ("wa", "wb"),
                "d_xn": ("w_in", "conv")}

    def __init__(self, shards, d, cp, nb):
        self.shards, self.d, self.cp, self.nb = shards, d, cp, nb
        self.first, self.full = {}, {}
        self.queue, self.flying, self.slots = {}, [], {}

    def comm_for(self, key):
        if key in self.AG_RIDES:
            names = [n for n, _ in self.AG_RIDES[key]]
            rows = []
            for n, half in self.AG_RIDES[key]:
                r = self.shards[n].shape[0] // 2
                rows.append(None if half is None else (half * r, r))
            return _ag_first_comm([self.shards[n] for n in names], rows, [self.first.get(n) for n in names])
        self.flying = [n for n in self.RS_RIDES.get(key, ()) if n in self.queue]
        if self.flying:
            return _rs_chips_comm([self.queue.pop(n) for n in self.flying])
        return None

    def done(self, key, res):
        if key in self.AG_RIDES:
            self.first.update(zip([n for n, _ in self.AG_RIDES[key]], res))
        else:
            self.slots.update(zip(self.flying, res))
            self.flying = []

    def weight(self, name):
        if name not in self.full:
            group = next(g for g in self.AG_GROUPS if name in g)
            outs = _ag_forward([self.first[n] for n in group], name=f"all_gather_forward_{group[0]}")
            self.full.update(zip(group, outs))
        g = self.full[name]
        if name in ("wa", "wb", "conv"):
            return _cols_of_blocks(g)
        if name == "wout":
            return g.reshape(self.d, self.d)
        if name == "wd":
            return g.reshape(N_DEV * self.cp, self.d)
        return g

    def grad(self, name, g):
        if name == "wout":
            g = g.reshape(N_DEV, self.d // N_DEV, self.d)
        if name == "wd":
            g = g.reshape(N_DEV, self.cp, self.d)
        if name == "conv":
            g = _blocks_of_cols(g.astype(BF16))
        self.reduce(name, g)

    def grad_w_in(self, g_main, g_small):
        self.reduce("w_in", _w_in_grad_blocks([g_main], g_small, [(0, g_main.shape[1])], self.nb, name="w_in_grad_blocks"))

    def reduce(self, name, blocks):
        (land,) = _rs_sibling([blocks], name=f"grads_to_sibling_{name}")
        self.queue[name] = _pair_sum(blocks, land, name=f"pair_sum_{name}")

    def flush(self):
        if self.queue:
            outs = _comm_call(_rs_chips_comm(list(self.queue.values())), name="grads_to_chips_tail")
            self.slots.update(zip(self.queue, outs))
            self.queue = {}


def _carried(plan, key, fn, *args, **kw):
    comm = plan.comm_for(key)
    if comm is None:
        return fn(*args, **kw)
    res, comm_res = fn(*args, comm=comm, **kw)
    plan.done(key, comm_res)
    return res


def _local_step(x, target, w_main, w_small, plan,
                norm_mix_w, norm_ffn_w, gdn_norm_w, fox_q_w, fox_k_w, a_row, b_row):
    t, d = x.shape
    cp = plan.cp
    fp = N_DEV * cp
    n_main = w_main.shape[1]
    off_gb = OFF_GA + d
    tm = 1024
    rt = 128
    fcol = fp // 1024 if fp % 1024 == 0 else max(fp // 512, 1)

    (xn,) = _rowwise_fwd(_fn_norm, [(x, 0, d)], [norm_mix_w], [(d, BF16)], tm=rt, name="mix_norm")
    p_main = _carried(plan, "in_proj", _mm, xn, w_main, mode="nn", m=t, n=n_main, k=d, tm=tm, tn=512, tk=d,
                      out_dtype=BF16, name="in_proj")
    p_small = _mm(xn, w_small, mode="nn", m=t, n=LANES, k=d, tm=tm, tn=LANES, tk=d, out_dtype=F32, name="in_proj_small")
    (gates,) = _rowwise_fwd(_fn_gates, [(p_small, 0, LANES)], [a_row, b_row], [(LANES, F32)], tm=512, name="gates")
    conv_w = plan.weight("conv")
    qkv = _conv_fwd(p_main, conv_w, width=3 * HW, name="conv_fwd")
    o_gdn, s_all = _carried(plan, "gdn_fwd", _gdn_fwd, qkv, gates, name="gdn_fwd")
    gdn_rows = [(o_gdn, 0, HW), (p_main, OFF_ZA, HW)]
    (oa,) = _rowwise_fwd(_fn_gdn_out, gdn_rows, [gdn_norm_w], [(HW, BF16)], tm=512, ncol=HEADS, name="gdn_out")
    wa = plan.weight("wa")
    ya = _mm(oa, wa, mode="nn", m=t, n=d, k=HW, tm=tm, tn=1024, tk=HW, out_dtype=BF16, name="branch_a")
    qk_rows = [(p_main, OFF_QB, HW), (p_main, OFF_KB, HW)]
    qn, kn = _rowwise_fwd(_fn_qknorm, qk_rows, [fox_q_w, fox_k_w], [(HW, BF16), (HW, BF16)], tm=512, ncol=HEADS,
                          name="fox_qk_norm")
    blk = min(FOX_BLK, t)
    c4 = _fox_cumsum(gates, name="fox_cumsum").reshape(HEADS, t // blk, 1, blk)
    ob, lse = _carried(plan, "fox_fwd", _fox_fwd, qn, kn, p_main, c4, v_off=OFF_VB, name="fox_fwd")
    wb = plan.weight("wb")
    yb = _mm(ob, wb, mode="nn", m=t, n=d, k=HW, tm=tm, tn=1024, tk=HW, out_dtype=BF16, name="branch_b")
    mcol = 2 if d >= 2 * HW else 1
    merge_rows = [(p_main, OFF_GA, d), (p_main, off_gb, d), (ya, 0, d), (yb, 0, d)]
    (merged,) = _rowwise_fwd(_fn_merge, merge_rows, [], [(d, BF16)], tm=256, ncol=mcol, name="merge")
    wout = plan.weight("wout")
    h = _mm(merged, wout, mode="nn", m=t, n=d, k=d, tm=tm, tn=512, tk=d, out_dtype=F32, add=x, name="out_proj")
    (hn,) = _rowwise_fwd(_fn_norm, [(h, 0, d)], [norm_ffn_w], [(d, BF16)], tm=rt, name="ffn_norm")
    wg = plan.weight("wg")
    gate = _carried(plan, "ffn_gate", _mm, hn, wg, mode="nn", m=t, n=fp, k=d, tm=512, tn=cp, tk=d, out_dtype=BF16,
                    b_blocked=True, name="ffn_gate")
    wu = plan.weight("wu")
    up = _carried(plan, "ffn_up", _mm, hn, wu, mode="nn", m=t, n=fp, k=d, tm=512, tn=cp, tk=d, out_dtype=BF16,
                  b_blocked=True, name="ffn_up")
    (act,) = _rowwise_fwd(_fn_swiglu, [(gate, 0, fp), (up, 0, fp)], [], [(fp, BF16)], tm=512, ncol=fcol, name="swiglu")
    wd = plan.weight("wd")
    y = _mm(act, wd, mode="nn", m=t, n=d, k=fp, tm=512, tn=256, tk=fp, out_dtype=F32, add=h, name="ffn_down")
    dy, dyb, loss_row = _loss_head(y, target, tm=rt, name="loss_head")

    dact = _mm(dyb, wd, mode="nt", m=t, n=fp, k=d, tm=tm, tn=512, tk=d, out_dtype=BF16, name="d_act")
    plan.grad("wd", _mm(act, dyb, mode="tn", m=fp, n=d, k=t, tm=512, tn=1024, tk=t, out_dtype=BF16, name="dw_ffn_down"))
    dgate, dup = _rowwise_bwd(_fn_swiglu, [(gate, 0, fp), (up, 0, fp)], [], [dact], [BF16, BF16], tm=512, ncol=fcol,
                              name="d_swiglu")
    plan.grad("wg", _mm(hn, dgate, mode="tn", m=d, n=fp, k=t, tm=512, tn=cp, tk=t, out_dtype=BF16, out_blocked=True,
                        name="dw_ffn_gate"))
    dhn = _carried(plan, "d_hn_gate", _mm, dgate, wg, mode="nt", m=t, n=d, k=fp, tm=tm, tn=1024, tk=cp, out_dtype=F32,
                   b_blocked=True, name="d_hn_gate")
    dhn = _carried(plan, "d_hn_up", _mm, dup, wu, mode="nt", m=t, n=d, k=fp, tm=tm, tn=1024, tk=cp, out_dtype=F32,
                   add=dhn, b_blocked=True, name="d_hn_up")
    plan.grad("wu", _mm(hn, dup, mode="tn", m=d, n=fp, k=t, tm=512, tn=cp, tk=t, out_dtype=BF16, out_blocked=True,
                        name="dw_ffn_up"))
    dh, d_norm_ffn = _rowwise_bwd(_fn_norm, [(h, 0, d)], [norm_ffn_w], [dhn], [F32], tm=rt, name="d_ffn_norm", adds=[dy])
    dmerged = _mm(dh, wout, mode="nt", m=t, n=d, k=d, tm=512, tn=512, tk=d, out_dtype=BF16, name="d_merged")
    plan.grad("wout", _mm(merged, dh, mode="tn", m=d, n=d, k=t, tm=512, tn=512, tk=t, out_dtype=BF16, name="dw_out"))
    dga, dgb, dya, dyb2 = _rowwise_bwd(_fn_merge, merge_rows, [], [dmerged], [BF16] * 4, tm=256, ncol=mcol, name="d_merge")
    doa = _mm(dya, wa, mode="nt", m=t, n=HW, k=d, tm=tm, tn=512, tk=d, out_dtype=BF16, name="d_oa")
    plan.grad("wa", _mm(oa, dya, mode="tn", m=HW, n=d, k=t, tm=1024, tn=d // N_DEV, tk=t, out_dtype=BF16,
                        out_blocked=True, name="dw_branch_a"))
    dob = _mm(dyb2, wb, mode="nt", m=t, n=HW, k=d, tm=tm, tn=512, tk=d, out_dtype=BF16, name="d_ob")
    plan.grad("wb", _mm(ob, dyb2, mode="tn", m=HW, n=d, k=t, tm=1024, tn=d // N_DEV, tk=t, out_dtype=BF16,
                        out_blocked=True, name="dw_branch_b"))
    do_gdn, dza, d_gdn_norm = _rowwise_bwd(_fn_gdn_out, gdn_rows, [gdn_norm_w], [doa], [F32, BF16], tm=512,
                                           ncol=HEADS, name="d_gdn_out")
    dqkv, dgates_gdn = _carried(plan, "gdn_bwd", _gdn_bwd, qkv, gates, s_all, do_gdn, name="gdn_bwd")
    dp_qkv, dconv = _conv_bwd(p_main, conv_w, dqkv, width=3 * HW, name="conv_bwd")
    plan.grad("conv", dconv)
    delta = _fox_delta(qn, kn, p_main, c4, dob, lse, v_off=OFF_VB, name="fox_delta")
    dqn, dkn, dvb, dc4 = _carried(plan, "fox_bwd", _fox_bwd, qn, kn, p_main, c4, delta, dob, lse, v_off=OFF_VB,
                                  name="fox_bwd")
    dqb, dkb, d_fox_q, d_fox_k = _rowwise_bwd(_fn_qknorm, qk_rows, [fox_q_w, fox_k_w], [dqn, dkn], [BF16, BF16],
                                              tm=512, ncol=HEADS, name="d_fox_qk_norm")
    dgates = _fox_cumsum_bwd(dc4.reshape(HEADS, t), dgates_gdn, name="fox_cumsum_bwd")
    dsmall, d_a_row, d_b_row = _rowwise_bwd(_fn_gates, [(p_small, 0, LANES)], [a_row, b_row], [dgates], [F32],
                                            tm=512, name="d_gates")
    dp_main = _concat_cols([dp_qkv, dza, dqb, dkb, dvb, dga, dgb], name="d_p_main")
    plan.grad_w_in(_mm(xn, dp_main, mode="tn", m=d, n=n_main, k=t, tm=1024, tn=math.gcd(n_main, 1024), tk=t,
                       out_dtype=BF16, name="dw_in"),
                   _mm(xn, dsmall, mode="tn", m=d, n=LANES, k=t, tm=1024, tn=LANES, tk=t, out_dtype=BF16,
                       name="dw_in_small"))
    dxn = _mm(dsmall, w_small, mode="nt", m=t, n=d, k=LANES, tm=tm, tn=1024, tk=LANES, out_dtype=F32, name="d_xn_small")
    dxn = _carried(plan, "d_xn", _mm, dp_main, w_main, mode="nt", m=t, n=d, k=n_main, tm=tm, tn=1024,
                   tk=math.gcd(n_main, 2048),
                   out_dtype=F32, add=dxn, name="d_xn")
    grad_x, d_norm_mix = _rowwise_bwd(_fn_norm, [(x, 0, d)], [norm_mix_w], [dxn], [F32], tm=rt, name="d_mix_norm",
                                      adds=[dh])
    small = dict(norm_mix=d_norm_mix, norm_ffn=d_norm_ffn, gdn_norm=d_gdn_norm, fox_q=d_fox_q, fox_k=d_fox_k,
                 a_row=d_a_row, b_row=d_b_row)
    return loss_row[0, 0], grad_x, small


def _lane_row(pieces):
    row = jnp.zeros((1, LANES), F32)
    for off, p in pieces:
        row = lax.dynamic_update_slice(row, p.astype(F32), (0, off))
    return row


def _pack_small(norm_mix, norm_ffn, gdn_norm, fox_q, fox_k, a_log, dt_bias, b_f):
    rows = [norm_mix.reshape(-1, LANES), norm_ffn.reshape(-1, LANES), gdn_norm, fox_q, fox_k,
            _lane_row([(HEADS, a_log)]), _lane_row([(HEADS, dt_bias), (2 * HEADS, b_f)])]
    packed = jnp.concatenate(rows, axis=0)
    return _pad_rows(packed, -(-packed.shape[0] // 8) * 8)


def _unpack_small(p, d):
    nd = d // LANES
    r = 2 * nd
    return (p[0:nd].reshape(1, d), p[r + 3:r + 4, HEADS:2 * HEADS], p[r + 4:r + 5, HEADS:2 * HEADS], p[r:r + 1],
            p[r + 4:r + 5, 2 * HEADS:3 * HEADS], p[r + 1:r + 2], p[r + 2:r + 3], p[nd:r].reshape(1, d))


def _blocks_of_cols(a):
    r, c8 = a.shape
    return a.reshape(r, N_DEV, c8 // N_DEV).transpose(1, 0, 2)


def _cols_of_blocks(g):
    _, r, c = g.shape
    return g.transpose(1, 0, 2).reshape(r, N_DEV * c)


def kernel(x, norm_mix_w, w_in, conv_w, a_log, dt_bias, gdn_norm_w, fox_b_f, fox_q_norm_w, fox_k_norm_w, w_branch_a, w_branch_b, w_out, norm_ffn_w, w_ffn_gate, w_ffn_up, w_ffn_down, loss_target, m_norm_mix_w, m_w_in, m_conv_w, m_a_log, m_dt_bias, m_gdn_norm_w, m_fox_b_f, m_fox_q_norm_w, m_fox_k_norm_w, m_w_branch_a, m_w_branch_b, m_w_out, m_norm_ffn_w, m_w_ffn_gate, m_w_ffn_up, m_w_ffn_down, v_norm_mix_w, v_w_in, v_conv_w, v_a_log, v_dt_bias, v_gdn_norm_w, v_fox_b_f, v_fox_q_norm_w, v_fox_k_norm_w, v_w_branch_a, v_w_branch_b, v_w_out, v_norm_ffn_w, v_w_ffn_gate, v_w_ffn_up, v_w_ffn_down):
    d = x.shape[-1]
    cp = -(-w_ffn_down.shape[1] // LANES) * LANES
    nb = w_in.shape[2]

    g_in = _all_gather_relayed(w_in[0].astype(BF16), name="w_in_all_gather")
    w_main, w_small = _w_in_to_aligned(g_in, name="w_in_to_aligned")
    plan = _FsdpPlan(dict(conv=conv_w[0], wa=w_branch_a[0].astype(BF16), wb=w_branch_b[0].astype(BF16), wout=w_out[0].astype(BF16),
                          wg=_pad_cols(w_ffn_gate[0].astype(BF16), cp), wu=_pad_cols(w_ffn_up[0].astype(BF16), cp),
                          wd=_pad_rows(w_ffn_down[0].astype(BF16), cp)), d, cp, nb)
    a_row = _lane_row([(HEADS, a_log)])
    b_row = _lane_row([(HEADS, dt_bias), (2 * HEADS, fox_b_f)])

    loss_part, grad_x, gs = _local_step(
        x[0], loss_target[0], w_main, w_small, plan,
        norm_mix_w, norm_ffn_w, gdn_norm_w, fox_q_norm_w, fox_k_norm_w, a_row, b_row)
    loss = lax.psum(loss_part, ("x", "y", "c"))

    plan.flush()
    big = dict(w_in=("w_in", w_in, m_w_in, v_w_in), w_branch_a=("wa", w_branch_a, m_w_branch_a, v_w_branch_a),
               w_branch_b=("wb", w_branch_b, m_w_branch_b, v_w_branch_b), w_out=("wout", w_out, m_w_out, v_w_out),
               w_ffn_gate=("wg", w_ffn_gate, m_w_ffn_gate, v_w_ffn_gate), w_ffn_up=("wu", w_ffn_up, m_w_ffn_up, v_w_ffn_up),
               w_ffn_down=("wd", w_ffn_down, m_w_ffn_down, v_w_ffn_down), conv_w=("conv", conv_w, m_conv_w, v_conv_w))
    res = {}
    for nm, (key, w, m, v) in big.items():
        res[nm] = [o[None] for o in _adamw(plan.slots[key], w[0], m[0], v[0], name=f"adamw_{nm}")]

    g_small = _pack_small(gs["norm_mix"], gs["norm_ffn"], gs["gdn_norm"], gs["fox_q"], gs["fox_k"],
                          gs["a_row"][:, HEADS:2 * HEADS], gs["b_row"][:, HEADS:2 * HEADS],
                          gs["b_row"][:, 2 * HEADS:3 * HEADS])
    (g_small_all,) = _all_gather([g_small], name="small_grads_all_gather")
    w_small_p = _pack_small(norm_mix_w, norm_ffn_w, gdn_norm_w, fox_q_norm_w, fox_k_norm_w, a_log, dt_bias, fox_b_f)
    m_small_p = _pack_small(m_norm_mix_w, m_norm_ffn_w, m_gdn_norm_w, m_fox_q_norm_w, m_fox_k_norm_w, m_a_log,
                            m_dt_bias, m_fox_b_f)
    v_small_p = _pack_small(v_norm_mix_w, v_norm_ffn_w, v_gdn_norm_w, v_fox_q_norm_w, v_fox_k_norm_w, v_a_log,
                            v_dt_bias, v_fox_b_f)
    small_res = [_unpack_small(o, d) for o in _adamw(g_small_all, w_small_p, m_small_p, v_small_p, name="adamw_small")]

    def group(k):
        s = small_res[k]
        return [s[0], res["w_in"][k], res["conv_w"][k], s[1], s[2], s[3], s[4], s[5], s[6], res["w_branch_a"][k],
                res["w_branch_b"][k], res["w_out"][k], s[7], res["w_ffn_gate"][k], res["w_ffn_up"][k],
                res["w_ffn_down"][k]]

    return (loss, grad_x[None], *group(0), *group(1), *group(2), *group(3))
```

```python
import functools
import math

import jax
import jax.numpy as jnp
from jax import lax
from jax.experimental import pallas as pl
from jax.experimental.pallas import tpu as pltpu

F32 = jnp.float32
BF16 = jnp.bfloat16
HI = lax.Precision.HIGHEST
SOLVE_PRECISION = lax.Precision.HIGH
MESH = pl.DeviceIdType.MESH

EPS = 1e-6
HEADS = 16
DH = 128
HW = HEADS * DH
CHUNK = 64
CONV_K = 4
N_DEV = 8
LANES = 128
VMEM_LIMIT = 52 * 1024 * 1024

ADAM_LR = 0.001
ADAM_B1 = 0.9
ADAM_B2 = 0.999
ADAM_EPS = 1e-08
ADAM_WD = 0.01
ADAM_STEP = 10

OFF_QA, OFF_KA, OFF_VA, OFF_ZA, OFF_QB, OFF_KB, OFF_VB, OFF_GA = 0, HW, 2 * HW, 3 * HW, 4 * HW, 5 * HW, 6 * HW, 7 * HW


def _cparams(sem=None, vmem=VMEM_LIMIT):
    return pltpu.CompilerParams(dimension_semantics=sem, vmem_limit_bytes=vmem)


class _Comm:
    def __init__(self, ins, out_shapes, sems, start, finish, aliases=None):
        self.ins, self.out_shapes, self.sems = list(ins), list(out_shapes), list(sems)
        self.start, self.finish, self.aliases = start, finish, dict(aliases or {})


def _pcall(body, *, name, grid, in_specs, out_specs, out_shape, args, sem, scratch_shapes=(), comm=None):
    multi = isinstance(out_shape, (list, tuple))
    out_specs = list(out_specs) if multi else [out_specs]
    out_shape = list(out_shape) if multi else [out_shape]
    scratch_shapes = list(scratch_shapes)
    if comm is None:
        res = pl.pallas_call(body, name=name, grid=grid, in_specs=list(in_specs), out_specs=out_specs,
                             out_shape=out_shape, scratch_shapes=scratch_shapes, compiler_params=_cparams(sem))(*args)
        return res if multi else res[0]
    ni, no, ns = len(in_specs), len(out_specs), len(scratch_shapes)
    ci, co = len(comm.ins), len(comm.out_shapes)

    def wrapped(*refs):
        cin = refs[ni:ni + ci]
        outs = refs[ni + ci:ni + ci + no]
        cout = refs[ni + ci + no:ni + ci + no + co]
        scr = refs[ni + ci + no + co:ni + ci + no + co + ns]
        csem = refs[ni + ci + no + co + ns:]
        ids = [pl.program_id(ax) for ax in range(len(grid))]
        first = functools.reduce(jnp.logical_and, [i == 0 for i in ids])
        last = functools.reduce(jnp.logical_and, [i == g - 1 for i, g in zip(ids, grid)])

        @pl.when(first)
        def _():
            comm.start(cin, cout, csem)

        body(*refs[:ni], *outs, *scr)

        @pl.when(last)
        def _():
            comm.finish(cin, cout, csem)

    any_spec = pl.BlockSpec(memory_space=pl.ANY)
    res = pl.pallas_call(
        wrapped, name=name, grid=grid, in_specs=list(in_specs) + [any_spec] * ci,
        out_specs=out_specs + [any_spec] * co, out_shape=out_shape + comm.out_shapes,
        scratch_shapes=scratch_shapes + comm.sems,
        input_output_aliases={ni + i: no + o for i, o in comm.aliases.items()},
        compiler_params=_cparams(("arbitrary",) * len(grid)))(*args, *comm.ins)
    return (res[:no] if multi else res[0]), res[no:]


def _mm(a, b, *, mode, m, n, k, tm, tn, tk, out_dtype, name, a_off=(0, 0), b_off=(0, 0), add=None,
        b_blocked=False, out_blocked=False, comm=None):
    tm, tn, tk = min(tm, m), min(tn, n), min(tk, k)
    assert m % tm == 0 and n % tn == 0 and k % tk == 0, (name, m, n, k, tm, tn, tk)
    nk = k // tk
    if mode == "nn":
        a_blk, b_blk = (tm, tk), (tk, tn)
        ao, bo = (a_off[0] // tm, a_off[1] // tk), (b_off[0] // tk, b_off[1] // tn)
        a_map = lambda i, j, kk: (i + ao[0], kk + ao[1])
        b_map = lambda i, j, kk: (kk + bo[0], j + bo[1])
        dims = (((1,), (0,)), ((), ()))
        if b_blocked:
            assert b.shape == (n // tn, k, tn) and b_off == (0, 0), (name, b.shape)
            b_blk, b_map = (None, tk, tn), lambda i, j, kk: (j, kk, 0)
    elif mode == "nt":
        a_blk, b_blk = (tm, tk), (tn, tk)
        ao, bo = (a_off[0] // tm, a_off[1] // tk), (b_off[0] // tn, b_off[1] // tk)
        a_map = lambda i, j, kk: (i + ao[0], kk + ao[1])
        b_map = lambda i, j, kk: (j + bo[0], kk + bo[1])
        dims = (((1,), (1,)), ((), ()))
        if b_blocked and tk == k:
            kblocks, cblk = b.shape[0], b.shape[2]
            assert b.shape == (kblocks, n, cblk) and kblocks * cblk == k and b_off == (0, 0), (name, b.shape)
            b_blk, b_map = (kblocks, tn, cblk), lambda i, j, kk: (0, j, 0)
        elif b_blocked:
            assert b.shape == (nk, n, tk) and b_off == (0, 0), (name, b.shape)
            b_blk, b_map = (None, tn, tk), lambda i, j, kk: (kk, j, 0)
    else:
        assert not b_blocked
        a_blk, b_blk = (tk, tm), (tk, tn)
        ao, bo = (a_off[0] // tk, a_off[1] // tm), (b_off[0] // tk, b_off[1] // tn)
        a_map = lambda i, j, kk: (kk + ao[0], i + ao[1])
        b_map = lambda i, j, kk: (kk + bo[0], j + bo[1])
        dims = (((0,), (0,)), ((), ()))
    if not b_blocked:
        for off, blk in ((a_off, a_blk), (b_off, b_blk)):
            assert off[0] % blk[0] == 0 and off[1] % blk[1] == 0, (name, off, blk)
    has_add = add is not None

    def body(*refs):
        if has_add:
            a_ref, b_ref, c_ref, o_ref, acc = refs
        else:
            a_ref, b_ref, o_ref, acc = refs
            c_ref = None
        if len(b_blk) == 3 and b_blk[0] is not None:
            cblk = b_blk[2]
            p = sum(lax.dot_general(a_ref[:, q * cblk:(q + 1) * cblk].astype(BF16), b_ref[q].astype(BF16), dims,
                                    preferred_element_type=F32) for q in range(b_blk[0]))
        else:
            p = lax.dot_general(a_ref[...].astype(BF16), b_ref[...].astype(BF16), dims, preferred_element_type=F32)
        if nk == 1:
            if has_add:
                p = p + c_ref[...].astype(F32)
            o_ref[...] = p.astype(o_ref.dtype)
        else:
            kk = pl.program_id(2)

            @pl.when(kk == 0)
            def _():
                acc[...] = p + c_ref[...].astype(F32) if has_add else p

            @pl.when(kk > 0)
            def _():
                acc[...] += p

            @pl.when(kk == nk - 1)
            def _():
                o_ref[...] = acc[...].astype(o_ref.dtype)

    in_specs = [pl.BlockSpec(a_blk, a_map), pl.BlockSpec(b_blk, b_map)]
    args = [a, b]
    if has_add:
        in_specs.append(pl.BlockSpec((tm, tn), lambda i, j, kk: (i, j)))
        args.append(add)
    acc_shape = (tm, tn) if nk > 1 else (8, LANES)
    if out_blocked:
        out_spec = pl.BlockSpec((None, tm, tn), lambda i, j, kk: (j, i, 0))
        out_shape = jax.ShapeDtypeStruct((n // tn, m, tn), out_dtype)
    else:
        out_spec = pl.BlockSpec((tm, tn), lambda i, j, kk: (i, j))
        out_shape = jax.ShapeDtypeStruct((m, n), out_dtype)
    return _pcall(body, name=name, grid=(m // tm, n // tn, nk), in_specs=in_specs, out_specs=out_spec,
                  out_shape=out_shape, scratch_shapes=[pltpu.VMEM(acc_shape, F32)], args=args,
                  sem=("parallel", "parallel", "arbitrary"), comm=comm)


def _row_specs(rows, tm, ncol):
    specs = []
    for arr, off, width in rows:
        bw = width // ncol
        assert width % ncol == 0 and off % bw == 0, (off, width, ncol)
        ob = off // bw
        specs.append(pl.BlockSpec((tm, bw), lambda i, j, ob=ob: (i, j + ob)))
    return specs


def _rowwise_fwd(fn, rows, params, outs, *, tm, ncol=1, name):
    t = rows[0][0].shape[0]
    tm = min(tm, t)
    nr, npar = len(rows), len(params)

    def body(*refs):
        ins = [r[...].astype(F32) for r in refs[:nr + npar]]
        res = fn(*ins)
        for o_ref, val in zip(refs[nr + npar:], res):
            o_ref[...] = val.astype(o_ref.dtype)

    in_specs = _row_specs(rows, tm, ncol) + [pl.BlockSpec(p.shape, lambda i, j: (0, 0)) for p in params]
    out_specs = [pl.BlockSpec((tm, w // ncol), lambda i, j: (i, j)) for w, _ in outs]
    out_shape = [jax.ShapeDtypeStruct((t, w), dt) for w, dt in outs]
    return pl.pallas_call(
        body, name=name, grid=(t // tm, ncol), in_specs=in_specs, out_specs=out_specs, out_shape=out_shape,
        compiler_params=_cparams(("parallel", "parallel")),
    )(*[r[0] for r in rows], *params)


def _rowwise_bwd(fn, rows, params, cts, grad_dtypes, *, tm, ncol=1, name, adds=None):
    t = rows[0][0].shape[0]
    tm = min(tm, t)
    nr, npar, nct = len(rows), len(params), len(cts)
    adds = adds or [None] * nr
    add_idx = [i for i, a in enumerate(adds) if a is not None]

    def body(*refs):
        ins = [r[...].astype(F32) for r in refs[:nr + npar]]
        ct = tuple(r[...].astype(F32) for r in refs[nr + npar:nr + npar + nct])
        add_refs = refs[nr + npar + nct:nr + npar + nct + len(add_idx)]
        outs = refs[nr + npar + nct + len(add_idx):]
        _, vjp = jax.vjp(lambda *a: tuple(fn(*a)), *ins)
        grads = vjp(ct)
        extra = dict(zip(add_idx, add_refs))
        for i in range(nr):
            g = grads[i]
            if i in extra:
                g = g + extra[i][...].astype(F32)
            outs[i][...] = g.astype(outs[i].dtype)
        first = jnp.logical_and(pl.program_id(0) == 0, pl.program_id(1) == 0)
        for pi in range(npar):
            o_ref = outs[nr + pi]
            g = grads[nr + pi]

            @pl.when(first)
            def _(o_ref=o_ref, g=g):
                o_ref[...] = g

            @pl.when(jnp.logical_not(first))
            def _(o_ref=o_ref, g=g):
                o_ref[...] += g

    in_specs = (_row_specs(rows, tm, ncol)
                + [pl.BlockSpec(p.shape, lambda i, j: (0, 0)) for p in params]
                + [pl.BlockSpec((tm, c.shape[1] // ncol), lambda i, j: (i, j)) for c in cts]
                + [pl.BlockSpec((tm, adds[i].shape[1] // ncol), lambda i, j: (i, j)) for i in add_idx])
    out_specs = ([pl.BlockSpec((tm, w // ncol), lambda i, j: (i, j)) for _, _, w in rows]
                 + [pl.BlockSpec(p.shape, lambda i, j: (0, 0)) for p in params])
    out_shape = ([jax.ShapeDtypeStruct((t, w), dt) for (_, _, w), dt in zip(rows, grad_dtypes)]
                 + [jax.ShapeDtypeStruct(p.shape, F32) for p in params])
    return pl.pallas_call(
        body, name=name, grid=(t // tm, ncol), in_specs=in_specs, out_specs=out_specs, out_shape=out_shape,
        compiler_params=_cparams(("arbitrary", "arbitrary")),
    )(*[r[0] for r in rows], *params, *cts, *[adds[i] for i in add_idx])


def _rms(x, w):
    return x * lax.rsqrt(jnp.mean(x * x, axis=-1, keepdims=True) + EPS) * w


def _fn_norm(x, w):
    return (_rms(x, w),)


def _fn_gates(z, a_row, b_row):
    lane = lax.broadcasted_iota(jnp.int32, z.shape, 1)
    beta = jax.nn.sigmoid(z)
    g = -jnp.exp(a_row) * jax.nn.softplus(z + b_row)
    logf = jax.nn.log_sigmoid(z + b_row)
    return (jnp.where(lane < HEADS, beta, jnp.where(lane < 2 * HEADS, g, jnp.where(lane < 3 * HEADS, logf, 0.0))),)


def _fn_qknorm(q, k, qw, kw):
    return _rms(q, qw), _rms(k, kw)


def _fn_gdn_out(o, z, w):
    return (_rms(o, w) * jax.nn.silu(z),)


def _fn_merge(ga, gb, ya, yb):
    return (jax.nn.sigmoid(ga) * ya + jax.nn.sigmoid(gb) * yb,)


def _fn_swiglu(g, u):
    return (jax.nn.silu(g) * u,)


def _loss_head(y, target, *, tm, name):
    t, d = y.shape
    tm = min(tm, t)

    def body(y_ref, t_ref, dyf_ref, dyb_ref, loss_ref):
        err = y_ref[...] - t_ref[...]
        dy = err * (1.0 / d)
        dyf_ref[...] = dy
        dyb_ref[...] = dy.astype(BF16)
        part = jnp.sum(err * err) * (0.5 / d)

        @pl.when(pl.program_id(0) == 0)
        def _():
            loss_ref[...] = jnp.zeros_like(loss_ref)

        loss_ref[...] += part

    blk = pl.BlockSpec((tm, d), lambda i: (i, 0))
    return pl.pallas_call(
        body, name=name, grid=(t // tm,), in_specs=[blk, blk],
        out_specs=[blk, blk, pl.BlockSpec((1, LANES), lambda i: (0, 0))],
        out_shape=[jax.ShapeDtypeStruct((t, d), F32), jax.ShapeDtypeStruct((t, d), BF16),
                   jax.ShapeDtypeStruct((1, LANES), F32)],
        compiler_params=_cparams(("arbitrary",)),
    )(y, target)


def _shift_down(x, s):
    if s == 0:
        return x
    row = lax.broadcasted_iota(jnp.int32, x.shape, 0)
    return jnp.where(row >= s, pltpu.roll(x, s, 0), 0.0)


def _shift_up(x, s):
    if s == 0:
        return x
    t = x.shape[0]
    row = lax.broadcasted_iota(jnp.int32, x.shape, 0)
    return jnp.where(row < t - s, pltpu.roll(x, t - s, 0), 0.0)


def _conv_pre(x, w):
    y = x * w[CONV_K - 1:CONV_K, :]
    for i in range(CONV_K - 1):
        y = y + _shift_down(x, CONV_K - 1 - i) * w[i:i + 1, :]
    return y


def _conv_fwd(p_main, conv_w, *, width, name):
    t = p_main.shape[0]
    tc = LANES

    def body(x_ref, w_ref, o_ref):
        y = _conv_pre(x_ref[...].astype(F32), w_ref[...])
        o_ref[...] = y * jax.nn.sigmoid(y)

    return pl.pallas_call(
        body, name=name, grid=(width // tc,),
        in_specs=[pl.BlockSpec((t, tc), lambda j: (0, j)), pl.BlockSpec((CONV_K, tc), lambda j: (0, j))],
        out_specs=pl.BlockSpec((t, tc), lambda j: (0, j)),
        out_shape=jax.ShapeDtypeStruct((t, width), F32),
        compiler_params=_cparams(("parallel",)),
    )(p_main, conv_w)


def _conv_bwd(p_main, conv_w, dy, *, width, name):
    t = p_main.shape[0]
    tc = LANES

    def body(x_ref, w_ref, dy_ref, dx_ref, dw_ref):
        x = x_ref[...].astype(F32)
        w = w_ref[...]
        pre = _conv_pre(x, w)
        sg = jax.nn.sigmoid(pre)
        dpre = dy_ref[...] * (sg * (1.0 + pre * (1.0 - sg)))
        dx = dpre * w[CONV_K - 1:CONV_K, :]
        dws = []
        for i in range(CONV_K - 1):
            s = CONV_K - 1 - i
            dx = dx + _shift_up(dpre, s) * w[i:i + 1, :]
            dws.append(jnp.sum(_shift_down(x, s) * dpre, axis=0, keepdims=True))
        dws.append(jnp.sum(x * dpre, axis=0, keepdims=True))
        dx_ref[...] = dx.astype(dx_ref.dtype)
        dw_ref[...] = jnp.concatenate(dws, axis=0)

    return pl.pallas_call(
        body, name=name, grid=(width // tc,),
        in_specs=[pl.BlockSpec((t, tc), lambda j: (0, j)), pl.BlockSpec((CONV_K, tc), lambda j: (0, j)),
                  pl.BlockSpec((t, tc), lambda j: (0, j))],
        out_specs=[pl.BlockSpec((t, tc), lambda j: (0, j)), pl.BlockSpec((CONV_K, tc), lambda j: (0, j))],
        out_shape=[jax.ShapeDtypeStruct((t, width), BF16), jax.ShapeDtypeStruct((CONV_K, width), F32)],
        compiler_params=_cparams(("parallel",)),
    )(p_main, conv_w, dy)


def _bmm(a, b, spec, precision=None):
    return jnp.einsum(spec, a, b, preferred_element_type=F32, precision=precision)


def _iota2(shape, dim):
    return lax.broadcasted_iota(jnp.int32, shape, dim)


@jax.custom_vjp
def _tri_inverse(a):
    return _tri_inverse_levels(a)


def _tri_inverse_fwd(a):
    t = _tri_inverse_levels(a)
    return t, t


def _tri_inverse_bwd(t, g):
    x = _bmm(t, g, "hji,hjk->hik", SOLVE_PRECISION)
    return (-_bmm(x, t, "hik,hjk->hij", SOLVE_PRECISION),)


_tri_inverse.defvjp(_tri_inverse_fwd, _tri_inverse_bwd)


def _tri_inverse_levels(a):
    c = a.shape[-1]
    r, m = _iota2((c, c), 0), _iota2((c, c), 1)
    eye = (r == m).astype(F32)
    inv = None
    b = 1
    while b < c:
        mask = jnp.logical_and(r // (2 * b) == m // (2 * b), jnp.logical_and(r % (2 * b) >= b, m % (2 * b) < b))
        off = jnp.where(mask[None], a, 0.0)
        if inv is None:
            inv = eye[None] - off
        else:
            inv = inv - _bmm(_bmm(inv, off, "hij,hjk->hik", SOLVE_PRECISION), inv, "hij,hjk->hik", SOLVE_PRECISION)
        b *= 2
    return inv


def _gdn_chunk(s, q3, k3, v3, b3, gc3):
    c = q3.shape[1]
    r, m = _iota2((c, c), 0), _iota2((c, c), 1)
    tril_incl = (r >= m)[None]
    tril_strict = (r > m)[None]
    eye = (r == m).astype(F32)[None]
    qn = q3 * lax.rsqrt(jnp.sum(q3 * q3, axis=-1, keepdims=True) + EPS) * (DH ** -0.5)
    kn = k3 * lax.rsqrt(jnp.sum(k3 * k3, axis=-1, keepdims=True) + EPS)
    ones = jnp.ones((q3.shape[0], c, c), F32)
    gc_row = _bmm(ones, gc3 * eye, "hij,hjk->hik", SOLVE_PRECISION)
    decay = jnp.where(tril_incl, jnp.exp(jnp.where(tril_incl, gc3 - gc_row, 0.0)), 0.0)
    a = jnp.where(tril_strict, _bmm(kn, kn, "hcd,hmd->hcm") * decay * b3, 0.0)
    tinv = _tri_inverse(a)
    egc = jnp.exp(gc3)
    u = _bmm(tinv, v3 * b3, "hij,hjk->hik", SOLVE_PRECISION)
    w = _bmm(tinv, kn * (b3 * egc), "hij,hjk->hik", SOLVE_PRECISION)
    qk = _bmm(qn, kn, "hcd,hmd->hcm") * decay
    v_new = u - _bmm(w, s, "hcd,hdv->hcv")
    o = _bmm(qn * egc, s, "hcd,hdv->hcv") + _bmm(qk, v_new, "hcm,hmv->hcv")
    row = _iota2((c, 1), 0)[None]
    g_last = jnp.sum(jnp.where(row == c - 1, gc3, 0.0), axis=1, keepdims=True)
    s_new = s * jnp.exp(g_last) + _bmm(kn * jnp.exp(g_last - gc3), v_new, "hcd,hcv->hdv")
    return s_new, o


GDN_HEAD_GROUP = 8


def _split_heads(ref, off, h0):
    return jnp.stack([ref[:, off + h * DH:off + (h + 1) * DH].astype(F32)
                      for h in range(h0, h0 + GDN_HEAD_GROUP)], axis=0)


def _store_heads(ref, x3, off, h0):
    for i in range(GDN_HEAD_GROUP):
        h = h0 + i
        ref[:, off + h * DH:off + (h + 1) * DH] = x3[i].astype(ref.dtype)


def _lane_cols(tile, lane0):
    lane = _iota2(tile.shape, 1)
    return jnp.stack([jnp.sum(jnp.where(lane == lane0 + i, tile, 0.0), axis=1, keepdims=True)
                      for i in range(GDN_HEAD_GROUP)], axis=0)


def _cols_to_lanes(cols3, lane0, shape):
    lane = _iota2(shape, 1)
    out = jnp.zeros(shape, F32)
    for i in range(GDN_HEAD_GROUP):
        out = out + jnp.where(lane == lane0 + i, cols3[i], 0.0)
    return out


def _chunk_cumsum_matrix():
    r, m = _iota2((CHUNK, CHUNK), 0), _iota2((CHUNK, CHUNK), 1)
    return (r >= m).astype(F32)


def _gdn_inputs(qkv_ref, gt, gcum, h0):
    return (_split_heads(qkv_ref, 0, h0), _split_heads(qkv_ref, HW, h0), _split_heads(qkv_ref, 2 * HW, h0),
            _lane_cols(gt, h0), _lane_cols(gcum, HEADS + h0))


def _gdn_fwd(qkv, gates, *, name, comm=None):
    t = qkv.shape[0]
    n = t // CHUNK

    def body(qkv_ref, gt_ref, o_ref, sall_ref, s_scr):
        @pl.when(pl.program_id(0) == 0)
        def _():
            s_scr[...] = jnp.zeros_like(s_scr)

        gt = gt_ref[...]
        gcum = jnp.dot(_chunk_cumsum_matrix(), gt, preferred_element_type=F32, precision=HI)
        for h0 in range(0, HEADS, GDN_HEAD_GROUP):
            grp = pl.ds(h0, GDN_HEAD_GROUP)
            s = s_scr[grp]
            sall_ref[0, grp] = s
            s_new, o3 = _gdn_chunk(s, *_gdn_inputs(qkv_ref, gt, gcum, h0))
            s_scr[grp] = s_new
            _store_heads(o_ref, o3, 0, h0)

    return _pcall(
        body, name=name, grid=(n,),
        in_specs=[pl.BlockSpec((CHUNK, 3 * HW), lambda i: (i, 0)), pl.BlockSpec((CHUNK, LANES), lambda i: (i, 0))],
        out_specs=[pl.BlockSpec((CHUNK, HW), lambda i: (i, 0)),
                   pl.BlockSpec((1, HEADS, DH, DH), lambda i: (i, 0, 0, 0))],
        out_shape=[jax.ShapeDtypeStruct((t, HW), F32), jax.ShapeDtypeStruct((n, HEADS, DH, DH), F32)],
        scratch_shapes=[pltpu.VMEM((HEADS, DH, DH), F32)], sem=("arbitrary",), args=(qkv, gates), comm=comm)


def _gdn_bwd(qkv, gates, s_all, do, *, name, comm=None):
    t = qkv.shape[0]
    n = t // CHUNK

    def body(qkv_ref, gt_ref, sall_ref, do_ref, dqkv_ref, dgt_ref, ds_scr):
        @pl.when(pl.program_id(0) == 0)
        def _():
            ds_scr[...] = jnp.zeros_like(ds_scr)

        gt = gt_ref[...]
        cum = _chunk_cumsum_matrix()
        gcum = jnp.dot(cum, gt, preferred_element_type=F32, precision=HI)
        shape = (CHUNK, LANES)
        dbeta = jnp.zeros(shape, F32)
        dgcum = jnp.zeros(shape, F32)
        for h0 in range(0, HEADS, GDN_HEAD_GROUP):
            grp = pl.ds(h0, GDN_HEAD_GROUP)
            _, vjp = jax.vjp(_gdn_chunk, sall_ref[0, grp], *_gdn_inputs(qkv_ref, gt, gcum, h0))
            ds, dq3, dk3, dv3, db3, dgc3 = vjp((ds_scr[grp], _split_heads(do_ref, 0, h0)))
            ds_scr[grp] = ds
            _store_heads(dqkv_ref, dq3, 0, h0)
            _store_heads(dqkv_ref, dk3, HW, h0)
            _store_heads(dqkv_ref, dv3, 2 * HW, h0)
            dbeta = dbeta + _cols_to_lanes(db3, h0, shape)
            dgcum = dgcum + _cols_to_lanes(dgc3, HEADS + h0, shape)
        dg = lax.dot_general(cum, dgcum, (((0,), (0,)), ((), ())), preferred_element_type=F32, precision=HI)
        dgt_ref[...] = dbeta + dg

    rev = lambda i: n - 1 - i
    return _pcall(
        body, name=name, grid=(n,),
        in_specs=[pl.BlockSpec((CHUNK, 3 * HW), lambda i: (rev(i), 0)), pl.BlockSpec((CHUNK, LANES), lambda i: (rev(i), 0)),
                  pl.BlockSpec((1, HEADS, DH, DH), lambda i: (rev(i), 0, 0, 0)),
                  pl.BlockSpec((CHUNK, HW), lambda i: (rev(i), 0))],
        out_specs=[pl.BlockSpec((CHUNK, 3 * HW), lambda i: (rev(i), 0)), pl.BlockSpec((CHUNK, LANES), lambda i: (rev(i), 0))],
        out_shape=[jax.ShapeDtypeStruct((t, 3 * HW), F32), jax.ShapeDtypeStruct((t, LANES), F32)],
        scratch_shapes=[pltpu.VMEM((HEADS, DH, DH), F32)], sem=("arbitrary",), args=(qkv, gates, s_all, do), comm=comm)


FOX_BLK = 512
FOX_ROW_SPLIT = 1
NEG = -1e30


def _fox_cumsum(gates, *, name):
    t = gates.shape[0]
    blk = min(FOX_BLK, t)

    def body(g_ref, c_ref):
        r, m = _iota2((blk, blk), 0), _iota2((blk, blk), 1)
        upper = (r <= m).astype(F32)
        carry = jnp.zeros((HEADS, 1), F32)
        for b in range(t // blk):
            lf = g_ref[b * blk:(b + 1) * blk, :].T[2 * HEADS:3 * HEADS, :]
            c_ref[:, b * blk:(b + 1) * blk] = jnp.dot(lf, upper, preferred_element_type=F32, precision=HI) + carry
            carry = carry + jnp.sum(lf, axis=1, keepdims=True)

    return pl.pallas_call(body, name=name, out_shape=jax.ShapeDtypeStruct((HEADS, t), F32),
                          compiler_params=_cparams())(gates)


def _fox_cumsum_bwd(dc, dgates_gdn, *, name):
    t = dc.shape[1]
    blk = min(FOX_BLK, t)

    def body(dc_ref, dg_ref, o_ref):
        r, m = _iota2((blk, blk), 0), _iota2((blk, blk), 1)
        lower = (r >= m).astype(F32)
        carry = jnp.zeros((HEADS, 1), F32)
        for b in reversed(range(t // blk)):
            d = dc_ref[:, b * blk:(b + 1) * blk]
            dlf = jnp.dot(d, lower, preferred_element_type=F32, precision=HI) + carry
            carry = carry + jnp.sum(d, axis=1, keepdims=True)
            tile = jnp.concatenate([jnp.zeros((2 * HEADS, blk), F32), dlf,
                                    jnp.zeros((LANES - 3 * HEADS, blk), F32)], axis=0)
            o_ref[b * blk:(b + 1) * blk, :] = tile.T + dg_ref[b * blk:(b + 1) * blk, :]

    return pl.pallas_call(body, name=name, out_shape=jax.ShapeDtypeStruct((t, LANES), F32),
                          compiler_params=_cparams())(dc, dgates_gdn)


def _fox_logits(q, k, c_row, row0=None):
    s = lax.dot_general(q, k, (((1,), (1,)), ((), ())), preferred_element_type=F32) * (DH ** -0.5) - c_row
    if row0 is None:
        return s
    return jnp.where(row0 + _iota2(s.shape, 0) >= _iota2(s.shape, 1), s, NEG)


def _fox_fwd(qn, kn, p_main, c4, *, v_off, name, comm=None):
    t = qn.shape[0]
    blk = min(FOX_BLK, t)
    nb = t // blk
    vb = v_off // DH

    def body(q_ref, k_ref, v_ref, c_ref, o_ref, lse_ref):
        qi = pl.program_id(1)
        q = q_ref[...]

        def step(j, carry, diagonal=False):
            m, l, acc = carry
            rows = pl.ds(pl.multiple_of(j * blk, blk), blk)
            s = _fox_logits(q, k_ref[rows, :], c_ref[0, j], 0 if diagonal else None)
            m_new = jnp.maximum(m, jnp.max(s, axis=1, keepdims=True))
            p = jnp.exp(s - m_new)
            scale = jnp.exp(m - m_new)
            l = scale * l + jnp.sum(p, axis=1, keepdims=True)
            acc = scale * acc + jnp.dot(p.astype(BF16), v_ref[rows, :], preferred_element_type=F32)
            return m_new, l, acc

        init = (jnp.full((blk, 1), NEG, F32), jnp.zeros((blk, 1), F32), jnp.zeros((blk, DH), F32))
        m, l, acc = step(qi, lax.fori_loop(0, qi, step, init), diagonal=True)
        o_ref[...] = (acc / l).astype(o_ref.dtype)
        lse_ref[0] = m + jnp.log(l)

    return _pcall(
        body, name=name, grid=(HEADS, nb),
        in_specs=[pl.BlockSpec((blk, DH), lambda h, i: (i, h)), pl.BlockSpec((t, DH), lambda h, i: (0, h)),
                  pl.BlockSpec((t, DH), lambda h, i: (0, vb + h)), pl.BlockSpec((1, nb, 1, blk), lambda h, i: (h, 0, 0, 0))],
        out_specs=[pl.BlockSpec((blk, DH), lambda h, i: (i, h)), pl.BlockSpec((1, blk, 1), lambda h, i: (h, i, 0))],
        out_shape=[jax.ShapeDtypeStruct((t, HW), BF16), jax.ShapeDtypeStruct((HEADS, t, 1), F32)],
        sem=("parallel", "arbitrary"), args=(qn, kn, p_main, c4), comm=comm)


def _fox_delta(qn, kn, p_main, c4, do, lse, *, v_off, name):
    t = qn.shape[0]
    blk = min(FOX_BLK, t)
    nb = t // blk
    vb = v_off // DH

    def body(q_ref, k_ref, v_ref, c_ref, do_ref, lse_ref, delta_ref):
        qi = pl.program_id(1)
        q = q_ref[...]
        dob = do_ref[...]
        lse = lse_ref[0]

        def step(j, delta, diagonal=False):
            rows = pl.ds(pl.multiple_of(j * blk, blk), blk)
            p = jnp.exp(_fox_logits(q, k_ref[rows, :], c_ref[0, j], 0 if diagonal else None) - lse)
            dp = lax.dot_general(dob, v_ref[rows, :], (((1,), (1,)), ((), ())), preferred_element_type=F32)
            return delta + jnp.sum(p * dp, axis=1, keepdims=True)

        delta_ref[0] = step(qi, lax.fori_loop(0, qi, step, jnp.zeros((blk, 1), F32)), diagonal=True)

    qblk = lambda h, i: (i, h)
    return pl.pallas_call(
        body, name=name, grid=(HEADS, nb),
        in_specs=[pl.BlockSpec((blk, DH), qblk), pl.BlockSpec((t, DH), lambda h, i: (0, h)),
                  pl.BlockSpec((t, DH), lambda h, i: (0, vb + h)), pl.BlockSpec((1, nb, 1, blk), lambda h, i: (h, 0, 0, 0)),
                  pl.BlockSpec((blk, DH), qblk), pl.BlockSpec((1, blk, 1), lambda h, i: (h, i, 0))],
        out_specs=pl.BlockSpec((1, blk, 1), lambda h, i: (h, i, 0)),
        out_shape=jax.ShapeDtypeStruct((HEADS, t, 1), F32),
        compiler_params=_cparams(("parallel", "arbitrary")),
    )(qn, kn, p_main, c4, do, lse)


def _fox_bwd(qn, kn, p_main, c4, delta, do, lse, *, v_off, name, comm=None):
    t = qn.shape[0]
    blk = min(FOX_BLK, t)
    nb = t // blk
    vb = v_off // DH
    sub = blk // FOX_ROW_SPLIT
    tn_dims = (((0,), (0,)), ((), ()))
    nt_dims = (((1,), (1,)), ((), ()))

    def body(q_ref, k_ref, v_ref, c_ref, delta_ref, do_ref, lse_ref, dq_ref, dk_ref, dv_ref, dc_ref):
        kj = pl.program_id(1)

        @pl.when(kj == 0)
        def _():
            dq_ref[...] = jnp.zeros_like(dq_ref)

        k = k_ref[...]
        v = v_ref[...]
        c_row = c_ref[0, 0]

        def step(i, carry, diagonal=False):
            dk, dv, dc = carry
            for u in range(FOX_ROW_SPLIT):
                rows = pl.ds(pl.multiple_of(i * blk + u * sub, sub), sub)
                q = q_ref[rows, :]
                dob = do_ref[rows, :]
                p = jnp.exp(_fox_logits(q, k, c_row, u * sub if diagonal else None) - lse_ref[0, rows, :])
                pb = p.astype(BF16)
                dv = dv + lax.dot_general(pb, dob, tn_dims, preferred_element_type=F32)
                dp = lax.dot_general(dob, v, nt_dims, preferred_element_type=F32)
                ds = p * (dp - delta_ref[0, rows, :])
                dsb = ds.astype(BF16)
                dq_ref[rows, :] += jnp.dot(dsb, k, preferred_element_type=F32) * (DH ** -0.5)
                dk = dk + lax.dot_general(dsb, q, tn_dims, preferred_element_type=F32) * (DH ** -0.5)
                dc = dc - jnp.sum(ds, axis=0, keepdims=True)
            return dk, dv, dc

        init = (jnp.zeros((blk, DH), F32), jnp.zeros((blk, DH), F32), jnp.zeros((1, blk), F32))
        dk, dv, dc = lax.fori_loop(kj + 1, nb, step, step(kj, init, diagonal=True))
        dk_ref[...] = dk
        dv_ref[...] = dv.astype(dv_ref.dtype)
        dc_ref[0, 0] = dc

    full = lambda h, j: (0, h)
    kvb = lambda h, j: (j, h)
    return _pcall(
        body, name=name, grid=(HEADS, nb), sem=("parallel", "arbitrary"), comm=comm,
        args=(qn, kn, p_main, c4, delta, do, lse),
        in_specs=[pl.BlockSpec((t, DH), full), pl.BlockSpec((blk, DH), kvb),
                  pl.BlockSpec((blk, DH), lambda h, j: (j, vb + h)), pl.BlockSpec((1, 1, 1, blk), lambda h, j: (h, j, 0, 0)),
                  pl.BlockSpec((1, t, 1), lambda h, j: (h, 0, 0)), pl.BlockSpec((t, DH), full),
                  pl.BlockSpec((1, t, 1), lambda h, j: (h, 0, 0))],
        out_specs=[pl.BlockSpec((t, DH), full), pl.BlockSpec((blk, DH), kvb), pl.BlockSpec((blk, DH), kvb),
                   pl.BlockSpec((1, 1, 1, blk), lambda h, j: (h, j, 0, 0))],
        out_shape=[jax.ShapeDtypeStruct((t, HW), F32), jax.ShapeDtypeStruct((t, HW), F32),
                   jax.ShapeDtypeStruct((t, HW), BF16), jax.ShapeDtypeStruct((HEADS, nb, 1, blk), F32)])


ANY = pl.BlockSpec(memory_space=pl.ANY)


def _mesh_pos():
    return lax.axis_index("x"), lax.axis_index("y"), lax.axis_index("c")


def _all_gather(blocks, *, name):
    n = len(blocks)

    def body(*refs):
        ins, outs = refs[:n], refs[n:2 * n]
        send, recv, local = refs[2 * n:]
        x, y, c = _mesh_pos()
        me, sibling = (x, y, c), (x, y, 1 - c)
        chips = [(1 - x, y), (x, 1 - y), (1 - x, 1 - y)]

        def copy(t, k, block, to, src=None):
            dst = outs[t].at[4 * block[0] + 2 * block[1] + block[2]]
            return pltpu.make_async_remote_copy(
                src_ref=dst if src is None else src, dst_ref=dst, send_sem=send.at[7 * t + k],
                recv_sem=recv.at[7 * t + k], device_id=to, device_id_type=MESH)

        mine = [pltpu.make_async_copy(ins[t], outs[t].at[4 * x + 2 * y + c], local.at[t]) for t in range(n)]
        for cp in mine:
            cp.start()
        first = []
        for t in range(n):
            first.append(copy(t, 0, me, sibling, src=ins[t]))
            first += [copy(t, 1 + j, me, (*chip, c), src=ins[t]) for j, chip in enumerate(chips)]
        for cp in first:
            cp.start()
        passed = []
        for j, chip in enumerate(chips):
            for t in range(n):
                copy(t, 1 + j, (*chip, c), me).wait_recv()
                fwd = copy(t, 4 + j, (*chip, c), sibling)
                fwd.start()
                passed.append(fwd)
        for t in range(n):
            copy(t, 0, sibling, me).wait_recv()
            for j, chip in enumerate(chips):
                copy(t, 4 + j, (*chip, 1 - c), me).wait_recv()
        for cp in first + passed:
            cp.wait_send()
        for cp in mine:
            cp.wait()

    return pl.pallas_call(
        body, name=name, in_specs=[ANY] * n, out_specs=[ANY] * n,
        out_shape=[jax.ShapeDtypeStruct((N_DEV,) + b.shape, b.dtype) for b in blocks],
        scratch_shapes=[pltpu.SemaphoreType.DMA((7 * n,)), pltpu.SemaphoreType.DMA((7 * n,)),
                        pltpu.SemaphoreType.DMA((n,))],
    )(*blocks)


def _all_gather_relayed(block, *, name):
    r = block.shape[0]
    half = r // 2
    assert half * 2 == r and half % 16 == 0, block.shape

    def body(in_ref, out_ref, send, recv, local):
        x, y, c = _mesh_pos()
        me, sibling, xn, yn, dg = (x, y, c), (x, y, 1 - c), (1 - x, y, c), (x, 1 - y, c), (1 - x, 1 - y, c)
        slot = lambda p: 4 * p[0] + 2 * p[1] + p[2]
        rows = {"a": pl.ds(0, half), "b": pl.ds(half, half)}

        def copy(k, src, dst, to):
            return pltpu.make_async_remote_copy(src_ref=src, dst_ref=dst, send_sem=send.at[k], recv_sem=recv.at[k],
                                                device_id=to, device_id_type=MESH)

        def part(p, h=None):
            ref = out_ref.at[slot(p)]
            return ref if h is None else ref.at[rows[h]]

        def landed(k, p, h=None):
            copy(k, part(p, h), part(p, h), me).wait_recv()

        mine = pltpu.make_async_copy(in_ref, part(me), local)
        mine.start()
        first = [copy(0, in_ref, part(me), sibling),
                 copy(1, in_ref.at[rows["a"]], part(me, "a"), xn), copy(2, in_ref.at[rows["b"]], part(me, "b"), xn),
                 copy(3, in_ref.at[rows["a"]], part(me, "a"), yn), copy(4, in_ref.at[rows["b"]], part(me, "b"), yn)]
        for cp in first:
            cp.start()
        landed(1, xn, "a")
        relay_a = copy(5, part(xn, "a"), part(xn, "a"), yn)
        relay_a.start()
        landed(4, yn, "b")
        relay_b = copy(6, part(yn, "b"), part(yn, "b"), xn)
        relay_b.start()
        landed(2, xn, "b")
        pass_x = copy(7, part(xn), part(xn), sibling)
        pass_x.start()
        landed(3, yn, "a")
        pass_y = copy(8, part(yn), part(yn), sibling)
        pass_y.start()
        landed(5, dg, "a")
        landed(6, dg, "b")
        pass_d = copy(9, part(dg), part(dg), sibling)
        pass_d.start()
        landed(0, sibling)
        for k, p in ((7, (1 - x, y, 1 - c)), (8, (x, 1 - y, 1 - c)), (9, (1 - x, 1 - y, 1 - c))):
            landed(k, p)
        for cp in first + [relay_a, relay_b, pass_x, pass_y, pass_d]:
            cp.wait_send()
        mine.wait()

    return pl.pallas_call(
        body, name=name, in_specs=[ANY], out_specs=ANY,
        out_shape=jax.ShapeDtypeStruct((N_DEV,) + block.shape, block.dtype),
        scratch_shapes=[pltpu.SemaphoreType.DMA((10,)), pltpu.SemaphoreType.DMA((10,)), pltpu.SemaphoreType.DMA],
    )(block)


def _comm_call(comm, *, name):
    ci, co = len(comm.ins), len(comm.out_shapes)

    def body(*refs):
        comm.start(refs[:ci], refs[ci:ci + co], refs[ci + co:])
        comm.finish(refs[:ci], refs[ci:ci + co], refs[ci + co:])

    return pl.pallas_call(body, name=name, in_specs=[ANY] * ci, out_specs=[ANY] * co, out_shape=comm.out_shapes,
                          scratch_shapes=comm.sems, input_output_aliases=comm.aliases)(*comm.ins)


def _ag_first_comm(shards, rows=None, into=None):
    n = len(shards)
    rows = rows or [None] * n
    into = into or [None] * n
    carried = [t for t in range(n) if into[t] is not None]

    def copies(cin, cout, sems):
        send, recv, local = sems
        x, y, c = _mesh_pos()
        peers = [(x, y, 1 - c), (1 - x, y, c), (x, 1 - y, c), (1 - x, 1 - y, c)]
        slot = lambda p: 4 * p[0] + 2 * p[1] + p[2]
        mine, out, inc = [], [], []
        for t in range(n):
            part = (lambda ref: ref) if rows[t] is None else (lambda ref, r=rows[t]: ref.at[pl.ds(r[0], r[1])])
            own = part(cout[t].at[slot((x, y, c))])
            mine.append(pltpu.make_async_copy(part(cin[t]), own, local.at[t]))
            for k, peer in enumerate(peers):
                sems_k = dict(send_sem=send.at[4 * t + k], recv_sem=recv.at[4 * t + k], device_id=peer,
                              device_id_type=MESH)
                theirs = part(cout[t].at[slot(peer)])
                out.append(pltpu.make_async_remote_copy(src_ref=part(cin[t]), dst_ref=own, **sems_k))
                inc.append(pltpu.make_async_remote_copy(src_ref=theirs, dst_ref=theirs, **sems_k))
        return mine, out, inc

    def start(cin, cout, sems):
        mine, out, _ = copies(cin, cout, sems)
        for cp in mine + out:
            cp.start()

    def finish(cin, cout, sems):
        mine, out, inc = copies(cin, cout, sems)
        for cp in inc:
            cp.wait_recv()
        for cp in out:
            cp.wait_send()
        for cp in mine:
            cp.wait()

    return _Comm(list(shards) + [into[t] for t in carried],
                 [jax.ShapeDtypeStruct((N_DEV,) + s.shape, s.dtype) for s in shards],
                 [pltpu.SemaphoreType.DMA((4 * n,)), pltpu.SemaphoreType.DMA((4 * n,)), pltpu.SemaphoreType.DMA((n,))],
                 start, finish, aliases={n + i: t for i, t in enumerate(carried)})


def _ag_pass_comm(gathered):
    n = len(gathered)

    def copies(cout, sems):
        send, recv = sems
        x, y, c = _mesh_pos()
        fwd, inc = [], []
        for t in range(n):
            for j, (px, py) in enumerate([(1 - x, y), (x, 1 - y), (1 - x, 1 - y)]):
                sems_j = dict(send_sem=send.at[3 * t + j], recv_sem=recv.at[3 * t + j], device_id=(x, y, 1 - c),
                              device_id_type=MESH)
                mine, theirs = cout[t].at[4 * px + 2 * py + c], cout[t].at[4 * px + 2 * py + 1 - c]
                fwd.append(pltpu.make_async_remote_copy(src_ref=mine, dst_ref=mine, **sems_j))
                inc.append(pltpu.make_async_remote_copy(src_ref=theirs, dst_ref=theirs, **sems_j))
        return fwd, inc

    def start(cin, cout, sems):
        for cp in copies(cout, sems)[0]:
            cp.start()

    def finish(cin, cout, sems):
        fwd, inc = copies(cout, sems)
        for cp in inc:
            cp.wait_recv()
        for cp in fwd:
            cp.wait_send()

    return _Comm(gathered, [jax.ShapeDtypeStruct(g.shape, g.dtype) for g in gathered],
                 [pltpu.SemaphoreType.DMA((3 * n,)), pltpu.SemaphoreType.DMA((3 * n,))], start, finish,
                 aliases={t: t for t in range(n)})


def _rs_sibling_comm(grads):
    n = len(grads)

    def copies(cin, cout, sems):
        send, recv = sems
        x, y, c = _mesh_pos()
        return [pltpu.make_async_remote_copy(
            src_ref=cin[t].at[2 * q + (1 - c)], dst_ref=cout[t].at[q], send_sem=send.at[4 * t + q],
            recv_sem=recv.at[4 * t + q], device_id=(x, y, 1 - c), device_id_type=MESH)
            for t in range(n) for q in range(4)]

    def start(cin, cout, sems):
        for cp in copies(cin, cout, sems):
            cp.start()

    def finish(cin, cout, sems):
        cps = copies(cin, cout, sems)
        for cp in cps:
            cp.wait_recv()
        for cp in cps:
            cp.wait_send()

    return _Comm(grads, [jax.ShapeDtypeStruct((4,) + g.shape[1:], g.dtype) for g in grads],
                 [pltpu.SemaphoreType.DMA((4 * n,)), pltpu.SemaphoreType.DMA((4 * n,))], start, finish)


def _join_comms(comms):
    ins, outs, sems, aliases, spans = [], [], [], {}, []
    for cm in comms:
        spans.append((len(ins), len(cm.ins), len(outs), len(cm.out_shapes), len(sems), len(cm.sems)))
        aliases.update({len(ins) + i: len(outs) + o for i, o in cm.aliases.items()})
        ins, outs, sems = ins + cm.ins, outs + cm.out_shapes, sems + cm.sems

    def run(which):
        def fn(cin, cout, csem):
            for cm, (i0, ni, o0, no, s0, ns) in zip(comms, spans):
                getattr(cm, which)(cin[i0:i0 + ni], cout[o0:o0 + no], csem[s0:s0 + ns])
        return fn

    return _Comm(ins, outs, sems, run("start"), run("finish"), aliases)


def _rs_chips_comm(parts):
    n = len(parts)

    def copies(cin, cout, sems):
        send, recv, local = sems
        x, y, c = _mesh_pos()
        my_chip = 2 * x + y
        mine = [pltpu.make_async_copy(cin[t].at[my_chip], cout[t].at[my_chip], local.at[t]) for t in range(n)]
        sends, lands = [], []
        for t in range(n):
            for k, (px, py) in enumerate([(1 - x, y), (x, 1 - y), (1 - x, 1 - y)]):
                sems_k = dict(send_sem=send.at[3 * t + k], recv_sem=recv.at[3 * t + k], device_id=(px, py, c),
                              device_id_type=MESH)
                sends.append(pltpu.make_async_remote_copy(src_ref=cin[t].at[2 * px + py], dst_ref=cout[t].at[my_chip],
                                                          **sems_k))
                lands.append(pltpu.make_async_remote_copy(src_ref=cout[t].at[2 * px + py],
                                                          dst_ref=cout[t].at[2 * px + py], **sems_k))
        return mine, sends, lands

    def start(cin, cout, sems):
        mine, sends, _ = copies(cin, cout, sems)
        for cp in mine + sends:
            cp.start()

    def finish(cin, cout, sems):
        mine, sends, lands = copies(cin, cout, sems)
        for cp in lands:
            cp.wait_recv()
        for cp in sends:
            cp.wait_send()
        for cp in mine:
            cp.wait()

    return _Comm(parts, [jax.ShapeDtypeStruct(p.shape, p.dtype) for p in parts],
                 [pltpu.SemaphoreType.DMA((3 * n,)), pltpu.SemaphoreType.DMA((3 * n,)), pltpu.SemaphoreType.DMA((n,))],
                 start, finish)


def _row_tile(r, c, itemsize, budget=3 * 1024 * 1024):
    best = None
    for tr in range(16, r + 1, 16):
        if r % tr == 0 and tr * c * itemsize <= budget:
            best = tr
    return best or r


def _pair_sum(grad, land, *, name):
    _, r, c = grad.shape
    tr = _row_tile(r, c, 2)

    def body(g_ref, l_ref, o_ref):
        o_ref[...] = (g_ref[...].astype(F32) + l_ref[...].astype(F32)).astype(o_ref.dtype)

    return pl.pallas_call(
        body, name=name, grid=(4, r // tr),
        in_specs=[pl.BlockSpec((1, tr, c), lambda q, i: (2 * q + lax.axis_index("c"), i, 0)),
                  pl.BlockSpec((1, tr, c), lambda q, i: (q, i, 0))],
        out_specs=pl.BlockSpec((1, tr, c), lambda q, i: (q, i, 0)),
        out_shape=jax.ShapeDtypeStruct((4, r, c), grad.dtype),
        compiler_params=_cparams(("parallel", "parallel")),
    )(grad, land)


def _adamw_math(w, g, m, v):
    m = ADAM_B1 * m + (1.0 - ADAM_B1) * g
    v = ADAM_B2 * v + (1.0 - ADAM_B2) * jnp.square(g)
    m_hat = m / (1.0 - ADAM_B1 ** ADAM_STEP)
    v_hat = v / (1.0 - ADAM_B2 ** ADAM_STEP)
    delta = -ADAM_LR * (m_hat / (jnp.sqrt(v_hat) + ADAM_EPS) + ADAM_WD * w)
    return delta, m, v


def _adamw(parts, w, m, v, *, name):
    s, _, cp = parts.shape
    r, c = w.shape
    tr = _row_tile(r, cp, 4, budget=1024 * 1024)

    def body(p_ref, w_ref, m_ref, v_ref, g_ref, d_ref, nm_ref, nv_ref):
        g = p_ref[0].astype(F32)
        for i in range(1, s):
            g = g + p_ref[i].astype(F32)
        g = g[:, :c]
        delta, nm, nv = _adamw_math(w_ref[...], g, m_ref[...], v_ref[...])
        g_ref[...] = g
        d_ref[...] = delta
        nm_ref[...] = nm
        nv_ref[...] = nv

    blk = pl.BlockSpec((tr, c), lambda i: (i, 0))
    return pl.pallas_call(
        body, name=name, grid=(r // tr,),
        in_specs=[pl.BlockSpec((s, tr, cp), lambda i: (0, i, 0)), blk, blk, blk],
        out_specs=[blk] * 4, out_shape=[jax.ShapeDtypeStruct((r, c), F32)] * 4,
        compiler_params=_cparams(("parallel",)),
    )(parts, w, m, v)


def _w_in_pieces(d, nb, sources):
    segs = [(0, 4 * HW, False, 0), (4 * HW, 4 * HW + 2 * HEADS, True, 0),
            (4 * HW + 2 * HEADS, 7 * HW + 2 * HEADS, False, 4 * HW),
            (7 * HW + 2 * HEADS, 7 * HW + 3 * HEADS, True, 2 * HEADS),
            (7 * HW + 3 * HEADS, 7 * HW + 3 * HEADS + 2 * d, False, 7 * HW)]
    out = []
    for dev in range(N_DEV):
        lo, hi = dev * nb, (dev + 1) * nb
        for s0, s1, is_small, a0 in segs:
            p, q = max(lo, s0), min(hi, s1)
            if p >= q:
                continue
            a, b = a0 + p - s0, a0 + q - s0
            if is_small:
                out.append((dev, p - lo, q - lo, len(sources), a, b))
                continue
            for si, (start, width) in enumerate(sources):
                u, v = max(a, start), min(b, start + width)
                if u < v:
                    out.append((dev, p - lo + (u - a), p - lo + (v - a), si, u - start, v - start))
    return out


def _concat_cols(parts, *, name):
    t = parts[0].shape[0]
    n = len(parts)
    offs = [sum(p.shape[1] for p in parts[:i]) for i in range(n)]
    tm = min(128, t)

    def body(*refs):
        for i in range(n):
            refs[n][:, offs[i]:offs[i] + parts[i].shape[1]] = refs[i][...]

    return pl.pallas_call(
        body, name=name, grid=(t // tm,),
        in_specs=[pl.BlockSpec((tm, p.shape[1]), lambda i: (i, 0)) for p in parts],
        out_specs=pl.BlockSpec((tm, offs[-1] + parts[-1].shape[1]), lambda i: (i, 0)),
        out_shape=jax.ShapeDtypeStruct((t, offs[-1] + parts[-1].shape[1]), parts[0].dtype),
        compiler_params=_cparams(("parallel",)))(*parts)


def _w_in_to_aligned(g_in, *, name):
    _, d, nb = g_in.shape
    n_main = 7 * HW + 2 * d
    tr = min(128, d)
    pieces = _w_in_pieces(d, nb, [(0, n_main)])

    def body(g_ref, main_ref, small_ref):
        small_ref[...] = jnp.zeros_like(small_ref)
        for dev, s, e, src, a, b in pieces:
            dst = main_ref if src == 0 else small_ref
            dst[:, a:b] = g_ref[dev, :, s:e]

    return pl.pallas_call(
        body, name=name, grid=(d // tr,), in_specs=[pl.BlockSpec((N_DEV, tr, nb), lambda i: (0, i, 0))],
        out_specs=[pl.BlockSpec((tr, n_main), lambda i: (i, 0)), pl.BlockSpec((tr, LANES), lambda i: (i, 0))],
        out_shape=[jax.ShapeDtypeStruct((d, n_main), g_in.dtype), jax.ShapeDtypeStruct((d, LANES), g_in.dtype)],
        compiler_params=_cparams(("parallel",)),
    )(g_in)


def _w_in_grad_blocks(seg_grads, small_grad, sources, nb, *, name):
    d = small_grad.shape[0]
    tr = min(128, d)
    pieces = _w_in_pieces(d, nb, sources)
    ns = len(seg_grads)

    def body(*refs):
        o_ref = refs[ns + 1]
        for dev, s, e, src, a, b in pieces:
            o_ref[dev, :, s:e] = refs[src][:, a:b]

    return pl.pallas_call(
        body, name=name, grid=(d // tr,),
        in_specs=[pl.BlockSpec((tr, g.shape[1]), lambda i: (i, 0)) for g in seg_grads + [small_grad]],
        out_specs=pl.BlockSpec((N_DEV, tr, nb), lambda i: (0, i, 0)),
        out_shape=jax.ShapeDtypeStruct((N_DEV, d, nb), small_grad.dtype),
        compiler_params=_cparams(("parallel",)),
    )(*seg_grads, small_grad)


def _pad_cols(a, n):
    return a if a.shape[1] == n else jnp.concatenate([a, jnp.zeros((a.shape[0], n - a.shape[1]), a.dtype)], axis=1)


def _pad_rows(a, n):
    return a if a.shape[0] == n else jnp.concatenate([a, jnp.zeros((n - a.shape[0], a.shape[1]), a.dtype)], axis=0)


class _StaticPlan:
    def __init__(self, weights, cp):
        self.w, self.cp, self.grads = weights, cp, {}

    def comm_for(self, key):
        return None

    def done(self, key, res):
        pass

    def weight(self, name):
        return self.w[name]

    def grad(self, name, g):
        self.grads[name] = g

    def grad_w_in(self, g_main, g_small):
        self.grads["w_main"], self.grads["w_small"] = g_main, g_small


class _FsdpPlan:
    RIDES = {
        "in_proj": (("gather", (("conv", None), ("wa", None), ("wb", None), ("wout", None), ("wg", 0), ("wd", 1))),),
        "gdn_fwd": (("gather", (("wg", 1),)), ("pass", ("wa", "wb", "wout"))),
        "fox_fwd": (("gather", (("wu", 0),)), ("pass", ("wg",))),
        "ffn_gate": (("gather", (("wu", 1),)),),
        "ffn_up": (("gather", (("wd", 0),)),),
        "dw_ffn_gate": (("sibling", ("wd",)),),
        "d_hn_gate": (("chips", ("wd",)), ("sibling", ("wg",))),
        "d_hn_up": (("chips", ("wg",)),),
        "d_merged": (("sibling", ("wu",)),),
        "d_oa": (("sibling", ("wout",)),),
        "d_ob": (("sibling", ("wa",)),),
        "gdn_bwd": (("chips", ("wu", "wout")), ("sibling", ("wb",))),
        "fox_bwd": (("chips", ("wa", "wb")),),
        "d_xn": (("chips", ("w_in", "conv")),),
    }
    PASS_GROUPS = (("conv",), ("wa", "wb", "wout"), ("wg",), ("wu",), ("wd",))

    def __init__(self, shards, d, cp, nb):
        self.shards, self.d, self.cp, self.nb = shards, d, cp, nb
        self.first, self.full = {}, {}
        self.blocks, self.queue, self.slots = {}, {}, {}
        self.flying = []

    def comm_for(self, key):
        comms, self.flying = [], []
        for kind, items in self.RIDES.get(key, ()):
            if kind == "gather":
                names = [n for n, _ in items]
                rows = [None if half is None else (half * (self.shards[n].shape[0] // 2), self.shards[n].shape[0] // 2)
                        for n, half in items]
                comm = _ag_first_comm([self.shards[n] for n in names], rows, [self.first.get(n) for n in names])
            elif kind == "pass":
                names = list(items)
                comm = _ag_pass_comm([self.first[n] for n in names])
            elif kind == "sibling":
                names = [n for n in items if n in self.blocks]
                comm = _rs_sibling_comm([self.blocks[n] for n in names]) if names else None
            else:
                for n in items:
                    if n in self.blocks:
                        self.sibling_now(n)
                names = [n for n in items if n in self.queue]
                comm = _rs_chips_comm([self.queue.pop(n) for n in names]) if names else None
            if comm is not None:
                comms.append(comm)
                self.flying.append((kind, names, len(comm.out_shapes)))
        return _join_comms(comms) if comms else None

    def done(self, key, res):
        res = list(res)
        for kind, names, n_out in self.flying:
            outs, res = res[:n_out], res[n_out:]
            if kind == "gather":
                self.first.update(zip(names, outs))
            elif kind == "pass":
                self.full.update(zip(names, outs))
            elif kind == "sibling":
                for n, land in zip(names, outs):
                    self.queue[n] = _pair_sum(self.blocks.pop(n), land, name=f"pair_sum_{n}")
            else:
                self.slots.update(zip(names, outs))
        self.flying = []

    def weight(self, name):
        if name not in self.full:
            group = next(g for g in self.PASS_GROUPS if name in g)
            outs = _comm_call(_ag_pass_comm([self.first[n] for n in group]), name=f"all_gather_pass_{group[0]}")
            self.full.update(zip(group, outs))
        g = self.full[name]
        if name in ("wa", "wb", "conv"):
            return _cols_of_blocks(g)
        if name == "wout":
            return g.reshape(self.d, self.d)
        if name == "wd":
            return g.reshape(N_DEV * self.cp, self.d)
        return g

    def grad(self, name, g):
        if name == "wout":
            g = g.reshape(N_DEV, self.d // N_DEV, self.d)
        if name == "wd":
            g = g.reshape(N_DEV, self.cp, self.d)
        if name == "conv":
            g = _blocks_of_cols(g.astype(BF16))
        self.blocks[name] = g

    def grad_w_in(self, g_main, g_small):
        self.blocks["w_in"] = _w_in_grad_blocks([g_main], g_small, [(0, g_main.shape[1])], self.nb,
                                                name="w_in_grad_blocks")

    def sibling_now(self, name):
        blocks = self.blocks.pop(name)
        (land,) = _comm_call(_rs_sibling_comm([blocks]), name=f"grads_to_sibling_{name}")
        self.queue[name] = _pair_sum(blocks, land, name=f"pair_sum_{name}")

    def flush(self):
        for name in list(self.blocks):
            self.sibling_now(name)
        if self.queue:
            outs = _comm_call(_rs_chips_comm(list(self.queue.values())), name="grads_to_chips_tail")
            self.slots.update(zip(self.queue, outs))
            self.queue = {}


def _carried(plan, key, fn, *args, **kw):
    comm = plan.comm_for(key)
    if comm is None:
        return fn(*args, **kw)
    res, comm_res = fn(*args, comm=comm, **kw)
    plan.done(key, comm_res)
    return res


def _local_step(x, target, w_main, w_small, plan,
                norm_mix_w, norm_ffn_w, gdn_norm_w, fox_q_w, fox_k_w, a_row, b_row):
    t, d = x.shape
    cp = plan.cp
    fp = N_DEV * cp
    n_main = w_main.shape[1]
    off_gb = OFF_GA + d
    tm = 1024
    rt = 128
    fcol = fp // 1024 if fp % 1024 == 0 else max(fp // 512, 1)

    (xn,) = _rowwise_fwd(_fn_norm, [(x, 0, d)], [norm_mix_w], [(d, BF16)], tm=rt, name="mix_norm")
    p_main = _carried(plan, "in_proj", _mm, xn, w_main, mode="nn", m=t, n=n_main, k=d, tm=tm, tn=512, tk=d,
                      out_dtype=BF16, name="in_proj")
    p_small = _mm(xn, w_small, mode="nn", m=t, n=LANES, k=d, tm=tm, tn=LANES, tk=d, out_dtype=F32, name="in_proj_small")
    (gates,) = _rowwise_fwd(_fn_gates, [(p_small, 0, LANES)], [a_row, b_row], [(LANES, F32)], tm=512, name="gates")
    conv_w = plan.weight("conv")
    qkv = _conv_fwd(p_main, conv_w, width=3 * HW, name="conv_fwd")
    o_gdn, s_all = _carried(plan, "gdn_fwd", _gdn_fwd, qkv, gates, name="gdn_fwd")
    gdn_rows = [(o_gdn, 0, HW), (p_main, OFF_ZA, HW)]
    (oa,) = _rowwise_fwd(_fn_gdn_out, gdn_rows, [gdn_norm_w], [(HW, BF16)], tm=512, ncol=HEADS, name="gdn_out")
    wa = plan.weight("wa")
    ya = _mm(oa, wa, mode="nn", m=t, n=d, k=HW, tm=tm, tn=1024, tk=HW, out_dtype=BF16, name="branch_a")
    qk_rows = [(p_main, OFF_QB, HW), (p_main, OFF_KB, HW)]
    qn, kn = _rowwise_fwd(_fn_qknorm, qk_rows, [fox_q_w, fox_k_w], [(HW, BF16), (HW, BF16)], tm=512, ncol=HEADS,
                          name="fox_qk_norm")
    blk = min(FOX_BLK, t)
    c4 = _fox_cumsum(gates, name="fox_cumsum").reshape(HEADS, t // blk, 1, blk)
    ob, lse = _carried(plan, "fox_fwd", _fox_fwd, qn, kn, p_main, c4, v_off=OFF_VB, name="fox_fwd")
    wb = plan.weight("wb")
    yb = _mm(ob, wb, mode="nn", m=t, n=d, k=HW, tm=tm, tn=1024, tk=HW, out_dtype=BF16, name="branch_b")
    mcol = 2 if d >= 2 * HW else 1
    merge_rows = [(p_main, OFF_GA, d), (p_main, off_gb, d), (ya, 0, d), (yb, 0, d)]
    (merged,) = _rowwise_fwd(_fn_merge, merge_rows, [], [(d, BF16)], tm=256, ncol=mcol, name="merge")
    wout = plan.weight("wout")
    h = _mm(merged, wout, mode="nn", m=t, n=d, k=d, tm=tm, tn=512, tk=d, out_dtype=F32, add=x, name="out_proj")
    (hn,) = _rowwise_fwd(_fn_norm, [(h, 0, d)], [norm_ffn_w], [(d, BF16)], tm=rt, name="ffn_norm")
    wg = plan.weight("wg")
    gate = _carried(plan, "ffn_gate", _mm, hn, wg, mode="nn", m=t, n=fp, k=d, tm=512, tn=cp, tk=d, out_dtype=BF16,
                    b_blocked=True, name="ffn_gate")
    wu = plan.weight("wu")
    up = _carried(plan, "ffn_up", _mm, hn, wu, mode="nn", m=t, n=fp, k=d, tm=512, tn=cp, tk=d, out_dtype=BF16,
                  b_blocked=True, name="ffn_up")
    (act,) = _rowwise_fwd(_fn_swiglu, [(gate, 0, fp), (up, 0, fp)], [], [(fp, BF16)], tm=512, ncol=fcol, name="swiglu")
    wd = plan.weight("wd")
    y = _mm(act, wd, mode="nn", m=t, n=d, k=fp, tm=512, tn=256, tk=fp, out_dtype=F32, add=h, name="ffn_down")
    dy, dyb, loss_row = _loss_head(y, target, tm=rt, name="loss_head")

    dact = _mm(dyb, wd, mode="nt", m=t, n=fp, k=d, tm=tm, tn=512, tk=d, out_dtype=BF16, name="d_act")
    plan.grad("wd", _mm(act, dyb, mode="tn", m=fp, n=d, k=t, tm=512, tn=1024, tk=t, out_dtype=BF16, name="dw_ffn_down"))
    dgate, dup = _rowwise_bwd(_fn_swiglu, [(gate, 0, fp), (up, 0, fp)], [], [dact], [BF16, BF16], tm=512, ncol=fcol,
                              name="d_swiglu")
    plan.grad("wg", _carried(plan, "dw_ffn_gate", _mm, hn, dgate, mode="tn", m=d, n=fp, k=t, tm=512, tn=cp, tk=t,
                             out_dtype=BF16, out_blocked=True, name="dw_ffn_gate"))
    dhn = _carried(plan, "d_hn_gate", _mm, dgate, wg, mode="nt", m=t, n=d, k=fp, tm=512, tn=256, tk=fp, out_dtype=F32,
                   b_blocked=True, name="d_hn_gate")
    dhn = _carried(plan, "d_hn_up", _mm, dup, wu, mode="nt", m=t, n=d, k=fp, tm=512, tn=256, tk=fp, out_dtype=F32,
                   add=dhn, b_blocked=True, name="d_hn_up")
    plan.grad("wu", _mm(hn, dup, mode="tn", m=d, n=fp, k=t, tm=512, tn=cp, tk=t, out_dtype=BF16, out_blocked=True,
                        name="dw_ffn_up"))
    dh, d_norm_ffn = _rowwise_bwd(_fn_norm, [(h, 0, d)], [norm_ffn_w], [dhn], [F32], tm=rt, name="d_ffn_norm", adds=[dy])
    dmerged = _carried(plan, "d_merged", _mm, dh, wout, mode="nt", m=t, n=d, k=d, tm=512, tn=512, tk=d, out_dtype=BF16,
                       name="d_merged")
    plan.grad("wout", _mm(merged, dh, mode="tn", m=d, n=d, k=t, tm=512, tn=512, tk=t, out_dtype=BF16, name="dw_out"))
    dga, dgb, dya, dyb2 = _rowwise_bwd(_fn_merge, merge_rows, [], [dmerged], [BF16] * 4, tm=256, ncol=mcol, name="d_merge")
    doa = _carried(plan, "d_oa", _mm, dya, wa, mode="nt", m=t, n=HW, k=d, tm=tm, tn=512, tk=d, out_dtype=BF16, name="d_oa")
    plan.grad("wa", _mm(oa, dya, mode="tn", m=HW, n=d, k=t, tm=1024, tn=d // N_DEV, tk=t, out_dtype=BF16,
                        out_blocked=True, name="dw_branch_a"))
    dob = _carried(plan, "d_ob", _mm, dyb2, wb, mode="nt", m=t, n=HW, k=d, tm=tm, tn=512, tk=d, out_dtype=BF16, name="d_ob")
    plan.grad("wb", _mm(ob, dyb2, mode="tn", m=HW, n=d, k=t, tm=1024, tn=d // N_DEV, tk=t, out_dtype=BF16,
                        out_blocked=True, name="dw_branch_b"))
    do_gdn, dza, d_gdn_norm = _rowwise_bwd(_fn_gdn_out, gdn_rows, [gdn_norm_w], [doa], [F32, BF16], tm=512,
                                           ncol=HEADS, name="d_gdn_out")
    dqkv, dgates_gdn = _carried(plan, "gdn_bwd", _gdn_bwd, qkv, gates, s_all, do_gdn, name="gdn_bwd")
    dp_qkv, dconv = _conv_bwd(p_main, conv_w, dqkv, width=3 * HW, name="conv_bwd")
    plan.grad("conv", dconv)
    delta = _fox_delta(qn, kn, p_main, c4, dob, lse, v_off=OFF_VB, name="fox_delta")
    dqn, dkn, dvb, dc4 = _carried(plan, "fox_bwd", _fox_bwd, qn, kn, p_main, c4, delta, dob, lse, v_off=OFF_VB,
                                  name="fox_bwd")
    dqb, dkb, d_fox_q, d_fox_k = _rowwise_bwd(_fn_qknorm, qk_rows, [fox_q_w, fox_k_w], [dqn, dkn], [BF16, BF16],
                                              tm=512, ncol=HEADS, name="d_fox_qk_norm")
    dgates = _fox_cumsum_bwd(dc4.reshape(HEADS, t), dgates_gdn, name="fox_cumsum_bwd")
    dsmall, d_a_row, d_b_row = _rowwise_bwd(_fn_gates, [(p_small, 0, LANES)], [a_row, b_row], [dgates], [F32],
                                            tm=512, name="d_gates")
    dp_main = _concat_cols([dp_qkv, dza, dqb, dkb, dvb, dga, dgb], name="d_p_main")
    plan.grad_w_in(_mm(xn, dp_main, mode="tn", m=d, n=n_main, k=t, tm=1024, tn=math.gcd(n_main, 1024), tk=t,
                       out_dtype=BF16, name="dw_in"),
                   _mm(xn, dsmall, mode="tn", m=d, n=LANES, k=t, tm=1024, tn=LANES, tk=t, out_dtype=BF16,
                       name="dw_in_small"))
    dxn = _mm(dsmall, w_small, mode="nt", m=t, n=d, k=LANES, tm=tm, tn=1024, tk=LANES, out_dtype=F32, name="d_xn_small")
    dxn = _carried(plan, "d_xn", _mm, dp_main, w_main, mode="nt", m=t, n=d, k=n_main, tm=tm, tn=1024,
                   tk=math.gcd(n_main, 2048),
                   out_dtype=F32, add=dxn, name="d_xn")
    grad_x, d_norm_mix = _rowwise_bwd(_fn_norm, [(x, 0, d)], [norm_mix_w], [dxn], [F32], tm=rt, name="d_mix_norm",
                                      adds=[dh])
    small = dict(norm_mix=d_norm_mix, norm_ffn=d_norm_ffn, gdn_norm=d_gdn_norm, fox_q=d_fox_q, fox_k=d_fox_k,
                 a_row=d_a_row, b_row=d_b_row)
    return loss_row[0, 0], grad_x, small


def _lane_row(pieces):
    row = jnp.zeros((1, LANES), F32)
    for off, p in pieces:
        row = lax.dynamic_update_slice(row, p.astype(F32), (0, off))
    return row


def _pack_small(norm_mix, norm_ffn, gdn_norm, fox_q, fox_k, a_log, dt_bias, b_f):
    rows = [norm_mix.reshape(-1, LANES), norm_ffn.reshape(-1, LANES), gdn_norm, fox_q, fox_k,
            _lane_row([(HEADS, a_log)]), _lane_row([(HEADS, dt_bias), (2 * HEADS, b_f)])]
    packed = jnp.concatenate(rows, axis=0)
    return _pad_rows(packed, -(-packed.shape[0] // 8) * 8)


def _unpack_small(p, d):
    nd = d // LANES
    r = 2 * nd
    return (p[0:nd].reshape(1, d), p[r + 3:r + 4, HEADS:2 * HEADS], p[r + 4:r + 5, HEADS:2 * HEADS], p[r:r + 1],
            p[r + 4:r + 5, 2 * HEADS:3 * HEADS], p[r + 1:r + 2], p[r + 2:r + 3], p[nd:r].reshape(1, d))


def _blocks_of_cols(a):
    r, c8 = a.shape
    return a.reshape(r, N_DEV, c8 // N_DEV).transpose(1, 0, 2)


def _cols_of_blocks(g):
    _, r, c = g.shape
    return g.transpose(1, 0, 2).reshape(r, N_DEV * c)


def kernel(x, norm_mix_w, w_in, conv_w, a_log, dt_bias, gdn_norm_w, fox_b_f, fox_q_norm_w, fox_k_norm_w, w_branch_a, w_branch_b, w_out, norm_ffn_w, w_ffn_gate, w_ffn_up, w_ffn_down, loss_target, m_norm_mix_w, m_w_in, m_conv_w, m_a_log, m_dt_bias, m_gdn_norm_w, m_fox_b_f, m_fox_q_norm_w, m_fox_k_norm_w, m_w_branch_a, m_w_branch_b, m_w_out, m_norm_ffn_w, m_w_ffn_gate, m_w_ffn_up, m_w_ffn_down, v_norm_mix_w, v_w_in, v_conv_w, v_a_log, v_dt_bias, v_gdn_norm_w, v_fox_b_f, v_fox_q_norm_w, v_fox_k_norm_w, v_w_branch_a, v_w_branch_b, v_w_out, v_norm_ffn_w, v_w_ffn_gate, v_w_ffn_up, v_w_ffn_down):
    d = x.shape[-1]
    cp = -(-w_ffn_down.shape[1] // LANES) * LANES
    nb = w_in.shape[2]

    g_in = _all_gather_relayed(w_in[0].astype(BF16), name="w_in_all_gather")
    w_main, w_small = _w_in_to_aligned(g_in, name="w_in_to_aligned")
    plan = _FsdpPlan(dict(conv=conv_w[0], wa=w_branch_a[0].astype(BF16), wb=w_branch_b[0].astype(BF16), wout=w_out[0].astype(BF16),
                          wg=_pad_cols(w_ffn_gate[0].astype(BF16), cp), wu=_pad_cols(w_ffn_up[0].astype(BF16), cp),
                          wd=_pad_rows(w_ffn_down[0].astype(BF16), cp)), d, cp, nb)
    a_row = _lane_row([(HEADS, a_log)])
    b_row = _lane_row([(HEADS, dt_bias), (2 * HEADS, fox_b_f)])

    loss_part, grad_x, gs = _local_step(
        x[0], loss_target[0], w_main, w_small, plan,
        norm_mix_w, norm_ffn_w, gdn_norm_w, fox_q_norm_w, fox_k_norm_w, a_row, b_row)
    loss = lax.psum(loss_part, ("x", "y", "c"))

    plan.flush()
    big = dict(w_in=("w_in", w_in, m_w_in, v_w_in), w_branch_a=("wa", w_branch_a, m_w_branch_a, v_w_branch_a),
               w_branch_b=("wb", w_branch_b, m_w_branch_b, v_w_branch_b), w_out=("wout", w_out, m_w_out, v_w_out),
               w_ffn_gate=("wg", w_ffn_gate, m_w_ffn_gate, v_w_ffn_gate), w_ffn_up=("wu", w_ffn_up, m_w_ffn_up, v_w_ffn_up),
               w_ffn_down=("wd", w_ffn_down, m_w_ffn_down, v_w_ffn_down), conv_w=("conv", conv_w, m_conv_w, v_conv_w))
    res = {}
    for nm, (key, w, m, v) in big.items():
        res[nm] = [o[None] for o in _adamw(plan.slots[key], w[0], m[0], v[0], name=f"adamw_{nm}")]

    g_small = _pack_small(gs["norm_mix"], gs["norm_ffn"], gs["gdn_norm"], gs["fox_q"], gs["fox_k"],
                          gs["a_row"][:, HEADS:2 * HEADS], gs["b_row"][:, HEADS:2 * HEADS],
                          gs["b_row"][:, 2 * HEADS:3 * HEADS])
    (g_small_all,) = _all_gather([g_small], name="small_grads_all_gather")
    w_small_p = _pack_small(norm_mix_w, norm_ffn_w, gdn_norm_w, fox_q_norm_w, fox_k_norm_w, a_log, dt_bias, fox_b_f)
    m_small_p = _pack_small(m_norm_mix_w, m_norm_ffn_w, m_gdn_norm_w, m_fox_q_norm_w, m_fox_k_norm_w, m_a_log,
                            m_dt_bias, m_fox_b_f)
    v_small_p = _pack_small(v_norm_mix_w, v_norm_ffn_w, v_gdn_norm_w, v_fox_q_norm_w, v_fox_k_norm_w, v_a_log,
                            v_dt_bias, v_fox_b_f)
    small_res = [_unpack_small(o, d) for o in _adamw(g_small_all, w_small_p, m_small_p, v_small_p, name="adamw_small")]

    def group(k):
        s = small_res[k]
        return [s[0], res["w_in"][k], res["conv_w"][k], s[1], s[2], s[3], s[4], s[5], s[6], res["w_branch_a"][k],
                res["w_branch_b"][k], res["w_out"][k], s[7], res["w_ffn_gate"][k], res["w_ffn_up"][k],
                res["w_ffn_down"][k]]

    return (loss, grad_x[None], *group(0), *group(1), *group(2), *group(3))
```

```python
import functools
import math

import jax
import jax.numpy as jnp
from jax import lax
from jax.experimental import pallas as pl
from jax.experimental.pallas import tpu as pltpu

F32 = jnp.float32
BF16 = jnp.bfloat16
HI = lax.Precision.HIGHEST
SOLVE_PRECISION = lax.Precision.HIGH
MESH = pl.DeviceIdType.MESH

EPS = 1e-6
HEADS = 16
DH = 128
HW = HEADS * DH
CHUNK = 64
CONV_K = 4
N_DEV = 8
LANES = 128
VMEM_LIMIT = 52 * 1024 * 1024

ADAM_LR = 0.001
ADAM_B1 = 0.9
ADAM_B2 = 0.999
ADAM_EPS = 1e-08
ADAM_WD = 0.01
ADAM_STEP = 10

OFF_QA, OFF_KA, OFF_VA, OFF_ZA, OFF_QB, OFF_KB, OFF_VB, OFF_GA = 0, HW, 2 * HW, 3 * HW, 4 * HW, 5 * HW, 6 * HW, 7 * HW


def _cparams(sem=None, vmem=VMEM_LIMIT):
    return pltpu.CompilerParams(dimension_semantics=sem, vmem_limit_bytes=vmem)


class _Comm:
    def __init__(self, ins, out_shapes, sems, start, finish, aliases=None):
        self.ins, self.out_shapes, self.sems = list(ins), list(out_shapes), list(sems)
        self.start, self.finish, self.aliases = start, finish, dict(aliases or {})


def _pcall(body, *, name, grid, in_specs, out_specs, out_shape, args, sem, scratch_shapes=(), comm=None):
    multi = isinstance(out_shape, (list, tuple))
    out_specs = list(out_specs) if multi else [out_specs]
    out_shape = list(out_shape) if multi else [out_shape]
    scratch_shapes = list(scratch_shapes)
    if comm is None:
        res = pl.pallas_call(body, name=name, grid=grid, in_specs=list(in_specs), out_specs=out_specs,
                             out_shape=out_shape, scratch_shapes=scratch_shapes, compiler_params=_cparams(sem))(*args)
        return res if multi else res[0]
    ni, no, ns = len(in_specs), len(out_specs), len(scratch_shapes)
    ci, co = len(comm.ins), len(comm.out_shapes)

    def wrapped(*refs):
        cin = refs[ni:ni + ci]
        outs = refs[ni + ci:ni + ci + no]
        cout = refs[ni + ci + no:ni + ci + no + co]
        scr = refs[ni + ci + no + co:ni + ci + no + co + ns]
        csem = refs[ni + ci + no + co + ns:]
        ids = [pl.program_id(ax) for ax in range(len(grid))]
        first = functools.reduce(jnp.logical_and, [i == 0 for i in ids])
        last = functools.reduce(jnp.logical_and, [i == g - 1 for i, g in zip(ids, grid)])

        @pl.when(first)
        def _():
            comm.start(cin, cout, csem)

        body(*refs[:ni], *outs, *scr)

        @pl.when(last)
        def _():
            comm.finish(cin, cout, csem)

    any_spec = pl.BlockSpec(memory_space=pl.ANY)
    res = pl.pallas_call(
        wrapped, name=name, grid=grid, in_specs=list(in_specs) + [any_spec] * ci,
        out_specs=out_specs + [any_spec] * co, out_shape=out_shape + comm.out_shapes,
        scratch_shapes=scratch_shapes + comm.sems,
        input_output_aliases={ni + i: no + o for i, o in comm.aliases.items()},
        compiler_params=_cparams(("arbitrary",) * len(grid)))(*args, *comm.ins)
    return (res[:no] if multi else res[0]), res[no:]


def _mm(a, b, *, mode, m, n, k, tm, tn, tk, out_dtype, name, a_off=(0, 0), b_off=(0, 0), add=None,
        b_blocked=False, out_blocked=False, comm=None):
    tm, tn, tk = min(tm, m), min(tn, n), min(tk, k)
    assert m % tm == 0 and n % tn == 0 and k % tk == 0, (name, m, n, k, tm, tn, tk)
    nk = k // tk
    if mode == "nn":
        a_blk, b_blk = (tm, tk), (tk, tn)
        ao, bo = (a_off[0] // tm, a_off[1] // tk), (b_off[0] // tk, b_off[1] // tn)
        a_map = lambda i, j, kk: (i + ao[0], kk + ao[1])
        b_map = lambda i, j, kk: (kk + bo[0], j + bo[1])
        dims = (((1,), (0,)), ((), ()))
        if b_blocked:
            assert b.shape == (n // tn, k, tn) and b_off == (0, 0), (name, b.shape)
            b_blk, b_map = (None, tk, tn), lambda i, j, kk: (j, kk, 0)
    elif mode == "nt":
        a_blk, b_blk = (tm, tk), (tn, tk)
        ao, bo = (a_off[0] // tm, a_off[1] // tk), (b_off[0] // tn, b_off[1] // tk)
        a_map = lambda i, j, kk: (i + ao[0], kk + ao[1])
        b_map = lambda i, j, kk: (j + bo[0], kk + bo[1])
        dims = (((1,), (1,)), ((), ()))
        if b_blocked and tk == k:
            kblocks, cblk = b.shape[0], b.shape[2]
            assert b.shape == (kblocks, n, cblk) and kblocks * cblk == k and b_off == (0, 0), (name, b.shape)
            b_blk, b_map = (kblocks, tn, cblk), lambda i, j, kk: (0, j, 0)
        elif b_blocked:
            assert b.shape == (nk, n, tk) and b_off == (0, 0), (name, b.shape)
            b_blk, b_map = (None, tn, tk), lambda i, j, kk: (kk, j, 0)
    else:
        assert not b_blocked
        a_blk, b_blk = (tk, tm), (tk, tn)
        ao, bo = (a_off[0] // tk, a_off[1] // tm), (b_off[0] // tk, b_off[1] // tn)
        a_map = lambda i, j, kk: (kk + ao[0], i + ao[1])
        b_map = lambda i, j, kk: (kk + bo[0], j + bo[1])
        dims = (((0,), (0,)), ((), ()))
    if not b_blocked:
        for off, blk in ((a_off, a_blk), (b_off, b_blk)):
            assert off[0] % blk[0] == 0 and off[1] % blk[1] == 0, (name, off, blk)
    has_add = add is not None

    def body(*refs):
        if has_add:
            a_ref, b_ref, c_ref, o_ref, acc = refs
        else:
            a_ref, b_ref, o_ref, acc = refs
            c_ref = None
        if len(b_blk) == 3 and b_blk[0] is not None:
            cblk = b_blk[2]
            p = sum(lax.dot_general(a_ref[:, q * cblk:(q + 1) * cblk].astype(BF16), b_ref[q].astype(BF16), dims,
                                    preferred_element_type=F32) for q in range(b_blk[0]))
        else:
            p = lax.dot_general(a_ref[...].astype(BF16), b_ref[...].astype(BF16), dims, preferred_element_type=F32)
        if nk == 1:
            if has_add:
                p = p + c_ref[...].astype(F32)
            o_ref[...] = p.astype(o_ref.dtype)
        else:
            kk = pl.program_id(2)

            @pl.when(kk == 0)
            def _():
                acc[...] = p + c_ref[...].astype(F32) if has_add else p

            @pl.when(kk > 0)
            def _():
                acc[...] += p

            @pl.when(kk == nk - 1)
            def _():
                o_ref[...] = acc[...].astype(o_ref.dtype)

    in_specs = [pl.BlockSpec(a_blk, a_map), pl.BlockSpec(b_blk, b_map)]
    args = [a, b]
    if has_add:
        in_specs.append(pl.BlockSpec((tm, tn), lambda i, j, kk: (i, j)))
        args.append(add)
    acc_shape = (tm, tn) if nk > 1 else (8, LANES)
    if out_blocked:
        out_spec = pl.BlockSpec((None, tm, tn), lambda i, j, kk: (j, i, 0))
        out_shape = jax.ShapeDtypeStruct((n // tn, m, tn), out_dtype)
    else:
        out_spec = pl.BlockSpec((tm, tn), lambda i, j, kk: (i, j))
        out_shape = jax.ShapeDtypeStruct((m, n), out_dtype)
    return _pcall(body, name=name, grid=(m // tm, n // tn, nk), in_specs=in_specs, out_specs=out_spec,
                  out_shape=out_shape, scratch_shapes=[pltpu.VMEM(acc_shape, F32)], args=args,
                  sem=("parallel", "parallel", "arbitrary"), comm=comm)


def _row_specs(rows, tm, ncol):
    specs = []
    for arr, off, width in rows:
        bw = width // ncol
        assert width % ncol == 0 and off % bw == 0, (off, width, ncol)
        ob = off // bw
        specs.append(pl.BlockSpec((tm, bw), lambda i, j, ob=ob: (i, j + ob)))
    return specs


def _rowwise_fwd(fn, rows, params, outs, *, tm, ncol=1, name):
    t = rows[0][0].shape[0]
    tm = min(tm, t)
    nr, npar = len(rows), len(params)

    def body(*refs):
        ins = [r[...].astype(F32) for r in refs[:nr + npar]]
        res = fn(*ins)
        for o_ref, val in zip(refs[nr + npar:], res):
            o_ref[...] = val.astype(o_ref.dtype)

    in_specs = _row_specs(rows, tm, ncol) + [pl.BlockSpec(p.shape, lambda i, j: (0, 0)) for p in params]
    out_specs = [pl.BlockSpec((tm, w // ncol), lambda i, j: (i, j)) for w, _ in outs]
    out_shape = [jax.ShapeDtypeStruct((t, w), dt) for w, dt in outs]
    return pl.pallas_call(
        body, name=name, grid=(t // tm, ncol), in_specs=in_specs, out_specs=out_specs, out_shape=out_shape,
        compiler_params=_cparams(("parallel", "parallel")),
    )(*[r[0] for r in rows], *params)


def _rowwise_bwd(fn, rows, params, cts, grad_dtypes, *, tm, ncol=1, name, adds=None):
    t = rows[0][0].shape[0]
    tm = min(tm, t)
    nr, npar, nct = len(rows), len(params), len(cts)
    adds = adds or [None] * nr
    add_idx = [i for i, a in enumerate(adds) if a is not None]

    def body(*refs):
        ins = [r[...].astype(F32) for r in refs[:nr + npar]]
        ct = tuple(r[...].astype(F32) for r in refs[nr + npar:nr + npar + nct])
        add_refs = refs[nr + npar + nct:nr + npar + nct + len(add_idx)]
        outs = refs[nr + npar + nct + len(add_idx):]
        _, vjp = jax.vjp(lambda *a: tuple(fn(*a)), *ins)
        grads = vjp(ct)
        extra = dict(zip(add_idx, add_refs))
        for i in range(nr):
            g = grads[i]
            if i in extra:
                g = g + extra[i][...].astype(F32)
            outs[i][...] = g.astype(outs[i].dtype)
        first = jnp.logical_and(pl.program_id(0) == 0, pl.program_id(1) == 0)
        for pi in range(npar):
            o_ref = outs[nr + pi]
            g = grads[nr + pi]

            @pl.when(first)
            def _(o_ref=o_ref, g=g):
                o_ref[...] = g

            @pl.when(jnp.logical_not(first))
            def _(o_ref=o_ref, g=g):
                o_ref[...] += g

    in_specs = (_row_specs(rows, tm, ncol)
                + [pl.BlockSpec(p.shape, lambda i, j: (0, 0)) for p in params]
                + [pl.BlockSpec((tm, c.shape[1] // ncol), lambda i, j: (i, j)) for c in cts]
                + [pl.BlockSpec((tm, adds[i].shape[1] // ncol), lambda i, j: (i, j)) for i in add_idx])
    out_specs = ([pl.BlockSpec((tm, w // ncol), lambda i, j: (i, j)) for _, _, w in rows]
                 + [pl.BlockSpec(p.shape, lambda i, j: (0, 0)) for p in params])
    out_shape = ([jax.ShapeDtypeStruct((t, w), dt) for (_, _, w), dt in zip(rows, grad_dtypes)]
                 + [jax.ShapeDtypeStruct(p.shape, F32) for p in params])
    return pl.pallas_call(
        body, name=name, grid=(t // tm, ncol), in_specs=in_specs, out_specs=out_specs, out_shape=out_shape,
        compiler_params=_cparams(("arbitrary", "arbitrary")),
    )(*[r[0] for r in rows], *params, *cts, *[adds[i] for i in add_idx])


def _rms(x, w):
    return x * lax.rsqrt(jnp.mean(x * x, axis=-1, keepdims=True) + EPS) * w


def _fn_norm(x, w):
    return (_rms(x, w),)


def _fn_gates(z, a_row, b_row):
    lane = lax.broadcasted_iota(jnp.int32, z.shape, 1)
    beta = jax.nn.sigmoid(z)
    g = -jnp.exp(a_row) * jax.nn.softplus(z + b_row)
    logf = jax.nn.log_sigmoid(z + b_row)
    return (jnp.where(lane < HEADS, beta, jnp.where(lane < 2 * HEADS, g, jnp.where(lane < 3 * HEADS, logf, 0.0))),)


def _fn_qknorm(q, k, qw, kw):
    return _rms(q, qw), _rms(k, kw)


def _fn_gdn_out(o, z, w):
    return (_rms(o, w) * jax.nn.silu(z),)


def _fn_merge(ga, gb, ya, yb):
    return (jax.nn.sigmoid(ga) * ya + jax.nn.sigmoid(gb) * yb,)


def _fn_swiglu(g, u):
    return (jax.nn.silu(g) * u,)


def _loss_head(y, target, *, tm, name):
    t, d = y.shape
    tm = min(tm, t)

    def body(y_ref, t_ref, dyf_ref, dyb_ref, loss_ref):
        err = y_ref[...] - t_ref[...]
        dy = err * (1.0 / d)
        dyf_ref[...] = dy
        dyb_ref[...] = dy.astype(BF16)
        part = jnp.sum(err * err) * (0.5 / d)

        @pl.when(pl.program_id(0) == 0)
        def _():
            loss_ref[...] = jnp.zeros_like(loss_ref)

        loss_ref[...] += part

    blk = pl.BlockSpec((tm, d), lambda i: (i, 0))
    return pl.pallas_call(
        body, name=name, grid=(t // tm,), in_specs=[blk, blk],
        out_specs=[blk, blk, pl.BlockSpec((1, LANES), lambda i: (0, 0))],
        out_shape=[jax.ShapeDtypeStruct((t, d), F32), jax.ShapeDtypeStruct((t, d), BF16),
                   jax.ShapeDtypeStruct((1, LANES), F32)],
        compiler_params=_cparams(("arbitrary",)),
    )(y, target)


def _shift_down(x, s):
    if s == 0:
        return x
    row = lax.broadcasted_iota(jnp.int32, x.shape, 0)
    return jnp.where(row >= s, pltpu.roll(x, s, 0), 0.0)


def _shift_up(x, s):
    if s == 0:
        return x
    t = x.shape[0]
    row = lax.broadcasted_iota(jnp.int32, x.shape, 0)
    return jnp.where(row < t - s, pltpu.roll(x, t - s, 0), 0.0)


def _conv_pre(x, w):
    y = x * w[CONV_K - 1:CONV_K, :]
    for i in range(CONV_K - 1):
        y = y + _shift_down(x, CONV_K - 1 - i) * w[i:i + 1, :]
    return y


def _conv_fwd(p_main, conv_w, *, width, name):
    t = p_main.shape[0]
    tc = LANES

    def body(x_ref, w_ref, o_ref):
        y = _conv_pre(x_ref[...].astype(F32), w_ref[...])
        o_ref[...] = y * jax.nn.sigmoid(y)

    return pl.pallas_call(
        body, name=name, grid=(width // tc,),
        in_specs=[pl.BlockSpec((t, tc), lambda j: (0, j)), pl.BlockSpec((CONV_K, tc), lambda j: (0, j))],
        out_specs=pl.BlockSpec((t, tc), lambda j: (0, j)),
        out_shape=jax.ShapeDtypeStruct((t, width), F32),
        compiler_params=_cparams(("parallel",)),
    )(p_main, conv_w)


def _conv_bwd(p_main, conv_w, dy, *, width, name):
    t = p_main.shape[0]
    tc = LANES

    def body(x_ref, w_ref, dy_ref, dx_ref, dw_ref):
        x = x_ref[...].astype(F32)
        w = w_ref[...]
        pre = _conv_pre(x, w)
        sg = jax.nn.sigmoid(pre)
        dpre = dy_ref[...] * (sg * (1.0 + pre * (1.0 - sg)))
        dx = dpre * w[CONV_K - 1:CONV_K, :]
        dws = []
        for i in range(CONV_K - 1):
            s = CONV_K - 1 - i
            dx = dx + _shift_up(dpre, s) * w[i:i + 1, :]
            dws.append(jnp.sum(_shift_down(x, s) * dpre, axis=0, keepdims=True))
        dws.append(jnp.sum(x * dpre, axis=0, keepdims=True))
        dx_ref[...] = dx.astype(dx_ref.dtype)
        dw_ref[...] = jnp.concatenate(dws, axis=0)

    return pl.pallas_call(
        body, name=name, grid=(width // tc,),
        in_specs=[pl.BlockSpec((t, tc), lambda j: (0, j)), pl.BlockSpec((CONV_K, tc), lambda j: (0, j)),
                  pl.BlockSpec((t, tc), lambda j: (0, j))],
        out_specs=[pl.BlockSpec((t, tc), lambda j: (0, j)), pl.BlockSpec((CONV_K, tc), lambda j: (0, j))],
        out_shape=[jax.ShapeDtypeStruct((t, width), BF16), jax.ShapeDtypeStruct((CONV_K, width), F32)],
        compiler_params=_cparams(("parallel",)),
    )(p_main, conv_w, dy)


def _bmm(a, b, spec, precision=None):
    return jnp.einsum(spec, a, b, preferred_element_type=F32, precision=precision)


def _iota2(shape, dim):
    return lax.broadcasted_iota(jnp.int32, shape, dim)


@jax.custom_vjp
def _tri_inverse(a):
    return _tri_inverse_levels(a)


def _tri_inverse_fwd(a):
    t = _tri_inverse_levels(a)
    return t, t


def _tri_inverse_bwd(t, g):
    x = _bmm(t, g, "hji,hjk->hik", SOLVE_PRECISION)
    return (-_bmm(x, t, "hik,hjk->hij", SOLVE_PRECISION),)


_tri_inverse.defvjp(_tri_inverse_fwd, _tri_inverse_bwd)


def _tri_inverse_levels(a):
    c = a.shape[-1]
    r, m = _iota2((c, c), 0), _iota2((c, c), 1)
    eye = (r == m).astype(F32)
    inv = None
    b = 1
    while b < c:
        mask = jnp.logical_and(r // (2 * b) == m // (2 * b), jnp.logical_and(r % (2 * b) >= b, m % (2 * b) < b))
        off = jnp.where(mask[None], a, 0.0)
        if inv is None:
            inv = eye[None] - off
        else:
            inv = inv - _bmm(_bmm(inv, off, "hij,hjk->hik", SOLVE_PRECISION), inv, "hij,hjk->hik", SOLVE_PRECISION)
        b *= 2
    return inv


def _gdn_chunk(s, q3, k3, v3, b3, gc3):
    c = q3.shape[1]
    r, m = _iota2((c, c), 0), _iota2((c, c), 1)
    tril_incl = (r >= m)[None]
    tril_strict = (r > m)[None]
    eye = (r == m).astype(F32)[None]
    qn = q3 * lax.rsqrt(jnp.sum(q3 * q3, axis=-1, keepdims=True) + EPS) * (DH ** -0.5)
    kn = k3 * lax.rsqrt(jnp.sum(k3 * k3, axis=-1, keepdims=True) + EPS)
    ones = jnp.ones((q3.shape[0], c, c), F32)
    gc_row = _bmm(ones, gc3 * eye, "hij,hjk->hik", SOLVE_PRECISION)
    decay = jnp.where(tril_incl, jnp.exp(jnp.where(tril_incl, gc3 - gc_row, 0.0)), 0.0)
    a = jnp.where(tril_strict, _bmm(kn, kn, "hcd,hmd->hcm") * decay * b3, 0.0)
    tinv = _tri_inverse(a)
    egc = jnp.exp(gc3)
    u = _bmm(tinv, v3 * b3, "hij,hjk->hik", SOLVE_PRECISION)
    w = _bmm(tinv, kn * (b3 * egc), "hij,hjk->hik", SOLVE_PRECISION)
    qk = _bmm(qn, kn, "hcd,hmd->hcm") * decay
    v_new = u - _bmm(w, s, "hcd,hdv->hcv")
    o = _bmm(qn * egc, s, "hcd,hdv->hcv") + _bmm(qk, v_new, "hcm,hmv->hcv")
    row = _iota2((c, 1), 0)[None]
    g_last = jnp.sum(jnp.where(row == c - 1, gc3, 0.0), axis=1, keepdims=True)
    s_new = s * jnp.exp(g_last) + _bmm(kn * jnp.exp(g_last - gc3), v_new, "hcd,hcv->hdv")
    return s_new, o


GDN_HEAD_GROUP = 8


def _split_heads(ref, off, h0):
    return jnp.stack([ref[:, off + h * DH:off + (h + 1) * DH].astype(F32)
                      for h in range(h0, h0 + GDN_HEAD_GROUP)], axis=0)


def _store_heads(ref, x3, off, h0):
    for i in range(GDN_HEAD_GROUP):
        h = h0 + i
        ref[:, off + h * DH:off + (h + 1) * DH] = x3[i].astype(ref.dtype)


def _lane_cols(tile, lane0):
    lane = _iota2(tile.shape, 1)
    return jnp.stack([jnp.sum(jnp.where(lane == lane0 + i, tile, 0.0), axis=1, keepdims=True)
                      for i in range(GDN_HEAD_GROUP)], axis=0)


def _cols_to_lanes(cols3, lane0, shape):
    lane = _iota2(shape, 1)
    out = jnp.zeros(shape, F32)
    for i in range(GDN_HEAD_GROUP):
        out = out + jnp.where(lane == lane0 + i, cols3[i], 0.0)
    return out


def _chunk_cumsum_matrix():
    r, m = _iota2((CHUNK, CHUNK), 0), _iota2((CHUNK, CHUNK), 1)
    return (r >= m).astype(F32)


def _gdn_inputs(qkv_ref, gt, gcum, h0):
    return (_split_heads(qkv_ref, 0, h0), _split_heads(qkv_ref, HW, h0), _split_heads(qkv_ref, 2 * HW, h0),
            _lane_cols(gt, h0), _lane_cols(gcum, HEADS + h0))


def _gdn_fwd(qkv, gates, *, name, comm=None):
    t = qkv.shape[0]
    n = t // CHUNK

    def body(qkv_ref, gt_ref, o_ref, sall_ref, s_scr):
        @pl.when(pl.program_id(0) == 0)
        def _():
            s_scr[...] = jnp.zeros_like(s_scr)

        gt = gt_ref[...]
        gcum = jnp.dot(_chunk_cumsum_matrix(), gt, preferred_element_type=F32, precision=HI)
        for h0 in range(0, HEADS, GDN_HEAD_GROUP):
            grp = pl.ds(h0, GDN_HEAD_GROUP)
            s = s_scr[grp]
            sall_ref[0, grp] = s
            s_new, o3 = _gdn_chunk(s, *_gdn_inputs(qkv_ref, gt, gcum, h0))
            s_scr[grp] = s_new
            _store_heads(o_ref, o3, 0, h0)

    return _pcall(
        body, name=name, grid=(n,),
        in_specs=[pl.BlockSpec((CHUNK, 3 * HW), lambda i: (i, 0)), pl.BlockSpec((CHUNK, LANES), lambda i: (i, 0))],
        out_specs=[pl.BlockSpec((CHUNK, HW), lambda i: (i, 0)),
                   pl.BlockSpec((1, HEADS, DH, DH), lambda i: (i, 0, 0, 0))],
        out_shape=[jax.ShapeDtypeStruct((t, HW), F32), jax.ShapeDtypeStruct((n, HEADS, DH, DH), F32)],
        scratch_shapes=[pltpu.VMEM((HEADS, DH, DH), F32)], sem=("arbitrary",), args=(qkv, gates), comm=comm)


def _gdn_bwd(qkv, gates, s_all, do, *, name, comm=None):
    t = qkv.shape[0]
    n = t // CHUNK

    def body(qkv_ref, gt_ref, sall_ref, do_ref, dqkv_ref, dgt_ref, ds_scr):
        @pl.when(pl.program_id(0) == 0)
        def _():
            ds_scr[...] = jnp.zeros_like(ds_scr)

        gt = gt_ref[...]
        cum = _chunk_cumsum_matrix()
        gcum = jnp.dot(cum, gt, preferred_element_type=F32, precision=HI)
        shape = (CHUNK, LANES)
        dbeta = jnp.zeros(shape, F32)
        dgcum = jnp.zeros(shape, F32)
        for h0 in range(0, HEADS, GDN_HEAD_GROUP):
            grp = pl.ds(h0, GDN_HEAD_GROUP)
            _, vjp = jax.vjp(_gdn_chunk, sall_ref[0, grp], *_gdn_inputs(qkv_ref, gt, gcum, h0))
            ds, dq3, dk3, dv3, db3, dgc3 = vjp((ds_scr[grp], _split_heads(do_ref, 0, h0)))
            ds_scr[grp] = ds
            _store_heads(dqkv_ref, dq3, 0, h0)
            _store_heads(dqkv_ref, dk3, HW, h0)
            _store_heads(dqkv_ref, dv3, 2 * HW, h0)
            dbeta = dbeta + _cols_to_lanes(db3, h0, shape)
            dgcum = dgcum + _cols_to_lanes(dgc3, HEADS + h0, shape)
        dg = lax.dot_general(cum, dgcum, (((0,), (0,)), ((), ())), preferred_element_type=F32, precision=HI)
        dgt_ref[...] = dbeta + dg

    rev = lambda i: n - 1 - i
    return _pcall(
        body, name=name, grid=(n,),
        in_specs=[pl.BlockSpec((CHUNK, 3 * HW), lambda i: (rev(i), 0)), pl.BlockSpec((CHUNK, LANES), lambda i: (rev(i), 0)),
                  pl.BlockSpec((1, HEADS, DH, DH), lambda i: (rev(i), 0, 0, 0)),
                  pl.BlockSpec((CHUNK, HW), lambda i: (rev(i), 0))],
        out_specs=[pl.BlockSpec((CHUNK, 3 * HW), lambda i: (rev(i), 0)), pl.BlockSpec((CHUNK, LANES), lambda i: (rev(i), 0))],
        out_shape=[jax.ShapeDtypeStruct((t, 3 * HW), F32), jax.ShapeDtypeStruct((t, LANES), F32)],
        scratch_shapes=[pltpu.VMEM((HEADS, DH, DH), F32)], sem=("arbitrary",), args=(qkv, gates, s_all, do), comm=comm)


FOX_BLK = 512
FOX_ROW_SPLIT = 1
NEG = -1e30


def _fox_cumsum(gates, *, name):
    t = gates.shape[0]
    blk = min(FOX_BLK, t)

    def body(g_ref, c_ref):
        r, m = _iota2((blk, blk), 0), _iota2((blk, blk), 1)
        upper = (r <= m).astype(F32)
        carry = jnp.zeros((HEADS, 1), F32)
        for b in range(t // blk):
            lf = g_ref[b * blk:(b + 1) * blk, :].T[2 * HEADS:3 * HEADS, :]
            c_ref[:, b * blk:(b + 1) * blk] = jnp.dot(lf, upper, preferred_element_type=F32, precision=HI) + carry
            carry = carry + jnp.sum(lf, axis=1, keepdims=True)

    return pl.pallas_call(body, name=name, out_shape=jax.ShapeDtypeStruct((HEADS, t), F32),
                          compiler_params=_cparams())(gates)


def _fox_cumsum_bwd(dc, dgates_gdn, *, name):
    t = dc.shape[1]
    blk = min(FOX_BLK, t)

    def body(dc_ref, dg_ref, o_ref):
        r, m = _iota2((blk, blk), 0), _iota2((blk, blk), 1)
        lower = (r >= m).astype(F32)
        carry = jnp.zeros((HEADS, 1), F32)
        for b in reversed(range(t // blk)):
            d = dc_ref[:, b * blk:(b + 1) * blk]
            dlf = jnp.dot(d, lower, preferred_element_type=F32, precision=HI) + carry
            carry = carry + jnp.sum(d, axis=1, keepdims=True)
            tile = jnp.concatenate([jnp.zeros((2 * HEADS, blk), F32), dlf,
                                    jnp.zeros((LANES - 3 * HEADS, blk), F32)], axis=0)
            o_ref[b * blk:(b + 1) * blk, :] = tile.T + dg_ref[b * blk:(b + 1) * blk, :]

    return pl.pallas_call(body, name=name, out_shape=jax.ShapeDtypeStruct((t, LANES), F32),
                          compiler_params=_cparams())(dc, dgates_gdn)


def _fox_logits(q, k, c_row, row0=None):
    s = lax.dot_general(q, k, (((1,), (1,)), ((), ())), preferred_element_type=F32) * (DH ** -0.5) - c_row
    if row0 is None:
        return s
    return jnp.where(row0 + _iota2(s.shape, 0) >= _iota2(s.shape, 1), s, NEG)


def _fox_fwd(qn, kn, p_main, c4, *, v_off, name, comm=None):
    t = qn.shape[0]
    blk = min(FOX_BLK, t)
    nb = t // blk
    vb = v_off // DH

    def body(q_ref, k_ref, v_ref, c_ref, o_ref, o32_ref, lse_ref):
        qi = pl.program_id(1)
        q = q_ref[...]

        def step(j, carry, diagonal=False):
            m, l, acc = carry
            rows = pl.ds(pl.multiple_of(j * blk, blk), blk)
            s = _fox_logits(q, k_ref[rows, :], c_ref[0, j], 0 if diagonal else None)
            m_new = jnp.maximum(m, jnp.max(s, axis=1, keepdims=True))
            p = jnp.exp(s - m_new)
            scale = jnp.exp(m - m_new)
            l = scale * l + jnp.sum(p, axis=1, keepdims=True)
            acc = scale * acc + jnp.dot(p.astype(BF16), v_ref[rows, :], preferred_element_type=F32)
            return m_new, l, acc

        init = (jnp.full((blk, 1), NEG, F32), jnp.zeros((blk, 1), F32), jnp.zeros((blk, DH), F32))
        m, l, acc = step(qi, lax.fori_loop(0, qi, step, init), diagonal=True)
        o = acc / l
        o_ref[...] = o.astype(o_ref.dtype)
        o32_ref[...] = o
        lse_ref[0] = m + jnp.log(l)

    return _pcall(
        body, name=name, grid=(HEADS, nb),
        in_specs=[pl.BlockSpec((blk, DH), lambda h, i: (i, h)), pl.BlockSpec((t, DH), lambda h, i: (0, h)),
                  pl.BlockSpec((t, DH), lambda h, i: (0, vb + h)), pl.BlockSpec((1, nb, 1, blk), lambda h, i: (h, 0, 0, 0))],
        out_specs=[pl.BlockSpec((blk, DH), lambda h, i: (i, h)), pl.BlockSpec((blk, DH), lambda h, i: (i, h)),
                   pl.BlockSpec((1, blk, 1), lambda h, i: (h, i, 0))],
        out_shape=[jax.ShapeDtypeStruct((t, HW), BF16), jax.ShapeDtypeStruct((t, HW), F32),
                   jax.ShapeDtypeStruct((HEADS, t, 1), F32)],
        sem=("parallel", "arbitrary"), args=(qn, kn, p_main, c4), comm=comm)


def _fox_bwd(qn, kn, p_main, c4, o32, do, lse, *, v_off, name, comm=None):
    t = qn.shape[0]
    blk = min(FOX_BLK, t)
    nb = t // blk
    vb = v_off // DH
    sub = blk // FOX_ROW_SPLIT
    tn_dims = (((0,), (0,)), ((), ()))
    nt_dims = (((1,), (1,)), ((), ()))

    def body(q_ref, k_ref, v_ref, c_ref, o_ref, do_ref, lse_ref, dq_ref, dk_ref, dv_ref, dc_ref, dcq_ref):
        kj = pl.program_id(1)

        @pl.when(kj == 0)
        def _():
            dq_ref[...] = jnp.zeros_like(dq_ref)
            dcq_ref[...] = jnp.zeros_like(dcq_ref)

        k = k_ref[...]
        v = v_ref[...]
        c_row = c_ref[0, 0]

        def step(i, carry, diagonal=False):
            dk, dv, dc = carry
            for u in range(FOX_ROW_SPLIT):
                rows = pl.ds(pl.multiple_of(i * blk + u * sub, sub), sub)
                q = q_ref[rows, :]
                dob = do_ref[rows, :]
                p = jnp.exp(_fox_logits(q, k, c_row, u * sub if diagonal else None) - lse_ref[0, rows, :])
                pb = p.astype(BF16)
                dv = dv + lax.dot_general(pb, dob, tn_dims, preferred_element_type=F32)
                dp = lax.dot_general(dob, v, nt_dims, preferred_element_type=F32)
                delta = jnp.sum(dob.astype(F32) * o_ref[rows, :], axis=1, keepdims=True)
                ds = p * (dp - delta)
                dcq_ref[0, rows, :] += jnp.sum(ds, axis=1, keepdims=True)
                dsb = ds.astype(BF16)
                dq_ref[rows, :] += jnp.dot(dsb, k, preferred_element_type=F32) * (DH ** -0.5)
                dk = dk + lax.dot_general(dsb, q, tn_dims, preferred_element_type=F32) * (DH ** -0.5)
                dc = dc - jnp.sum(ds, axis=0, keepdims=True)
            return dk, dv, dc

        init = (jnp.zeros((blk, DH), F32), jnp.zeros((blk, DH), F32), jnp.zeros((1, blk), F32))
        dk, dv, dc = lax.fori_loop(kj + 1, nb, step, step(kj, init, diagonal=True))
        dk_ref[...] = dk
        dv_ref[...] = dv.astype(dv_ref.dtype)
        dc_ref[0, 0] = dc

    full = lambda h, j: (0, h)
    kvb = lambda h, j: (j, h)
    return _pcall(
        body, name=name, grid=(HEADS, nb), sem=("parallel", "arbitrary"), comm=comm,
        args=(qn, kn, p_main, c4, o32, do, lse),
        in_specs=[pl.BlockSpec((t, DH), full), pl.BlockSpec((blk, DH), kvb),
                  pl.BlockSpec((blk, DH), lambda h, j: (j, vb + h)), pl.BlockSpec((1, 1, 1, blk), lambda h, j: (h, j, 0, 0)),
                  pl.BlockSpec((t, DH), full), pl.BlockSpec((t, DH), full),
                  pl.BlockSpec((1, t, 1), lambda h, j: (h, 0, 0))],
        out_specs=[pl.BlockSpec((t, DH), full), pl.BlockSpec((blk, DH), kvb), pl.BlockSpec((blk, DH), kvb),
                   pl.BlockSpec((1, 1, 1, blk), lambda h, j: (h, j, 0, 0)), pl.BlockSpec((1, t, 1), lambda h, j: (h, 0, 0))],
        out_shape=[jax.ShapeDtypeStruct((t, HW), F32), jax.ShapeDtypeStruct((t, HW), F32),
                   jax.ShapeDtypeStruct((t, HW), BF16), jax.ShapeDtypeStruct((HEADS, nb, 1, blk), F32),
                   jax.ShapeDtypeStruct((HEADS, t, 1), F32)])


ANY = pl.BlockSpec(memory_space=pl.ANY)


def _mesh_pos():
    return lax.axis_index("x"), lax.axis_index("y"), lax.axis_index("c")


def _all_gather(blocks, *, name):
    n = len(blocks)

    def body(*refs):
        ins, outs = refs[:n], refs[n:2 * n]
        send, recv, local = refs[2 * n:]
        x, y, c = _mesh_pos()
        me, sibling = (x, y, c), (x, y, 1 - c)
        chips = [(1 - x, y), (x, 1 - y), (1 - x, 1 - y)]

        def copy(t, k, block, to, src=None):
            dst = outs[t].at[4 * block[0] + 2 * block[1] + block[2]]
            return pltpu.make_async_remote_copy(
                src_ref=dst if src is None else src, dst_ref=dst, send_sem=send.at[7 * t + k],
                recv_sem=recv.at[7 * t + k], device_id=to, device_id_type=MESH)

        mine = [pltpu.make_async_copy(ins[t], outs[t].at[4 * x + 2 * y + c], local.at[t]) for t in range(n)]
        for cp in mine:
            cp.start()
        first = []
        for t in range(n):
            first.append(copy(t, 0, me, sibling, src=ins[t]))
            first += [copy(t, 1 + j, me, (*chip, c), src=ins[t]) for j, chip in enumerate(chips)]
        for cp in first:
            cp.start()
        passed = []
        for j, chip in enumerate(chips):
            for t in range(n):
                copy(t, 1 + j, (*chip, c), me).wait_recv()
                fwd = copy(t, 4 + j, (*chip, c), sibling)
                fwd.start()
                passed.append(fwd)
        for t in range(n):
            copy(t, 0, sibling, me).wait_recv()
            for j, chip in enumerate(chips):
                copy(t, 4 + j, (*chip, 1 - c), me).wait_recv()
        for cp in first + passed:
            cp.wait_send()
        for cp in mine:
            cp.wait()

    return pl.pallas_call(
        body, name=name, in_specs=[ANY] * n, out_specs=[ANY] * n,
        out_shape=[jax.ShapeDtypeStruct((N_DEV,) + b.shape, b.dtype) for b in blocks],
        scratch_shapes=[pltpu.SemaphoreType.DMA((7 * n,)), pltpu.SemaphoreType.DMA((7 * n,)),
                        pltpu.SemaphoreType.DMA((n,))],
    )(*blocks)


def _all_gather_relayed(block, *, name):
    r = block.shape[0]
    half = r // 2
    assert half * 2 == r and half % 16 == 0, block.shape

    def body(in_ref, out_ref, send, recv, local):
        x, y, c = _mesh_pos()
        me, sibling, xn, yn, dg = (x, y, c), (x, y, 1 - c), (1 - x, y, c), (x, 1 - y, c), (1 - x, 1 - y, c)
        slot = lambda p: 4 * p[0] + 2 * p[1] + p[2]
        rows = {"a": pl.ds(0, half), "b": pl.ds(half, half)}

        def copy(k, src, dst, to):
            return pltpu.make_async_remote_copy(src_ref=src, dst_ref=dst, send_sem=send.at[k], recv_sem=recv.at[k],
                                                device_id=to, device_id_type=MESH)

        def part(p, h=None):
            ref = out_ref.at[slot(p)]
            return ref if h is None else ref.at[rows[h]]

        def landed(k, p, h=None):
            copy(k, part(p, h), part(p, h), me).wait_recv()

        mine = pltpu.make_async_copy(in_ref, part(me), local)
        mine.start()
        first = [copy(0, in_ref, part(me), sibling),
                 copy(1, in_ref.at[rows["a"]], part(me, "a"), xn), copy(2, in_ref.at[rows["b"]], part(me, "b"), xn),
                 copy(3, in_ref.at[rows["a"]], part(me, "a"), yn), copy(4, in_ref.at[rows["b"]], part(me, "b"), yn)]
        for cp in first:
            cp.start()
        landed(1, xn, "a")
        relay_a = copy(5, part(xn, "a"), part(xn, "a"), yn)
        relay_a.start()
        landed(4, yn, "b")
        relay_b = copy(6, part(yn, "b"), part(yn, "b"), xn)
        relay_b.start()
        landed(2, xn, "b")
        pass_x = copy(7, part(xn), part(xn), sibling)
        pass_x.start()
        landed(3, yn, "a")
        pass_y = copy(8, part(yn), part(yn), sibling)
        pass_y.start()
        landed(5, dg, "a")
        landed(6, dg, "b")
        pass_d = copy(9, part(dg), part(dg), sibling)
        pass_d.start()
        landed(0, sibling)
        for k, p in ((7, (1 - x, y, 1 - c)), (8, (x, 1 - y, 1 - c)), (9, (1 - x, 1 - y, 1 - c))):
            landed(k, p)
        for cp in first + [relay_a, relay_b, pass_x, pass_y, pass_d]:
            cp.wait_send()
        mine.wait()

    return pl.pallas_call(
        body, name=name, in_specs=[ANY], out_specs=ANY,
        out_shape=jax.ShapeDtypeStruct((N_DEV,) + block.shape, block.dtype),
        scratch_shapes=[pltpu.SemaphoreType.DMA((10,)), pltpu.SemaphoreType.DMA((10,)), pltpu.SemaphoreType.DMA],
    )(block)


def _comm_call(comm, *, name):
    ci, co = len(comm.ins), len(comm.out_shapes)

    def body(*refs):
        comm.start(refs[:ci], refs[ci:ci + co], refs[ci + co:])
        comm.finish(refs[:ci], refs[ci:ci + co], refs[ci + co:])

    return pl.pallas_call(body, name=name, in_specs=[ANY] * ci, out_specs=[ANY] * co, out_shape=comm.out_shapes,
                          scratch_shapes=comm.sems, input_output_aliases=comm.aliases)(*comm.ins)


def _ag_first_comm(shards, rows=None, into=None):
    n = len(shards)
    rows = rows or [None] * n
    into = into or [None] * n
    carried = [t for t in range(n) if into[t] is not None]

    def copies(cin, cout, sems):
        send, recv, local = sems
        x, y, c = _mesh_pos()
        peers = [(x, y, 1 - c), (1 - x, y, c), (x, 1 - y, c), (1 - x, 1 - y, c)]
        slot = lambda p: 4 * p[0] + 2 * p[1] + p[2]
        mine, out, inc = [], [], []
        for t in range(n):
            part = (lambda ref: ref) if rows[t] is None else (lambda ref, r=rows[t]: ref.at[pl.ds(r[0], r[1])])
            own = part(cout[t].at[slot((x, y, c))])
            mine.append(pltpu.make_async_copy(part(cin[t]), own, local.at[t]))
            for k, peer in enumerate(peers):
                sems_k = dict(send_sem=send.at[4 * t + k], recv_sem=recv.at[4 * t + k], device_id=peer,
                              device_id_type=MESH)
                theirs = part(cout[t].at[slot(peer)])
                out.append(pltpu.make_async_remote_copy(src_ref=part(cin[t]), dst_ref=own, **sems_k))
                inc.append(pltpu.make_async_remote_copy(src_ref=theirs, dst_ref=theirs, **sems_k))
        return mine, out, inc

    def start(cin, cout, sems):
        mine, out, _ = copies(cin, cout, sems)
        for cp in mine + out:
            cp.start()

    def finish(cin, cout, sems):
        mine, out, inc = copies(cin, cout, sems)
        for cp in inc:
            cp.wait_recv()
        for cp in out:
            cp.wait_send()
        for cp in mine:
            cp.wait()

    return _Comm(list(shards) + [into[t] for t in carried],
                 [jax.ShapeDtypeStruct((N_DEV,) + s.shape, s.dtype) for s in shards],
                 [pltpu.SemaphoreType.DMA((4 * n,)), pltpu.SemaphoreType.DMA((4 * n,)), pltpu.SemaphoreType.DMA((n,))],
                 start, finish, aliases={n + i: t for i, t in enumerate(carried)})


def _ag_pass_comm(gathered):
    n = len(gathered)

    def copies(cout, sems):
        send, recv = sems
        x, y, c = _mesh_pos()
        fwd, inc = [], []
        for t in range(n):
            for j, (px, py) in enumerate([(1 - x, y), (x, 1 - y), (1 - x, 1 - y)]):
                sems_j = dict(send_sem=send.at[3 * t + j], recv_sem=recv.at[3 * t + j], device_id=(x, y, 1 - c),
                              device_id_type=MESH)
                mine, theirs = cout[t].at[4 * px + 2 * py + c], cout[t].at[4 * px + 2 * py + 1 - c]
                fwd.append(pltpu.make_async_remote_copy(src_ref=mine, dst_ref=mine, **sems_j))
                inc.append(pltpu.make_async_remote_copy(src_ref=theirs, dst_ref=theirs, **sems_j))
        return fwd, inc

    def start(cin, cout, sems):
        for cp in copies(cout, sems)[0]:
            cp.start()

    def finish(cin, cout, sems):
        fwd, inc = copies(cout, sems)
        for cp in inc:
            cp.wait_recv()
        for cp in fwd:
            cp.wait_send()

    return _Comm(gathered, [jax.ShapeDtypeStruct(g.shape, g.dtype) for g in gathered],
                 [pltpu.SemaphoreType.DMA((3 * n,)), pltpu.SemaphoreType.DMA((3 * n,))], start, finish,
                 aliases={t: t for t in range(n)})


def _rs_sibling_comm(grads):
    n = len(grads)

    def copies(cin, cout, sems):
        send, recv = sems
        x, y, c = _mesh_pos()
        return [pltpu.make_async_remote_copy(
            src_ref=cin[t].at[2 * q + (1 - c)], dst_ref=cout[t].at[q], send_sem=send.at[4 * t + q],
            recv_sem=recv.at[4 * t + q], device_id=(x, y, 1 - c), device_id_type=MESH)
            for t in range(n) for q in range(4)]

    def start(cin, cout, sems):
        for cp in copies(cin, cout, sems):
            cp.start()

    def finish(cin, cout, sems):
        cps = copies(cin, cout, sems)
        for cp in cps:
            cp.wait_recv()
        for cp in cps:
            cp.wait_send()

    return _Comm(grads, [jax.ShapeDtypeStruct((4,) + g.shape[1:], g.dtype) for g in grads],
                 [pltpu.SemaphoreType.DMA((4 * n,)), pltpu.SemaphoreType.DMA((4 * n,))], start, finish)


def _join_comms(comms):
    ins, outs, sems, aliases, spans = [], [], [], {}, []
    for cm in comms:
        spans.append((len(ins), len(cm.ins), len(outs), len(cm.out_shapes), len(sems), len(cm.sems)))
        aliases.update({len(ins) + i: len(outs) + o for i, o in cm.aliases.items()})
        ins, outs, sems = ins + cm.ins, outs + cm.out_shapes, sems + cm.sems

    def run(which):
        def fn(cin, cout, csem):
            for cm, (i0, ni, o0, no, s0, ns) in zip(comms, spans):
                getattr(cm, which)(cin[i0:i0 + ni], cout[o0:o0 + no], csem[s0:s0 + ns])
        return fn

    return _Comm(ins, outs, sems, run("start"), run("finish"), aliases)


def _rs_chips_comm(parts):
    n = len(parts)

    def copies(cin, cout, sems):
        send, recv, local = sems
        x, y, c = _mesh_pos()
        my_chip = 2 * x + y
        mine = [pltpu.make_async_copy(cin[t].at[my_chip], cout[t].at[my_chip], local.at[t]) for t in range(n)]
        sends, lands = [], []
        for t in range(n):
            for k, (px, py) in enumerate([(1 - x, y), (x, 1 - y), (1 - x, 1 - y)]):
                sems_k = dict(send_sem=send.at[3 * t + k], recv_sem=recv.at[3 * t + k], device_id=(px, py, c),
                              device_id_type=MESH)
                sends.append(pltpu.make_async_remote_copy(src_ref=cin[t].at[2 * px + py], dst_ref=cout[t].at[my_chip],
                                                          **sems_k))
                lands.append(pltpu.make_async_remote_copy(src_ref=cout[t].at[2 * px + py],
                                                          dst_ref=cout[t].at[2 * px + py], **sems_k))
        return mine, sends, lands

    def start(cin, cout, sems):
        mine, sends, _ = copies(cin, cout, sems)
        for cp in mine + sends:
            cp.start()

    def finish(cin, cout, sems):
        mine, sends, lands = copies(cin, cout, sems)
        for cp in lands:
            cp.wait_recv()
        for cp in sends:
            cp.wait_send()
        for cp in mine:
            cp.wait()

    return _Comm(parts, [jax.ShapeDtypeStruct(p.shape, p.dtype) for p in parts],
                 [pltpu.SemaphoreType.DMA((3 * n,)), pltpu.SemaphoreType.DMA((3 * n,)), pltpu.SemaphoreType.DMA((n,))],
                 start, finish)


def _row_tile(r, c, itemsize, budget=3 * 1024 * 1024):
    best = None
    for tr in range(16, r + 1, 16):
        if r % tr == 0 and tr * c * itemsize <= budget:
            best = tr
    return best or r


def _pair_sum(grad, land, *, name):
    _, r, c = grad.shape
    tr = _row_tile(r, c, 2)

    def body(g_ref, l_ref, o_ref):
        o_ref[...] = (g_ref[...].astype(F32) + l_ref[...].astype(F32)).astype(o_ref.dtype)

    return pl.pallas_call(
        body, name=name, grid=(4, r // tr),
        in_specs=[pl.BlockSpec((1, tr, c), lambda q, i: (2 * q + lax.axis_index("c"), i, 0)),
                  pl.BlockSpec((1, tr, c), lambda q, i: (q, i, 0))],
        out_specs=pl.BlockSpec((1, tr, c), lambda q, i: (q, i, 0)),
        out_shape=jax.ShapeDtypeStruct((4, r, c), grad.dtype),
        compiler_params=_cparams(("parallel", "parallel")),
    )(grad, land)


def _adamw_math(w, g, m, v):
    m = ADAM_B1 * m + (1.0 - ADAM_B1) * g
    v = ADAM_B2 * v + (1.0 - ADAM_B2) * jnp.square(g)
    m_hat = m / (1.0 - ADAM_B1 ** ADAM_STEP)
    v_hat = v / (1.0 - ADAM_B2 ** ADAM_STEP)
    delta = -ADAM_LR * (m_hat / (jnp.sqrt(v_hat) + ADAM_EPS) + ADAM_WD * w)
    return delta, m, v


def _adamw(parts, w, m, v, *, name):
    s, _, cp = parts.shape
    r, c = w.shape
    tr = _row_tile(r, cp, 4, budget=1024 * 1024)

    def body(p_ref, w_ref, m_ref, v_ref, g_ref, d_ref, nm_ref, nv_ref):
        g = p_ref[0].astype(F32)
        for i in range(1, s):
            g = g + p_ref[i].astype(F32)
        g = g[:, :c]
        delta, nm, nv = _adamw_math(w_ref[...], g, m_ref[...], v_ref[...])
        g_ref[...] = g
        d_ref[...] = delta
        nm_ref[...] = nm
        nv_ref[...] = nv

    blk = pl.BlockSpec((tr, c), lambda i: (i, 0))
    return pl.pallas_call(
        body, name=name, grid=(r // tr,),
        in_specs=[pl.BlockSpec((s, tr, cp), lambda i: (0, i, 0)), blk, blk, blk],
        out_specs=[blk] * 4, out_shape=[jax.ShapeDtypeStruct((r, c), F32)] * 4,
        compiler_params=_cparams(("parallel",)),
    )(parts, w, m, v)


def _w_in_pieces(d, nb, sources):
    segs = [(0, 4 * HW, False, 0), (4 * HW, 4 * HW + 2 * HEADS, True, 0),
            (4 * HW + 2 * HEADS, 7 * HW + 2 * HEADS, False, 4 * HW),
            (7 * HW + 2 * HEADS, 7 * HW + 3 * HEADS, True, 2 * HEADS),
            (7 * HW + 3 * HEADS, 7 * HW + 3 * HEADS + 2 * d, False, 7 * HW)]
    out = []
    for dev in range(N_DEV):
        lo, hi = dev * nb, (dev + 1) * nb
        for s0, s1, is_small, a0 in segs:
            p, q = max(lo, s0), min(hi, s1)
            if p >= q:
                continue
            a, b = a0 + p - s0, a0 + q - s0
            if is_small:
                out.append((dev, p - lo, q - lo, len(sources), a, b))
                continue
            for si, (start, width) in enumerate(sources):
                u, v = max(a, start), min(b, start + width)
                if u < v:
                    out.append((dev, p - lo + (u - a), p - lo + (v - a), si, u - start, v - start))
    return out


def _concat_cols(parts, *, name):
    t = parts[0].shape[0]
    n = len(parts)
    offs = [sum(p.shape[1] for p in parts[:i]) for i in range(n)]
    tm = min(128, t)

    def body(*refs):
        for i in range(n):
            refs[n][:, offs[i]:offs[i] + parts[i].shape[1]] = refs[i][...]

    return pl.pallas_call(
        body, name=name, grid=(t // tm,),
        in_specs=[pl.BlockSpec((tm, p.shape[1]), lambda i: (i, 0)) for p in parts],
        out_specs=pl.BlockSpec((tm, offs[-1] + parts[-1].shape[1]), lambda i: (i, 0)),
        out_shape=jax.ShapeDtypeStruct((t, offs[-1] + parts[-1].shape[1]), parts[0].dtype),
        compiler_params=_cparams(("parallel",)))(*parts)


def _w_in_to_aligned(g_in, *, name):
    _, d, nb = g_in.shape
    n_main = 7 * HW + 2 * d
    tr = min(128, d)
    pieces = _w_in_pieces(d, nb, [(0, n_main)])

    def body(g_ref, main_ref, small_ref):
        small_ref[...] = jnp.zeros_like(small_ref)
        for dev, s, e, src, a, b in pieces:
            dst = main_ref if src == 0 else small_ref
            dst[:, a:b] = g_ref[dev, :, s:e]

    return pl.pallas_call(
        body, name=name, grid=(d // tr,), in_specs=[pl.BlockSpec((N_DEV, tr, nb), lambda i: (0, i, 0))],
        out_specs=[pl.BlockSpec((tr, n_main), lambda i: (i, 0)), pl.BlockSpec((tr, LANES), lambda i: (i, 0))],
        out_shape=[jax.ShapeDtypeStruct((d, n_main), g_in.dtype), jax.ShapeDtypeStruct((d, LANES), g_in.dtype)],
        compiler_params=_cparams(("parallel",)),
    )(g_in)


def _w_in_grad_blocks(seg_grads, small_grad, sources, nb, *, name):
    d = small_grad.shape[0]
    tr = min(128, d)
    pieces = _w_in_pieces(d, nb, sources)
    ns = len(seg_grads)

    def body(*refs):
        o_ref = refs[ns + 1]
        for dev, s, e, src, a, b in pieces:
            o_ref[dev, :, s:e] = refs[src][:, a:b]

    return pl.pallas_call(
        body, name=name, grid=(d // tr,),
        in_specs=[pl.BlockSpec((tr, g.shape[1]), lambda i: (i, 0)) for g in seg_grads + [small_grad]],
        out_specs=pl.BlockSpec((N_DEV, tr, nb), lambda i: (0, i, 0)),
        out_shape=jax.ShapeDtypeStruct((N_DEV, d, nb), small_grad.dtype),
        compiler_params=_cparams(("parallel",)),
    )(*seg_grads, small_grad)


def _pad_cols(a, n):
    return a if a.shape[1] == n else jnp.concatenate([a, jnp.zeros((a.shape[0], n - a.shape[1]), a.dtype)], axis=1)


def _pad_rows(a, n):
    return a if a.shape[0] == n else jnp.concatenate([a, jnp.zeros((n - a.shape[0], a.shape[1]), a.dtype)], axis=0)


class _StaticPlan:
    def __init__(self, weights, cp):
        self.w, self.cp, self.grads = weights, cp, {}

    def comm_for(self, key):
        return None

    def done(self, key, res):
        pass

    def weight(self, name):
        return self.w[name]

    def grad(self, name, g):
        self.grads[name] = g

    def grad_w_in(self, g_main, g_small):
        self.grads["w_main"], self.grads["w_small"] = g_main, g_small


class _FsdpPlan:
    RIDES = {
        "in_proj": (("gather", (("conv", None), ("wa", None), ("wb", None), ("wout", None), ("wg", 0), ("wd", 1))),),
        "gdn_fwd": (("gather", (("wg", 1),)), ("pass", ("wa", "wb", "wout"))),
        "fox_fwd": (("gather", (("wu", 0),)), ("pass", ("wg",))),
        "ffn_gate": (("gather", (("wu", 1),)),),
        "ffn_up": (("gather", (("wd", 0),)),),
        "dw_ffn_gate": (("sibling", ("wd",)),),
        "d_hn_gate": (("chips", ("wd",)), ("sibling", ("wg",))),
        "d_hn_up": (("chips", ("wg",)),),
        "d_merged": (("sibling", ("wu",)),),
        "d_oa": (("sibling", ("wout",)),),
        "d_ob": (("sibling", ("wa",)),),
        "gdn_bwd": (("chips", ("wu", "wout")), ("sibling", ("wb",))),
        "fox_bwd": (("chips", ("wa", "wb")),),
        "d_xn": (("chips", ("w_in", "conv")),),
    }
    PASS_GROUPS = (("conv",), ("wa", "wb", "wout"), ("wg",), ("wu",), ("wd",))

    def __init__(self, shards, d, cp, nb):
        self.shards, self.d, self.cp, self.nb = shards, d, cp, nb
        self.first, self.full = {}, {}
        self.blocks, self.queue, self.slots = {}, {}, {}
        self.flying = []

    def comm_for(self, key):
        comms, self.flying = [], []
        for kind, items in self.RIDES.get(key, ()):
            if kind == "gather":
                names = [n for n, _ in items]
                rows = [None if half is None else (half * (self.shards[n].shape[0] // 2), self.shards[n].shape[0] // 2)
                        for n, half in items]
                comm = _ag_first_comm([self.shards[n] for n in names], rows, [self.first.get(n) for n in names])
            elif kind == "pass":
                names = list(items)
                comm = _ag_pass_comm([self.first[n] for n in names])
            elif kind == "sibling":
                names = [n for n in items if n in self.blocks]
                comm = _rs_sibling_comm([self.blocks[n] for n in names]) if names else None
            else:
                for n in items:
                    if n in self.blocks:
                        self.sibling_now(n)
                names = [n for n in items if n in self.queue]
                comm = _rs_chips_comm([self.queue.pop(n) for n in names]) if names else None
            if comm is not None:
                comms.append(comm)
                self.flying.append((kind, names, len(comm.out_shapes)))
        return _join_comms(comms) if comms else None

    def done(self, key, res):
        res = list(res)
        for kind, names, n_out in self.flying:
            outs, res = res[:n_out], res[n_out:]
            if kind == "gather":
                self.first.update(zip(names, outs))
            elif kind == "pass":
                self.full.update(zip(names, outs))
            elif kind == "sibling":
                for n, land in zip(names, outs):
                    self.queue[n] = _pair_sum(self.blocks.pop(n), land, name=f"pair_sum_{n}")
            else:
                self.slots.update(zip(names, outs))
        self.flying = []

    def weight(self, name):
        if name not in self.full:
            group = next(g for g in self.PASS_GROUPS if name in g)
            outs = _comm_call(_ag_pass_comm([self.first[n] for n in group]), name=f"all_gather_pass_{group[0]}")
            self.full.update(zip(group, outs))
        g = self.full[name]
        if name in ("wa", "wb", "conv"):
            return _cols_of_blocks(g)
        if name == "wout":
            return g.reshape(self.d, self.d)
        if name == "wd":
            return g.reshape(N_DEV * self.cp, self.d)
        return g

    def grad(self, name, g):
        if name == "wout":
            g = g.reshape(N_DEV, self.d // N_DEV, self.d)
        if name == "wd":
            g = g.reshape(N_DEV, self.cp, self.d)
        if name == "conv":
            g = _blocks_of_cols(g.astype(BF16))
        self.blocks[name] = g

    def grad_w_in(self, g_main, g_small):
        self.blocks["w_in"] = _w_in_grad_blocks([g_main], g_small, [(0, g_main.shape[1])], self.nb,
                                                name="w_in_grad_blocks")

    def sibling_now(self, name):
        blocks = self.blocks.pop(name)
        (land,) = _comm_call(_rs_sibling_comm([blocks]), name=f"grads_to_sibling_{name}")
        self.queue[name] = _pair_sum(blocks, land, name=f"pair_sum_{name}")

    def flush(self):
        for name in list(self.blocks):
            self.sibling_now(name)
        if self.queue:
            outs = _comm_call(_rs_chips_comm(list(self.queue.values())), name="grads_to_chips_tail")
            self.slots.update(zip(self.queue, outs))
            self.queue = {}


def _carried(plan, key, fn, *args, **kw):
    comm = plan.comm_for(key)
    if comm is None:
        return fn(*args, **kw)
    res, comm_res = fn(*args, comm=comm, **kw)
    plan.done(key, comm_res)
    return res


def _local_step(x, target, w_main, w_small, plan,
                norm_mix_w, norm_ffn_w, gdn_norm_w, fox_q_w, fox_k_w, a_row, b_row):
    t, d = x.shape
    cp = plan.cp
    fp = N_DEV * cp
    n_main = w_main.shape[1]
    off_gb = OFF_GA + d
    tm = 1024
    rt = 128
    fcol = fp // 1024 if fp % 1024 == 0 else max(fp // 512, 1)

    (xn,) = _rowwise_fwd(_fn_norm, [(x, 0, d)], [norm_mix_w], [(d, BF16)], tm=rt, name="mix_norm")
    p_main = _carried(plan, "in_proj", _mm, xn, w_main, mode="nn", m=t, n=n_main, k=d, tm=tm, tn=512, tk=d,
                      out_dtype=BF16, name="in_proj")
    p_small = _mm(xn, w_small, mode="nn", m=t, n=LANES, k=d, tm=tm, tn=LANES, tk=d, out_dtype=F32, name="in_proj_small")
    (gates,) = _rowwise_fwd(_fn_gates, [(p_small, 0, LANES)], [a_row, b_row], [(LANES, F32)], tm=512, name="gates")
    conv_w = plan.weight("conv")
    qkv = _conv_fwd(p_main, conv_w, width=3 * HW, name="conv_fwd")
    o_gdn, s_all = _carried(plan, "gdn_fwd", _gdn_fwd, qkv, gates, name="gdn_fwd")
    gdn_rows = [(o_gdn, 0, HW), (p_main, OFF_ZA, HW)]
    (oa,) = _rowwise_fwd(_fn_gdn_out, gdn_rows, [gdn_norm_w], [(HW, BF16)], tm=512, ncol=HEADS, name="gdn_out")
    wa = plan.weight("wa")
    ya = _mm(oa, wa, mode="nn", m=t, n=d, k=HW, tm=tm, tn=1024, tk=HW, out_dtype=BF16, name="branch_a")
    qk_rows = [(p_main, OFF_QB, HW), (p_main, OFF_KB, HW)]
    qn, kn = _rowwise_fwd(_fn_qknorm, qk_rows, [fox_q_w, fox_k_w], [(HW, BF16), (HW, BF16)], tm=512, ncol=HEADS,
                          name="fox_qk_norm")
    blk = min(FOX_BLK, t)
    c4 = _fox_cumsum(gates, name="fox_cumsum").reshape(HEADS, t // blk, 1, blk)
    ob, ob32, lse = _carried(plan, "fox_fwd", _fox_fwd, qn, kn, p_main, c4, v_off=OFF_VB, name="fox_fwd")
    wb = plan.weight("wb")
    yb = _mm(ob, wb, mode="nn", m=t, n=d, k=HW, tm=tm, tn=1024, tk=HW, out_dtype=BF16, name="branch_b")
    mcol = 2 if d >= 2 * HW else 1
    merge_rows = [(p_main, OFF_GA, d), (p_main, off_gb, d), (ya, 0, d), (yb, 0, d)]
    (merged,) = _rowwise_fwd(_fn_merge, merge_rows, [], [(d, BF16)], tm=256, ncol=mcol, name="merge")
    wout = plan.weight("wout")
    h = _mm(merged, wout, mode="nn", m=t, n=d, k=d, tm=tm, tn=512, tk=d, out_dtype=F32, add=x, name="out_proj")
    (hn,) = _rowwise_fwd(_fn_norm, [(h, 0, d)], [norm_ffn_w], [(d, BF16)], tm=rt, name="ffn_norm")
    wg = plan.weight("wg")
    gate = _carried(plan, "ffn_gate", _mm, hn, wg, mode="nn", m=t, n=fp, k=d, tm=512, tn=cp, tk=d, out_dtype=BF16,
                    b_blocked=True, name="ffn_gate")
    wu = plan.weight("wu")
    up = _carried(plan, "ffn_up", _mm, hn, wu, mode="nn", m=t, n=fp, k=d, tm=512, tn=cp, tk=d, out_dtype=BF16,
                  b_blocked=True, name="ffn_up")
    (act,) = _rowwise_fwd(_fn_swiglu, [(gate, 0, fp), (up, 0, fp)], [], [(fp, BF16)], tm=512, ncol=fcol, name="swiglu")
    wd = plan.weight("wd")
    y = _mm(act, wd, mode="nn", m=t, n=d, k=fp, tm=512, tn=256, tk=fp, out_dtype=F32, add=h, name="ffn_down")
    dy, dyb, loss_row = _loss_head(y, target, tm=rt, name="loss_head")

    dact = _mm(dyb, wd, mode="nt", m=t, n=fp, k=d, tm=tm, tn=512, tk=d, out_dtype=BF16, name="d_act")
    plan.grad("wd", _mm(act, dyb, mode="tn", m=fp, n=d, k=t, tm=512, tn=1024, tk=t, out_dtype=BF16, name="dw_ffn_down"))
    dgate, dup = _rowwise_bwd(_fn_swiglu, [(gate, 0, fp), (up, 0, fp)], [], [dact], [BF16, BF16], tm=512, ncol=fcol,
                              name="d_swiglu")
    plan.grad("wg", _carried(plan, "dw_ffn_gate", _mm, hn, dgate, mode="tn", m=d, n=fp, k=t, tm=512, tn=cp, tk=t,
                             out_dtype=BF16, out_blocked=True, name="dw_ffn_gate"))
    dhn = _carried(plan, "d_hn_gate", _mm, dgate, wg, mode="nt", m=t, n=d, k=fp, tm=512, tn=256, tk=fp, out_dtype=F32,
                   b_blocked=True, name="d_hn_gate")
    dhn = _carried(plan, "d_hn_up", _mm, dup, wu, mode="nt", m=t, n=d, k=fp, tm=512, tn=256, tk=fp, out_dtype=F32,
                   add=dhn, b_blocked=True, name="d_hn_up")
    plan.grad("wu", _mm(hn, dup, mode="tn", m=d, n=fp, k=t, tm=512, tn=cp, tk=t, out_dtype=BF16, out_blocked=True,
                        name="dw_ffn_up"))
    dh, d_norm_ffn = _rowwise_bwd(_fn_norm, [(h, 0, d)], [norm_ffn_w], [dhn], [F32], tm=rt, name="d_ffn_norm", adds=[dy])
    dmerged = _carried(plan, "d_merged", _mm, dh, wout, mode="nt", m=t, n=d, k=d, tm=512, tn=512, tk=d, out_dtype=BF16,
                       name="d_merged")
    plan.grad("wout", _mm(merged, dh, mode="tn", m=d, n=d, k=t, tm=512, tn=512, tk=t, out_dtype=BF16, name="dw_out"))
    dga, dgb, dya, dyb2 = _rowwise_bwd(_fn_merge, merge_rows, [], [dmerged], [BF16] * 4, tm=256, ncol=mcol, name="d_merge")
    doa = _carried(plan, "d_oa", _mm, dya, wa, mode="nt", m=t, n=HW, k=d, tm=tm, tn=512, tk=d, out_dtype=BF16, name="d_oa")
    plan.grad("wa", _mm(oa, dya, mode="tn", m=HW, n=d, k=t, tm=1024, tn=d // N_DEV, tk=t, out_dtype=BF16,
                        out_blocked=True, name="dw_branch_a"))
    dob = _carried(plan, "d_ob", _mm, dyb2, wb, mode="nt", m=t, n=HW, k=d, tm=tm, tn=512, tk=d, out_dtype=BF16, name="d_ob")
    plan.grad("wb", _mm(ob, dyb2, mode="tn", m=HW, n=d, k=t, tm=1024, tn=d // N_DEV, tk=t, out_dtype=BF16,
                        out_blocked=True, name="dw_branch_b"))
    do_gdn, dza, d_gdn_norm = _rowwise_bwd(_fn_gdn_out, gdn_rows, [gdn_norm_w], [doa], [F32, BF16], tm=512,
                                           ncol=HEADS, name="d_gdn_out")
    dqkv, dgates_gdn = _carried(plan, "gdn_bwd", _gdn_bwd, qkv, gates, s_all, do_gdn, name="gdn_bwd")
    dp_qkv, dconv = _conv_bwd(p_main, conv_w, dqkv, width=3 * HW, name="conv_bwd")
    plan.grad("conv", dconv)
    dqn, dkn, dvb, dc4, dcq = _carried(plan, "fox_bwd", _fox_bwd, qn, kn, p_main, c4, ob32, dob, lse, v_off=OFF_VB,
                                       name="fox_bwd")
    dqb, dkb, d_fox_q, d_fox_k = _rowwise_bwd(_fn_qknorm, qk_rows, [fox_q_w, fox_k_w], [dqn, dkn], [BF16, BF16],
                                              tm=512, ncol=HEADS, name="d_fox_qk_norm")
    dgates = _fox_cumsum_bwd(dc4.reshape(HEADS, t) + dcq.reshape(HEADS, t), dgates_gdn, name="fox_cumsum_bwd")
    dsmall, d_a_row, d_b_row = _rowwise_bwd(_fn_gates, [(p_small, 0, LANES)], [a_row, b_row], [dgates], [F32],
                                            tm=512, name="d_gates")
    dp_main = _concat_cols([dp_qkv, dza, dqb, dkb, dvb, dga, dgb], name="d_p_main")
    plan.grad_w_in(_mm(xn, dp_main, mode="tn", m=d, n=n_main, k=t, tm=1024, tn=math.gcd(n_main, 1024), tk=t,
                       out_dtype=BF16, name="dw_in"),
                   _mm(xn, dsmall, mode="tn", m=d, n=LANES, k=t, tm=1024, tn=LANES, tk=t, out_dtype=BF16,
                       name="dw_in_small"))
    dxn = _mm(dsmall, w_small, mode="nt", m=t, n=d, k=LANES, tm=tm, tn=1024, tk=LANES, out_dtype=F32, name="d_xn_small")
    dxn = _carried(plan, "d_xn", _mm, dp_main, w_main, mode="nt", m=t, n=d, k=n_main, tm=tm, tn=1024,
                   tk=math.gcd(n_main, 2048),
                   out_dtype=F32, add=dxn, name="d_xn")
    grad_x, d_norm_mix = _rowwise_bwd(_fn_norm, [(x, 0, d)], [norm_mix_w], [dxn], [F32], tm=rt, name="d_mix_norm",
                                      adds=[dh])
    small = dict(norm_mix=d_norm_mix, norm_ffn=d_norm_ffn, gdn_norm=d_gdn_norm, fox_q=d_fox_q, fox_k=d_fox_k,
                 a_row=d_a_row, b_row=d_b_row)
    return loss_row[0, 0], grad_x, small


def _lane_row(pieces):
    row = jnp.zeros((1, LANES), F32)
    for off, p in pieces:
        row = lax.dynamic_update_slice(row, p.astype(F32), (0, off))
    return row


def _pack_small(norm_mix, norm_ffn, gdn_norm, fox_q, fox_k, a_log, dt_bias, b_f):
    rows = [norm_mix.reshape(-1, LANES), norm_ffn.reshape(-1, LANES), gdn_norm, fox_q, fox_k,
            _lane_row([(HEADS, a_log)]), _lane_row([(HEADS, dt_bias), (2 * HEADS, b_f)])]
    packed = jnp.concatenate(rows, axis=0)
    return _pad_rows(packed, -(-packed.shape[0] // 8) * 8)


def _unpack_small(p, d):
    nd = d // LANES
    r = 2 * nd
    return (p[0:nd].reshape(1, d), p[r + 3:r + 4, HEADS:2 * HEADS], p[r + 4:r + 5, HEADS:2 * HEADS], p[r:r + 1],
            p[r + 4:r + 5, 2 * HEADS:3 * HEADS], p[r + 1:r + 2], p[r + 2:r + 3], p[nd:r].reshape(1, d))


def _blocks_of_cols(a):
    r, c8 = a.shape
    return a.reshape(r, N_DEV, c8 // N_DEV).transpose(1, 0, 2)


def _cols_of_blocks(g):
    _, r, c = g.shape
    return g.transpose(1, 0, 2).reshape(r, N_DEV * c)


def kernel(x, norm_mix_w, w_in, conv_w, a_log, dt_bias, gdn_norm_w, fox_b_f, fox_q_norm_w, fox_k_norm_w, w_branch_a, w_branch_b, w_out, norm_ffn_w, w_ffn_gate, w_ffn_up, w_ffn_down, loss_target, m_norm_mix_w, m_w_in, m_conv_w, m_a_log, m_dt_bias, m_gdn_norm_w, m_fox_b_f, m_fox_q_norm_w, m_fox_k_norm_w, m_w_branch_a, m_w_branch_b, m_w_out, m_norm_ffn_w, m_w_ffn_gate, m_w_ffn_up, m_w_ffn_down, v_norm_mix_w, v_w_in, v_conv_w, v_a_log, v_dt_bias, v_gdn_norm_w, v_fox_b_f, v_fox_q_norm_w, v_fox_k_norm_w, v_w_branch_a, v_w_branch_b, v_w_out, v_norm_ffn_w, v_w_ffn_gate, v_w_ffn_up, v_w_ffn_down):
    d = x.shape[-1]
    cp = -(-w_ffn_down.shape[1] // LANES) * LANES
    nb = w_in.shape[2]

    g_in = _all_gather_relayed(w_in[0].astype(BF16), name="w_in_all_gather")
    w_main, w_small = _w_in_to_aligned(g_in, name="w_in_to_aligned")
    plan = _FsdpPlan(dict(conv=conv_w[0], wa=w_branch_a[0].astype(BF16), wb=w_branch_b[0].astype(BF16), wout=w_out[0].astype(BF16),
                          wg=_pad_cols(w_ffn_gate[0].astype(BF16), cp), wu=_pad_cols(w_ffn_up[0].astype(BF16), cp),
                          wd=_pad_rows(w_ffn_down[0].astype(BF16), cp)), d, cp, nb)
    a_row = _lane_row([(HEADS, a_log)])
    b_row = _lane_row([(HEADS, dt_bias), (2 * HEADS, fox_b_f)])

    loss_part, grad_x, gs = _local_step(
        x[0], loss_target[0], w_main, w_small, plan,
        norm_mix_w, norm_ffn_w, gdn_norm_w, fox_q_norm_w, fox_k_norm_w, a_row, b_row)
    loss = lax.psum(loss_part, ("x", "y", "c"))

    plan.flush()
    big = dict(w_in=("w_in", w_in, m_w_in, v_w_in), w_branch_a=("wa", w_branch_a, m_w_branch_a, v_w_branch_a),
               w_branch_b=("wb", w_branch_b, m_w_branch_b, v_w_branch_b), w_out=("wout", w_out, m_w_out, v_w_out),
               w_ffn_gate=("wg", w_ffn_gate, m_w_ffn_gate, v_w_ffn_gate), w_ffn_up=("wu", w_ffn_up, m_w_ffn_up, v_w_ffn_up),
               w_ffn_down=("wd", w_ffn_down, m_w_ffn_down, v_w_ffn_down), conv_w=("conv", conv_w, m_conv_w, v_conv_w))
    res = {}
    for nm, (key, w, m, v) in big.items():
        res[nm] = [o[None] for o in _adamw(plan.slots[key], w[0], m[0], v[0], name=f"adamw_{nm}")]

    g_small = _pack_small(gs["norm_mix"], gs["norm_ffn"], gs["gdn_norm"], gs["fox_q"], gs["fox_k"],
                          gs["a_row"][:, HEADS:2 * HEADS], gs["b_row"][:, HEADS:2 * HEADS],
                          gs["b_row"][:, 2 * HEADS:3 * HEADS])
    (g_small_all,) = _all_gather([g_small], name="small_grads_all_gather")
    w_small_p = _pack_small(norm_mix_w, norm_ffn_w, gdn_norm_w, fox_q_norm_w, fox_k_norm_w, a_log, dt_bias, fox_b_f)
    m_small_p = _pack_small(m_norm_mix_w, m_norm_ffn_w, m_gdn_norm_w, m_fox_q_norm_w, m_fox_k_norm_w, m_a_log,
                            m_dt_bias, m_fox_b_f)
    v_small_p = _pack_small(v_norm_mix_w, v_norm_ffn_w, v_gdn_norm_w, v_fox_q_norm_w, v_fox_k_norm_w, v_a_log,
                            v_dt_bias, v_fox_b_f)
    small_res = [_unpack_small(o, d) for o in _adamw(g_small_all, w_small_p, m_small_p, v_small_p, name="adamw_small")]

    def group(k):
        s = small_res[k]
        return [s[0], res["w_in"][k], res["conv_w"][k], s[1], s[2], s[3], s[4], s[5], s[6], res["w_branch_a"][k],
                res["w_branch_b"][k], res["w_out"][k], s[7], res["w_ffn_gate"][k], res["w_ffn_up"][k],
                res["w_ffn_down"][k]]

    return (loss, grad_x[None], *group(0), *group(1), *group(2), *group(3))
```

```python
import functools
import math

import jax
import jax.numpy as jnp
from jax import lax
from jax.experimental import pallas as pl
from jax.experimental.pallas import tpu as pltpu

F32 = jnp.float32
BF16 = jnp.bfloat16
HI = lax.Precision.HIGHEST
SOLVE_PRECISION = lax.Precision.HIGH
MESH = pl.DeviceIdType.MESH

EPS = 1e-6
HEADS = 16
DH = 128
HW = HEADS * DH
CHUNK = 64
CONV_K = 4
N_DEV = 8
LANES = 128
VMEM_LIMIT = 52 * 1024 * 1024

ADAM_LR = 0.001
ADAM_B1 = 0.9
ADAM_B2 = 0.999
ADAM_EPS = 1e-08
ADAM_WD = 0.01
ADAM_STEP = 10

OFF_QA, OFF_KA, OFF_VA, OFF_ZA, OFF_QB, OFF_KB, OFF_VB, OFF_GA = 0, HW, 2 * HW, 3 * HW, 4 * HW, 5 * HW, 6 * HW, 7 * HW


def _cparams(sem=None, vmem=VMEM_LIMIT):
    return pltpu.CompilerParams(dimension_semantics=sem, vmem_limit_bytes=vmem)


class _Comm:
    def __init__(self, ins, out_shapes, sems, start, finish, aliases=None):
        self.ins, self.out_shapes, self.sems = list(ins), list(out_shapes), list(sems)
        self.start, self.finish, self.aliases = start, finish, dict(aliases or {})


def _pcall(body, *, name, grid, in_specs, out_specs, out_shape, args, sem, scratch_shapes=(), comm=None):
    multi = isinstance(out_shape, (list, tuple))
    out_specs = list(out_specs) if multi else [out_specs]
    out_shape = list(out_shape) if multi else [out_shape]
    scratch_shapes = list(scratch_shapes)
    if comm is None:
        res = pl.pallas_call(body, name=name, grid=grid, in_specs=list(in_specs), out_specs=out_specs,
                             out_shape=out_shape, scratch_shapes=scratch_shapes, compiler_params=_cparams(sem))(*args)
        return res if multi else res[0]
    ni, no, ns = len(in_specs), len(out_specs), len(scratch_shapes)
    ci, co = len(comm.ins), len(comm.out_shapes)

    def wrapped(*refs):
        cin = refs[ni:ni + ci]
        outs = refs[ni + ci:ni + ci + no]
        cout = refs[ni + ci + no:ni + ci + no + co]
        scr = refs[ni + ci + no + co:ni + ci + no + co + ns]
        csem = refs[ni + ci + no + co + ns:]
        ids = [pl.program_id(ax) for ax in range(len(grid))]
        first = functools.reduce(jnp.logical_and, [i == 0 for i in ids])
        last = functools.reduce(jnp.logical_and, [i == g - 1 for i, g in zip(ids, grid)])

        @pl.when(first)
        def _():
            comm.start(cin, cout, csem)

        body(*refs[:ni], *outs, *scr)

        @pl.when(last)
        def _():
            comm.finish(cin, cout, csem)

    any_spec = pl.BlockSpec(memory_space=pl.ANY)
    res = pl.pallas_call(
        wrapped, name=name, grid=grid, in_specs=list(in_specs) + [any_spec] * ci,
        out_specs=out_specs + [any_spec] * co, out_shape=out_shape + comm.out_shapes,
        scratch_shapes=scratch_shapes + comm.sems,
        input_output_aliases={ni + i: no + o for i, o in comm.aliases.items()},
        compiler_params=_cparams(("arbitrary",) * len(grid)))(*args, *comm.ins)
    return (res[:no] if multi else res[0]), res[no:]


def _mm(a, b, *, mode, m, n, k, tm, tn, tk, out_dtype, name, a_off=(0, 0), b_off=(0, 0), add=None,
        b_blocked=False, out_blocked=False, comm=None, epilogue=None, tiles=(), out_dtypes=()):
    tm, tn, tk = min(tm, m), min(tn, n), min(tk, k)
    assert m % tm == 0 and n % tn == 0 and k % tk == 0, (name, m, n, k, tm, tn, tk)
    nk = k // tk
    if mode == "nn":
        a_blk, b_blk = (tm, tk), (tk, tn)
        ao, bo = (a_off[0] // tm, a_off[1] // tk), (b_off[0] // tk, b_off[1] // tn)
        a_map = lambda i, j, kk: (i + ao[0], kk + ao[1])
        b_map = lambda i, j, kk: (kk + bo[0], j + bo[1])
        dims = (((1,), (0,)), ((), ()))
        if b_blocked:
            assert b.shape == (n // tn, k, tn) and b_off == (0, 0), (name, b.shape)
            b_blk, b_map = (None, tk, tn), lambda i, j, kk: (j, kk, 0)
    elif mode == "nt":
        a_blk, b_blk = (tm, tk), (tn, tk)
        ao, bo = (a_off[0] // tm, a_off[1] // tk), (b_off[0] // tn, b_off[1] // tk)
        a_map = lambda i, j, kk: (i + ao[0], kk + ao[1])
        b_map = lambda i, j, kk: (j + bo[0], kk + bo[1])
        dims = (((1,), (1,)), ((), ()))
        if b_blocked and tk == k:
            kblocks, cblk = b.shape[0], b.shape[2]
            assert b.shape == (kblocks, n, cblk) and kblocks * cblk == k and b_off == (0, 0), (name, b.shape)
            b_blk, b_map = (kblocks, tn, cblk), lambda i, j, kk: (0, j, 0)
        elif b_blocked:
            assert b.shape == (nk, n, tk) and b_off == (0, 0), (name, b.shape)
            b_blk, b_map = (None, tn, tk), lambda i, j, kk: (kk, j, 0)
    else:
        assert not b_blocked
        a_blk, b_blk = (tk, tm), (tk, tn)
        ao, bo = (a_off[0] // tk, a_off[1] // tm), (b_off[0] // tk, b_off[1] // tn)
        a_map = lambda i, j, kk: (kk + ao[0], i + ao[1])
        b_map = lambda i, j, kk: (kk + bo[0], j + bo[1])
        dims = (((0,), (0,)), ((), ()))
    if not b_blocked:
        for off, blk in ((a_off, a_blk), (b_off, b_blk)):
            assert off[0] % blk[0] == 0 and off[1] % blk[1] == 0, (name, off, blk)
    has_add = add is not None
    n_tiles, n_outs = len(tiles), len(out_dtypes) if epilogue is not None else 1
    assert epilogue is None or (nk == 1 and not out_blocked and not has_add), name

    def body(*refs):
        a_ref, b_ref = refs[:2]
        c_ref = refs[2] if has_add else None
        tile_refs = refs[2 + has_add:2 + has_add + n_tiles]
        out_refs = refs[2 + has_add + n_tiles:2 + has_add + n_tiles + n_outs]
        o_ref, acc = out_refs[0], refs[-1]
        if len(b_blk) == 3 and b_blk[0] is not None:
            cblk = b_blk[2]
            p = sum(lax.dot_general(a_ref[:, q * cblk:(q + 1) * cblk].astype(BF16), b_ref[q].astype(BF16), dims,
                                    preferred_element_type=F32) for q in range(b_blk[0]))
        else:
            p = lax.dot_general(a_ref[...].astype(BF16), b_ref[...].astype(BF16), dims, preferred_element_type=F32)
        if epilogue is not None:
            for ref, val in zip(out_refs, epilogue(p, *[r[...].astype(F32) for r in tile_refs])):
                ref[...] = val.astype(ref.dtype)
        elif nk == 1:
            if has_add:
                p = p + c_ref[...].astype(F32)
            o_ref[...] = p.astype(o_ref.dtype)
        else:
            kk = pl.program_id(2)

            @pl.when(kk == 0)
            def _():
                acc[...] = p + c_ref[...].astype(F32) if has_add else p

            @pl.when(kk > 0)
            def _():
                acc[...] += p

            @pl.when(kk == nk - 1)
            def _():
                o_ref[...] = acc[...].astype(o_ref.dtype)

    in_specs = [pl.BlockSpec(a_blk, a_map), pl.BlockSpec(b_blk, b_map)]
    args = [a, b]
    if has_add:
        in_specs.append(pl.BlockSpec((tm, tn), lambda i, j, kk: (i, j)))
        args.append(add)
    for arr, off in tiles:
        assert off % tn == 0, (name, off, tn)
        in_specs.append(pl.BlockSpec((tm, tn), lambda i, j, kk, ob=off // tn: (i, j + ob)))
        args.append(arr)
    acc_shape = (tm, tn) if nk > 1 else (8, LANES)
    if out_blocked:
        out_spec = pl.BlockSpec((None, tm, tn), lambda i, j, kk: (j, i, 0))
        out_shape = jax.ShapeDtypeStruct((n // tn, m, tn), out_dtype)
    else:
        out_spec = pl.BlockSpec((tm, tn), lambda i, j, kk: (i, j))
        out_shape = jax.ShapeDtypeStruct((m, n), out_dtype)
    if epilogue is not None:
        out_spec = [out_spec] * n_outs
        out_shape = [jax.ShapeDtypeStruct((m, n), dt) for dt in out_dtypes]
    return _pcall(body, name=name, grid=(m // tm, n // tn, nk), in_specs=in_specs, out_specs=out_spec,
                  out_shape=out_shape, scratch_shapes=[pltpu.VMEM(acc_shape, F32)], args=args,
                  sem=("parallel", "parallel", "arbitrary"), comm=comm)


def _row_specs(rows, tm, ncol):
    specs = []
    for arr, off, width in rows:
        bw = width // ncol
        assert width % ncol == 0 and off % bw == 0, (off, width, ncol)
        ob = off // bw
        specs.append(pl.BlockSpec((tm, bw), lambda i, j, ob=ob: (i, j + ob)))
    return specs


def _col_group(ref, s, inner):
    w = ref.shape[1] // inner
    return slice(None), slice(s * w, (s + 1) * w)


def _rowwise_fwd(fn, rows, params, outs, *, tm, ncol=1, inner=1, name):
    t = rows[0][0].shape[0]
    tm = min(tm, t)
    nr, npar = len(rows), len(params)

    def body(*refs):
        par = [r[...].astype(F32) for r in refs[nr:nr + npar]]
        for s in range(inner):
            res = fn(*[r[_col_group(r, s, inner)].astype(F32) for r in refs[:nr]], *par)
            for o_ref, val in zip(refs[nr + npar:], res):
                o_ref[_col_group(o_ref, s, inner)] = val.astype(o_ref.dtype)

    in_specs = _row_specs(rows, tm, ncol) + [pl.BlockSpec(p.shape, lambda i, j: (0, 0)) for p in params]
    out_specs = [pl.BlockSpec((tm, w // ncol), lambda i, j: (i, j)) for w, _ in outs]
    out_shape = [jax.ShapeDtypeStruct((t, w), dt) for w, dt in outs]
    return pl.pallas_call(
        body, name=name, grid=(t // tm, ncol), in_specs=in_specs, out_specs=out_specs, out_shape=out_shape,
        compiler_params=_cparams(("parallel", "parallel")),
    )(*[r[0] for r in rows], *params)


def _rowwise_bwd(fn, rows, params, cts, grad_dtypes, *, tm, ncol=1, inner=1, name, adds=None, bf16_copy_of=None):
    t = rows[0][0].shape[0]
    tm = min(tm, t)
    nr, npar, nct = len(rows), len(params), len(cts)
    adds = adds or [None] * nr
    add_idx = [i for i, a in enumerate(adds) if a is not None]

    def body(*refs):
        par = [r[...].astype(F32) for r in refs[nr:nr + npar]]
        ct_refs = refs[nr + npar:nr + npar + nct]
        add_refs = refs[nr + npar + nct:nr + npar + nct + len(add_idx)]
        outs = refs[nr + npar + nct + len(add_idx):]
        extra = dict(zip(add_idx, add_refs))
        par_grads = [None] * npar
        for s in range(inner):
            ins = [r[_col_group(r, s, inner)].astype(F32) for r in refs[:nr]]
            _, vjp = jax.vjp(lambda *a: tuple(fn(*a)), *ins, *par)
            grads = vjp(tuple(r[_col_group(r, s, inner)].astype(F32) for r in ct_refs))
            for i in range(nr):
                g = grads[i]
                if i in extra:
                    g = g + extra[i][_col_group(extra[i], s, inner)].astype(F32)
                outs[i][_col_group(outs[i], s, inner)] = g.astype(outs[i].dtype)
                if i == bf16_copy_of:
                    outs[nr + npar][_col_group(outs[nr + npar], s, inner)] = g.astype(BF16)
            par_grads = [g if acc is None else acc + g for acc, g in zip(par_grads, grads[nr:])]
        first = jnp.logical_and(pl.program_id(0) == 0, pl.program_id(1) == 0)
        for pi in range(npar):
            o_ref = outs[nr + pi]
            g = par_grads[pi]

            @pl.when(first)
            def _(o_ref=o_ref, g=g):
                o_ref[...] = g

            @pl.when(jnp.logical_not(first))
            def _(o_ref=o_ref, g=g):
                o_ref[...] += g

    in_specs = (_row_specs(rows, tm, ncol)
                + [pl.BlockSpec(p.shape, lambda i, j: (0, 0)) for p in params]
                + [pl.BlockSpec((tm, c.shape[1] // ncol), lambda i, j: (i, j)) for c in cts]
                + [pl.BlockSpec((tm, adds[i].shape[1] // ncol), lambda i, j: (i, j)) for i in add_idx])
    out_specs = ([pl.BlockSpec((tm, w // ncol), lambda i, j: (i, j)) for _, _, w in rows]
                 + [pl.BlockSpec(p.shape, lambda i, j: (0, 0)) for p in params])
    out_shape = ([jax.ShapeDtypeStruct((t, w), dt) for (_, _, w), dt in zip(rows, grad_dtypes)]
                 + [jax.ShapeDtypeStruct(p.shape, F32) for p in params])
    if bf16_copy_of is not None:
        w = rows[bf16_copy_of][2]
        out_specs.append(pl.BlockSpec((tm, w // ncol), lambda i, j: (i, j)))
        out_shape.append(jax.ShapeDtypeStruct((t, w), BF16))
    return pl.pallas_call(
        body, name=name, grid=(t // tm, ncol), in_specs=in_specs, out_specs=out_specs, out_shape=out_shape,
        compiler_params=_cparams(("arbitrary", "arbitrary")),
    )(*[r[0] for r in rows], *params, *cts, *[adds[i] for i in add_idx])


def _rms(x, w):
    return x * lax.rsqrt(jnp.mean(x * x, axis=-1, keepdims=True) + EPS) * w


def _fn_norm(x, w):
    return (_rms(x, w),)


def _fn_gates(z, a_row, b_row):
    lane = lax.broadcasted_iota(jnp.int32, z.shape, 1)
    beta = jax.nn.sigmoid(z)
    g = -jnp.exp(a_row) * jax.nn.softplus(z + b_row)
    logf = jax.nn.log_sigmoid(z + b_row)
    return (jnp.where(lane < HEADS, beta, jnp.where(lane < 2 * HEADS, g, jnp.where(lane < 3 * HEADS, logf, 0.0))),)


def _fn_qknorm(q, k, qw, kw):
    return _rms(q, qw), _rms(k, kw)


def _fn_gdn_out(o, z, w):
    return (_rms(o, w) * jax.nn.silu(z),)


def _fn_merge(ga, gb, ya, yb):
    return (jax.nn.sigmoid(ga) * ya + jax.nn.sigmoid(gb) * yb,)


def _fn_swiglu(g, u):
    return (jax.nn.silu(g) * u,)


def _bf16_round(x):
    return x.astype(BF16).astype(F32)


def _loss_head(y, target, *, tm, name):
    t, d = y.shape
    tm = min(tm, t)

    def body(y_ref, t_ref, dyf_ref, dyb_ref, loss_ref):
        err = y_ref[...] - t_ref[...]
        dy = err * (1.0 / d)
        dyf_ref[...] = dy
        dyb_ref[...] = dy.astype(BF16)
        part = jnp.sum(err * err) * (0.5 / d)

        @pl.when(pl.program_id(0) == 0)
        def _():
            loss_ref[...] = jnp.zeros_like(loss_ref)

        loss_ref[...] += part

    blk = pl.BlockSpec((tm, d), lambda i: (i, 0))
    return pl.pallas_call(
        body, name=name, grid=(t // tm,), in_specs=[blk, blk],
        out_specs=[blk, blk, pl.BlockSpec((1, LANES), lambda i: (0, 0))],
        out_shape=[jax.ShapeDtypeStruct((t, d), F32), jax.ShapeDtypeStruct((t, d), BF16),
                   jax.ShapeDtypeStruct((1, LANES), F32)],
        compiler_params=_cparams(("arbitrary",)),
    )(y, target)


def _shift_down(x, s):
    if s == 0:
        return x
    row = lax.broadcasted_iota(jnp.int32, x.shape, 0)
    return jnp.where(row >= s, pltpu.roll(x, s, 0), 0.0)


def _shift_up(x, s):
    if s == 0:
        return x
    t = x.shape[0]
    row = lax.broadcasted_iota(jnp.int32, x.shape, 0)
    return jnp.where(row < t - s, pltpu.roll(x, t - s, 0), 0.0)


def _conv_pre(x, w):
    y = x * w[CONV_K - 1:CONV_K, :]
    for i in range(CONV_K - 1):
        y = y + _shift_down(x, CONV_K - 1 - i) * w[i:i + 1, :]
    return y


def _conv_fwd(p_main, conv_w, *, width, name):
    t = p_main.shape[0]
    tc = LANES

    def body(x_ref, w_ref, o_ref):
        y = _conv_pre(x_ref[...].astype(F32), w_ref[...])
        o_ref[...] = y * jax.nn.sigmoid(y)

    return pl.pallas_call(
        body, name=name, grid=(width // tc,),
        in_specs=[pl.BlockSpec((t, tc), lambda j: (0, j)), pl.BlockSpec((CONV_K, tc), lambda j: (0, j))],
        out_specs=pl.BlockSpec((t, tc), lambda j: (0, j)),
        out_shape=jax.ShapeDtypeStruct((t, width), F32),
        compiler_params=_cparams(("parallel",)),
    )(p_main, conv_w)


def _conv_bwd(p_main, conv_w, dy, *, width, name):
    t = p_main.shape[0]
    tc = LANES

    def body(x_ref, w_ref, dy_ref, dx_ref, dw_ref):
        x = x_ref[...].astype(F32)
        w = w_ref[...]
        pre = _conv_pre(x, w)
        sg = jax.nn.sigmoid(pre)
        dpre = dy_ref[...] * (sg * (1.0 + pre * (1.0 - sg)))
        dx = dpre * w[CONV_K - 1:CONV_K, :]
        dws = []
        for i in range(CONV_K - 1):
            s = CONV_K - 1 - i
            dx = dx + _shift_up(dpre, s) * w[i:i + 1, :]
            dws.append(jnp.sum(_shift_down(x, s) * dpre, axis=0, keepdims=True))
        dws.append(jnp.sum(x * dpre, axis=0, keepdims=True))
        dx_ref[...] = dx.astype(dx_ref.dtype)
        dw_ref[...] = jnp.concatenate(dws, axis=0)

    return pl.pallas_call(
        body, name=name, grid=(width // tc,),
        in_specs=[pl.BlockSpec((t, tc), lambda j: (0, j)), pl.BlockSpec((CONV_K, tc), lambda j: (0, j)),
                  pl.BlockSpec((t, tc), lambda j: (0, j))],
        out_specs=[pl.BlockSpec((t, tc), lambda j: (0, j)), pl.BlockSpec((CONV_K, tc), lambda j: (0, j))],
        out_shape=[jax.ShapeDtypeStruct((t, width), BF16), jax.ShapeDtypeStruct((CONV_K, width), F32)],
        compiler_params=_cparams(("parallel",)),
    )(p_main, conv_w, dy)


def _bmm(a, b, spec, precision=None):
    return jnp.einsum(spec, a, b, preferred_element_type=F32, precision=precision)


def _iota2(shape, dim):
    return lax.broadcasted_iota(jnp.int32, shape, dim)


@jax.custom_vjp
def _tri_inverse(a):
    return _tri_inverse_levels(a)


def _tri_inverse_fwd(a):
    t = _tri_inverse_levels(a)
    return t, t


def _tri_inverse_bwd(t, g):
    x = _bmm(t, g, "hji,hjk->hik", SOLVE_PRECISION)
    return (-_bmm(x, t, "hik,hjk->hij", SOLVE_PRECISION),)


_tri_inverse.defvjp(_tri_inverse_fwd, _tri_inverse_bwd)


def _tri_inverse_levels(a):
    c = a.shape[-1]
    r, m = _iota2((c, c), 0), _iota2((c, c), 1)
    eye = (r == m).astype(F32)
    inv = None
    b = 1
    while b < c:
        mask = jnp.logical_and(r // (2 * b) == m // (2 * b), jnp.logical_and(r % (2 * b) >= b, m % (2 * b) < b))
        off = jnp.where(mask[None], a, 0.0)
        if inv is None:
            inv = eye[None] - off
        else:
            inv = inv - _bmm(_bmm(inv, off, "hij,hjk->hik", SOLVE_PRECISION), inv, "hij,hjk->hik", SOLVE_PRECISION)
        b *= 2
    return inv


def _gdn_chunk(s, q3, k3, v3, b3, gc3):
    c = q3.shape[1]
    r, m = _iota2((c, c), 0), _iota2((c, c), 1)
    tril_incl = (r >= m)[None]
    tril_strict = (r > m)[None]
    eye = (r == m).astype(F32)[None]
    qn = q3 * lax.rsqrt(jnp.sum(q3 * q3, axis=-1, keepdims=True) + EPS) * (DH ** -0.5)
    kn = k3 * lax.rsqrt(jnp.sum(k3 * k3, axis=-1, keepdims=True) + EPS)
    ones = jnp.ones((q3.shape[0], c, c), F32)
    gc_row = _bmm(ones, gc3 * eye, "hij,hjk->hik", SOLVE_PRECISION)
    decay = jnp.where(tril_incl, jnp.exp(jnp.where(tril_incl, gc3 - gc_row, 0.0)), 0.0)
    a = jnp.where(tril_strict, _bmm(kn, kn, "hcd,hmd->hcm") * decay * b3, 0.0)
    tinv = _tri_inverse(a)
    egc = jnp.exp(gc3)
    u = _bmm(tinv, v3 * b3, "hij,hjk->hik", SOLVE_PRECISION)
    w = _bmm(tinv, kn * (b3 * egc), "hij,hjk->hik", SOLVE_PRECISION)
    qk = _bmm(qn, kn, "hcd,hmd->hcm") * decay
    v_new = u - _bmm(w, s, "hcd,hdv->hcv")
    o = _bmm(qn * egc, s, "hcd,hdv->hcv") + _bmm(qk, v_new, "hcm,hmv->hcv")
    row = _iota2((c, 1), 0)[None]
    g_last = jnp.sum(jnp.where(row == c - 1, gc3, 0.0), axis=1, keepdims=True)
    s_new = s * jnp.exp(g_last) + _bmm(kn * jnp.exp(g_last - gc3), v_new, "hcd,hcv->hdv")
    return s_new, o


GDN_HEAD_GROUP = 8


def _split_heads(ref, off, h0):
    return jnp.stack([ref[:, off + h * DH:off + (h + 1) * DH].astype(F32)
                      for h in range(h0, h0 + GDN_HEAD_GROUP)], axis=0)


def _store_heads(ref, x3, off, h0):
    for i in range(GDN_HEAD_GROUP):
        h = h0 + i
        ref[:, off + h * DH:off + (h + 1) * DH] = x3[i].astype(ref.dtype)


def _lane_cols(tile, lane0):
    lane = _iota2(tile.shape, 1)
    return jnp.stack([jnp.sum(jnp.where(lane == lane0 + i, tile, 0.0), axis=1, keepdims=True)
                      for i in range(GDN_HEAD_GROUP)], axis=0)


def _cols_to_lanes(cols3, lane0, shape):
    lane = _iota2(shape, 1)
    out = jnp.zeros(shape, F32)
    for i in range(GDN_HEAD_GROUP):
        out = out + jnp.where(lane == lane0 + i, cols3[i], 0.0)
    return out


def _chunk_cumsum_matrix():
    r, m = _iota2((CHUNK, CHUNK), 0), _iota2((CHUNK, CHUNK), 1)
    return (r >= m).astype(F32)


def _gdn_inputs(qkv_ref, gt, gcum, h0):
    return (_split_heads(qkv_ref, 0, h0), _split_heads(qkv_ref, HW, h0), _split_heads(qkv_ref, 2 * HW, h0),
            _lane_cols(gt, h0), _lane_cols(gcum, HEADS + h0))


def _gdn_fwd(qkv, gates, *, name, comm=None):
    t = qkv.shape[0]
    n = t // CHUNK

    def body(qkv_ref, gt_ref, o_ref, sall_ref, s_scr):
        @pl.when(pl.program_id(0) == 0)
        def _():
            s_scr[...] = jnp.zeros_like(s_scr)

        gt = gt_ref[...]
        gcum = jnp.dot(_chunk_cumsum_matrix(), gt, preferred_element_type=F32, precision=HI)
        for h0 in range(0, HEADS, GDN_HEAD_GROUP):
            grp = pl.ds(h0, GDN_HEAD_GROUP)
            s = s_scr[grp]
            sall_ref[0, grp] = s
            s_new, o3 = _gdn_chunk(s, *_gdn_inputs(qkv_ref, gt, gcum, h0))
            s_scr[grp] = s_new
            _store_heads(o_ref, o3, 0, h0)

    return _pcall(
        body, name=name, grid=(n,),
        in_specs=[pl.BlockSpec((CHUNK, 3 * HW), lambda i: (i, 0)), pl.BlockSpec((CHUNK, LANES), lambda i: (i, 0))],
        out_specs=[pl.BlockSpec((CHUNK, HW), lambda i: (i, 0)),
                   pl.BlockSpec((1, HEADS, DH, DH), lambda i: (i, 0, 0, 0))],
        out_shape=[jax.ShapeDtypeStruct((t, HW), F32), jax.ShapeDtypeStruct((n, HEADS, DH, DH), F32)],
        scratch_shapes=[pltpu.VMEM((HEADS, DH, DH), F32)], sem=("arbitrary",), args=(qkv, gates), comm=comm)


def _gdn_bwd(qkv, gates, s_all, do, *, name, comm=None):
    t = qkv.shape[0]
    n = t // CHUNK

    def body(qkv_ref, gt_ref, sall_ref, do_ref, dqkv_ref, dgt_ref, ds_scr):
        @pl.when(pl.program_id(0) == 0)
        def _():
            ds_scr[...] = jnp.zeros_like(ds_scr)

        gt = gt_ref[...]
        cum = _chunk_cumsum_matrix()
        gcum = jnp.dot(cum, gt, preferred_element_type=F32, precision=HI)
        shape = (CHUNK, LANES)
        dbeta = jnp.zeros(shape, F32)
        dgcum = jnp.zeros(shape, F32)
        for h0 in range(0, HEADS, GDN_HEAD_GROUP):
            grp = pl.ds(h0, GDN_HEAD_GROUP)
            _, vjp = jax.vjp(_gdn_chunk, sall_ref[0, grp], *_gdn_inputs(qkv_ref, gt, gcum, h0))
            ds, dq3, dk3, dv3, db3, dgc3 = vjp((ds_scr[grp], _split_heads(do_ref, 0, h0)))
            ds_scr[grp] = ds
            _store_heads(dqkv_ref, dq3, 0, h0)
            _store_heads(dqkv_ref, dk3, HW, h0)
            _store_heads(dqkv_ref, dv3, 2 * HW, h0)
            dbeta = dbeta + _cols_to_lanes(db3, h0, shape)
            dgcum = dgcum + _cols_to_lanes(dgc3, HEADS + h0, shape)
        dg = lax.dot_general(cum, dgcum, (((0,), (0,)), ((), ())), preferred_element_type=F32, precision=HI)
        dgt_ref[...] = dbeta + dg

    rev = lambda i: n - 1 - i
    return _pcall(
        body, name=name, grid=(n,),
        in_specs=[pl.BlockSpec((CHUNK, 3 * HW), lambda i: (rev(i), 0)), pl.BlockSpec((CHUNK, LANES), lambda i: (rev(i), 0)),
                  pl.BlockSpec((1, HEADS, DH, DH), lambda i: (rev(i), 0, 0, 0)),
                  pl.BlockSpec((CHUNK, HW), lambda i: (rev(i), 0))],
        out_specs=[pl.BlockSpec((CHUNK, 3 * HW), lambda i: (rev(i), 0)), pl.BlockSpec((CHUNK, LANES), lambda i: (rev(i), 0))],
        out_shape=[jax.ShapeDtypeStruct((t, 3 * HW), F32), jax.ShapeDtypeStruct((t, LANES), F32)],
        scratch_shapes=[pltpu.VMEM((HEADS, DH, DH), F32)], sem=("arbitrary",), args=(qkv, gates, s_all, do), comm=comm)


FOX_BLK = 512
FOX_ROW_SPLIT = 1
NEG = -1e30


def _fox_cumsum(gates, *, name):
    t = gates.shape[0]
    blk = min(FOX_BLK, t)

    def body(g_ref, c_ref):
        r, m = _iota2((blk, blk), 0), _iota2((blk, blk), 1)
        upper = (r <= m).astype(F32)
        carry = jnp.zeros((HEADS, 1), F32)
        for b in range(t // blk):
            lf = g_ref[b * blk:(b + 1) * blk, :].T[2 * HEADS:3 * HEADS, :]
            c_ref[:, b * blk:(b + 1) * blk] = jnp.dot(lf, upper, preferred_element_type=F32, precision=HI) + carry
            carry = carry + jnp.sum(lf, axis=1, keepdims=True)

    return pl.pallas_call(body, name=name, out_shape=jax.ShapeDtypeStruct((HEADS, t), F32),
                          compiler_params=_cparams())(gates)


def _fox_cumsum_bwd(dc, dgates_gdn, *, name):
    t = dc.shape[1]
    blk = min(FOX_BLK, t)

    def body(dc_ref, dg_ref, o_ref):
        r, m = _iota2((blk, blk), 0), _iota2((blk, blk), 1)
        lower = (r >= m).astype(F32)
        carry = jnp.zeros((HEADS, 1), F32)
        for b in reversed(range(t // blk)):
            d = dc_ref[:, b * blk:(b + 1) * blk]
            dlf = jnp.dot(d, lower, preferred_element_type=F32, precision=HI) + carry
            carry = carry + jnp.sum(d, axis=1, keepdims=True)
            tile = jnp.concatenate([jnp.zeros((2 * HEADS, blk), F32), dlf,
                                    jnp.zeros((LANES - 3 * HEADS, blk), F32)], axis=0)
            o_ref[b * blk:(b + 1) * blk, :] = tile.T + dg_ref[b * blk:(b + 1) * blk, :]

    return pl.pallas_call(body, name=name, out_shape=jax.ShapeDtypeStruct((t, LANES), F32),
                          compiler_params=_cparams())(dc, dgates_gdn)


def _fox_logits(q, k, c_row, row0=None):
    s = lax.dot_general(q, k, (((1,), (1,)), ((), ())), preferred_element_type=F32) * (DH ** -0.5) - c_row
    if row0 is None:
        return s
    return jnp.where(row0 + _iota2(s.shape, 0) >= _iota2(s.shape, 1), s, NEG)


def _fox_fwd(qn, kn, p_main, c4, *, v_off, name, comm=None):
    t = qn.shape[0]
    blk = min(FOX_BLK, t)
    nb = t // blk
    vb = v_off // DH

    def body(q_ref, k_ref, v_ref, c_ref, o_ref, o32_ref, lse_ref):
        qi = pl.program_id(1)
        q = q_ref[...]

        def step(j, carry, diagonal=False):
            m, l, acc = carry
            rows = pl.ds(pl.multiple_of(j * blk, blk), blk)
            s = _fox_logits(q, k_ref[rows, :], c_ref[0, j], 0 if diagonal else None)
            m_new = jnp.maximum(m, jnp.max(s, axis=1, keepdims=True))
            p = jnp.exp(s - m_new)
            scale = jnp.exp(m - m_new)
            l = scale * l + jnp.sum(p, axis=1, keepdims=True)
            acc = scale * acc + jnp.dot(p.astype(BF16), v_ref[rows, :], preferred_element_type=F32)
            return m_new, l, acc

        init = (jnp.full((blk, 1), NEG, F32), jnp.zeros((blk, 1), F32), jnp.zeros((blk, DH), F32))
        m, l, acc = step(qi, lax.fori_loop(0, qi, step, init), diagonal=True)
        o = acc / l
        o_ref[...] = o.astype(o_ref.dtype)
        o32_ref[...] = o
        lse_ref[0] = m + jnp.log(l)

    return _pcall(
        body, name=name, grid=(HEADS, nb),
        in_specs=[pl.BlockSpec((blk, DH), lambda h, i: (i, h)), pl.BlockSpec((t, DH), lambda h, i: (0, h)),
                  pl.BlockSpec((t, DH), lambda h, i: (0, vb + h)), pl.BlockSpec((1, nb, 1, blk), lambda h, i: (h, 0, 0, 0))],
        out_specs=[pl.BlockSpec((blk, DH), lambda h, i: (i, h)), pl.BlockSpec((blk, DH), lambda h, i: (i, h)),
                   pl.BlockSpec((1, blk, 1), lambda h, i: (h, i, 0))],
        out_shape=[jax.ShapeDtypeStruct((t, HW), BF16), jax.ShapeDtypeStruct((t, HW), F32),
                   jax.ShapeDtypeStruct((HEADS, t, 1), F32)],
        sem=("parallel", "arbitrary"), args=(qn, kn, p_main, c4), comm=comm)


def _fox_bwd(qn, kn, p_main, c4, o32, do, lse, *, v_off, name, comm=None):
    t = qn.shape[0]
    blk = min(FOX_BLK, t)
    nb = t // blk
    vb = v_off // DH
    sub = blk // FOX_ROW_SPLIT
    tn_dims = (((0,), (0,)), ((), ()))
    nt_dims = (((1,), (1,)), ((), ()))

    def body(q_ref, k_ref, v_ref, c_ref, o_ref, do_ref, lse_ref, dq_ref, dk_ref, dv_ref, dc_ref, dcq_ref):
        kj = pl.program_id(1)

        @pl.when(kj == 0)
        def _():
            dq_ref[...] = jnp.zeros_like(dq_ref)
            dcq_ref[...] = jnp.zeros_like(dcq_ref)

        k = k_ref[...]
        v = v_ref[...]
        c_row = c_ref[0, 0]

        def step(i, carry, diagonal=False):
            dk, dv, dc = carry
            for u in range(FOX_ROW_SPLIT):
                rows = pl.ds(pl.multiple_of(i * blk + u * sub, sub), sub)
                q = q_ref[rows, :]
                dob = do_ref[rows, :]
                p = jnp.exp(_fox_logits(q, k, c_row, u * sub if diagonal else None) - lse_ref[0, rows, :])
                pb = p.astype(BF16)
                dv = dv + lax.dot_general(pb, dob, tn_dims, preferred_element_type=F32)
                dp = lax.dot_general(dob, v, nt_dims, preferred_element_type=F32)
                delta = jnp.sum(dob.astype(F32) * o_ref[rows, :], axis=1, keepdims=True)
                ds = p * (dp - delta)
                dcq_ref[0, rows, :] += jnp.sum(ds, axis=1, keepdims=True)
                dsb = ds.astype(BF16)
                dq_ref[rows, :] += jnp.dot(dsb, k, preferred_element_type=F32) * (DH ** -0.5)
                dk = dk + lax.dot_general(dsb, q, tn_dims, preferred_element_type=F32) * (DH ** -0.5)
                dc = dc - jnp.sum(ds, axis=0, keepdims=True)
            return dk, dv, dc

        init = (jnp.zeros((blk, DH), F32), jnp.zeros((blk, DH), F32), jnp.zeros((1, blk), F32))
        dk, dv, dc = lax.fori_loop(kj + 1, nb, step, step(kj, init, diagonal=True))
        dk_ref[...] = dk
        dv_ref[...] = dv.astype(dv_ref.dtype)
        dc_ref[0, 0] = dc

    full = lambda h, j: (0, h)
    kvb = lambda h, j: (j, h)
    return _pcall(
        body, name=name, grid=(HEADS, nb), sem=("parallel", "arbitrary"), comm=comm,
        args=(qn, kn, p_main, c4, o32, do, lse),
        in_specs=[pl.BlockSpec((t, DH), full), pl.BlockSpec((blk, DH), kvb),
                  pl.BlockSpec((blk, DH), lambda h, j: (j, vb + h)), pl.BlockSpec((1, 1, 1, blk), lambda h, j: (h, j, 0, 0)),
                  pl.BlockSpec((t, DH), full), pl.BlockSpec((t, DH), full),
                  pl.BlockSpec((1, t, 1), lambda h, j: (h, 0, 0))],
        out_specs=[pl.BlockSpec((t, DH), full), pl.BlockSpec((blk, DH), kvb), pl.BlockSpec((blk, DH), kvb),
                   pl.BlockSpec((1, 1, 1, blk), lambda h, j: (h, j, 0, 0)), pl.BlockSpec((1, t, 1), lambda h, j: (h, 0, 0))],
        out_shape=[jax.ShapeDtypeStruct((t, HW), F32), jax.ShapeDtypeStruct((t, HW), F32),
                   jax.ShapeDtypeStruct((t, HW), BF16), jax.ShapeDtypeStruct((HEADS, nb, 1, blk), F32),
                   jax.ShapeDtypeStruct((HEADS, t, 1), F32)])


ANY = pl.BlockSpec(memory_space=pl.ANY)


def _mesh_pos():
    return lax.axis_index("x"), lax.axis_index("y"), lax.axis_index("c")


def _all_gather(blocks, *, name):
    n = len(blocks)

    def body(*refs):
        ins, outs = refs[:n], refs[n:2 * n]
        send, recv, local = refs[2 * n:]
        x, y, c = _mesh_pos()
        me, sibling = (x, y, c), (x, y, 1 - c)
        chips = [(1 - x, y), (x, 1 - y), (1 - x, 1 - y)]

        def copy(t, k, block, to, src=None):
            dst = outs[t].at[4 * block[0] + 2 * block[1] + block[2]]
            return pltpu.make_async_remote_copy(
                src_ref=dst if src is None else src, dst_ref=dst, send_sem=send.at[7 * t + k],
                recv_sem=recv.at[7 * t + k], device_id=to, device_id_type=MESH)

        mine = [pltpu.make_async_copy(ins[t], outs[t].at[4 * x + 2 * y + c], local.at[t]) for t in range(n)]
        for cp in mine:
            cp.start()
        first = []
        for t in range(n):
            first.append(copy(t, 0, me, sibling, src=ins[t]))
            first += [copy(t, 1 + j, me, (*chip, c), src=ins[t]) for j, chip in enumerate(chips)]
        for cp in first:
            cp.start()
        passed = []
        for j, chip in enumerate(chips):
            for t in range(n):
                copy(t, 1 + j, (*chip, c), me).wait_recv()
                fwd = copy(t, 4 + j, (*chip, c), sibling)
                fwd.start()
                passed.append(fwd)
        for t in range(n):
            copy(t, 0, sibling, me).wait_recv()
            for j, chip in enumerate(chips):
                copy(t, 4 + j, (*chip, 1 - c), me).wait_recv()
        for cp in first + passed:
            cp.wait_send()
        for cp in mine:
            cp.wait()

    return pl.pallas_call(
        body, name=name, in_specs=[ANY] * n, out_specs=[ANY] * n,
        out_shape=[jax.ShapeDtypeStruct((N_DEV,) + b.shape, b.dtype) for b in blocks],
        scratch_shapes=[pltpu.SemaphoreType.DMA((7 * n,)), pltpu.SemaphoreType.DMA((7 * n,)),
                        pltpu.SemaphoreType.DMA((n,))],
    )(*blocks)


def _all_gather_relayed(block, *, name):
    r = block.shape[0]
    half = r // 2
    assert half * 2 == r and half % 16 == 0, block.shape

    def body(in_ref, out_ref, send, recv, local):
        x, y, c = _mesh_pos()
        me, sibling, xn, yn, dg = (x, y, c), (x, y, 1 - c), (1 - x, y, c), (x, 1 - y, c), (1 - x, 1 - y, c)
        slot = lambda p: 4 * p[0] + 2 * p[1] + p[2]
        rows = {"a": pl.ds(0, half), "b": pl.ds(half, half)}

        def copy(k, src, dst, to):
            return pltpu.make_async_remote_copy(src_ref=src, dst_ref=dst, send_sem=send.at[k], recv_sem=recv.at[k],
                                                device_id=to, device_id_type=MESH)

        def part(p, h=None):
            ref = out_ref.at[slot(p)]
            return ref if h is None else ref.at[rows[h]]

        def landed(k, p, h=None):
            copy(k, part(p, h), part(p, h), me).wait_recv()

        mine = pltpu.make_async_copy(in_ref, part(me), local)
        mine.start()
        first = [copy(0, in_ref, part(me), sibling),
                 copy(1, in_ref.at[rows["a"]], part(me, "a"), xn), copy(2, in_ref.at[rows["b"]], part(me, "b"), xn),
                 copy(3, in_ref.at[rows["a"]], part(me, "a"), yn), copy(4, in_ref.at[rows["b"]], part(me, "b"), yn)]
        for cp in first:
            cp.start()
        landed(1, xn, "a")
        relay_a = copy(5, part(xn, "a"), part(xn, "a"), yn)
        relay_a.start()
        landed(4, yn, "b")
        relay_b = copy(6, part(yn, "b"), part(yn, "b"), xn)
        relay_b.start()
        landed(2, xn, "b")
        pass_x = copy(7, part(xn), part(xn), sibling)
        pass_x.start()
        landed(3, yn, "a")
        pass_y = copy(8, part(yn), part(yn), sibling)
        pass_y.start()
        landed(5, dg, "a")
        landed(6, dg, "b")
        pass_d = copy(9, part(dg), part(dg), sibling)
        pass_d.start()
        landed(0, sibling)
        for k, p in ((7, (1 - x, y, 1 - c)), (8, (x, 1 - y, 1 - c)), (9, (1 - x, 1 - y, 1 - c))):
            landed(k, p)
        for cp in first + [relay_a, relay_b, pass_x, pass_y, pass_d]:
            cp.wait_send()
        mine.wait()

    return pl.pallas_call(
        body, name=name, in_specs=[ANY], out_specs=ANY,
        out_shape=jax.ShapeDtypeStruct((N_DEV,) + block.shape, block.dtype),
        scratch_shapes=[pltpu.SemaphoreType.DMA((10,)), pltpu.SemaphoreType.DMA((10,)), pltpu.SemaphoreType.DMA],
    )(block)


def _comm_call(comm, *, name):
    ci, co = len(comm.ins), len(comm.out_shapes)

    def body(*refs):
        comm.start(refs[:ci], refs[ci:ci + co], refs[ci + co:])
        comm.finish(refs[:ci], refs[ci:ci + co], refs[ci + co:])

    return pl.pallas_call(body, name=name, in_specs=[ANY] * ci, out_specs=[ANY] * co, out_shape=comm.out_shapes,
                          scratch_shapes=comm.sems, input_output_aliases=comm.aliases)(*comm.ins)


def _ag_first_comm(shards, rows=None, into=None):
    n = len(shards)
    rows = rows or [None] * n
    into = into or [None] * n
    carried = [t for t in range(n) if into[t] is not None]

    def copies(cin, cout, sems):
        send, recv, local = sems
        x, y, c = _mesh_pos()
        peers = [(x, y, 1 - c), (1 - x, y, c), (x, 1 - y, c), (1 - x, 1 - y, c)]
        slot = lambda p: 4 * p[0] + 2 * p[1] + p[2]
        mine, out, inc = [], [], []
        for t in range(n):
            part = (lambda ref: ref) if rows[t] is None else (lambda ref, r=rows[t]: ref.at[pl.ds(r[0], r[1])])
            own = part(cout[t].at[slot((x, y, c))])
            mine.append(pltpu.make_async_copy(part(cin[t]), own, local.at[t]))
            for k, peer in enumerate(peers):
                sems_k = dict(send_sem=send.at[4 * t + k], recv_sem=recv.at[4 * t + k], device_id=peer,
                              device_id_type=MESH)
                theirs = part(cout[t].at[slot(peer)])
                out.append(pltpu.make_async_remote_copy(src_ref=part(cin[t]), dst_ref=own, **sems_k))
                inc.append(pltpu.make_async_remote_copy(src_ref=theirs, dst_ref=theirs, **sems_k))
        return mine, out, inc

    def start(cin, cout, sems):
        mine, out, _ = copies(cin, cout, sems)
        for cp in mine + out:
            cp.start()

    def finish(cin, cout, sems):
        mine, out, inc = copies(cin, cout, sems)
        for cp in inc:
            cp.wait_recv()
        for cp in out:
            cp.wait_send()
        for cp in mine:
            cp.wait()

    return _Comm(list(shards) + [into[t] for t in carried],
                 [jax.ShapeDtypeStruct((N_DEV,) + s.shape, s.dtype) for s in shards],
                 [pltpu.SemaphoreType.DMA((4 * n,)), pltpu.SemaphoreType.DMA((4 * n,)), pltpu.SemaphoreType.DMA((n,))],
                 start, finish, aliases={n + i: t for i, t in enumerate(carried)})


def _ag_pass_comm(gathered):
    n = len(gathered)

    def copies(cout, sems):
        send, recv = sems
        x, y, c = _mesh_pos()
        fwd, inc = [], []
        for t in range(n):
            for j, (px, py) in enumerate([(1 - x, y), (x, 1 - y), (1 - x, 1 - y)]):
                sems_j = dict(send_sem=send.at[3 * t + j], recv_sem=recv.at[3 * t + j], device_id=(x, y, 1 - c),
                              device_id_type=MESH)
                mine, theirs = cout[t].at[4 * px + 2 * py + c], cout[t].at[4 * px + 2 * py + 1 - c]
                fwd.append(pltpu.make_async_remote_copy(src_ref=mine, dst_ref=mine, **sems_j))
                inc.append(pltpu.make_async_remote_copy(src_ref=theirs, dst_ref=theirs, **sems_j))
        return fwd, inc

    def start(cin, cout, sems):
        for cp in copies(cout, sems)[0]:
            cp.start()

    def finish(cin, cout, sems):
        fwd, inc = copies(cout, sems)
        for cp in inc:
            cp.wait_recv()
        for cp in fwd:
            cp.wait_send()

    return _Comm(gathered, [jax.ShapeDtypeStruct(g.shape, g.dtype) for g in gathered],
                 [pltpu.SemaphoreType.DMA((3 * n,)), pltpu.SemaphoreType.DMA((3 * n,))], start, finish,
                 aliases={t: t for t in range(n)})


def _rs_sibling_comm(grads):
    n = len(grads)

    def copies(cin, cout, sems):
        send, recv = sems
        x, y, c = _mesh_pos()
        return [pltpu.make_async_remote_copy(
            src_ref=cin[t].at[2 * q + (1 - c)], dst_ref=cout[t].at[q], send_sem=send.at[4 * t + q],
            recv_sem=recv.at[4 * t + q], device_id=(x, y, 1 - c), device_id_type=MESH)
            for t in range(n) for q in range(4)]

    def start(cin, cout, sems):
        for cp in copies(cin, cout, sems):
            cp.start()

    def finish(cin, cout, sems):
        cps = copies(cin, cout, sems)
        for cp in cps:
            cp.wait_recv()
        for cp in cps:
            cp.wait_send()

    return _Comm(grads, [jax.ShapeDtypeStruct((4,) + g.shape[1:], g.dtype) for g in grads],
                 [pltpu.SemaphoreType.DMA((4 * n,)), pltpu.SemaphoreType.DMA((4 * n,))], start, finish)


def _join_comms(comms):
    ins, outs, sems, aliases, spans = [], [], [], {}, []
    for cm in comms:
        spans.append((len(ins), len(cm.ins), len(outs), len(cm.out_shapes), len(sems), len(cm.sems)))
        aliases.update({len(ins) + i: len(outs) + o for i, o in cm.aliases.items()})
        ins, outs, sems = ins + cm.ins, outs + cm.out_shapes, sems + cm.sems

    def run(which):
        def fn(cin, cout, csem):
            for cm, (i0, ni, o0, no, s0, ns) in zip(comms, spans):
                getattr(cm, which)(cin[i0:i0 + ni], cout[o0:o0 + no], csem[s0:s0 + ns])
        return fn

    return _Comm(ins, outs, sems, run("start"), run("finish"), aliases)


def _rs_chips_comm(parts):
    n = len(parts)

    def copies(cin, cout, sems):
        send, recv, local = sems
        x, y, c = _mesh_pos()
        my_chip = 2 * x + y
        mine = [pltpu.make_async_copy(cin[t].at[my_chip], cout[t].at[my_chip], local.at[t]) for t in range(n)]
        sends, lands = [], []
        for t in range(n):
            for k, (px, py) in enumerate([(1 - x, y), (x, 1 - y), (1 - x, 1 - y)]):
                sems_k = dict(send_sem=send.at[3 * t + k], recv_sem=recv.at[3 * t + k], device_id=(px, py, c),
                              device_id_type=MESH)
                sends.append(pltpu.make_async_remote_copy(src_ref=cin[t].at[2 * px + py], dst_ref=cout[t].at[my_chip],
                                                          **sems_k))
                lands.append(pltpu.make_async_remote_copy(src_ref=cout[t].at[2 * px + py],
                                                          dst_ref=cout[t].at[2 * px + py], **sems_k))
        return mine, sends, lands

    def start(cin, cout, sems):
        mine, sends, _ = copies(cin, cout, sems)
        for cp in mine + sends:
            cp.start()

    def finish(cin, cout, sems):
        mine, sends, lands = copies(cin, cout, sems)
        for cp in lands:
            cp.wait_recv()
        for cp in sends:
            cp.wait_send()
        for cp in mine:
            cp.wait()

    return _Comm(parts, [jax.ShapeDtypeStruct(p.shape, p.dtype) for p in parts],
                 [pltpu.SemaphoreType.DMA((3 * n,)), pltpu.SemaphoreType.DMA((3 * n,)), pltpu.SemaphoreType.DMA((n,))],
                 start, finish)


def _row_tile(r, c, itemsize, budget=3 * 1024 * 1024):
    best = None
    for tr in range(16, r + 1, 16):
        if r % tr == 0 and tr * c * itemsize <= budget:
            best = tr
    return best or r


def _pair_sum(grad, land, *, name):
    _, r, c = grad.shape
    tr = _row_tile(r, c, 2)

    def body(g_ref, l_ref, o_ref):
        o_ref[...] = (g_ref[...].astype(F32) + l_ref[...].astype(F32)).astype(o_ref.dtype)

    return pl.pallas_call(
        body, name=name, grid=(4, r // tr),
        in_specs=[pl.BlockSpec((1, tr, c), lambda q, i: (2 * q + lax.axis_index("c"), i, 0)),
                  pl.BlockSpec((1, tr, c), lambda q, i: (q, i, 0))],
        out_specs=pl.BlockSpec((1, tr, c), lambda q, i: (q, i, 0)),
        out_shape=jax.ShapeDtypeStruct((4, r, c), grad.dtype),
        compiler_params=_cparams(("parallel", "parallel")),
    )(grad, land)


def _adamw_math(w, g, m, v):
    m = ADAM_B1 * m + (1.0 - ADAM_B1) * g
    v = ADAM_B2 * v + (1.0 - ADAM_B2) * jnp.square(g)
    m_hat = m / (1.0 - ADAM_B1 ** ADAM_STEP)
    v_hat = v / (1.0 - ADAM_B2 ** ADAM_STEP)
    delta = -ADAM_LR * (m_hat / (jnp.sqrt(v_hat) + ADAM_EPS) + ADAM_WD * w)
    return delta, m, v


def _adamw(parts, w, m, v, *, name):
    s, _, cp = parts.shape
    r, c = w.shape
    tr = _row_tile(r, cp, 4, budget=1024 * 1024)

    def body(p_ref, w_ref, m_ref, v_ref, g_ref, d_ref, nm_ref, nv_ref):
        g = p_ref[0].astype(F32)
        for i in range(1, s):
            g = g + p_ref[i].astype(F32)
        g = g[:, :c]
        delta, nm, nv = _adamw_math(w_ref[...], g, m_ref[...], v_ref[...])
        g_ref[...] = g
        d_ref[...] = delta
        nm_ref[...] = nm
        nv_ref[...] = nv

    blk = pl.BlockSpec((tr, c), lambda i: (i, 0))
    return pl.pallas_call(
        body, name=name, grid=(r // tr,),
        in_specs=[pl.BlockSpec((s, tr, cp), lambda i: (0, i, 0)), blk, blk, blk],
        out_specs=[blk] * 4, out_shape=[jax.ShapeDtypeStruct((r, c), F32)] * 4,
        compiler_params=_cparams(("parallel",)),
    )(parts, w, m, v)


def _w_in_pieces(d, nb, sources):
    segs = [(0, 4 * HW, False, 0), (4 * HW, 4 * HW + 2 * HEADS, True, 0),
            (4 * HW + 2 * HEADS, 7 * HW + 2 * HEADS, False, 4 * HW),
            (7 * HW + 2 * HEADS, 7 * HW + 3 * HEADS, True, 2 * HEADS),
            (7 * HW + 3 * HEADS, 7 * HW + 3 * HEADS + 2 * d, False, 7 * HW)]
    out = []
    for dev in range(N_DEV):
        lo, hi = dev * nb, (dev + 1) * nb
        for s0, s1, is_small, a0 in segs:
            p, q = max(lo, s0), min(hi, s1)
            if p >= q:
                continue
            a, b = a0 + p - s0, a0 + q - s0
            if is_small:
                out.append((dev, p - lo, q - lo, len(sources), a, b))
                continue
            for si, (start, width) in enumerate(sources):
                u, v = max(a, start), min(b, start + width)
                if u < v:
                    out.append((dev, p - lo + (u - a), p - lo + (v - a), si, u - start, v - start))
    return out


def _concat_cols(parts, *, name):
    t = parts[0].shape[0]
    n = len(parts)
    offs = [sum(p.shape[1] for p in parts[:i]) for i in range(n)]
    tm = min(128, t)

    def body(*refs):
        for i in range(n):
            refs[n][:, offs[i]:offs[i] + parts[i].shape[1]] = refs[i][...]

    return pl.pallas_call(
        body, name=name, grid=(t // tm,),
        in_specs=[pl.BlockSpec((tm, p.shape[1]), lambda i: (i, 0)) for p in parts],
        out_specs=pl.BlockSpec((tm, offs[-1] + parts[-1].shape[1]), lambda i: (i, 0)),
        out_shape=jax.ShapeDtypeStruct((t, offs[-1] + parts[-1].shape[1]), parts[0].dtype),
        compiler_params=_cparams(("parallel",)))(*parts)


def _w_in_to_aligned(g_in, *, name):
    _, d, nb = g_in.shape
    n_main = 7 * HW + 2 * d
    tr = min(128, d)
    pieces = _w_in_pieces(d, nb, [(0, n_main)])

    def body(g_ref, main_ref, small_ref):
        small_ref[...] = jnp.zeros_like(small_ref)
        for dev, s, e, src, a, b in pieces:
            dst = main_ref if src == 0 else small_ref
            dst[:, a:b] = g_ref[dev, :, s:e]

    return pl.pallas_call(
        body, name=name, grid=(d // tr,), in_specs=[pl.BlockSpec((N_DEV, tr, nb), lambda i: (0, i, 0))],
        out_specs=[pl.BlockSpec((tr, n_main), lambda i: (i, 0)), pl.BlockSpec((tr, LANES), lambda i: (i, 0))],
        out_shape=[jax.ShapeDtypeStruct((d, n_main), g_in.dtype), jax.ShapeDtypeStruct((d, LANES), g_in.dtype)],
        compiler_params=_cparams(("parallel",)),
    )(g_in)


def _w_in_grad_blocks(seg_grads, small_grad, sources, nb, *, name):
    d = small_grad.shape[0]
    tr = min(128, d)
    pieces = _w_in_pieces(d, nb, sources)
    ns = len(seg_grads)

    def body(*refs):
        o_ref = refs[ns + 1]
        for dev, s, e, src, a, b in pieces:
            o_ref[dev, :, s:e] = refs[src][:, a:b]

    return pl.pallas_call(
        body, name=name, grid=(d // tr,),
        in_specs=[pl.BlockSpec((tr, g.shape[1]), lambda i: (i, 0)) for g in seg_grads + [small_grad]],
        out_specs=pl.BlockSpec((N_DEV, tr, nb), lambda i: (0, i, 0)),
        out_shape=jax.ShapeDtypeStruct((N_DEV, d, nb), small_grad.dtype),
        compiler_params=_cparams(("parallel",)),
    )(*seg_grads, small_grad)


def _pad_cols(a, n):
    return a if a.shape[1] == n else jnp.concatenate([a, jnp.zeros((a.shape[0], n - a.shape[1]), a.dtype)], axis=1)


def _pad_rows(a, n):
    return a if a.shape[0] == n else jnp.concatenate([a, jnp.zeros((n - a.shape[0], a.shape[1]), a.dtype)], axis=0)


class _StaticPlan:
    def __init__(self, weights, cp):
        self.w, self.cp, self.grads = weights, cp, {}

    def comm_for(self, key):
        return None

    def done(self, key, res):
        pass

    def weight(self, name):
        return self.w[name]

    def grad(self, name, g):
        self.grads[name] = g

    def grad_w_in(self, g_main, g_small):
        self.grads["w_main"], self.grads["w_small"] = g_main, g_small


class _FsdpPlan:
    RIDES = {
        "in_proj": (("gather", (("conv", None), ("wa", None), ("wb", None), ("wout", None), ("wg", 0), ("wd", 1))),),
        "gdn_fwd": (("gather", (("wg", 1),)), ("pass", ("wa", "wb", "wout"))),
        "fox_fwd": (("gather", (("wu", 0),)), ("pass", ("wg",))),
        "ffn_gate": (("gather", (("wu", 1),)),),
        "ffn_up": (("gather", (("wd", 0),)),),
        "dw_ffn_gate": (("sibling", ("wd",)),),
        "d_hn_gate": (("chips", ("wd",)), ("sibling", ("wg",))),
        "d_hn_up": (("chips", ("wg",)),),
        "d_merged": (("sibling", ("wu",)),),
        "d_oa": (("sibling", ("wout",)),),
        "d_ob": (("sibling", ("wa",)),),
        "gdn_bwd": (("chips", ("wu", "wout")), ("sibling", ("wb",))),
        "fox_bwd": (("chips", ("wa", "wb")),),
        "d_xn": (("chips", ("w_in", "conv")),),
    }
    PASS_GROUPS = (("conv",), ("wa", "wb", "wout"), ("wg",), ("wu",), ("wd",))

    def __init__(self, shards, d, cp, nb):
        self.shards, self.d, self.cp, self.nb = shards, d, cp, nb
        self.first, self.full = {}, {}
        self.blocks, self.queue, self.slots = {}, {}, {}
        self.flying = []

    def comm_for(self, key):
        comms, self.flying = [], []
        for kind, items in self.RIDES.get(key, ()):
            if kind == "gather":
                names = [n for n, _ in items]
                rows = [None if half is None else (half * (self.shards[n].shape[0] // 2), self.shards[n].shape[0] // 2)
                        for n, half in items]
                comm = _ag_first_comm([self.shards[n] for n in names], rows, [self.first.get(n) for n in names])
            elif kind == "pass":
                names = list(items)
                comm = _ag_pass_comm([self.first[n] for n in names])
            elif kind == "sibling":
                names = [n for n in items if n in self.blocks]
                comm = _rs_sibling_comm([self.blocks[n] for n in names]) if names else None
            else:
                for n in items:
                    if n in self.blocks:
                        self.sibling_now(n)
                names = [n for n in items if n in self.queue]
                comm = _rs_chips_comm([self.queue.pop(n) for n in names]) if names else None
            if comm is not None:
                comms.append(comm)
                self.flying.append((kind, names, len(comm.out_shapes)))
        return _join_comms(comms) if comms else None

    def done(self, key, res):
        res = list(res)
        for kind, names, n_out in self.flying:
            outs, res = res[:n_out], res[n_out:]
            if kind == "gather":
                self.first.update(zip(names, outs))
            elif kind == "pass":
                self.full.update(zip(names, outs))
            elif kind == "sibling":
                for n, land in zip(names, outs):
                    self.queue[n] = _pair_sum(self.blocks.pop(n), land, name=f"pair_sum_{n}")
            else:
                self.slots.update(zip(names, outs))
        self.flying = []

    def weight(self, name):
        if name not in self.full:
            group = next(g for g in self.PASS_GROUPS if name in g)
            outs = _comm_call(_ag_pass_comm([self.first[n] for n in group]), name=f"all_gather_pass_{group[0]}")
            self.full.update(zip(group, outs))
        g = self.full[name]
        if name in ("wa", "wb", "conv"):
            return _cols_of_blocks(g)
        if name == "wout":
            return g.reshape(self.d, self.d)
        if name == "wd":
            return g.reshape(N_DEV * self.cp, self.d)
        return g

    def grad(self, name, g):
        if name == "wout":
            g = g.reshape(N_DEV, self.d // N_DEV, self.d)
        if name == "wd":
            g = g.reshape(N_DEV, self.cp, self.d)
        if name == "conv":
            g = _blocks_of_cols(g.astype(BF16))
        self.blocks[name] = g

    def grad_w_in(self, g_main, g_small):
        self.blocks["w_in"] = _w_in_grad_blocks([g_main], g_small, [(0, g_main.shape[1])], self.nb,
                                                name="w_in_grad_blocks")

    def sibling_now(self, name):
        blocks = self.blocks.pop(name)
        (land,) = _comm_call(_rs_sibling_comm([blocks]), name=f"grads_to_sibling_{name}")
        self.queue[name] = _pair_sum(blocks, land, name=f"pair_sum_{name}")

    def flush(self):
        for name in list(self.blocks):
            self.sibling_now(name)
        if self.queue:
            outs = _comm_call(_rs_chips_comm(list(self.queue.values())), name="grads_to_chips_tail")
            self.slots.update(zip(self.queue, outs))
            self.queue = {}


def _carried(plan, key, fn, *args, **kw):
    comm = plan.comm_for(key)
    if comm is None:
        return fn(*args, **kw)
    res, comm_res = fn(*args, comm=comm, **kw)
    plan.done(key, comm_res)
    return res


def _local_step(x, target, w_main, w_small, plan,
                norm_mix_w, norm_ffn_w, gdn_norm_w, fox_q_w, fox_k_w, a_row, b_row):
    t, d = x.shape
    cp = plan.cp
    fp = N_DEV * cp
    n_main = w_main.shape[1]
    off_gb = OFF_GA + d
    tm = 1024
    rt = 128

    (xn,) = _rowwise_fwd(_fn_norm, [(x, 0, d)], [norm_mix_w], [(d, BF16)], tm=rt, name="mix_norm")
    p_main = _carried(plan, "in_proj", _mm, xn, w_main, mode="nn", m=t, n=n_main, k=d, tm=tm, tn=512, tk=d,
                      out_dtype=BF16, name="in_proj")
    p_small = _mm(xn, w_small, mode="nn", m=t, n=LANES, k=d, tm=tm, tn=LANES, tk=d, out_dtype=F32, name="in_proj_small")
    (gates,) = _rowwise_fwd(_fn_gates, [(p_small, 0, LANES)], [a_row, b_row], [(LANES, F32)], tm=512, name="gates")
    conv_w = plan.weight("conv")
    qkv = _conv_fwd(p_main, conv_w, width=3 * HW, name="conv_fwd")
    o_gdn, s_all = _carried(plan, "gdn_fwd", _gdn_fwd, qkv, gates, name="gdn_fwd")
    gdn_rows = [(o_gdn, 0, HW), (p_main, OFF_ZA, HW)]
    (oa,) = _rowwise_fwd(_fn_gdn_out, gdn_rows, [gdn_norm_w], [(HW, BF16)], tm=512, inner=HEADS, name="gdn_out")
    wa = plan.weight("wa")
    ya = _mm(oa, wa, mode="nn", m=t, n=d, k=HW, tm=tm, tn=1024, tk=HW, out_dtype=BF16, name="branch_a")
    qk_rows = [(p_main, OFF_QB, HW), (p_main, OFF_KB, HW)]
    qn, kn = _rowwise_fwd(_fn_qknorm, qk_rows, [fox_q_w, fox_k_w], [(HW, BF16), (HW, BF16)], tm=512, inner=HEADS,
                          name="fox_qk_norm")
    blk = min(FOX_BLK, t)
    c4 = _fox_cumsum(gates, name="fox_cumsum").reshape(HEADS, t // blk, 1, blk)
    ob, ob32, lse = _carried(plan, "fox_fwd", _fox_fwd, qn, kn, p_main, c4, v_off=OFF_VB, name="fox_fwd")
    wb = plan.weight("wb")
    yb = _mm(ob, wb, mode="nn", m=t, n=d, k=HW, tm=tm, tn=1024, tk=HW, out_dtype=BF16, name="branch_b")
    mcol = 2 if d >= 2 * HW else 1
    merge_rows = [(p_main, OFF_GA, d), (p_main, off_gb, d), (ya, 0, d), (yb, 0, d)]
    (merged,) = _rowwise_fwd(_fn_merge, merge_rows, [], [(d, BF16)], tm=256, ncol=mcol, name="merge")
    wout = plan.weight("wout")
    h = _mm(merged, wout, mode="nn", m=t, n=d, k=d, tm=tm, tn=512, tk=d, out_dtype=F32, add=x, name="out_proj")
    (hn,) = _rowwise_fwd(_fn_norm, [(h, 0, d)], [norm_ffn_w], [(d, BF16)], tm=rt, name="ffn_norm")
    wg = plan.weight("wg")
    gate = _carried(plan, "ffn_gate", _mm, hn, wg, mode="nn", m=t, n=fp, k=d, tm=512, tn=cp, tk=d, out_dtype=BF16,
                    b_blocked=True, name="ffn_gate")
    wu = plan.weight("wu")
    up, act = _carried(plan, "ffn_up", _mm, hn, wu, mode="nn", m=t, n=fp, k=d, tm=512, tn=cp, tk=d, out_dtype=BF16,
                       b_blocked=True, name="ffn_up", tiles=[(gate, 0)], out_dtypes=[BF16, BF16],
                       epilogue=lambda p, g: (p, _fn_swiglu(g, _bf16_round(p))[0]))
    wd = plan.weight("wd")
    y = _mm(act, wd, mode="nn", m=t, n=d, k=fp, tm=512, tn=256, tk=fp, out_dtype=F32, add=h, name="ffn_down")
    dy, dyb, loss_row = _loss_head(y, target, tm=rt, name="loss_head")

    dgate, dup = _mm(dyb, wd, mode="nt", m=t, n=fp, k=d, tm=tm, tn=512, tk=d, out_dtype=BF16, name="d_act",
                     tiles=[(gate, 0), (up, 0)], out_dtypes=[BF16, BF16],
                     epilogue=lambda p, g, u: jax.vjp(_fn_swiglu, g, u)[1]((_bf16_round(p),)))
    plan.grad("wd", _mm(act, dyb, mode="tn", m=fp, n=d, k=t, tm=512, tn=1024, tk=t, out_dtype=BF16, name="dw_ffn_down"))
    plan.grad("wg", _carried(plan, "dw_ffn_gate", _mm, hn, dgate, mode="tn", m=d, n=fp, k=t, tm=512, tn=cp, tk=t,
                             out_dtype=BF16, out_blocked=True, name="dw_ffn_gate"))
    dhn = _carried(plan, "d_hn_gate", _mm, dgate, wg, mode="nt", m=t, n=d, k=fp, tm=512, tn=256, tk=fp, out_dtype=F32,
                   b_blocked=True, name="d_hn_gate")
    dhn = _carried(plan, "d_hn_up", _mm, dup, wu, mode="nt", m=t, n=d, k=fp, tm=512, tn=256, tk=fp, out_dtype=F32,
                   add=dhn, b_blocked=True, name="d_hn_up")
    plan.grad("wu", _mm(hn, dup, mode="tn", m=d, n=fp, k=t, tm=512, tn=cp, tk=t, out_dtype=BF16, out_blocked=True,
                        name="dw_ffn_up"))
    dh, d_norm_ffn, dhb = _rowwise_bwd(_fn_norm, [(h, 0, d)], [norm_ffn_w], [dhn], [F32], tm=rt, name="d_ffn_norm",
                                       adds=[dy], bf16_copy_of=0)
    dga, dgb, dya, dyb2 = _carried(
        plan, "d_merged", _mm, dhb, wout, mode="nt", m=t, n=d, k=d, tm=512, tn=512, tk=d, out_dtype=BF16, name="d_merged",
        tiles=[(p_main, OFF_GA), (p_main, off_gb), (ya, 0), (yb, 0)], out_dtypes=[BF16] * 4,
        epilogue=lambda p, *gy: jax.vjp(_fn_merge, *gy)[1]((_bf16_round(p),)))
    plan.grad("wout", _mm(merged, dhb, mode="tn", m=d, n=d, k=t, tm=512, tn=512, tk=t, out_dtype=BF16, name="dw_out"))
    doa = _carried(plan, "d_oa", _mm, dya, wa, mode="nt", m=t, n=HW, k=d, tm=tm, tn=512, tk=d, out_dtype=BF16, name="d_oa")
    plan.grad("wa", _mm(oa, dya, mode="tn", m=HW, n=d, k=t, tm=1024, tn=d // N_DEV, tk=t, out_dtype=BF16,
                        out_blocked=True, name="dw_branch_a"))
    dob = _carried(plan, "d_ob", _mm, dyb2, wb, mode="nt", m=t, n=HW, k=d, tm=tm, tn=512, tk=d, out_dtype=BF16, name="d_ob")
    plan.grad("wb", _mm(ob, dyb2, mode="tn", m=HW, n=d, k=t, tm=1024, tn=d // N_DEV, tk=t, out_dtype=BF16,
                        out_blocked=True, name="dw_branch_b"))
    do_gdn, dza, d_gdn_norm = _rowwise_bwd(_fn_gdn_out, gdn_rows, [gdn_norm_w], [doa], [F32, BF16], tm=256,
                                           inner=HEADS, name="d_gdn_out")
    dqkv, dgates_gdn = _carried(plan, "gdn_bwd", _gdn_bwd, qkv, gates, s_all, do_gdn, name="gdn_bwd")
    dp_qkv, dconv = _conv_bwd(p_main, conv_w, dqkv, width=3 * HW, name="conv_bwd")
    plan.grad("conv", dconv)
    dqn, dkn, dvb, dc4, dcq = _carried(plan, "fox_bwd", _fox_bwd, qn, kn, p_main, c4, ob32, dob, lse, v_off=OFF_VB,
                                       name="fox_bwd")
    dqb, dkb, d_fox_q, d_fox_k = _rowwise_bwd(_fn_qknorm, qk_rows, [fox_q_w, fox_k_w], [dqn, dkn], [BF16, BF16],
                                              tm=256, inner=HEADS, name="d_fox_qk_norm")
    dgates = _fox_cumsum_bwd(dc4.reshape(HEADS, t) + dcq.reshape(HEADS, t), dgates_gdn, name="fox_cumsum_bwd")
    dsmall, d_a_row, d_b_row = _rowwise_bwd(_fn_gates, [(p_small, 0, LANES)], [a_row, b_row], [dgates], [F32],
                                            tm=512, name="d_gates")
    dp_main = _concat_cols([dp_qkv, dza, dqb, dkb, dvb, dga, dgb], name="d_p_main")
    plan.grad_w_in(_mm(xn, dp_main, mode="tn", m=d, n=n_main, k=t, tm=1024, tn=math.gcd(n_main, 1024), tk=t,
                       out_dtype=BF16, name="dw_in"),
                   _mm(xn, dsmall, mode="tn", m=d, n=LANES, k=t, tm=1024, tn=LANES, tk=t, out_dtype=BF16,
                       name="dw_in_small"))
    dxn = _mm(dsmall, w_small, mode="nt", m=t, n=d, k=LANES, tm=tm, tn=1024, tk=LANES, out_dtype=F32, name="d_xn_small")
    dxn = _carried(plan, "d_xn", _mm, dp_main, w_main, mode="nt", m=t, n=d, k=n_main, tm=tm, tn=1024,
                   tk=math.gcd(n_main, 2048),
                   out_dtype=F32, add=dxn, name="d_xn")
    grad_x, d_norm_mix = _rowwise_bwd(_fn_norm, [(x, 0, d)], [norm_mix_w], [dxn], [F32], tm=rt, name="d_mix_norm",
                                      adds=[dh])
    small = dict(norm_mix=d_norm_mix, norm_ffn=d_norm_ffn, gdn_norm=d_gdn_norm, fox_q=d_fox_q, fox_k=d_fox_k,
                 a_row=d_a_row, b_row=d_b_row)
    return loss_row[0, 0], grad_x, small


def _lane_row(pieces):
    row = jnp.zeros((1, LANES), F32)
    for off, p in pieces:
        row = lax.dynamic_update_slice(row, p.astype(F32), (0, off))
    return row


def _pack_small(norm_mix, norm_ffn, gdn_norm, fox_q, fox_k, a_log, dt_bias, b_f):
    rows = [norm_mix.reshape(-1, LANES), norm_ffn.reshape(-1, LANES), gdn_norm, fox_q, fox_k,
            _lane_row([(HEADS, a_log)]), _lane_row([(HEADS, dt_bias), (2 * HEADS, b_f)])]
    packed = jnp.concatenate(rows, axis=0)
    return _pad_rows(packed, -(-packed.shape[0] // 8) * 8)


def _unpack_small(p, d):
    nd = d // LANES
    r = 2 * nd
    return (p[0:nd].reshape(1, d), p[r + 3:r + 4, HEADS:2 * HEADS], p[r + 4:r + 5, HEADS:2 * HEADS], p[r:r + 1],
            p[r + 4:r + 5, 2 * HEADS:3 * HEADS], p[r + 1:r + 2], p[r + 2:r + 3], p[nd:r].reshape(1, d))


def _blocks_of_cols(a):
    r, c8 = a.shape
    return a.reshape(r, N_DEV, c8 // N_DEV).transpose(1, 0, 2)


def _cols_of_blocks(g):
    _, r, c = g.shape
    return g.transpose(1, 0, 2).reshape(r, N_DEV * c)


def kernel(x, norm_mix_w, w_in, conv_w, a_log, dt_bias, gdn_norm_w, fox_b_f, fox_q_norm_w, fox_k_norm_w, w_branch_a, w_branch_b, w_out, norm_ffn_w, w_ffn_gate, w_ffn_up, w_ffn_down, loss_target, m_norm_mix_w, m_w_in, m_conv_w, m_a_log, m_dt_bias, m_gdn_norm_w, m_fox_b_f, m_fox_q_norm_w, m_fox_k_norm_w, m_w_branch_a, m_w_branch_b, m_w_out, m_norm_ffn_w, m_w_ffn_gate, m_w_ffn_up, m_w_ffn_down, v_norm_mix_w, v_w_in, v_conv_w, v_a_log, v_dt_bias, v_gdn_norm_w, v_fox_b_f, v_fox_q_norm_w, v_fox_k_norm_w, v_w_branch_a, v_w_branch_b, v_w_out, v_norm_ffn_w, v_w_ffn_gate, v_w_ffn_up, v_w_ffn_down):
    d = x.shape[-1]
    cp = -(-w_ffn_down.shape[1] // LANES) * LANES
    nb = w_in.shape[2]

    g_in = _all_gather_relayed(w_in[0].astype(BF16), name="w_in_all_gather")
    w_main, w_small = _w_in_to_aligned(g_in, name="w_in_to_aligned")
    plan = _FsdpPlan(dict(conv=conv_w[0], wa=w_branch_a[0].astype(BF16), wb=w_branch_b[0].astype(BF16), wout=w_out[0].astype(BF16),
                          wg=_pad_cols(w_ffn_gate[0].astype(BF16), cp), wu=_pad_cols(w_ffn_up[0].astype(BF16), cp),
                          wd=_pad_rows(w_ffn_down[0].astype(BF16), cp)), d, cp, nb)
    a_row = _lane_row([(HEADS, a_log)])
    b_row = _lane_row([(HEADS, dt_bias), (2 * HEADS, fox_b_f)])

    loss_part, grad_x, gs = _local_step(
        x[0], loss_target[0], w_main, w_small, plan,
        norm_mix_w, norm_ffn_w, gdn_norm_w, fox_q_norm_w, fox_k_norm_w, a_row, b_row)
    loss = lax.psum(loss_part, ("x", "y", "c"))

    plan.flush()
    big = dict(w_in=("w_in", w_in, m_w_in, v_w_in), w_branch_a=("wa", w_branch_a, m_w_branch_a, v_w_branch_a),
               w_branch_b=("wb", w_branch_b, m_w_branch_b, v_w_branch_b), w_out=("wout", w_out, m_w_out, v_w_out),
               w_ffn_gate=("wg", w_ffn_gate, m_w_ffn_gate, v_w_ffn_gate), w_ffn_up=("wu", w_ffn_up, m_w_ffn_up, v_w_ffn_up),
               w_ffn_down=("wd", w_ffn_down, m_w_ffn_down, v_w_ffn_down), conv_w=("conv", conv_w, m_conv_w, v_conv_w))
    res = {}
    for nm, (key, w, m, v) in big.items():
        res[nm] = [o[None] for o in _adamw(plan.slots[key], w[0], m[0], v[0], name=f"adamw_{nm}")]

    g_small = _pack_small(gs["norm_mix"], gs["norm_ffn"], gs["gdn_norm"], gs["fox_q"], gs["fox_k"],
                          gs["a_row"][:, HEADS:2 * HEADS], gs["b_row"][:, HEADS:2 * HEADS],
                          gs["b_row"][:, 2 * HEADS:3 * HEADS])
    (g_small_all,) = _all_gather([g_small], name="small_grads_all_gather")
    w_small_p = _pack_small(norm_mix_w, norm_ffn_w, gdn_norm_w, fox_q_norm_w, fox_k_norm_w, a_log, dt_bias, fox_b_f)
    m_small_p = _pack_small(m_norm_mix_w, m_norm_ffn_w, m_gdn_norm_w, m_fox_q_norm_w, m_fox_k_norm_w, m_a_log,
                            m_dt_bias, m_fox_b_f)
    v_small_p = _pack_small(v_norm_mix_w, v_norm_ffn_w, v_gdn_norm_w, v_fox_q_norm_w, v_fox_k_norm_w, v_a_log,
                            v_dt_bias, v_fox_b_f)
    small_res = [_unpack_small(o, d) for o in _adamw(g_small_all, w_small_p, m_small_p, v_small_p, name="adamw_small")]

    def group(k):
        s = small_res[k]
        return [s[0], res["w_in"][k], res["conv_w"][k], s[1], s[2], s[3], s[4], s[5], s[6], res["w_branch_a"][k],
                res["w_branch_b"][k], res["w_out"][k], s[7], res["w_ffn_gate"][k], res["w_ffn_up"][k],
                res["w_ffn_down"][k]]

    return (loss, grad_x[None], *group(0), *group(1), *group(2), *group(3))
```

```python
import functools
import math

import jax
import jax.numpy as jnp
from jax import lax
from jax.experimental import pallas as pl
from jax.experimental.pallas import tpu as pltpu

F32 = jnp.float32
BF16 = jnp.bfloat16
HI = lax.Precision.HIGHEST
SOLVE_PRECISION = lax.Precision.HIGH
MESH = pl.DeviceIdType.MESH

EPS = 1e-6
HEADS = 16
DH = 128
HW = HEADS * DH
CHUNK = 64
CONV_K = 4
N_DEV = 8
LANES = 128
VMEM_LIMIT = 52 * 1024 * 1024

ADAM_LR = 0.001
ADAM_B1 = 0.9
ADAM_B2 = 0.999
ADAM_EPS = 1e-08
ADAM_WD = 0.01
ADAM_STEP = 10

OFF_QA, OFF_KA, OFF_VA, OFF_ZA, OFF_QB, OFF_KB, OFF_VB, OFF_GA = 0, HW, 2 * HW, 3 * HW, 4 * HW, 5 * HW, 6 * HW, 7 * HW


def _cparams(sem=None, vmem=VMEM_LIMIT):
    return pltpu.CompilerParams(dimension_semantics=sem, vmem_limit_bytes=vmem)


class _Comm:
    def __init__(self, ins, out_shapes, sems, start, finish, aliases=None):
        self.ins, self.out_shapes, self.sems = list(ins), list(out_shapes), list(sems)
        self.start, self.finish, self.aliases = start, finish, dict(aliases or {})


def _pcall(body, *, name, grid, in_specs, out_specs, out_shape, args, sem, scratch_shapes=(), comm=None):
    multi = isinstance(out_shape, (list, tuple))
    out_specs = list(out_specs) if multi else [out_specs]
    out_shape = list(out_shape) if multi else [out_shape]
    scratch_shapes = list(scratch_shapes)
    if comm is None:
        res = pl.pallas_call(body, name=name, grid=grid, in_specs=list(in_specs), out_specs=out_specs,
                             out_shape=out_shape, scratch_shapes=scratch_shapes, compiler_params=_cparams(sem))(*args)
        return res if multi else res[0]
    ni, no, ns = len(in_specs), len(out_specs), len(scratch_shapes)
    ci, co = len(comm.ins), len(comm.out_shapes)

    def wrapped(*refs):
        cin = refs[ni:ni + ci]
        outs = refs[ni + ci:ni + ci + no]
        cout = refs[ni + ci + no:ni + ci + no + co]
        scr = refs[ni + ci + no + co:ni + ci + no + co + ns]
        csem = refs[ni + ci + no + co + ns:]
        ids = [pl.program_id(ax) for ax in range(len(grid))]
        first = functools.reduce(jnp.logical_and, [i == 0 for i in ids])
        last = functools.reduce(jnp.logical_and, [i == g - 1 for i, g in zip(ids, grid)])

        @pl.when(first)
        def _():
            comm.start(cin, cout, csem)

        body(*refs[:ni], *outs, *scr)

        @pl.when(last)
        def _():
            comm.finish(cin, cout, csem)

    any_spec = pl.BlockSpec(memory_space=pl.ANY)
    res = pl.pallas_call(
        wrapped, name=name, grid=grid, in_specs=list(in_specs) + [any_spec] * ci,
        out_specs=out_specs + [any_spec] * co, out_shape=out_shape + comm.out_shapes,
        scratch_shapes=scratch_shapes + comm.sems,
        input_output_aliases={ni + i: no + o for i, o in comm.aliases.items()},
        compiler_params=_cparams(("arbitrary",) * len(grid)))(*args, *comm.ins)
    return (res[:no] if multi else res[0]), res[no:]


def _mm(a, b, *, mode, m, n, k, tm, tn, tk, out_dtype, name, a_off=(0, 0), b_off=(0, 0), add=None,
        b_blocked=False, out_blocked=False, comm=None, epilogue=None, tiles=(), out_dtypes=()):
    tm, tn, tk = min(tm, m), min(tn, n), min(tk, k)
    assert m % tm == 0 and n % tn == 0 and k % tk == 0, (name, m, n, k, tm, tn, tk)
    nk = k // tk
    if mode == "nn":
        a_blk, b_blk = (tm, tk), (tk, tn)
        ao, bo = (a_off[0] // tm, a_off[1] // tk), (b_off[0] // tk, b_off[1] // tn)
        a_map = lambda i, j, kk: (i + ao[0], kk + ao[1])
        b_map = lambda i, j, kk: (kk + bo[0], j + bo[1])
        dims = (((1,), (0,)), ((), ()))
        if b_blocked:
            assert b.shape == (n // tn, k, tn) and b_off == (0, 0), (name, b.shape)
            b_blk, b_map = (None, tk, tn), lambda i, j, kk: (j, kk, 0)
    elif mode == "nt":
        a_blk, b_blk = (tm, tk), (tn, tk)
        ao, bo = (a_off[0] // tm, a_off[1] // tk), (b_off[0] // tn, b_off[1] // tk)
        a_map = lambda i, j, kk: (i + ao[0], kk + ao[1])
        b_map = lambda i, j, kk: (j + bo[0], kk + bo[1])
        dims = (((1,), (1,)), ((), ()))
        if b_blocked and tk == k:
            kblocks, cblk = b.shape[0], b.shape[2]
            assert b.shape == (kblocks, n, cblk) and kblocks * cblk == k and b_off == (0, 0), (name, b.shape)
            b_blk, b_map = (kblocks, tn, cblk), lambda i, j, kk: (0, j, 0)
        elif b_blocked:
            assert b.shape == (nk, n, tk) and b_off == (0, 0), (name, b.shape)
            b_blk, b_map = (None, tn, tk), lambda i, j, kk: (kk, j, 0)
    else:
        assert not b_blocked
        a_blk, b_blk = (tk, tm), (tk, tn)
        ao, bo = (a_off[0] // tk, a_off[1] // tm), (b_off[0] // tk, b_off[1] // tn)
        a_map = lambda i, j, kk: (kk + ao[0], i + ao[1])
        b_map = lambda i, j, kk: (kk + bo[0], j + bo[1])
        dims = (((0,), (0,)), ((), ()))
    if not b_blocked:
        for off, blk in ((a_off, a_blk), (b_off, b_blk)):
            assert off[0] % blk[0] == 0 and off[1] % blk[1] == 0, (name, off, blk)
    has_add = add is not None
    n_tiles, n_outs = len(tiles), len(out_dtypes) if epilogue is not None else 1
    assert epilogue is None or (nk == 1 and not out_blocked and not has_add), name

    def body(*refs):
        a_ref, b_ref = refs[:2]
        c_ref = refs[2] if has_add else None
        tile_refs = refs[2 + has_add:2 + has_add + n_tiles]
        out_refs = refs[2 + has_add + n_tiles:2 + has_add + n_tiles + n_outs]
        o_ref, acc = out_refs[0], refs[-1]
        if len(b_blk) == 3 and b_blk[0] is not None:
            cblk = b_blk[2]
            p = sum(lax.dot_general(a_ref[:, q * cblk:(q + 1) * cblk].astype(BF16), b_ref[q].astype(BF16), dims,
                                    preferred_element_type=F32) for q in range(b_blk[0]))
        else:
            p = lax.dot_general(a_ref[...].astype(BF16), b_ref[...].astype(BF16), dims, preferred_element_type=F32)
        if epilogue is not None:
            for ref, val in zip(out_refs, epilogue(p, *[r[...].astype(F32) for r in tile_refs])):
                ref[...] = val.astype(ref.dtype)
        elif nk == 1:
            if has_add:
                p = p + c_ref[...].astype(F32)
            o_ref[...] = p.astype(o_ref.dtype)
        else:
            kk = pl.program_id(2)

            @pl.when(kk == 0)
            def _():
                acc[...] = p + c_ref[...].astype(F32) if has_add else p

            @pl.when(kk > 0)
            def _():
                acc[...] += p

            @pl.when(kk == nk - 1)
            def _():
                o_ref[...] = acc[...].astype(o_ref.dtype)

    in_specs = [pl.BlockSpec(a_blk, a_map), pl.BlockSpec(b_blk, b_map)]
    args = [a, b]
    if has_add:
        in_specs.append(pl.BlockSpec((tm, tn), lambda i, j, kk: (i, j)))
        args.append(add)
    for arr, off in tiles:
        assert off % tn == 0, (name, off, tn)
        in_specs.append(pl.BlockSpec((tm, tn), lambda i, j, kk, ob=off // tn: (i, j + ob)))
        args.append(arr)
    acc_shape = (tm, tn) if nk > 1 else (8, LANES)
    if out_blocked:
        out_spec = pl.BlockSpec((None, tm, tn), lambda i, j, kk: (j, i, 0))
        out_shape = jax.ShapeDtypeStruct((n // tn, m, tn), out_dtype)
    else:
        out_spec = pl.BlockSpec((tm, tn), lambda i, j, kk: (i, j))
        out_shape = jax.ShapeDtypeStruct((m, n), out_dtype)
    if epilogue is not None:
        out_spec = [out_spec] * n_outs
        out_shape = [jax.ShapeDtypeStruct((m, n), dt) for dt in out_dtypes]
    return _pcall(body, name=name, grid=(m // tm, n // tn, nk), in_specs=in_specs, out_specs=out_spec,
                  out_shape=out_shape, scratch_shapes=[pltpu.VMEM(acc_shape, F32)], args=args,
                  sem=("parallel", "parallel", "arbitrary"), comm=comm)


def _row_specs(rows, tm, ncol):
    specs = []
    for arr, off, width in rows:
        bw = width // ncol
        assert width % ncol == 0 and off % bw == 0, (off, width, ncol)
        ob = off // bw
        specs.append(pl.BlockSpec((tm, bw), lambda i, j, ob=ob: (i, j + ob)))
    return specs


def _col_group(ref, s, inner):
    w = ref.shape[1] // inner
    return slice(None), slice(s * w, (s + 1) * w)


def _rowwise_fwd(fn, rows, params, outs, *, tm, ncol=1, inner=1, name):
    t = rows[0][0].shape[0]
    tm = min(tm, t)
    nr, npar = len(rows), len(params)

    def body(*refs):
        par = [r[...].astype(F32) for r in refs[nr:nr + npar]]
        for s in range(inner):
            res = fn(*[r[_col_group(r, s, inner)].astype(F32) for r in refs[:nr]], *par)
            for o_ref, val in zip(refs[nr + npar:], res):
                o_ref[_col_group(o_ref, s, inner)] = val.astype(o_ref.dtype)

    in_specs = _row_specs(rows, tm, ncol) + [pl.BlockSpec(p.shape, lambda i, j: (0, 0)) for p in params]
    out_specs = [pl.BlockSpec((tm, w // ncol), lambda i, j: (i, j)) for w, _ in outs]
    out_shape = [jax.ShapeDtypeStruct((t, w), dt) for w, dt in outs]
    return pl.pallas_call(
        body, name=name, grid=(t // tm, ncol), in_specs=in_specs, out_specs=out_specs, out_shape=out_shape,
        compiler_params=_cparams(("parallel", "parallel")),
    )(*[r[0] for r in rows], *params)


def _rowwise_bwd(fn, rows, params, cts, grad_dtypes, *, tm, ncol=1, inner=1, name, adds=None, bf16_copy_of=None):
    t = rows[0][0].shape[0]
    tm = min(tm, t)
    nr, npar, nct = len(rows), len(params), len(cts)
    adds = adds or [None] * nr
    add_idx = [i for i, a in enumerate(adds) if a is not None]

    def body(*refs):
        par = [r[...].astype(F32) for r in refs[nr:nr + npar]]
        ct_refs = refs[nr + npar:nr + npar + nct]
        add_refs = refs[nr + npar + nct:nr + npar + nct + len(add_idx)]
        outs = refs[nr + npar + nct + len(add_idx):]
        extra = dict(zip(add_idx, add_refs))
        par_grads = [None] * npar
        for s in range(inner):
            ins = [r[_col_group(r, s, inner)].astype(F32) for r in refs[:nr]]
            _, vjp = jax.vjp(lambda *a: tuple(fn(*a)), *ins, *par)
            grads = vjp(tuple(r[_col_group(r, s, inner)].astype(F32) for r in ct_refs))
            for i in range(nr):
                g = grads[i]
                if i in extra:
                    g = g + extra[i][_col_group(extra[i], s, inner)].astype(F32)
                outs[i][_col_group(outs[i], s, inner)] = g.astype(outs[i].dtype)
                if i == bf16_copy_of:
                    outs[nr + npar][_col_group(outs[nr + npar], s, inner)] = g.astype(BF16)
            par_grads = [g if acc is None else acc + g for acc, g in zip(par_grads, grads[nr:])]
        first = jnp.logical_and(pl.program_id(0) == 0, pl.program_id(1) == 0)
        for pi in range(npar):
            o_ref = outs[nr + pi]
            g = par_grads[pi]

            @pl.when(first)
            def _(o_ref=o_ref, g=g):
                o_ref[...] = g

            @pl.when(jnp.logical_not(first))
            def _(o_ref=o_ref, g=g):
                o_ref[...] += g

    in_specs = (_row_specs(rows, tm, ncol)
                + [pl.BlockSpec(p.shape, lambda i, j: (0, 0)) for p in params]
                + [pl.BlockSpec((tm, c.shape[1] // ncol), lambda i, j: (i, j)) for c in cts]
                + [pl.BlockSpec((tm, adds[i].shape[1] // ncol), lambda i, j: (i, j)) for i in add_idx])
    out_specs = ([pl.BlockSpec((tm, w // ncol), lambda i, j: (i, j)) for _, _, w in rows]
                 + [pl.BlockSpec(p.shape, lambda i, j: (0, 0)) for p in params])
    out_shape = ([jax.ShapeDtypeStruct((t, w), dt) for (_, _, w), dt in zip(rows, grad_dtypes)]
                 + [jax.ShapeDtypeStruct(p.shape, F32) for p in params])
    if bf16_copy_of is not None:
        w = rows[bf16_copy_of][2]
        out_specs.append(pl.BlockSpec((tm, w // ncol), lambda i, j: (i, j)))
        out_shape.append(jax.ShapeDtypeStruct((t, w), BF16))
    return pl.pallas_call(
        body, name=name, grid=(t // tm, ncol), in_specs=in_specs, out_specs=out_specs, out_shape=out_shape,
        compiler_params=_cparams(("arbitrary", "arbitrary")),
    )(*[r[0] for r in rows], *params, *cts, *[adds[i] for i in add_idx])


def _rms(x, w):
    return x * lax.rsqrt(jnp.mean(x * x, axis=-1, keepdims=True) + EPS) * w


def _fn_norm(x, w):
    return (_rms(x, w),)


def _fn_gates(z, a_row, b_row):
    lane = lax.broadcasted_iota(jnp.int32, z.shape, 1)
    beta = jax.nn.sigmoid(z)
    g = -jnp.exp(a_row) * jax.nn.softplus(z + b_row)
    logf = jax.nn.log_sigmoid(z + b_row)
    return (jnp.where(lane < HEADS, beta, jnp.where(lane < 2 * HEADS, g, jnp.where(lane < 3 * HEADS, logf, 0.0))),)


def _fn_qknorm(q, k, qw, kw):
    return _rms(q, qw) * (DH ** -0.5 * LOG2E), _rms(k, kw)


def _fn_gdn_out(o, z, w):
    return (_rms(o, w) * jax.nn.silu(z),)


def _fn_merge(ga, gb, ya, yb):
    return (jax.nn.sigmoid(ga) * ya + jax.nn.sigmoid(gb) * yb,)


def _fn_swiglu(g, u):
    return (jax.nn.silu(g) * u,)


def _bf16_round(x):
    return x.astype(BF16).astype(F32)


def _loss_head(y, target, *, tm, name):
    t, d = y.shape
    tm = min(tm, t)

    def body(y_ref, t_ref, dyf_ref, dyb_ref, loss_ref):
        err = y_ref[...] - t_ref[...]
        dy = err * (1.0 / d)
        dyf_ref[...] = dy
        dyb_ref[...] = dy.astype(BF16)
        part = jnp.sum(err * err) * (0.5 / d)

        @pl.when(pl.program_id(0) == 0)
        def _():
            loss_ref[...] = jnp.zeros_like(loss_ref)

        loss_ref[...] += part

    blk = pl.BlockSpec((tm, d), lambda i: (i, 0))
    return pl.pallas_call(
        body, name=name, grid=(t // tm,), in_specs=[blk, blk],
        out_specs=[blk, blk, pl.BlockSpec((1, LANES), lambda i: (0, 0))],
        out_shape=[jax.ShapeDtypeStruct((t, d), F32), jax.ShapeDtypeStruct((t, d), BF16),
                   jax.ShapeDtypeStruct((1, LANES), F32)],
        compiler_params=_cparams(("arbitrary",)),
    )(y, target)


def _shift_down(x, s):
    if s == 0:
        return x
    row = lax.broadcasted_iota(jnp.int32, x.shape, 0)
    return jnp.where(row >= s, pltpu.roll(x, s, 0), 0.0)


def _shift_up(x, s):
    if s == 0:
        return x
    t = x.shape[0]
    row = lax.broadcasted_iota(jnp.int32, x.shape, 0)
    return jnp.where(row < t - s, pltpu.roll(x, t - s, 0), 0.0)


def _conv_pre(x, w):
    y = x * w[CONV_K - 1:CONV_K, :]
    for i in range(CONV_K - 1):
        y = y + _shift_down(x, CONV_K - 1 - i) * w[i:i + 1, :]
    return y


def _conv_fwd(p_main, conv_w, *, width, name):
    t = p_main.shape[0]
    tc = LANES

    def body(x_ref, w_ref, o_ref):
        y = _conv_pre(x_ref[...].astype(F32), w_ref[...])
        o_ref[...] = y * jax.nn.sigmoid(y)

    return pl.pallas_call(
        body, name=name, grid=(width // tc,),
        in_specs=[pl.BlockSpec((t, tc), lambda j: (0, j)), pl.BlockSpec((CONV_K, tc), lambda j: (0, j))],
        out_specs=pl.BlockSpec((t, tc), lambda j: (0, j)),
        out_shape=jax.ShapeDtypeStruct((t, width), F32),
        compiler_params=_cparams(("parallel",)),
    )(p_main, conv_w)


def _conv_bwd(p_main, conv_w, dy, *, width, name):
    t = p_main.shape[0]
    tc = LANES

    def body(x_ref, w_ref, dy_ref, dx_ref, dw_ref):
        x = x_ref[...].astype(F32)
        w = w_ref[...]
        pre = _conv_pre(x, w)
        sg = jax.nn.sigmoid(pre)
        dpre = dy_ref[...] * (sg * (1.0 + pre * (1.0 - sg)))
        dx = dpre * w[CONV_K - 1:CONV_K, :]
        dws = []
        for i in range(CONV_K - 1):
            s = CONV_K - 1 - i
            dx = dx + _shift_up(dpre, s) * w[i:i + 1, :]
            dws.append(jnp.sum(_shift_down(x, s) * dpre, axis=0, keepdims=True))
        dws.append(jnp.sum(x * dpre, axis=0, keepdims=True))
        dx_ref[...] = dx.astype(dx_ref.dtype)
        dw_ref[...] = jnp.concatenate(dws, axis=0)

    return pl.pallas_call(
        body, name=name, grid=(width // tc,),
        in_specs=[pl.BlockSpec((t, tc), lambda j: (0, j)), pl.BlockSpec((CONV_K, tc), lambda j: (0, j)),
                  pl.BlockSpec((t, tc), lambda j: (0, j))],
        out_specs=[pl.BlockSpec((t, tc), lambda j: (0, j)), pl.BlockSpec((CONV_K, tc), lambda j: (0, j))],
        out_shape=[jax.ShapeDtypeStruct((t, width), BF16), jax.ShapeDtypeStruct((CONV_K, width), F32)],
        compiler_params=_cparams(("parallel",)),
    )(p_main, conv_w, dy)


def _bmm(a, b, spec, precision=None):
    return jnp.einsum(spec, a, b, preferred_element_type=F32, precision=precision)


def _iota2(shape, dim):
    return lax.broadcasted_iota(jnp.int32, shape, dim)


@jax.custom_vjp
def _tri_inverse(a):
    return _tri_inverse_levels(a)


def _tri_inverse_fwd(a):
    t = _tri_inverse_levels(a)
    return t, t


def _tri_inverse_bwd(t, g):
    x = _bmm(t, g, "hji,hjk->hik", SOLVE_PRECISION)
    return (-_bmm(x, t, "hik,hjk->hij", SOLVE_PRECISION),)


_tri_inverse.defvjp(_tri_inverse_fwd, _tri_inverse_bwd)


def _tri_inverse_levels(a):
    c = a.shape[-1]
    r, m = _iota2((c, c), 0), _iota2((c, c), 1)
    eye = (r == m).astype(F32)
    inv = None
    b = 1
    while b < c:
        mask = jnp.logical_and(r // (2 * b) == m // (2 * b), jnp.logical_and(r % (2 * b) >= b, m % (2 * b) < b))
        off = jnp.where(mask[None], a, 0.0)
        if inv is None:
            inv = eye[None] - off
        else:
            inv = inv - _bmm(_bmm(inv, off, "hij,hjk->hik", SOLVE_PRECISION), inv, "hij,hjk->hik", SOLVE_PRECISION)
        b *= 2
    return inv


def _gdn_chunk(s, q3, k3, v3, b3, gc3):
    c = q3.shape[1]
    r, m = _iota2((c, c), 0), _iota2((c, c), 1)
    tril_incl = (r >= m)[None]
    tril_strict = (r > m)[None]
    eye = (r == m).astype(F32)[None]
    qn = q3 * lax.rsqrt(jnp.sum(q3 * q3, axis=-1, keepdims=True) + EPS) * (DH ** -0.5)
    kn = k3 * lax.rsqrt(jnp.sum(k3 * k3, axis=-1, keepdims=True) + EPS)
    ones = jnp.ones((q3.shape[0], c, c), F32)
    gc_row = _bmm(ones, gc3 * eye, "hij,hjk->hik", SOLVE_PRECISION)
    decay = jnp.where(tril_incl, jnp.exp(jnp.where(tril_incl, gc3 - gc_row, 0.0)), 0.0)
    a = jnp.where(tril_strict, _bmm(kn, kn, "hcd,hmd->hcm") * decay * b3, 0.0)
    tinv = _tri_inverse(a)
    egc = jnp.exp(gc3)
    u = _bmm(tinv, v3 * b3, "hij,hjk->hik", SOLVE_PRECISION)
    w = _bmm(tinv, kn * (b3 * egc), "hij,hjk->hik", SOLVE_PRECISION)
    qk = _bmm(qn, kn, "hcd,hmd->hcm") * decay
    v_new = u - _bmm(w, s, "hcd,hdv->hcv")
    o = _bmm(qn * egc, s, "hcd,hdv->hcv") + _bmm(qk, v_new, "hcm,hmv->hcv")
    row = _iota2((c, 1), 0)[None]
    g_last = jnp.sum(jnp.where(row == c - 1, gc3, 0.0), axis=1, keepdims=True)
    s_new = s * jnp.exp(g_last) + _bmm(kn * jnp.exp(g_last - gc3), v_new, "hcd,hcv->hdv")
    return s_new, o


GDN_HEAD_GROUP = 8


def _split_heads(ref, off, h0):
    return jnp.stack([ref[:, off + h * DH:off + (h + 1) * DH].astype(F32)
                      for h in range(h0, h0 + GDN_HEAD_GROUP)], axis=0)


def _store_heads(ref, x3, off, h0):
    for i in range(GDN_HEAD_GROUP):
        h = h0 + i
        ref[:, off + h * DH:off + (h + 1) * DH] = x3[i].astype(ref.dtype)


def _lane_cols(tile, lane0):
    lane = _iota2(tile.shape, 1)
    return jnp.stack([jnp.sum(jnp.where(lane == lane0 + i, tile, 0.0), axis=1, keepdims=True)
                      for i in range(GDN_HEAD_GROUP)], axis=0)


def _cols_to_lanes(cols3, lane0, shape):
    lane = _iota2(shape, 1)
    out = jnp.zeros(shape, F32)
    for i in range(GDN_HEAD_GROUP):
        out = out + jnp.where(lane == lane0 + i, cols3[i], 0.0)
    return out


def _chunk_cumsum_matrix():
    r, m = _iota2((CHUNK, CHUNK), 0), _iota2((CHUNK, CHUNK), 1)
    return (r >= m).astype(F32)


def _gdn_inputs(qkv_ref, gt, gcum, h0):
    return (_split_heads(qkv_ref, 0, h0), _split_heads(qkv_ref, HW, h0), _split_heads(qkv_ref, 2 * HW, h0),
            _lane_cols(gt, h0), _lane_cols(gcum, HEADS + h0))


def _gdn_fwd(qkv, gates, *, name, comm=None):
    t = qkv.shape[0]
    n = t // CHUNK

    def body(qkv_ref, gt_ref, o_ref, sall_ref, s_scr):
        @pl.when(pl.program_id(0) == 0)
        def _():
            s_scr[...] = jnp.zeros_like(s_scr)

        gt = gt_ref[...]
        gcum = jnp.dot(_chunk_cumsum_matrix(), gt, preferred_element_type=F32, precision=HI)
        for h0 in range(0, HEADS, GDN_HEAD_GROUP):
            grp = pl.ds(h0, GDN_HEAD_GROUP)
            s = s_scr[grp]
            sall_ref[0, grp] = s
            s_new, o3 = _gdn_chunk(s, *_gdn_inputs(qkv_ref, gt, gcum, h0))
            s_scr[grp] = s_new
            _store_heads(o_ref, o3, 0, h0)

    return _pcall(
        body, name=name, grid=(n,),
        in_specs=[pl.BlockSpec((CHUNK, 3 * HW), lambda i: (i, 0)), pl.BlockSpec((CHUNK, LANES), lambda i: (i, 0))],
        out_specs=[pl.BlockSpec((CHUNK, HW), lambda i: (i, 0)),
                   pl.BlockSpec((1, HEADS, DH, DH), lambda i: (i, 0, 0, 0))],
        out_shape=[jax.ShapeDtypeStruct((t, HW), F32), jax.ShapeDtypeStruct((n, HEADS, DH, DH), F32)],
        scratch_shapes=[pltpu.VMEM((HEADS, DH, DH), F32)], sem=("arbitrary",), args=(qkv, gates), comm=comm)


def _gdn_bwd(qkv, gates, s_all, do, *, name, comm=None):
    t = qkv.shape[0]
    n = t // CHUNK

    def body(qkv_ref, gt_ref, sall_ref, do_ref, dqkv_ref, dgt_ref, ds_scr):
        @pl.when(pl.program_id(0) == 0)
        def _():
            ds_scr[...] = jnp.zeros_like(ds_scr)

        gt = gt_ref[...]
        cum = _chunk_cumsum_matrix()
        gcum = jnp.dot(cum, gt, preferred_element_type=F32, precision=HI)
        shape = (CHUNK, LANES)
        dbeta = jnp.zeros(shape, F32)
        dgcum = jnp.zeros(shape, F32)
        for h0 in range(0, HEADS, GDN_HEAD_GROUP):
            grp = pl.ds(h0, GDN_HEAD_GROUP)
            _, vjp = jax.vjp(_gdn_chunk, sall_ref[0, grp], *_gdn_inputs(qkv_ref, gt, gcum, h0))
            ds, dq3, dk3, dv3, db3, dgc3 = vjp((ds_scr[grp], _split_heads(do_ref, 0, h0)))
            ds_scr[grp] = ds
            _store_heads(dqkv_ref, dq3, 0, h0)
            _store_heads(dqkv_ref, dk3, HW, h0)
            _store_heads(dqkv_ref, dv3, 2 * HW, h0)
            dbeta = dbeta + _cols_to_lanes(db3, h0, shape)
            dgcum = dgcum + _cols_to_lanes(dgc3, HEADS + h0, shape)
        dg = lax.dot_general(cum, dgcum, (((0,), (0,)), ((), ())), preferred_element_type=F32, precision=HI)
        dgt_ref[...] = dbeta + dg

    rev = lambda i: n - 1 - i
    return _pcall(
        body, name=name, grid=(n,),
        in_specs=[pl.BlockSpec((CHUNK, 3 * HW), lambda i: (rev(i), 0)), pl.BlockSpec((CHUNK, LANES), lambda i: (rev(i), 0)),
                  pl.BlockSpec((1, HEADS, DH, DH), lambda i: (rev(i), 0, 0, 0)),
                  pl.BlockSpec((CHUNK, HW), lambda i: (rev(i), 0))],
        out_specs=[pl.BlockSpec((CHUNK, 3 * HW), lambda i: (rev(i), 0)), pl.BlockSpec((CHUNK, LANES), lambda i: (rev(i), 0))],
        out_shape=[jax.ShapeDtypeStruct((t, 3 * HW), F32), jax.ShapeDtypeStruct((t, LANES), F32)],
        scratch_shapes=[pltpu.VMEM((HEADS, DH, DH), F32)], sem=("arbitrary",), args=(qkv, gates, s_all, do), comm=comm)


FOX_BLK = 1024
LOG2E = math.log2(math.e)
FOX_ROW_SPLIT = 1
NEG = -1e30


def _fox_cumsum(gates, *, name):
    t = gates.shape[0]
    blk = min(FOX_BLK, t)

    def body(g_ref, c_ref):
        r, m = _iota2((blk, blk), 0), _iota2((blk, blk), 1)
        upper = (r <= m).astype(F32)
        carry = jnp.zeros((HEADS, 1), F32)
        for b in range(t // blk):
            lf = g_ref[b * blk:(b + 1) * blk, :].T[2 * HEADS:3 * HEADS, :]
            c = jnp.dot(lf, upper, preferred_element_type=F32, precision=HI) + carry
            c_ref[:, b * blk:(b + 1) * blk] = c * LOG2E
            carry = carry + jnp.sum(lf, axis=1, keepdims=True)

    return pl.pallas_call(body, name=name, out_shape=jax.ShapeDtypeStruct((HEADS, t), F32),
                          compiler_params=_cparams())(gates)


def _fox_cumsum_bwd(dc, dgates_gdn, *, name):
    t = dc.shape[1]
    blk = min(FOX_BLK, t)

    def body(dc_ref, dg_ref, o_ref):
        r, m = _iota2((blk, blk), 0), _iota2((blk, blk), 1)
        lower = (r >= m).astype(F32)
        carry = jnp.zeros((HEADS, 1), F32)
        for b in reversed(range(t // blk)):
            d = dc_ref[:, b * blk:(b + 1) * blk]
            dlf = jnp.dot(d, lower, preferred_element_type=F32, precision=HI) + carry
            carry = carry + jnp.sum(d, axis=1, keepdims=True)
            tile = jnp.concatenate([jnp.zeros((2 * HEADS, blk), F32), dlf,
                                    jnp.zeros((LANES - 3 * HEADS, blk), F32)], axis=0)
            o_ref[b * blk:(b + 1) * blk, :] = tile.T + dg_ref[b * blk:(b + 1) * blk, :]

    return pl.pallas_call(body, name=name, out_shape=jax.ShapeDtypeStruct((t, LANES), F32),
                          compiler_params=_cparams())(dc, dgates_gdn)


def _fox_logits(q, k, c_row, row0=None):
    s = lax.dot_general(q, k, (((1,), (1,)), ((), ())), preferred_element_type=F32) - c_row
    if row0 is None:
        return s
    return jnp.where(row0 + _iota2(s.shape, 0) >= _iota2(s.shape, 1), s, NEG)


def _fox_fwd(qn, kn, p_main, c4, *, v_off, name, comm=None):
    t = qn.shape[0]
    blk = min(FOX_BLK, t)
    nb = t // blk
    vb = v_off // DH

    def body(q_ref, k_ref, v_ref, c_ref, o_ref, o32_ref, lse_ref):
        qi = pl.program_id(1)
        q = q_ref[...]

        def step(j, carry, diagonal=False):
            m, l, acc = carry
            rows = pl.ds(pl.multiple_of(j * blk, blk), blk)
            s = _fox_logits(q, k_ref[rows, :], c_ref[0, j], 0 if diagonal else None)
            m_new = jnp.maximum(m, jnp.max(s, axis=1, keepdims=True))
            p = jnp.exp2(s - m_new)
            scale = jnp.exp2(m - m_new)
            l = scale * l + jnp.sum(p, axis=1, keepdims=True)
            acc = scale * acc + jnp.dot(p.astype(BF16), v_ref[rows, :], preferred_element_type=F32)
            return m_new, l, acc

        init = (jnp.full((blk, 1), NEG, F32), jnp.zeros((blk, 1), F32), jnp.zeros((blk, DH), F32))
        m, l, acc = step(qi, lax.fori_loop(0, qi, step, init), diagonal=True)
        o = acc / l
        o_ref[...] = o.astype(o_ref.dtype)
        o32_ref[...] = o
        lse_ref[0] = m + jnp.log(l) * LOG2E

    return _pcall(
        body, name=name, grid=(HEADS, nb),
        in_specs=[pl.BlockSpec((blk, DH), lambda h, i: (i, h)), pl.BlockSpec((t, DH), lambda h, i: (0, h)),
                  pl.BlockSpec((t, DH), lambda h, i: (0, vb + h)), pl.BlockSpec((1, nb, 1, blk), lambda h, i: (h, 0, 0, 0))],
        out_specs=[pl.BlockSpec((blk, DH), lambda h, i: (i, h)), pl.BlockSpec((blk, DH), lambda h, i: (i, h)),
                   pl.BlockSpec((1, blk, 1), lambda h, i: (h, i, 0))],
        out_shape=[jax.ShapeDtypeStruct((t, HW), BF16), jax.ShapeDtypeStruct((t, HW), F32),
                   jax.ShapeDtypeStruct((HEADS, t, 1), F32)],
        sem=("parallel", "arbitrary"), args=(qn, kn, p_main, c4), comm=comm)


def _fox_bwd(qn, kn, p_main, c4, o32, do, lse, *, v_off, name, comm=None):
    t = qn.shape[0]
    blk = min(FOX_BLK, t)
    nb = t // blk
    vb = v_off // DH
    sub = blk // FOX_ROW_SPLIT
    tn_dims = (((0,), (0,)), ((), ()))
    nt_dims = (((1,), (1,)), ((), ()))

    def body(q_ref, k_ref, v_ref, c_ref, o_ref, do_ref, lse_ref, dq_ref, dk_ref, dv_ref, dc_ref, dcq_ref):
        kj = pl.program_id(1)

        @pl.when(kj == 0)
        def _():
            dq_ref[...] = jnp.zeros_like(dq_ref)
            dcq_ref[...] = jnp.zeros_like(dcq_ref)

        k = k_ref[...]
        v = v_ref[...]
        c_row = c_ref[0, 0]

        def step(i, carry, diagonal=False):
            dk, dv, dc = carry
            for u in range(FOX_ROW_SPLIT):
                rows = pl.ds(pl.multiple_of(i * blk + u * sub, sub), sub)
                q = q_ref[rows, :]
                dob = do_ref[rows, :]
                p = jnp.exp2(_fox_logits(q, k, c_row, u * sub if diagonal else None) - lse_ref[0, rows, :])
                pb = p.astype(BF16)
                dv = dv + lax.dot_general(pb, dob, tn_dims, preferred_element_type=F32)
                dp = lax.dot_general(dob, v, nt_dims, preferred_element_type=F32)
                delta = jnp.sum(dob.astype(F32) * o_ref[rows, :], axis=1, keepdims=True)
                ds = p * (dp - delta)
                dcq_ref[0, rows, :] += jnp.sum(ds, axis=1, keepdims=True)
                dsb = ds.astype(BF16)
                dq_ref[rows, :] += jnp.dot(dsb, k, preferred_element_type=F32) * (1.0 / LOG2E)
                dk = dk + lax.dot_general(dsb, q, tn_dims, preferred_element_type=F32) * (1.0 / LOG2E)
                dc = dc - jnp.sum(ds, axis=0, keepdims=True)
            return dk, dv, dc

        init = (jnp.zeros((blk, DH), F32), jnp.zeros((blk, DH), F32), jnp.zeros((1, blk), F32))
        dk, dv, dc = lax.fori_loop(kj + 1, nb, step, step(kj, init, diagonal=True))
        dk_ref[...] = dk
        dv_ref[...] = dv.astype(dv_ref.dtype)
        dc_ref[0, 0] = dc

    full = lambda h, j: (0, h)
    kvb = lambda h, j: (j, h)
    return _pcall(
        body, name=name, grid=(HEADS, nb), sem=("parallel", "arbitrary"), comm=comm,
        args=(qn, kn, p_main, c4, o32, do, lse),
        in_specs=[pl.BlockSpec((t, DH), full), pl.BlockSpec((blk, DH), kvb),
                  pl.BlockSpec((blk, DH), lambda h, j: (j, vb + h)), pl.BlockSpec((1, 1, 1, blk), lambda h, j: (h, j, 0, 0)),
                  pl.BlockSpec((t, DH), full), pl.BlockSpec((t, DH), full),
                  pl.BlockSpec((1, t, 1), lambda h, j: (h, 0, 0))],
        out_specs=[pl.BlockSpec((t, DH), full), pl.BlockSpec((blk, DH), kvb), pl.BlockSpec((blk, DH), kvb),
                   pl.BlockSpec((1, 1, 1, blk), lambda h, j: (h, j, 0, 0)), pl.BlockSpec((1, t, 1), lambda h, j: (h, 0, 0))],
        out_shape=[jax.ShapeDtypeStruct((t, HW), F32), jax.ShapeDtypeStruct((t, HW), F32),
                   jax.ShapeDtypeStruct((t, HW), BF16), jax.ShapeDtypeStruct((HEADS, nb, 1, blk), F32),
                   jax.ShapeDtypeStruct((HEADS, t, 1), F32)])


ANY = pl.BlockSpec(memory_space=pl.ANY)


def _mesh_pos():
    return lax.axis_index("x"), lax.axis_index("y"), lax.axis_index("c")


def _all_gather(blocks, *, name):
    n = len(blocks)

    def body(*refs):
        ins, outs = refs[:n], refs[n:2 * n]
        send, recv, local = refs[2 * n:]
        x, y, c = _mesh_pos()
        me, sibling = (x, y, c), (x, y, 1 - c)
        chips = [(1 - x, y), (x, 1 - y), (1 - x, 1 - y)]

        def copy(t, k, block, to, src=None):
            dst = outs[t].at[4 * block[0] + 2 * block[1] + block[2]]
            return pltpu.make_async_remote_copy(
                src_ref=dst if src is None else src, dst_ref=dst, send_sem=send.at[7 * t + k],
                recv_sem=recv.at[7 * t + k], device_id=to, device_id_type=MESH)

        mine = [pltpu.make_async_copy(ins[t], outs[t].at[4 * x + 2 * y + c], local.at[t]) for t in range(n)]
        for cp in mine:
            cp.start()
        first = []
        for t in range(n):
            first.append(copy(t, 0, me, sibling, src=ins[t]))
            first += [copy(t, 1 + j, me, (*chip, c), src=ins[t]) for j, chip in enumerate(chips)]
        for cp in first:
            cp.start()
        passed = []
        for j, chip in enumerate(chips):
            for t in range(n):
                copy(t, 1 + j, (*chip, c), me).wait_recv()
                fwd = copy(t, 4 + j, (*chip, c), sibling)
                fwd.start()
                passed.append(fwd)
        for t in range(n):
            copy(t, 0, sibling, me).wait_recv()
            for j, chip in enumerate(chips):
                copy(t, 4 + j, (*chip, 1 - c), me).wait_recv()
        for cp in first + passed:
            cp.wait_send()
        for cp in mine:
            cp.wait()

    return pl.pallas_call(
        body, name=name, in_specs=[ANY] * n, out_specs=[ANY] * n,
        out_shape=[jax.ShapeDtypeStruct((N_DEV,) + b.shape, b.dtype) for b in blocks],
        scratch_shapes=[pltpu.SemaphoreType.DMA((7 * n,)), pltpu.SemaphoreType.DMA((7 * n,)),
                        pltpu.SemaphoreType.DMA((n,))],
    )(*blocks)


def _all_gather_relayed(block, *, name):
    r = block.shape[0]
    half = r // 2
    assert half * 2 == r and half % 16 == 0, block.shape

    def body(in_ref, out_ref, send, recv, local):
        x, y, c = _mesh_pos()
        me, sibling, xn, yn, dg = (x, y, c), (x, y, 1 - c), (1 - x, y, c), (x, 1 - y, c), (1 - x, 1 - y, c)
        slot = lambda p: 4 * p[0] + 2 * p[1] + p[2]
        rows = {"a": pl.ds(0, half), "b": pl.ds(half, half)}

        def copy(k, src, dst, to):
            return pltpu.make_async_remote_copy(src_ref=src, dst_ref=dst, send_sem=send.at[k], recv_sem=recv.at[k],
                                                device_id=to, device_id_type=MESH)

        def part(p, h=None):
            ref = out_ref.at[slot(p)]
            return ref if h is None else ref.at[rows[h]]

        def landed(k, p, h=None):
            copy(k, part(p, h), part(p, h), me).wait_recv()

        mine = pltpu.make_async_copy(in_ref, part(me), local)
        mine.start()
        first = [copy(0, in_ref, part(me), sibling),
                 copy(1, in_ref.at[rows["a"]], part(me, "a"), xn), copy(2, in_ref.at[rows["b"]], part(me, "b"), xn),
                 copy(3, in_ref.at[rows["a"]], part(me, "a"), yn), copy(4, in_ref.at[rows["b"]], part(me, "b"), yn)]
        for cp in first:
            cp.start()
        landed(1, xn, "a")
        relay_a = copy(5, part(xn, "a"), part(xn, "a"), yn)
        relay_a.start()
        landed(4, yn, "b")
        relay_b = copy(6, part(yn, "b"), part(yn, "b"), xn)
        relay_b.start()
        landed(2, xn, "b")
        pass_x = copy(7, part(xn), part(xn), sibling)
        pass_x.start()
        landed(3, yn, "a")
        pass_y = copy(8, part(yn), part(yn), sibling)
        pass_y.start()
        landed(5, dg, "a")
        landed(6, dg, "b")
        pass_d = copy(9, part(dg), part(dg), sibling)
        pass_d.start()
        landed(0, sibling)
        for k, p in ((7, (1 - x, y, 1 - c)), (8, (x, 1 - y, 1 - c)), (9, (1 - x, 1 - y, 1 - c))):
            landed(k, p)
        for cp in first + [relay_a, relay_b, pass_x, pass_y, pass_d]:
            cp.wait_send()
        mine.wait()

    return pl.pallas_call(
        body, name=name, in_specs=[ANY], out_specs=ANY,
        out_shape=jax.ShapeDtypeStruct((N_DEV,) + block.shape, block.dtype),
        scratch_shapes=[pltpu.SemaphoreType.DMA((10,)), pltpu.SemaphoreType.DMA((10,)), pltpu.SemaphoreType.DMA],
    )(block)


def _comm_call(comm, *, name):
    ci, co = len(comm.ins), len(comm.out_shapes)

    def body(*refs):
        comm.start(refs[:ci], refs[ci:ci + co], refs[ci + co:])
        comm.finish(refs[:ci], refs[ci:ci + co], refs[ci + co:])

    return pl.pallas_call(body, name=name, in_specs=[ANY] * ci, out_specs=[ANY] * co, out_shape=comm.out_shapes,
                          scratch_shapes=comm.sems, input_output_aliases=comm.aliases)(*comm.ins)


def _ag_first_comm(shards, rows=None, into=None):
    n = len(shards)
    rows = rows or [None] * n
    into = into or [None] * n
    carried = [t for t in range(n) if into[t] is not None]

    def copies(cin, cout, sems):
        send, recv, local = sems
        x, y, c = _mesh_pos()
        peers = [(x, y, 1 - c), (1 - x, y, c), (x, 1 - y, c), (1 - x, 1 - y, c)]
        slot = lambda p: 4 * p[0] + 2 * p[1] + p[2]
        mine, out, inc = [], [], []
        for t in range(n):
            part = (lambda ref: ref) if rows[t] is None else (lambda ref, r=rows[t]: ref.at[pl.ds(r[0], r[1])])
            own = part(cout[t].at[slot((x, y, c))])
            mine.append(pltpu.make_async_copy(part(cin[t]), own, local.at[t]))
            for k, peer in enumerate(peers):
                sems_k = dict(send_sem=send.at[4 * t + k], recv_sem=recv.at[4 * t + k], device_id=peer,
                              device_id_type=MESH)
                theirs = part(cout[t].at[slot(peer)])
                out.append(pltpu.make_async_remote_copy(src_ref=part(cin[t]), dst_ref=own, **sems_k))
                inc.append(pltpu.make_async_remote_copy(src_ref=theirs, dst_ref=theirs, **sems_k))
        return mine, out, inc

    def start(cin, cout, sems):
        mine, out, _ = copies(cin, cout, sems)
        for cp in mine + out:
            cp.start()

    def finish(cin, cout, sems):
        mine, out, inc = copies(cin, cout, sems)
        for cp in inc:
            cp.wait_recv()
        for cp in out:
            cp.wait_send()
        for cp in mine:
            cp.wait()

    return _Comm(list(shards) + [into[t] for t in carried],
                 [jax.ShapeDtypeStruct((N_DEV,) + s.shape, s.dtype) for s in shards],
                 [pltpu.SemaphoreType.DMA((4 * n,)), pltpu.SemaphoreType.DMA((4 * n,)), pltpu.SemaphoreType.DMA((n,))],
                 start, finish, aliases={n + i: t for i, t in enumerate(carried)})


def _ag_pass_comm(gathered):
    n = len(gathered)

    def copies(cout, sems):
        send, recv = sems
        x, y, c = _mesh_pos()
        fwd, inc = [], []
        for t in range(n):
            for j, (px, py) in enumerate([(1 - x, y), (x, 1 - y), (1 - x, 1 - y)]):
                sems_j = dict(send_sem=send.at[3 * t + j], recv_sem=recv.at[3 * t + j], device_id=(x, y, 1 - c),
                              device_id_type=MESH)
                mine, theirs = cout[t].at[4 * px + 2 * py + c], cout[t].at[4 * px + 2 * py + 1 - c]
                fwd.append(pltpu.make_async_remote_copy(src_ref=mine, dst_ref=mine, **sems_j))
                inc.append(pltpu.make_async_remote_copy(src_ref=theirs, dst_ref=theirs, **sems_j))
        return fwd, inc

    def start(cin, cout, sems):
        for cp in copies(cout, sems)[0]:
            cp.start()

    def finish(cin, cout, sems):
        fwd, inc = copies(cout, sems)
        for cp in inc:
            cp.wait_recv()
        for cp in fwd:
            cp.wait_send()

    return _Comm(gathered, [jax.ShapeDtypeStruct(g.shape, g.dtype) for g in gathered],
                 [pltpu.SemaphoreType.DMA((3 * n,)), pltpu.SemaphoreType.DMA((3 * n,))], start, finish,
                 aliases={t: t for t in range(n)})


def _rs_sibling_comm(grads):
    n = len(grads)

    def copies(cin, cout, sems):
        send, recv = sems
        x, y, c = _mesh_pos()
        return [pltpu.make_async_remote_copy(
            src_ref=cin[t].at[2 * q + (1 - c)], dst_ref=cout[t].at[q], send_sem=send.at[4 * t + q],
            recv_sem=recv.at[4 * t + q], device_id=(x, y, 1 - c), device_id_type=MESH)
            for t in range(n) for q in range(4)]

    def start(cin, cout, sems):
        for cp in copies(cin, cout, sems):
            cp.start()

    def finish(cin, cout, sems):
        cps = copies(cin, cout, sems)
        for cp in cps:
            cp.wait_recv()
        for cp in cps:
            cp.wait_send()

    return _Comm(grads, [jax.ShapeDtypeStruct((4,) + g.shape[1:], g.dtype) for g in grads],
                 [pltpu.SemaphoreType.DMA((4 * n,)), pltpu.SemaphoreType.DMA((4 * n,))], start, finish)


def _join_comms(comms):
    ins, outs, sems, aliases, spans = [], [], [], {}, []
    for cm in comms:
        spans.append((len(ins), len(cm.ins), len(outs), len(cm.out_shapes), len(sems), len(cm.sems)))
        aliases.update({len(ins) + i: len(outs) + o for i, o in cm.aliases.items()})
        ins, outs, sems = ins + cm.ins, outs + cm.out_shapes, sems + cm.sems

    def run(which):
        def fn(cin, cout, csem):
            for cm, (i0, ni, o0, no, s0, ns) in zip(comms, spans):
                getattr(cm, which)(cin[i0:i0 + ni], cout[o0:o0 + no], csem[s0:s0 + ns])
        return fn

    return _Comm(ins, outs, sems, run("start"), run("finish"), aliases)


def _rs_chips_comm(parts):
    n = len(parts)

    def copies(cin, cout, sems):
        send, recv, local = sems
        x, y, c = _mesh_pos()
        my_chip = 2 * x + y
        mine = [pltpu.make_async_copy(cin[t].at[my_chip], cout[t].at[my_chip], local.at[t]) for t in range(n)]
        sends, lands = [], []
        for t in range(n):
            for k, (px, py) in enumerate([(1 - x, y), (x, 1 - y), (1 - x, 1 - y)]):
                sems_k = dict(send_sem=send.at[3 * t + k], recv_sem=recv.at[3 * t + k], device_id=(px, py, c),
                              device_id_type=MESH)
                sends.append(pltpu.make_async_remote_copy(src_ref=cin[t].at[2 * px + py], dst_ref=cout[t].at[my_chip],
                                                          **sems_k))
                lands.append(pltpu.make_async_remote_copy(src_ref=cout[t].at[2 * px + py],
                                                          dst_ref=cout[t].at[2 * px + py], **sems_k))
        return mine, sends, lands

    def start(cin, cout, sems):
        mine, sends, _ = copies(cin, cout, sems)
        for cp in mine + sends:
            cp.start()

    def finish(cin, cout, sems):
        mine, sends, lands = copies(cin, cout, sems)
        for cp in lands:
            cp.wait_recv()
        for cp in sends:
            cp.wait_send()
        for cp in mine:
            cp.wait()

    return _Comm(parts, [jax.ShapeDtypeStruct(p.shape, p.dtype) for p in parts],
                 [pltpu.SemaphoreType.DMA((3 * n,)), pltpu.SemaphoreType.DMA((3 * n,)), pltpu.SemaphoreType.DMA((n,))],
                 start, finish)


def _row_tile(r, c, itemsize, budget=3 * 1024 * 1024):
    best = None
    for tr in range(16, r + 1, 16):
        if r % tr == 0 and tr * c * itemsize <= budget:
            best = tr
    return best or r


def _pair_sum(grad, land, *, name):
    _, r, c = grad.shape
    tr = _row_tile(r, c, 2)

    def body(g_ref, l_ref, o_ref):
        o_ref[...] = (g_ref[...].astype(F32) + l_ref[...].astype(F32)).astype(o_ref.dtype)

    return pl.pallas_call(
        body, name=name, grid=(4, r // tr),
        in_specs=[pl.BlockSpec((1, tr, c), lambda q, i: (2 * q + lax.axis_index("c"), i, 0)),
                  pl.BlockSpec((1, tr, c), lambda q, i: (q, i, 0))],
        out_specs=pl.BlockSpec((1, tr, c), lambda q, i: (q, i, 0)),
        out_shape=jax.ShapeDtypeStruct((4, r, c), grad.dtype),
        compiler_params=_cparams(("parallel", "parallel")),
    )(grad, land)


def _adamw_math(w, g, m, v):
    m = ADAM_B1 * m + (1.0 - ADAM_B1) * g
    v = ADAM_B2 * v + (1.0 - ADAM_B2) * jnp.square(g)
    m_hat = m / (1.0 - ADAM_B1 ** ADAM_STEP)
    v_hat = v / (1.0 - ADAM_B2 ** ADAM_STEP)
    delta = -ADAM_LR * (m_hat / (jnp.sqrt(v_hat) + ADAM_EPS) + ADAM_WD * w)
    return delta, m, v


def _adamw(parts, w, m, v, *, name):
    s, _, cp = parts.shape
    r, c = w.shape
    tr = _row_tile(r, cp, 4, budget=1024 * 1024)

    def body(p_ref, w_ref, m_ref, v_ref, g_ref, d_ref, nm_ref, nv_ref):
        g = p_ref[0].astype(F32)
        for i in range(1, s):
            g = g + p_ref[i].astype(F32)
        g = g[:, :c]
        delta, nm, nv = _adamw_math(w_ref[...], g, m_ref[...], v_ref[...])
        g_ref[...] = g
        d_ref[...] = delta
        nm_ref[...] = nm
        nv_ref[...] = nv

    blk = pl.BlockSpec((tr, c), lambda i: (i, 0))
    return pl.pallas_call(
        body, name=name, grid=(r // tr,),
        in_specs=[pl.BlockSpec((s, tr, cp), lambda i: (0, i, 0)), blk, blk, blk],
        out_specs=[blk] * 4, out_shape=[jax.ShapeDtypeStruct((r, c), F32)] * 4,
        compiler_params=_cparams(("parallel",)),
    )(parts, w, m, v)


def _w_in_pieces(d, nb, sources):
    segs = [(0, 4 * HW, False, 0), (4 * HW, 4 * HW + 2 * HEADS, True, 0),
            (4 * HW + 2 * HEADS, 7 * HW + 2 * HEADS, False, 4 * HW),
            (7 * HW + 2 * HEADS, 7 * HW + 3 * HEADS, True, 2 * HEADS),
            (7 * HW + 3 * HEADS, 7 * HW + 3 * HEADS + 2 * d, False, 7 * HW)]
    out = []
    for dev in range(N_DEV):
        lo, hi = dev * nb, (dev + 1) * nb
        for s0, s1, is_small, a0 in segs:
            p, q = max(lo, s0), min(hi, s1)
            if p >= q:
                continue
            a, b = a0 + p - s0, a0 + q - s0
            if is_small:
                out.append((dev, p - lo, q - lo, len(sources), a, b))
                continue
            for si, (start, width) in enumerate(sources):
                u, v = max(a, start), min(b, start + width)
                if u < v:
                    out.append((dev, p - lo + (u - a), p - lo + (v - a), si, u - start, v - start))
    return out


def _concat_cols(parts, *, name):
    t = parts[0].shape[0]
    n = len(parts)
    offs = [sum(p.shape[1] for p in parts[:i]) for i in range(n)]
    tm = min(128, t)

    def body(*refs):
        for i in range(n):
            refs[n][:, offs[i]:offs[i] + parts[i].shape[1]] = refs[i][...]

    return pl.pallas_call(
        body, name=name, grid=(t // tm,),
        in_specs=[pl.BlockSpec((tm, p.shape[1]), lambda i: (i, 0)) for p in parts],
        out_specs=pl.BlockSpec((tm, offs[-1] + parts[-1].shape[1]), lambda i: (i, 0)),
        out_shape=jax.ShapeDtypeStruct((t, offs[-1] + parts[-1].shape[1]), parts[0].dtype),
        compiler_params=_cparams(("parallel",)))(*parts)


def _w_in_to_aligned(g_in, *, name):
    _, d, nb = g_in.shape
    n_main = 7 * HW + 2 * d
    tr = min(128, d)
    pieces = _w_in_pieces(d, nb, [(0, n_main)])

    def body(g_ref, main_ref, small_ref):
        small_ref[...] = jnp.zeros_like(small_ref)
        for dev, s, e, src, a, b in pieces:
            dst = main_ref if src == 0 else small_ref
            dst[:, a:b] = g_ref[dev, :, s:e]

    return pl.pallas_call(
        body, name=name, grid=(d // tr,), in_specs=[pl.BlockSpec((N_DEV, tr, nb), lambda i: (0, i, 0))],
        out_specs=[pl.BlockSpec((tr, n_main), lambda i: (i, 0)), pl.BlockSpec((tr, LANES), lambda i: (i, 0))],
        out_shape=[jax.ShapeDtypeStruct((d, n_main), g_in.dtype), jax.ShapeDtypeStruct((d, LANES), g_in.dtype)],
        compiler_params=_cparams(("parallel",)),
    )(g_in)


def _w_in_grad_blocks(seg_grads, small_grad, sources, nb, *, name):
    d = small_grad.shape[0]
    tr = min(128, d)
    pieces = _w_in_pieces(d, nb, sources)
    ns = len(seg_grads)

    def body(*refs):
        o_ref = refs[ns + 1]
        for dev, s, e, src, a, b in pieces:
            o_ref[dev, :, s:e] = refs[src][:, a:b]

    return pl.pallas_call(
        body, name=name, grid=(d // tr,),
        in_specs=[pl.BlockSpec((tr, g.shape[1]), lambda i: (i, 0)) for g in seg_grads + [small_grad]],
        out_specs=pl.BlockSpec((N_DEV, tr, nb), lambda i: (0, i, 0)),
        out_shape=jax.ShapeDtypeStruct((N_DEV, d, nb), small_grad.dtype),
        compiler_params=_cparams(("parallel",)),
    )(*seg_grads, small_grad)


def _pad_cols(a, n):
    return a if a.shape[1] == n else jnp.concatenate([a, jnp.zeros((a.shape[0], n - a.shape[1]), a.dtype)], axis=1)


def _pad_rows(a, n):
    return a if a.shape[0] == n else jnp.concatenate([a, jnp.zeros((n - a.shape[0], a.shape[1]), a.dtype)], axis=0)


class _StaticPlan:
    def __init__(self, weights, cp):
        self.w, self.cp, self.grads = weights, cp, {}

    def comm_for(self, key):
        return None

    def done(self, key, res):
        pass

    def weight(self, name):
        return self.w[name]

    def grad(self, name, g):
        self.grads[name] = g

    def grad_w_in(self, g_main, g_small):
        self.grads["w_main"], self.grads["w_small"] = g_main, g_small


class _FsdpPlan:
    RIDES = {
        "in_proj": (("gather", (("conv", None), ("wa", None), ("wb", None), ("wout", None), ("wg", 0), ("wd", 1))),),
        "gdn_fwd": (("gather", (("wg", 1),)), ("pass", ("wa", "wb", "wout"))),
        "fox_fwd": (("gather", (("wu", 0),)), ("pass", ("wg",))),
        "ffn_gate": (("gather", (("wu", 1),)),),
        "ffn_up": (("gather", (("wd", 0),)),),
        "dw_ffn_gate": (("sibling", ("wd",)),),
        "d_hn_gate": (("chips", ("wd",)), ("sibling", ("wg",))),
        "d_hn_up": (("chips", ("wg",)),),
        "d_merged": (("sibling", ("wu",)),),
        "d_oa": (("sibling", ("wout",)),),
        "d_ob": (("sibling", ("wa",)),),
        "gdn_bwd": (("chips", ("wu", "wout")), ("sibling", ("wb",))),
        "fox_bwd": (("chips", ("wa", "wb")),),
        "d_xn": (("chips", ("w_in", "conv")),),
    }
    PASS_GROUPS = (("conv",), ("wa", "wb", "wout"), ("wg",), ("wu",), ("wd",))

    def __init__(self, shards, d, cp, nb):
        self.shards, self.d, self.cp, self.nb = shards, d, cp, nb
        self.first, self.full = {}, {}
        self.blocks, self.queue, self.slots = {}, {}, {}
        self.flying = []

    def comm_for(self, key):
        comms, self.flying = [], []
        for kind, items in self.RIDES.get(key, ()):
            if kind == "gather":
                names = [n for n, _ in items]
                rows = [None if half is None else (half * (self.shards[n].shape[0] // 2), self.shards[n].shape[0] // 2)
                        for n, half in items]
                comm = _ag_first_comm([self.shards[n] for n in names], rows, [self.first.get(n) for n in names])
            elif kind == "pass":
                names = list(items)
                comm = _ag_pass_comm([self.first[n] for n in names])
            elif kind == "sibling":
                names = [n for n in items if n in self.blocks]
                comm = _rs_sibling_comm([self.blocks[n] for n in names]) if names else None
            else:
                for n in items:
                    if n in self.blocks:
                        self.sibling_now(n)
                names = [n for n in items if n in self.queue]
                comm = _rs_chips_comm([self.queue.pop(n) for n in names]) if names else None
            if comm is not None:
                comms.append(comm)
                self.flying.append((kind, names, len(comm.out_shapes)))
        return _join_comms(comms) if comms else None

    def done(self, key, res):
        res = list(res)
        for kind, names, n_out in self.flying:
            outs, res = res[:n_out], res[n_out:]
            if kind == "gather":
                self.first.update(zip(names, outs))
            elif kind == "pass":
                self.full.update(zip(names, outs))
            elif kind == "sibling":
                for n, land in zip(names, outs):
                    self.queue[n] = _pair_sum(self.blocks.pop(n), land, name=f"pair_sum_{n}")
            else:
                self.slots.update(zip(names, outs))
        self.flying = []

    def weight(self, name):
        if name not in self.full:
            group = next(g for g in self.PASS_GROUPS if name in g)
            outs = _comm_call(_ag_pass_comm([self.first[n] for n in group]), name=f"all_gather_pass_{group[0]}")
            self.full.update(zip(group, outs))
        g = self.full[name]
        if name in ("wa", "wb", "conv"):
            return _cols_of_blocks(g)
        if name == "wout":
            return g.reshape(self.d, self.d)
        if name == "wd":
            return g.reshape(N_DEV * self.cp, self.d)
        return g

    def grad(self, name, g):
        if name == "wout":
            g = g.reshape(N_DEV, self.d // N_DEV, self.d)
        if name == "wd":
            g = g.reshape(N_DEV, self.cp, self.d)
        if name == "conv":
            g = _blocks_of_cols(g.astype(BF16))
        self.blocks[name] = g

    def grad_w_in(self, g_main, g_small):
        self.blocks["w_in"] = _w_in_grad_blocks([g_main], g_small, [(0, g_main.shape[1])], self.nb,
                                                name="w_in_grad_blocks")

    def sibling_now(self, name):
        blocks = self.blocks.pop(name)
        (land,) = _comm_call(_rs_sibling_comm([blocks]), name=f"grads_to_sibling_{name}")
        self.queue[name] = _pair_sum(blocks, land, name=f"pair_sum_{name}")

    def flush(self):
        for name in list(self.blocks):
            self.sibling_now(name)
        if self.queue:
            outs = _comm_call(_rs_chips_comm(list(self.queue.values())), name="grads_to_chips_tail")
            self.slots.update(zip(self.queue, outs))
            self.queue = {}


def _carried(plan, key, fn, *args, **kw):
    comm = plan.comm_for(key)
    if comm is None:
        return fn(*args, **kw)
    res, comm_res = fn(*args, comm=comm, **kw)
    plan.done(key, comm_res)
    return res


def _local_step(x, target, w_main, w_small, plan,
                norm_mix_w, norm_ffn_w, gdn_norm_w, fox_q_w, fox_k_w, a_row, b_row):
    t, d = x.shape
    cp = plan.cp
    fp = N_DEV * cp
    n_main = w_main.shape[1]
    off_gb = OFF_GA + d
    tm = 1024
    rt = 128

    (xn,) = _rowwise_fwd(_fn_norm, [(x, 0, d)], [norm_mix_w], [(d, BF16)], tm=rt, name="mix_norm")
    p_main = _carried(plan, "in_proj", _mm, xn, w_main, mode="nn", m=t, n=n_main, k=d, tm=tm, tn=512, tk=d,
                      out_dtype=BF16, name="in_proj")
    p_small = _mm(xn, w_small, mode="nn", m=t, n=LANES, k=d, tm=tm, tn=LANES, tk=d, out_dtype=F32, name="in_proj_small")
    (gates,) = _rowwise_fwd(_fn_gates, [(p_small, 0, LANES)], [a_row, b_row], [(LANES, F32)], tm=512, name="gates")
    conv_w = plan.weight("conv")
    qkv = _conv_fwd(p_main, conv_w, width=3 * HW, name="conv_fwd")
    o_gdn, s_all = _carried(plan, "gdn_fwd", _gdn_fwd, qkv, gates, name="gdn_fwd")
    gdn_rows = [(o_gdn, 0, HW), (p_main, OFF_ZA, HW)]
    (oa,) = _rowwise_fwd(_fn_gdn_out, gdn_rows, [gdn_norm_w], [(HW, BF16)], tm=512, inner=HEADS, name="gdn_out")
    wa = plan.weight("wa")
    ya = _mm(oa, wa, mode="nn", m=t, n=d, k=HW, tm=tm, tn=1024, tk=HW, out_dtype=BF16, name="branch_a")
    qk_rows = [(p_main, OFF_QB, HW), (p_main, OFF_KB, HW)]
    qn, kn = _rowwise_fwd(_fn_qknorm, qk_rows, [fox_q_w, fox_k_w], [(HW, BF16), (HW, BF16)], tm=512, inner=HEADS,
                          name="fox_qk_norm")
    blk = min(FOX_BLK, t)
    c4 = _fox_cumsum(gates, name="fox_cumsum").reshape(HEADS, t // blk, 1, blk)
    ob, ob32, lse = _carried(plan, "fox_fwd", _fox_fwd, qn, kn, p_main, c4, v_off=OFF_VB, name="fox_fwd")
    wb = plan.weight("wb")
    yb, merged = _mm(ob, wb, mode="nn", m=t, n=d, k=HW, tm=tm, tn=1024, tk=HW, out_dtype=BF16, name="branch_b",
                     tiles=[(p_main, OFF_GA), (p_main, off_gb), (ya, 0)], out_dtypes=[BF16, BF16],
                     epilogue=lambda p, ga, gb, a: (p, _fn_merge(ga, gb, a, _bf16_round(p))[0]))
    wout = plan.weight("wout")
    h = _mm(merged, wout, mode="nn", m=t, n=d, k=d, tm=tm, tn=512, tk=d, out_dtype=F32, add=x, name="out_proj")
    (hn,) = _rowwise_fwd(_fn_norm, [(h, 0, d)], [norm_ffn_w], [(d, BF16)], tm=rt, name="ffn_norm")
    wg = plan.weight("wg")
    gate = _carried(plan, "ffn_gate", _mm, hn, wg, mode="nn", m=t, n=fp, k=d, tm=512, tn=cp, tk=d, out_dtype=BF16,
                    b_blocked=True, name="ffn_gate")
    wu = plan.weight("wu")
    up, act = _carried(plan, "ffn_up", _mm, hn, wu, mode="nn", m=t, n=fp, k=d, tm=512, tn=cp, tk=d, out_dtype=BF16,
                       b_blocked=True, name="ffn_up", tiles=[(gate, 0)], out_dtypes=[BF16, BF16],
                       epilogue=lambda p, g: (p, _fn_swiglu(g, _bf16_round(p))[0]))
    wd = plan.weight("wd")
    y = _mm(act, wd, mode="nn", m=t, n=d, k=fp, tm=512, tn=256, tk=fp, out_dtype=F32, add=h, name="ffn_down")
    dy, dyb, loss_row = _loss_head(y, target, tm=rt, name="loss_head")

    dgate, dup = _mm(dyb, wd, mode="nt", m=t, n=fp, k=d, tm=tm, tn=512, tk=d, out_dtype=BF16, name="d_act",
                     tiles=[(gate, 0), (up, 0)], out_dtypes=[BF16, BF16],
                     epilogue=lambda p, g, u: jax.vjp(_fn_swiglu, g, u)[1]((_bf16_round(p),)))
    plan.grad("wd", _mm(act, dyb, mode="tn", m=fp, n=d, k=t, tm=512, tn=1024, tk=t, out_dtype=BF16, name="dw_ffn_down"))
    plan.grad("wg", _carried(plan, "dw_ffn_gate", _mm, hn, dgate, mode="tn", m=d, n=fp, k=t, tm=512, tn=cp, tk=t,
                             out_dtype=BF16, out_blocked=True, name="dw_ffn_gate"))
    dhn = _carried(plan, "d_hn_gate", _mm, dgate, wg, mode="nt", m=t, n=d, k=fp, tm=512, tn=256, tk=fp, out_dtype=F32,
                   b_blocked=True, name="d_hn_gate")
    dhn = _carried(plan, "d_hn_up", _mm, dup, wu, mode="nt", m=t, n=d, k=fp, tm=512, tn=256, tk=fp, out_dtype=F32,
                   add=dhn, b_blocked=True, name="d_hn_up")
    plan.grad("wu", _mm(hn, dup, mode="tn", m=d, n=fp, k=t, tm=512, tn=cp, tk=t, out_dtype=BF16, out_blocked=True,
                        name="dw_ffn_up"))
    dh, d_norm_ffn, dhb = _rowwise_bwd(_fn_norm, [(h, 0, d)], [norm_ffn_w], [dhn], [F32], tm=rt, name="d_ffn_norm",
                                       adds=[dy], bf16_copy_of=0)
    dga, dgb, dya, dyb2 = _carried(
        plan, "d_merged", _mm, dhb, wout, mode="nt", m=t, n=d, k=d, tm=512, tn=512, tk=d, out_dtype=BF16, name="d_merged",
        tiles=[(p_main, OFF_GA), (p_main, off_gb), (ya, 0), (yb, 0)], out_dtypes=[BF16] * 4,
        epilogue=lambda p, *gy: jax.vjp(_fn_merge, *gy)[1]((_bf16_round(p),)))
    plan.grad("wout", _mm(merged, dhb, mode="tn", m=d, n=d, k=t, tm=512, tn=512, tk=t, out_dtype=BF16, name="dw_out"))
    doa = _carried(plan, "d_oa", _mm, dya, wa, mode="nt", m=t, n=HW, k=d, tm=tm, tn=512, tk=d, out_dtype=BF16, name="d_oa")
    plan.grad("wa", _mm(oa, dya, mode="tn", m=HW, n=d, k=t, tm=1024, tn=d // N_DEV, tk=t, out_dtype=BF16,
                        out_blocked=True, name="dw_branch_a"))
    dob = _carried(plan, "d_ob", _mm, dyb2, wb, mode="nt", m=t, n=HW, k=d, tm=tm, tn=512, tk=d, out_dtype=BF16, name="d_ob")
    plan.grad("wb", _mm(ob, dyb2, mode="tn", m=HW, n=d, k=t, tm=1024, tn=d // N_DEV, tk=t, out_dtype=BF16,
                        out_blocked=True, name="dw_branch_b"))
    do_gdn, dza, d_gdn_norm = _rowwise_bwd(_fn_gdn_out, gdn_rows, [gdn_norm_w], [doa], [F32, BF16], tm=256,
                                           inner=HEADS, name="d_gdn_out")
    dqkv, dgates_gdn = _carried(plan, "gdn_bwd", _gdn_bwd, qkv, gates, s_all, do_gdn, name="gdn_bwd")
    dp_qkv, dconv = _conv_bwd(p_main, conv_w, dqkv, width=3 * HW, name="conv_bwd")
    plan.grad("conv", dconv)
    dqn, dkn, dvb, dc4, dcq = _carried(plan, "fox_bwd", _fox_bwd, qn, kn, p_main, c4, ob32, dob, lse, v_off=OFF_VB,
                                       name="fox_bwd")
    dqb, dkb, d_fox_q, d_fox_k = _rowwise_bwd(_fn_qknorm, qk_rows, [fox_q_w, fox_k_w], [dqn, dkn], [BF16, BF16],
                                              tm=256, inner=HEADS, name="d_fox_qk_norm")
    dgates = _fox_cumsum_bwd(dc4.reshape(HEADS, t) + dcq.reshape(HEADS, t), dgates_gdn, name="fox_cumsum_bwd")
    dsmall, d_a_row, d_b_row = _rowwise_bwd(_fn_gates, [(p_small, 0, LANES)], [a_row, b_row], [dgates], [F32],
                                            tm=512, name="d_gates")
    dp_main = _concat_cols([dp_qkv, dza, dqb, dkb, dvb, dga, dgb], name="d_p_main")
    plan.grad_w_in(_mm(xn, dp_main, mode="tn", m=d, n=n_main, k=t, tm=1024, tn=math.gcd(n_main, 1024), tk=t,
                       out_dtype=BF16, name="dw_in"),
                   _mm(xn, dsmall, mode="tn", m=d, n=LANES, k=t, tm=1024, tn=LANES, tk=t, out_dtype=BF16,
                       name="dw_in_small"))
    dxn = _mm(dsmall, w_small, mode="nt", m=t, n=d, k=LANES, tm=tm, tn=1024, tk=LANES, out_dtype=F32, name="d_xn_small")
    dxn = _carried(plan, "d_xn", _mm, dp_main, w_main, mode="nt", m=t, n=d, k=n_main, tm=tm, tn=1024,
                   tk=math.gcd(n_main, 2048),
                   out_dtype=F32, add=dxn, name="d_xn")
    grad_x, d_norm_mix = _rowwise_bwd(_fn_norm, [(x, 0, d)], [norm_mix_w], [dxn], [F32], tm=rt, name="d_mix_norm",
                                      adds=[dh])
    small = dict(norm_mix=d_norm_mix, norm_ffn=d_norm_ffn, gdn_norm=d_gdn_norm, fox_q=d_fox_q, fox_k=d_fox_k,
                 a_row=d_a_row, b_row=d_b_row)
    return loss_row[0, 0], grad_x, small


def _lane_row(pieces):
    row = jnp.zeros((1, LANES), F32)
    for off, p in pieces:
        row = lax.dynamic_update_slice(row, p.astype(F32), (0, off))
    return row


def _pack_small(norm_mix, norm_ffn, gdn_norm, fox_q, fox_k, a_log, dt_bias, b_f):
    rows = [norm_mix.reshape(-1, LANES), norm_ffn.reshape(-1, LANES), gdn_norm, fox_q, fox_k,
            _lane_row([(HEADS, a_log)]), _lane_row([(HEADS, dt_bias), (2 * HEADS, b_f)])]
    packed = jnp.concatenate(rows, axis=0)
    return _pad_rows(packed, -(-packed.shape[0] // 8) * 8)


def _unpack_small(p, d):
    nd = d // LANES
    r = 2 * nd
    return (p[0:nd].reshape(1, d), p[r + 3:r + 4, HEADS:2 * HEADS], p[r + 4:r + 5, HEADS:2 * HEADS], p[r:r + 1],
            p[r + 4:r + 5, 2 * HEADS:3 * HEADS], p[r + 1:r + 2], p[r + 2:r + 3], p[nd:r].reshape(1, d))


def _blocks_of_cols(a):
    r, c8 = a.shape
    return a.reshape(r, N_DEV, c8 // N_DEV).transpose(1, 0, 2)


def _cols_of_blocks(g):
    _, r, c = g.shape
    return g.transpose(1, 0, 2).reshape(r, N_DEV * c)


def kernel(x, norm_mix_w, w_in, conv_w, a_log, dt_bias, gdn_norm_w, fox_b_f, fox_q_norm_w, fox_k_norm_w, w_branch_a, w_branch_b, w_out, norm_ffn_w, w_ffn_gate, w_ffn_up, w_ffn_down, loss_target, m_norm_mix_w, m_w_in, m_conv_w, m_a_log, m_dt_bias, m_gdn_norm_w, m_fox_b_f, m_fox_q_norm_w, m_fox_k_norm_w, m_w_branch_a, m_w_branch_b, m_w_out, m_norm_ffn_w, m_w_ffn_gate, m_w_ffn_up, m_w_ffn_down, v_norm_mix_w, v_w_in, v_conv_w, v_a_log, v_dt_bias, v_gdn_norm_w, v_fox_b_f, v_fox_q_norm_w, v_fox_k_norm_w, v_w_branch_a, v_w_branch_b, v_w_out, v_norm_ffn_w, v_w_ffn_gate, v_w_ffn_up, v_w_ffn_down):
    d = x.shape[-1]
    cp = -(-w_ffn_down.shape[1] // LANES) * LANES
    nb = w_in.shape[2]

    g_in = _all_gather_relayed(w_in[0].astype(BF16), name="w_in_all_gather")
    w_main, w_small = _w_in_to_aligned(g_in, name="w_in_to_aligned")
    plan = _FsdpPlan(dict(conv=conv_w[0], wa=w_branch_a[0].astype(BF16), wb=w_branch_b[0].astype(BF16), wout=w_out[0].astype(BF16),
                          wg=_pad_cols(w_ffn_gate[0].astype(BF16), cp), wu=_pad_cols(w_ffn_up[0].astype(BF16), cp),
                          wd=_pad_rows(w_ffn_down[0].astype(BF16), cp)), d, cp, nb)
    a_row = _lane_row([(HEADS, a_log)])
    b_row = _lane_row([(HEADS, dt_bias), (2 * HEADS, fox_b_f)])

    loss_part, grad_x, gs = _local_step(
        x[0], loss_target[0], w_main, w_small, plan,
        norm_mix_w, norm_ffn_w, gdn_norm_w, fox_q_norm_w, fox_k_norm_w, a_row, b_row)
    loss = lax.psum(loss_part, ("x", "y", "c"))

    plan.flush()
    big = dict(w_in=("w_in", w_in, m_w_in, v_w_in), w_branch_a=("wa", w_branch_a, m_w_branch_a, v_w_branch_a),
               w_branch_b=("wb", w_branch_b, m_w_branch_b, v_w_branch_b), w_out=("wout", w_out, m_w_out, v_w_out),
               w_ffn_gate=("wg", w_ffn_gate, m_w_ffn_gate, v_w_ffn_gate), w_ffn_up=("wu", w_ffn_up, m_w_ffn_up, v_w_ffn_up),
               w_ffn_down=("wd", w_ffn_down, m_w_ffn_down, v_w_ffn_down), conv_w=("conv", conv_w, m_conv_w, v_conv_w))
    res = {}
    for nm, (key, w, m, v) in big.items():
        res[nm] = [o[None] for o in _adamw(plan.slots[key], w[0], m[0], v[0], name=f"adamw_{nm}")]

    g_small = _pack_small(gs["norm_mix"], gs["norm_ffn"], gs["gdn_norm"], gs["fox_q"], gs["fox_k"],
                          gs["a_row"][:, HEADS:2 * HEADS], gs["b_row"][:, HEADS:2 * HEADS],
                          gs["b_row"][:, 2 * HEADS:3 * HEADS])
    (g_small_all,) = _all_gather([g_small], name="small_grads_all_gather")
    w_small_p = _pack_small(norm_mix_w, norm_ffn_w, gdn_norm_w, fox_q_norm_w, fox_k_norm_w, a_log, dt_bias, fox_b_f)
    m_small_p = _pack_small(m_norm_mix_w, m_norm_ffn_w, m_gdn_norm_w, m_fox_q_norm_w, m_fox_k_norm_w, m_a_log,
                            m_dt_bias, m_fox_b_f)
    v_small_p = _pack_small(v_norm_mix_w, v_norm_ffn_w, v_gdn_norm_w, v_fox_q_norm_w, v_fox_k_norm_w, v_a_log,
                            v_dt_bias, v_fox_b_f)
    small_res = [_unpack_small(o, d) for o in _adamw(g_small_all, w_small_p, m_small_p, v_small_p, name="adamw_small")]

    def group(k):
        s = small_res[k]
        return [s[0], res["w_in"][k], res["conv_w"][k], s[1], s[2], s[3], s[4], s[5], s[6], res["w_branch_a"][k],
                res["w_branch_b"][k], res["w_out"][k], s[7], res["w_ffn_gate"][k], res["w_ffn_up"][k],
                res["w_ffn_down"][k]]

    return (loss, grad_x[None], *group(0), *group(1), *group(2), *group(3))
```

```python
import functools
import math

import jax
import jax.numpy as jnp
from jax import lax
from jax.experimental import pallas as pl
from jax.experimental.pallas import tpu as pltpu

F32 = jnp.float32
BF16 = jnp.bfloat16
HI = lax.Precision.HIGHEST
SOLVE_PRECISION = lax.Precision.HIGH
MESH = pl.DeviceIdType.MESH

EPS = 1e-6
HEADS = 16
DH = 128
HW = HEADS * DH
CHUNK = 64
CONV_K = 4
N_DEV = 8
LANES = 128
VMEM_LIMIT = 52 * 1024 * 1024

ADAM_LR = 0.001
ADAM_B1 = 0.9
ADAM_B2 = 0.999
ADAM_EPS = 1e-08
ADAM_WD = 0.01
ADAM_STEP = 10

OFF_QA, OFF_KA, OFF_VA, OFF_ZA, OFF_QB, OFF_KB, OFF_VB, OFF_GA = 0, HW, 2 * HW, 3 * HW, 4 * HW, 5 * HW, 6 * HW, 7 * HW


def _cparams(sem=None, vmem=VMEM_LIMIT):
    return pltpu.CompilerParams(dimension_semantics=sem, vmem_limit_bytes=vmem)


class _Comm:
    def __init__(self, ins, out_shapes, sems, start, finish, aliases=None):
        self.ins, self.out_shapes, self.sems = list(ins), list(out_shapes), list(sems)
        self.start, self.finish, self.aliases = start, finish, dict(aliases or {})


def _pcall(body, *, name, grid, in_specs, out_specs, out_shape, args, sem, scratch_shapes=(), comm=None):
    multi = isinstance(out_shape, (list, tuple))
    out_specs = list(out_specs) if multi else [out_specs]
    out_shape = list(out_shape) if multi else [out_shape]
    scratch_shapes = list(scratch_shapes)
    if comm is None:
        res = pl.pallas_call(body, name=name, grid=grid, in_specs=list(in_specs), out_specs=out_specs,
                             out_shape=out_shape, scratch_shapes=scratch_shapes, compiler_params=_cparams(sem))(*args)
        return res if multi else res[0]
    ni, no, ns = len(in_specs), len(out_specs), len(scratch_shapes)
    ci, co = len(comm.ins), len(comm.out_shapes)

    def wrapped(*refs):
        cin = refs[ni:ni + ci]
        outs = refs[ni + ci:ni + ci + no]
        cout = refs[ni + ci + no:ni + ci + no + co]
        scr = refs[ni + ci + no + co:ni + ci + no + co + ns]
        csem = refs[ni + ci + no + co + ns:]
        ids = [pl.program_id(ax) for ax in range(len(grid))]
        first = functools.reduce(jnp.logical_and, [i == 0 for i in ids])
        last = functools.reduce(jnp.logical_and, [i == g - 1 for i, g in zip(ids, grid)])

        @pl.when(first)
        def _():
            comm.start(cin, cout, csem)

        body(*refs[:ni], *outs, *scr)

        @pl.when(last)
        def _():
            comm.finish(cin, cout, csem)

    any_spec = pl.BlockSpec(memory_space=pl.ANY)
    res = pl.pallas_call(
        wrapped, name=name, grid=grid, in_specs=list(in_specs) + [any_spec] * ci,
        out_specs=out_specs + [any_spec] * co, out_shape=out_shape + comm.out_shapes,
        scratch_shapes=scratch_shapes + comm.sems,
        input_output_aliases={ni + i: no + o for i, o in comm.aliases.items()},
        compiler_params=_cparams(("arbitrary",) * len(grid)))(*args, *comm.ins)
    return (res[:no] if multi else res[0]), res[no:]


def _mm(a, b, *, mode, m, n, k, tm, tn, tk, out_dtype, name, a_off=(0, 0), b_off=(0, 0), add=None,
        b_blocked=False, out_blocked=False, comm=None, epilogue=None, tiles=(), out_dtypes=()):
    tm, tn, tk = min(tm, m), min(tn, n), min(tk, k)
    assert m % tm == 0 and n % tn == 0 and k % tk == 0, (name, m, n, k, tm, tn, tk)
    nk = k // tk
    if mode == "nn":
        a_blk, b_blk = (tm, tk), (tk, tn)
        ao, bo = (a_off[0] // tm, a_off[1] // tk), (b_off[0] // tk, b_off[1] // tn)
        a_map = lambda i, j, kk: (i + ao[0], kk + ao[1])
        b_map = lambda i, j, kk: (kk + bo[0], j + bo[1])
        dims = (((1,), (0,)), ((), ()))
        if b_blocked:
            assert b.shape == (n // tn, k, tn) and b_off == (0, 0), (name, b.shape)
            b_blk, b_map = (None, tk, tn), lambda i, j, kk: (j, kk, 0)
    elif mode == "nt":
        a_blk, b_blk = (tm, tk), (tn, tk)
        ao, bo = (a_off[0] // tm, a_off[1] // tk), (b_off[0] // tn, b_off[1] // tk)
        a_map = lambda i, j, kk: (i + ao[0], kk + ao[1])
        b_map = lambda i, j, kk: (j + bo[0], kk + bo[1])
        dims = (((1,), (1,)), ((), ()))
        if b_blocked and tk == k:
            kblocks, cblk = b.shape[0], b.shape[2]
            assert b.shape == (kblocks, n, cblk) and kblocks * cblk == k and b_off == (0, 0), (name, b.shape)
            b_blk, b_map = (kblocks, tn, cblk), lambda i, j, kk: (0, j, 0)
        elif b_blocked:
            assert b.shape == (nk, n, tk) and b_off == (0, 0), (name, b.shape)
            b_blk, b_map = (None, tn, tk), lambda i, j, kk: (kk, j, 0)
    else:
        assert not b_blocked
        a_blk, b_blk = (tk, tm), (tk, tn)
        ao, bo = (a_off[0] // tk, a_off[1] // tm), (b_off[0] // tk, b_off[1] // tn)
        a_map = lambda i, j, kk: (kk + ao[0], i + ao[1])
        b_map = lambda i, j, kk: (kk + bo[0], j + bo[1])
        dims = (((0,), (0,)), ((), ()))
    if not b_blocked:
        for off, blk in ((a_off, a_blk), (b_off, b_blk)):
            assert off[0] % blk[0] == 0 and off[1] % blk[1] == 0, (name, off, blk)
    has_add = add is not None
    n_tiles, n_outs = len(tiles), len(out_dtypes) if epilogue is not None else 1
    assert epilogue is None or (nk == 1 and not out_blocked and not has_add), name

    def body(*refs):
        a_ref, b_ref = refs[:2]
        c_ref = refs[2] if has_add else None
        tile_refs = refs[2 + has_add:2 + has_add + n_tiles]
        out_refs = refs[2 + has_add + n_tiles:2 + has_add + n_tiles + n_outs]
        o_ref, acc = out_refs[0], refs[-1]
        if len(b_blk) == 3 and b_blk[0] is not None:
            cblk = b_blk[2]
            p = sum(lax.dot_general(a_ref[:, q * cblk:(q + 1) * cblk].astype(BF16), b_ref[q].astype(BF16), dims,
                                    preferred_element_type=F32) for q in range(b_blk[0]))
        else:
            p = lax.dot_general(a_ref[...].astype(BF16), b_ref[...].astype(BF16), dims, preferred_element_type=F32)
        if epilogue is not None:
            for ref, val in zip(out_refs, epilogue(p, *[r[...].astype(F32) for r in tile_refs])):
                ref[...] = val.astype(ref.dtype)
        elif nk == 1:
            if has_add:
                p = p + c_ref[...].astype(F32)
            o_ref[...] = p.astype(o_ref.dtype)
        else:
            kk = pl.program_id(2)

            @pl.when(kk == 0)
            def _():
                acc[...] = p + c_ref[...].astype(F32) if has_add else p

            @pl.when(kk > 0)
            def _():
                acc[...] += p

            @pl.when(kk == nk - 1)
            def _():
                o_ref[...] = acc[...].astype(o_ref.dtype)

    in_specs = [pl.BlockSpec(a_blk, a_map), pl.BlockSpec(b_blk, b_map)]
    args = [a, b]
    if has_add:
        in_specs.append(pl.BlockSpec((tm, tn), lambda i, j, kk: (i, j)))
        args.append(add)
    for arr, off in tiles:
        assert off % tn == 0, (name, off, tn)
        in_specs.append(pl.BlockSpec((tm, tn), lambda i, j, kk, ob=off // tn: (i, j + ob)))
        args.append(arr)
    acc_shape = (tm, tn) if nk > 1 else (8, LANES)
    if out_blocked:
        out_spec = pl.BlockSpec((None, tm, tn), lambda i, j, kk: (j, i, 0))
        out_shape = jax.ShapeDtypeStruct((n // tn, m, tn), out_dtype)
    else:
        out_spec = pl.BlockSpec((tm, tn), lambda i, j, kk: (i, j))
        out_shape = jax.ShapeDtypeStruct((m, n), out_dtype)
    if epilogue is not None:
        out_spec = [out_spec] * n_outs
        out_shape = [jax.ShapeDtypeStruct((m, n), dt) for dt in out_dtypes]
    return _pcall(body, name=name, grid=(m // tm, n // tn, nk), in_specs=in_specs, out_specs=out_spec,
                  out_shape=out_shape, scratch_shapes=[pltpu.VMEM(acc_shape, F32)], args=args,
                  sem=("parallel", "parallel", "arbitrary"), comm=comm)


def _row_specs(rows, tm, ncol):
    specs = []
    for arr, off, width in rows:
        bw = width // ncol
        assert width % ncol == 0 and off % bw == 0, (off, width, ncol)
        ob = off // bw
        specs.append(pl.BlockSpec((tm, bw), lambda i, j, ob=ob: (i, j + ob)))
    return specs


def _col_group(ref, s, inner):
    w = ref.shape[1] // inner
    return slice(None), slice(s * w, (s + 1) * w)


def _rowwise_fwd(fn, rows, params, outs, *, tm, ncol=1, inner=1, name):
    t = rows[0][0].shape[0]
    tm = min(tm, t)
    nr, npar = len(rows), len(params)

    def body(*refs):
        par = [r[...].astype(F32) for r in refs[nr:nr + npar]]
        for s in range(inner):
            res = fn(*[r[_col_group(r, s, inner)].astype(F32) for r in refs[:nr]], *par)
            for o_ref, val in zip(refs[nr + npar:], res):
                o_ref[_col_group(o_ref, s, inner)] = val.astype(o_ref.dtype)

    in_specs = _row_specs(rows, tm, ncol) + [pl.BlockSpec(p.shape, lambda i, j: (0, 0)) for p in params]
    out_specs = [pl.BlockSpec((tm, w // ncol), lambda i, j: (i, j)) for w, _ in outs]
    out_shape = [jax.ShapeDtypeStruct((t, w), dt) for w, dt in outs]
    return pl.pallas_call(
        body, name=name, grid=(t // tm, ncol), in_specs=in_specs, out_specs=out_specs, out_shape=out_shape,
        compiler_params=_cparams(("parallel", "parallel")),
    )(*[r[0] for r in rows], *params)


def _rowwise_bwd(fn, rows, params, cts, grad_dtypes, *, tm, ncol=1, inner=1, name, adds=None, bf16_copy_of=None):
    t = rows[0][0].shape[0]
    tm = min(tm, t)
    nr, npar, nct = len(rows), len(params), len(cts)
    adds = adds or [None] * nr
    add_idx = [i for i, a in enumerate(adds) if a is not None]

    def body(*refs):
        par = [r[...].astype(F32) for r in refs[nr:nr + npar]]
        ct_refs = refs[nr + npar:nr + npar + nct]
        add_refs = refs[nr + npar + nct:nr + npar + nct + len(add_idx)]
        outs = refs[nr + npar + nct + len(add_idx):]
        extra = dict(zip(add_idx, add_refs))
        par_grads = [None] * npar
        for s in range(inner):
            ins = [r[_col_group(r, s, inner)].astype(F32) for r in refs[:nr]]
            _, vjp = jax.vjp(lambda *a: tuple(fn(*a)), *ins, *par)
            grads = vjp(tuple(r[_col_group(r, s, inner)].astype(F32) for r in ct_refs))
            for i in range(nr):
                g = grads[i]
                if i in extra:
                    g = g + extra[i][_col_group(extra[i], s, inner)].astype(F32)
                outs[i][_col_group(outs[i], s, inner)] = g.astype(outs[i].dtype)
                if i == bf16_copy_of:
                    outs[nr + npar][_col_group(outs[nr + npar], s, inner)] = g.astype(BF16)
            par_grads = [g if acc is None else acc + g for acc, g in zip(par_grads, grads[nr:])]
        first = jnp.logical_and(pl.program_id(0) == 0, pl.program_id(1) == 0)
        for pi in range(npar):
            o_ref = outs[nr + pi]
            g = par_grads[pi]

            @pl.when(first)
            def _(o_ref=o_ref, g=g):
                o_ref[...] = g

            @pl.when(jnp.logical_not(first))
            def _(o_ref=o_ref, g=g):
                o_ref[...] += g

    in_specs = (_row_specs(rows, tm, ncol)
                + [pl.BlockSpec(p.shape, lambda i, j: (0, 0)) for p in params]
                + [pl.BlockSpec((tm, c.shape[1] // ncol), lambda i, j: (i, j)) for c in cts]
                + [pl.BlockSpec((tm, adds[i].shape[1] // ncol), lambda i, j: (i, j)) for i in add_idx])
    out_specs = ([pl.BlockSpec((tm, w // ncol), lambda i, j: (i, j)) for _, _, w in rows]
                 + [pl.BlockSpec(p.shape, lambda i, j: (0, 0)) for p in params])
    out_shape = ([jax.ShapeDtypeStruct((t, w), dt) for (_, _, w), dt in zip(rows, grad_dtypes)]
                 + [jax.ShapeDtypeStruct(p.shape, F32) for p in params])
    if bf16_copy_of is not None:
        w = rows[bf16_copy_of][2]
        out_specs.append(pl.BlockSpec((tm, w // ncol), lambda i, j: (i, j)))
        out_shape.append(jax.ShapeDtypeStruct((t, w), BF16))
    return pl.pallas_call(
        body, name=name, grid=(t // tm, ncol), in_specs=in_specs, out_specs=out_specs, out_shape=out_shape,
        compiler_params=_cparams(("arbitrary", "arbitrary")),
    )(*[r[0] for r in rows], *params, *cts, *[adds[i] for i in add_idx])


def _rms(x, w):
    return x * lax.rsqrt(jnp.mean(x * x, axis=-1, keepdims=True) + EPS) * w


def _fn_norm(x, w):
    return (_rms(x, w),)


def _fn_gates(z, a_row, b_row):
    lane = lax.broadcasted_iota(jnp.int32, z.shape, 1)
    beta = jax.nn.sigmoid(z)
    g = -jnp.exp(a_row) * jax.nn.softplus(z + b_row)
    logf = jax.nn.log_sigmoid(z + b_row)
    return (jnp.where(lane < HEADS, beta, jnp.where(lane < 2 * HEADS, g, jnp.where(lane < 3 * HEADS, logf, 0.0))),)


def _fn_qknorm(q, k, qw, kw):
    return _rms(q, qw) * (DH ** -0.5 * LOG2E), _rms(k, kw)


def _fn_gdn_out(o, z, w):
    return (_rms(o, w) * jax.nn.silu(z),)


def _fn_merge(ga, gb, ya, yb):
    return (jax.nn.sigmoid(ga) * ya + jax.nn.sigmoid(gb) * yb,)


def _fn_swiglu(g, u):
    return (jax.nn.silu(g) * u,)


def _bf16_round(x):
    return x.astype(BF16).astype(F32)


def _loss_head(y, target, *, tm, name):
    t, d = y.shape
    tm = min(tm, t)

    def body(y_ref, t_ref, dyf_ref, dyb_ref, loss_ref):
        err = y_ref[...] - t_ref[...]
        dy = err * (1.0 / d)
        dyf_ref[...] = dy
        dyb_ref[...] = dy.astype(BF16)
        part = jnp.sum(err * err) * (0.5 / d)

        @pl.when(pl.program_id(0) == 0)
        def _():
            loss_ref[...] = jnp.zeros_like(loss_ref)

        loss_ref[...] += part

    blk = pl.BlockSpec((tm, d), lambda i: (i, 0))
    return pl.pallas_call(
        body, name=name, grid=(t // tm,), in_specs=[blk, blk],
        out_specs=[blk, blk, pl.BlockSpec((1, LANES), lambda i: (0, 0))],
        out_shape=[jax.ShapeDtypeStruct((t, d), F32), jax.ShapeDtypeStruct((t, d), BF16),
                   jax.ShapeDtypeStruct((1, LANES), F32)],
        compiler_params=_cparams(("arbitrary",)),
    )(y, target)


def _shift_down(x, s):
    if s == 0:
        return x
    row = lax.broadcasted_iota(jnp.int32, x.shape, 0)
    return jnp.where(row >= s, pltpu.roll(x, s, 0), 0.0)


def _shift_up(x, s):
    if s == 0:
        return x
    t = x.shape[0]
    row = lax.broadcasted_iota(jnp.int32, x.shape, 0)
    return jnp.where(row < t - s, pltpu.roll(x, t - s, 0), 0.0)


def _conv_pre(x, w):
    y = x * w[CONV_K - 1:CONV_K, :]
    for i in range(CONV_K - 1):
        y = y + _shift_down(x, CONV_K - 1 - i) * w[i:i + 1, :]
    return y


def _conv_fwd(p_main, conv_w, *, width, name):
    t = p_main.shape[0]
    tc = LANES

    def body(x_ref, w_ref, o_ref):
        y = _conv_pre(x_ref[...].astype(F32), w_ref[...])
        o_ref[...] = y * jax.nn.sigmoid(y)

    return pl.pallas_call(
        body, name=name, grid=(width // tc,),
        in_specs=[pl.BlockSpec((t, tc), lambda j: (0, j)), pl.BlockSpec((CONV_K, tc), lambda j: (0, j))],
        out_specs=pl.BlockSpec((t, tc), lambda j: (0, j)),
        out_shape=jax.ShapeDtypeStruct((t, width), F32),
        compiler_params=_cparams(("parallel",)),
    )(p_main, conv_w)


def _conv_bwd(p_main, conv_w, dy, *, width, name):
    t = p_main.shape[0]
    tc = LANES

    def body(x_ref, w_ref, dy_ref, dx_ref, dw_ref):
        x = x_ref[...].astype(F32)
        w = w_ref[...]
        pre = _conv_pre(x, w)
        sg = jax.nn.sigmoid(pre)
        dpre = dy_ref[...] * (sg * (1.0 + pre * (1.0 - sg)))
        dx = dpre * w[CONV_K - 1:CONV_K, :]
        dws = []
        for i in range(CONV_K - 1):
            s = CONV_K - 1 - i
            dx = dx + _shift_up(dpre, s) * w[i:i + 1, :]
            dws.append(jnp.sum(_shift_down(x, s) * dpre, axis=0, keepdims=True))
        dws.append(jnp.sum(x * dpre, axis=0, keepdims=True))
        dx_ref[...] = dx.astype(dx_ref.dtype)
        dw_ref[...] = jnp.concatenate(dws, axis=0)

    return pl.pallas_call(
        body, name=name, grid=(width // tc,),
        in_specs=[pl.BlockSpec((t, tc), lambda j: (0, j)), pl.BlockSpec((CONV_K, tc), lambda j: (0, j)),
                  pl.BlockSpec((t, tc), lambda j: (0, j))],
        out_specs=[pl.BlockSpec((t, tc), lambda j: (0, j)), pl.BlockSpec((CONV_K, tc), lambda j: (0, j))],
        out_shape=[jax.ShapeDtypeStruct((t, width), BF16), jax.ShapeDtypeStruct((CONV_K, width), F32)],
        compiler_params=_cparams(("parallel",)),
    )(p_main, conv_w, dy)


def _bmm(a, b, spec, precision=None):
    return jnp.einsum(spec, a, b, preferred_element_type=F32, precision=precision)


def _iota2(shape, dim):
    return lax.broadcasted_iota(jnp.int32, shape, dim)


@jax.custom_vjp
def _tri_inverse(a):
    return _tri_inverse_levels(a)


def _tri_inverse_fwd(a):
    t = _tri_inverse_levels(a)
    return t, t


def _tri_inverse_bwd(t, g):
    x = _bmm(t, g, "hji,hjk->hik", SOLVE_PRECISION)
    return (-_bmm(x, t, "hik,hjk->hij", SOLVE_PRECISION),)


_tri_inverse.defvjp(_tri_inverse_fwd, _tri_inverse_bwd)


def _tri_inverse_levels(a):
    c = a.shape[-1]
    r, m = _iota2((c, c), 0), _iota2((c, c), 1)
    eye = (r == m).astype(F32)
    inv = None
    b = 1
    while b < c:
        mask = jnp.logical_and(r // (2 * b) == m // (2 * b), jnp.logical_and(r % (2 * b) >= b, m % (2 * b) < b))
        off = jnp.where(mask[None], a, 0.0)
        if inv is None:
            inv = eye[None] - off
        else:
            inv = inv - _bmm(_bmm(inv, off, "hij,hjk->hik", SOLVE_PRECISION), inv, "hij,hjk->hik", SOLVE_PRECISION)
        b *= 2
    return inv


def _gdn_chunk(s, q3, k3, v3, b3, gc3):
    c = q3.shape[1]
    r, m = _iota2((c, c), 0), _iota2((c, c), 1)
    tril_incl = (r >= m)[None]
    tril_strict = (r > m)[None]
    eye = (r == m).astype(F32)[None]
    qn = q3 * lax.rsqrt(jnp.sum(q3 * q3, axis=-1, keepdims=True) + EPS) * (DH ** -0.5)
    kn = k3 * lax.rsqrt(jnp.sum(k3 * k3, axis=-1, keepdims=True) + EPS)
    ones = jnp.ones((q3.shape[0], c, c), F32)
    gc_row = _bmm(ones, gc3 * eye, "hij,hjk->hik", SOLVE_PRECISION)
    decay = jnp.where(tril_incl, jnp.exp(jnp.where(tril_incl, gc3 - gc_row, 0.0)), 0.0)
    a = jnp.where(tril_strict, _bmm(kn, kn, "hcd,hmd->hcm") * decay * b3, 0.0)
    tinv = _tri_inverse(a)
    egc = jnp.exp(gc3)
    u = _bmm(tinv, v3 * b3, "hij,hjk->hik", SOLVE_PRECISION)
    w = _bmm(tinv, kn * (b3 * egc), "hij,hjk->hik", SOLVE_PRECISION)
    qk = _bmm(qn, kn, "hcd,hmd->hcm") * decay
    v_new = u - _bmm(w, s, "hcd,hdv->hcv")
    o = _bmm(qn * egc, s, "hcd,hdv->hcv") + _bmm(qk, v_new, "hcm,hmv->hcv")
    row = _iota2((c, 1), 0)[None]
    g_last = jnp.sum(jnp.where(row == c - 1, gc3, 0.0), axis=1, keepdims=True)
    s_new = s * jnp.exp(g_last) + _bmm(kn * jnp.exp(g_last - gc3), v_new, "hcd,hcv->hdv")
    return s_new, o


GDN_HEAD_GROUP = 16


def _split_heads(ref, off, h0):
    return jnp.stack([ref[:, off + h * DH:off + (h + 1) * DH].astype(F32)
                      for h in range(h0, h0 + GDN_HEAD_GROUP)], axis=0)


def _store_heads(ref, x3, off, h0):
    for i in range(GDN_HEAD_GROUP):
        h = h0 + i
        ref[:, off + h * DH:off + (h + 1) * DH] = x3[i].astype(ref.dtype)


def _lane_cols(tile, lane0):
    lane = _iota2(tile.shape, 1)
    return jnp.stack([jnp.sum(jnp.where(lane == lane0 + i, tile, 0.0), axis=1, keepdims=True)
                      for i in range(GDN_HEAD_GROUP)], axis=0)


def _cols_to_lanes(cols3, lane0, shape):
    lane = _iota2(shape, 1)
    out = jnp.zeros(shape, F32)
    for i in range(GDN_HEAD_GROUP):
        out = out + jnp.where(lane == lane0 + i, cols3[i], 0.0)
    return out


def _chunk_cumsum_matrix():
    r, m = _iota2((CHUNK, CHUNK), 0), _iota2((CHUNK, CHUNK), 1)
    return (r >= m).astype(F32)


def _gdn_inputs(qkv_ref, gt, gcum, h0):
    return (_split_heads(qkv_ref, 0, h0), _split_heads(qkv_ref, HW, h0), _split_heads(qkv_ref, 2 * HW, h0),
            _lane_cols(gt, h0), _lane_cols(gcum, HEADS + h0))


def _gdn_fwd(qkv, gates, *, name, comm=None):
    t = qkv.shape[0]
    n = t // CHUNK

    def body(qkv_ref, gt_ref, o_ref, sall_ref, s_scr):
        @pl.when(pl.program_id(0) == 0)
        def _():
            s_scr[...] = jnp.zeros_like(s_scr)

        gt = gt_ref[...]
        gcum = jnp.dot(_chunk_cumsum_matrix(), gt, preferred_element_type=F32, precision=HI)
        for h0 in range(0, HEADS, GDN_HEAD_GROUP):
            grp = pl.ds(h0, GDN_HEAD_GROUP)
            s = s_scr[grp]
            sall_ref[0, grp] = s
            s_new, o3 = _gdn_chunk(s, *_gdn_inputs(qkv_ref, gt, gcum, h0))
            s_scr[grp] = s_new
            _store_heads(o_ref, o3, 0, h0)

    return _pcall(
        body, name=name, grid=(n,),
        in_specs=[pl.BlockSpec((CHUNK, 3 * HW), lambda i: (i, 0)), pl.BlockSpec((CHUNK, LANES), lambda i: (i, 0))],
        out_specs=[pl.BlockSpec((CHUNK, HW), lambda i: (i, 0)),
                   pl.BlockSpec((1, HEADS, DH, DH), lambda i: (i, 0, 0, 0))],
        out_shape=[jax.ShapeDtypeStruct((t, HW), F32), jax.ShapeDtypeStruct((n, HEADS, DH, DH), F32)],
        scratch_shapes=[pltpu.VMEM((HEADS, DH, DH), F32)], sem=("arbitrary",), args=(qkv, gates), comm=comm)


def _gdn_bwd(qkv, gates, s_all, do, *, name, comm=None):
    t = qkv.shape[0]
    n = t // CHUNK

    def body(qkv_ref, gt_ref, sall_ref, do_ref, dqkv_ref, dgt_ref, ds_scr):
        @pl.when(pl.program_id(0) == 0)
        def _():
            ds_scr[...] = jnp.zeros_like(ds_scr)

        gt = gt_ref[...]
        cum = _chunk_cumsum_matrix()
        gcum = jnp.dot(cum, gt, preferred_element_type=F32, precision=HI)
        shape = (CHUNK, LANES)
        dbeta = jnp.zeros(shape, F32)
        dgcum = jnp.zeros(shape, F32)
        for h0 in range(0, HEADS, GDN_HEAD_GROUP):
            grp = pl.ds(h0, GDN_HEAD_GROUP)
            _, vjp = jax.vjp(_gdn_chunk, sall_ref[0, grp], *_gdn_inputs(qkv_ref, gt, gcum, h0))
            ds, dq3, dk3, dv3, db3, dgc3 = vjp((ds_scr[grp], _split_heads(do_ref, 0, h0)))
            ds_scr[grp] = ds
            _store_heads(dqkv_ref, dq3, 0, h0)
            _store_heads(dqkv_ref, dk3, HW, h0)
            _store_heads(dqkv_ref, dv3, 2 * HW, h0)
            dbeta = dbeta + _cols_to_lanes(db3, h0, shape)
            dgcum = dgcum + _cols_to_lanes(dgc3, HEADS + h0, shape)
        dg = lax.dot_general(cum, dgcum, (((0,), (0,)), ((), ())), preferred_element_type=F32, precision=HI)
        dgt_ref[...] = dbeta + dg

    rev = lambda i: n - 1 - i
    return _pcall(
        body, name=name, grid=(n,),
        in_specs=[pl.BlockSpec((CHUNK, 3 * HW), lambda i: (rev(i), 0)), pl.BlockSpec((CHUNK, LANES), lambda i: (rev(i), 0)),
                  pl.BlockSpec((1, HEADS, DH, DH), lambda i: (rev(i), 0, 0, 0)),
                  pl.BlockSpec((CHUNK, HW), lambda i: (rev(i), 0))],
        out_specs=[pl.BlockSpec((CHUNK, 3 * HW), lambda i: (rev(i), 0)), pl.BlockSpec((CHUNK, LANES), lambda i: (rev(i), 0))],
        out_shape=[jax.ShapeDtypeStruct((t, 3 * HW), F32), jax.ShapeDtypeStruct((t, LANES), F32)],
        scratch_shapes=[pltpu.VMEM((HEADS, DH, DH), F32)], sem=("arbitrary",), args=(qkv, gates, s_all, do), comm=comm)


FOX_BLK = 1024
LOG2E = math.log2(math.e)
FOX_ROW_SPLIT = 1
NEG = -1e30


def _fox_cumsum(gates, *, name):
    t = gates.shape[0]
    blk = min(FOX_BLK, t)

    def body(g_ref, c_ref):
        r, m = _iota2((blk, blk), 0), _iota2((blk, blk), 1)
        upper = (r <= m).astype(F32)
        carry = jnp.zeros((HEADS, 1), F32)
        for b in range(t // blk):
            lf = g_ref[b * blk:(b + 1) * blk, :].T[2 * HEADS:3 * HEADS, :]
            c = jnp.dot(lf, upper, preferred_element_type=F32, precision=HI) + carry
            c_ref[:, b * blk:(b + 1) * blk] = c * LOG2E
            carry = carry + jnp.sum(lf, axis=1, keepdims=True)

    return pl.pallas_call(body, name=name, out_shape=jax.ShapeDtypeStruct((HEADS, t), F32),
                          compiler_params=_cparams())(gates)


def _fox_cumsum_bwd(dc, dgates_gdn, *, name):
    t = dc.shape[1]
    blk = min(FOX_BLK, t)

    def body(dc_ref, dg_ref, o_ref):
        r, m = _iota2((blk, blk), 0), _iota2((blk, blk), 1)
        lower = (r >= m).astype(F32)
        carry = jnp.zeros((HEADS, 1), F32)
        for b in reversed(range(t // blk)):
            d = dc_ref[:, b * blk:(b + 1) * blk]
            dlf = jnp.dot(d, lower, preferred_element_type=F32, precision=HI) + carry
            carry = carry + jnp.sum(d, axis=1, keepdims=True)
            tile = jnp.concatenate([jnp.zeros((2 * HEADS, blk), F32), dlf,
                                    jnp.zeros((LANES - 3 * HEADS, blk), F32)], axis=0)
            o_ref[b * blk:(b + 1) * blk, :] = tile.T + dg_ref[b * blk:(b + 1) * blk, :]

    return pl.pallas_call(body, name=name, out_shape=jax.ShapeDtypeStruct((t, LANES), F32),
                          compiler_params=_cparams())(dc, dgates_gdn)


def _fox_logits(q, k, c_row, row0=None):
    s = lax.dot_general(q, k, (((1,), (1,)), ((), ())), preferred_element_type=F32) - c_row
    if row0 is None:
        return s
    return jnp.where(row0 + _iota2(s.shape, 0) >= _iota2(s.shape, 1), s, NEG)


def _fox_fwd(qn, kn, p_main, c4, *, v_off, name, comm=None):
    t = qn.shape[0]
    blk = min(FOX_BLK, t)
    nb = t // blk
    vb = v_off // DH

    def body(q_ref, k_ref, v_ref, c_ref, o_ref, o32_ref, lse_ref):
        qi = pl.program_id(1)
        q = q_ref[...]

        def step(j, carry, diagonal=False):
            m, l, acc = carry
            rows = pl.ds(pl.multiple_of(j * blk, blk), blk)
            s = _fox_logits(q, k_ref[rows, :], c_ref[0, j], 0 if diagonal else None)
            m_new = jnp.maximum(m, jnp.max(s, axis=1, keepdims=True))
            p = jnp.exp2(s - m_new)
            scale = jnp.exp2(m - m_new)
            l = scale * l + jnp.sum(p, axis=1, keepdims=True)
            acc = scale * acc + jnp.dot(p.astype(BF16), v_ref[rows, :], preferred_element_type=F32)
            return m_new, l, acc

        init = (jnp.full((blk, 1), NEG, F32), jnp.zeros((blk, 1), F32), jnp.zeros((blk, DH), F32))
        m, l, acc = step(qi, lax.fori_loop(0, qi, step, init), diagonal=True)
        o = acc / l
        o_ref[...] = o.astype(o_ref.dtype)
        o32_ref[...] = o
        lse_ref[0] = m + jnp.log(l) * LOG2E

    return _pcall(
        body, name=name, grid=(HEADS, nb),
        in_specs=[pl.BlockSpec((blk, DH), lambda h, i: (i, h)), pl.BlockSpec((t, DH), lambda h, i: (0, h)),
                  pl.BlockSpec((t, DH), lambda h, i: (0, vb + h)), pl.BlockSpec((1, nb, 1, blk), lambda h, i: (h, 0, 0, 0))],
        out_specs=[pl.BlockSpec((blk, DH), lambda h, i: (i, h)), pl.BlockSpec((blk, DH), lambda h, i: (i, h)),
                   pl.BlockSpec((1, blk, 1), lambda h, i: (h, i, 0))],
        out_shape=[jax.ShapeDtypeStruct((t, HW), BF16), jax.ShapeDtypeStruct((t, HW), F32),
                   jax.ShapeDtypeStruct((HEADS, t, 1), F32)],
        sem=("parallel", "arbitrary"), args=(qn, kn, p_main, c4), comm=comm)


def _fox_bwd(qn, kn, p_main, c4, o32, do, lse, *, v_off, name, comm=None):
    t = qn.shape[0]
    blk = min(FOX_BLK, t)
    nb = t // blk
    vb = v_off // DH
    sub = blk // FOX_ROW_SPLIT
    tn_dims = (((0,), (0,)), ((), ()))
    nt_dims = (((1,), (1,)), ((), ()))

    def body(q_ref, k_ref, v_ref, c_ref, o_ref, do_ref, lse_ref, dq_ref, dk_ref, dv_ref, dc_ref, dcq_ref):
        kj = pl.program_id(1)

        @pl.when(kj == 0)
        def _():
            dq_ref[...] = jnp.zeros_like(dq_ref)
            dcq_ref[...] = jnp.zeros_like(dcq_ref)

        k = k_ref[...]
        v = v_ref[...]
        c_row = c_ref[0, 0]

        def step(i, carry, diagonal=False):
            dk, dv, dc = carry
            for u in range(FOX_ROW_SPLIT):
                rows = pl.ds(pl.multiple_of(i * blk + u * sub, sub), sub)
                q = q_ref[rows, :]
                dob = do_ref[rows, :]
                p = jnp.exp2(_fox_logits(q, k, c_row, u * sub if diagonal else None) - lse_ref[0, rows, :])
                pb = p.astype(BF16)
                dv = dv + lax.dot_general(pb, dob, tn_dims, preferred_element_type=F32)
                dp = lax.dot_general(dob, v, nt_dims, preferred_element_type=F32)
                delta = jnp.sum(dob.astype(F32) * o_ref[rows, :], axis=1, keepdims=True)
                ds = p * (dp - delta)
                dcq_ref[0, rows, :] += jnp.sum(ds, axis=1, keepdims=True)
                dsb = ds.astype(BF16)
                dq_ref[rows, :] += jnp.dot(dsb, k, preferred_element_type=F32) * (1.0 / LOG2E)
                dk = dk + lax.dot_general(dsb, q, tn_dims, preferred_element_type=F32) * (1.0 / LOG2E)
                dc = dc - jnp.sum(ds, axis=0, keepdims=True)
            return dk, dv, dc

        init = (jnp.zeros((blk, DH), F32), jnp.zeros((blk, DH), F32), jnp.zeros((1, blk), F32))
        dk, dv, dc = lax.fori_loop(kj + 1, nb, step, step(kj, init, diagonal=True))
        dk_ref[...] = dk
        dv_ref[...] = dv.astype(dv_ref.dtype)
        dc_ref[0, 0] = dc

    full = lambda h, j: (0, h)
    kvb = lambda h, j: (j, h)
    return _pcall(
        body, name=name, grid=(HEADS, nb), sem=("parallel", "arbitrary"), comm=comm,
        args=(qn, kn, p_main, c4, o32, do, lse),
        in_specs=[pl.BlockSpec((t, DH), full), pl.BlockSpec((blk, DH), kvb),
                  pl.BlockSpec((blk, DH), lambda h, j: (j, vb + h)), pl.BlockSpec((1, 1, 1, blk), lambda h, j: (h, j, 0, 0)),
                  pl.BlockSpec((t, DH), full), pl.BlockSpec((t, DH), full),
                  pl.BlockSpec((1, t, 1), lambda h, j: (h, 0, 0))],
        out_specs=[pl.BlockSpec((t, DH), full), pl.BlockSpec((blk, DH), kvb), pl.BlockSpec((blk, DH), kvb),
                   pl.BlockSpec((1, 1, 1, blk), lambda h, j: (h, j, 0, 0)), pl.BlockSpec((1, t, 1), lambda h, j: (h, 0, 0))],
        out_shape=[jax.ShapeDtypeStruct((t, HW), F32), jax.ShapeDtypeStruct((t, HW), F32),
                   jax.ShapeDtypeStruct((t, HW), BF16), jax.ShapeDtypeStruct((HEADS, nb, 1, blk), F32),
                   jax.ShapeDtypeStruct((HEADS, t, 1), F32)])


ANY = pl.BlockSpec(memory_space=pl.ANY)


def _mesh_pos():
    return lax.axis_index("x"), lax.axis_index("y"), lax.axis_index("c")


def _all_gather(blocks, *, name):
    n = len(blocks)

    def body(*refs):
        ins, outs = refs[:n], refs[n:2 * n]
        send, recv, local = refs[2 * n:]
        x, y, c = _mesh_pos()
        me, sibling = (x, y, c), (x, y, 1 - c)
        chips = [(1 - x, y), (x, 1 - y), (1 - x, 1 - y)]

        def copy(t, k, block, to, src=None):
            dst = outs[t].at[4 * block[0] + 2 * block[1] + block[2]]
            return pltpu.make_async_remote_copy(
                src_ref=dst if src is None else src, dst_ref=dst, send_sem=send.at[7 * t + k],
                recv_sem=recv.at[7 * t + k], device_id=to, device_id_type=MESH)

        mine = [pltpu.make_async_copy(ins[t], outs[t].at[4 * x + 2 * y + c], local.at[t]) for t in range(n)]
        for cp in mine:
            cp.start()
        first = []
        for t in range(n):
            first.append(copy(t, 0, me, sibling, src=ins[t]))
            first += [copy(t, 1 + j, me, (*chip, c), src=ins[t]) for j, chip in enumerate(chips)]
        for cp in first:
            cp.start()
        passed = []
        for j, chip in enumerate(chips):
            for t in range(n):
                copy(t, 1 + j, (*chip, c), me).wait_recv()
                fwd = copy(t, 4 + j, (*chip, c), sibling)
                fwd.start()
                passed.append(fwd)
        for t in range(n):
            copy(t, 0, sibling, me).wait_recv()
            for j, chip in enumerate(chips):
                copy(t, 4 + j, (*chip, 1 - c), me).wait_recv()
        for cp in first + passed:
            cp.wait_send()
        for cp in mine:
            cp.wait()

    return pl.pallas_call(
        body, name=name, in_specs=[ANY] * n, out_specs=[ANY] * n,
        out_shape=[jax.ShapeDtypeStruct((N_DEV,) + b.shape, b.dtype) for b in blocks],
        scratch_shapes=[pltpu.SemaphoreType.DMA((7 * n,)), pltpu.SemaphoreType.DMA((7 * n,)),
                        pltpu.SemaphoreType.DMA((n,))],
    )(*blocks)


def _all_gather_relayed(block, *, name):
    r = block.shape[0]
    half = r // 2
    assert half * 2 == r and half % 16 == 0, block.shape

    def body(in_ref, out_ref, send, recv, local):
        x, y, c = _mesh_pos()
        me, sibling, xn, yn, dg = (x, y, c), (x, y, 1 - c), (1 - x, y, c), (x, 1 - y, c), (1 - x, 1 - y, c)
        slot = lambda p: 4 * p[0] + 2 * p[1] + p[2]
        rows = {"a": pl.ds(0, half), "b": pl.ds(half, half)}

        def copy(k, src, dst, to):
            return pltpu.make_async_remote_copy(src_ref=src, dst_ref=dst, send_sem=send.at[k], recv_sem=recv.at[k],
                                                device_id=to, device_id_type=MESH)

        def part(p, h=None):
            ref = out_ref.at[slot(p)]
            return ref if h is None else ref.at[rows[h]]

        def landed(k, p, h=None):
            copy(k, part(p, h), part(p, h), me).wait_recv()

        mine = pltpu.make_async_copy(in_ref, part(me), local)
        mine.start()
        first = [copy(0, in_ref, part(me), sibling),
                 copy(1, in_ref.at[rows["a"]], part(me, "a"), xn), copy(2, in_ref.at[rows["b"]], part(me, "b"), xn),
                 copy(3, in_ref.at[rows["a"]], part(me, "a"), yn), copy(4, in_ref.at[rows["b"]], part(me, "b"), yn)]
        for cp in first:
            cp.start()
        landed(1, xn, "a")
        relay_a = copy(5, part(xn, "a"), part(xn, "a"), yn)
        relay_a.start()
        landed(4, yn, "b")
        relay_b = copy(6, part(yn, "b"), part(yn, "b"), xn)
        relay_b.start()
        landed(2, xn, "b")
        pass_x = copy(7, part(xn), part(xn), sibling)
        pass_x.start()
        landed(3, yn, "a")
        pass_y = copy(8, part(yn), part(yn), sibling)
        pass_y.start()
        landed(5, dg, "a")
        landed(6, dg, "b")
        pass_d = copy(9, part(dg), part(dg), sibling)
        pass_d.start()
        landed(0, sibling)
        for k, p in ((7, (1 - x, y, 1 - c)), (8, (x, 1 - y, 1 - c)), (9, (1 - x, 1 - y, 1 - c))):
            landed(k, p)
        for cp in first + [relay_a, relay_b, pass_x, pass_y, pass_d]:
            cp.wait_send()
        mine.wait()

    return pl.pallas_call(
        body, name=name, in_specs=[ANY], out_specs=ANY,
        out_shape=jax.ShapeDtypeStruct((N_DEV,) + block.shape, block.dtype),
        scratch_shapes=[pltpu.SemaphoreType.DMA((10,)), pltpu.SemaphoreType.DMA((10,)), pltpu.SemaphoreType.DMA],
    )(block)


def _comm_call(comm, *, name):
    ci, co = len(comm.ins), len(comm.out_shapes)

    def body(*refs):
        comm.start(refs[:ci], refs[ci:ci + co], refs[ci + co:])
        comm.finish(refs[:ci], refs[ci:ci + co], refs[ci + co:])

    return pl.pallas_call(body, name=name, in_specs=[ANY] * ci, out_specs=[ANY] * co, out_shape=comm.out_shapes,
                          scratch_shapes=comm.sems, input_output_aliases=comm.aliases)(*comm.ins)


def _ag_first_comm(shards, rows=None, into=None):
    n = len(shards)
    rows = rows or [None] * n
    into = into or [None] * n
    carried = [t for t in range(n) if into[t] is not None]

    def copies(cin, cout, sems):
        send, recv, local = sems
        x, y, c = _mesh_pos()
        peers = [(x, y, 1 - c), (1 - x, y, c), (x, 1 - y, c), (1 - x, 1 - y, c)]
        slot = lambda p: 4 * p[0] + 2 * p[1] + p[2]
        mine, out, inc = [], [], []
        for t in range(n):
            part = (lambda ref: ref) if rows[t] is None else (lambda ref, r=rows[t]: ref.at[pl.ds(r[0], r[1])])
            own = part(cout[t].at[slot((x, y, c))])
            mine.append(pltpu.make_async_copy(part(cin[t]), own, local.at[t]))
            for k, peer in enumerate(peers):
                sems_k = dict(send_sem=send.at[4 * t + k], recv_sem=recv.at[4 * t + k], device_id=peer,
                              device_id_type=MESH)
                theirs = part(cout[t].at[slot(peer)])
                out.append(pltpu.make_async_remote_copy(src_ref=part(cin[t]), dst_ref=own, **sems_k))
                inc.append(pltpu.make_async_remote_copy(src_ref=theirs, dst_ref=theirs, **sems_k))
        return mine, out, inc

    def start(cin, cout, sems):
        mine, out, _ = copies(cin, cout, sems)
        for cp in mine + out:
            cp.start()

    def finish(cin, cout, sems):
        mine, out, inc = copies(cin, cout, sems)
        for cp in inc:
            cp.wait_recv()
        for cp in out:
            cp.wait_send()
        for cp in mine:
            cp.wait()

    return _Comm(list(shards) + [into[t] for t in carried],
                 [jax.ShapeDtypeStruct((N_DEV,) + s.shape, s.dtype) for s in shards],
                 [pltpu.SemaphoreType.DMA((4 * n,)), pltpu.SemaphoreType.DMA((4 * n,)), pltpu.SemaphoreType.DMA((n,))],
                 start, finish, aliases={n + i: t for i, t in enumerate(carried)})


def _ag_pass_comm(gathered):
    n = len(gathered)

    def copies(cout, sems):
        send, recv = sems
        x, y, c = _mesh_pos()
        fwd, inc = [], []
        for t in range(n):
            for j, (px, py) in enumerate([(1 - x, y), (x, 1 - y), (1 - x, 1 - y)]):
                sems_j = dict(send_sem=send.at[3 * t + j], recv_sem=recv.at[3 * t + j], device_id=(x, y, 1 - c),
                              device_id_type=MESH)
                mine, theirs = cout[t].at[4 * px + 2 * py + c], cout[t].at[4 * px + 2 * py + 1 - c]
                fwd.append(pltpu.make_async_remote_copy(src_ref=mine, dst_ref=mine, **sems_j))
                inc.append(pltpu.make_async_remote_copy(src_ref=theirs, dst_ref=theirs, **sems_j))
        return fwd, inc

    def start(cin, cout, sems):
        for cp in copies(cout, sems)[0]:
            cp.start()

    def finish(cin, cout, sems):
        fwd, inc = copies(cout, sems)
        for cp in inc:
            cp.wait_recv()
        for cp in fwd:
            cp.wait_send()

    return _Comm(gathered, [jax.ShapeDtypeStruct(g.shape, g.dtype) for g in gathered],
                 [pltpu.SemaphoreType.DMA((3 * n,)), pltpu.SemaphoreType.DMA((3 * n,))], start, finish,
                 aliases={t: t for t in range(n)})


def _rs_sibling_comm(grads):
    n = len(grads)

    def copies(cin, cout, sems):
        send, recv = sems
        x, y, c = _mesh_pos()
        return [pltpu.make_async_remote_copy(
            src_ref=cin[t].at[2 * q + (1 - c)], dst_ref=cout[t].at[q], send_sem=send.at[4 * t + q],
            recv_sem=recv.at[4 * t + q], device_id=(x, y, 1 - c), device_id_type=MESH)
            for t in range(n) for q in range(4)]

    def start(cin, cout, sems):
        for cp in copies(cin, cout, sems):
            cp.start()

    def finish(cin, cout, sems):
        cps = copies(cin, cout, sems)
        for cp in cps:
            cp.wait_recv()
        for cp in cps:
            cp.wait_send()

    return _Comm(grads, [jax.ShapeDtypeStruct((4,) + g.shape[1:], g.dtype) for g in grads],
                 [pltpu.SemaphoreType.DMA((4 * n,)), pltpu.SemaphoreType.DMA((4 * n,))], start, finish)


def _join_comms(comms):
    ins, outs, sems, aliases, spans = [], [], [], {}, []
    for cm in comms:
        spans.append((len(ins), len(cm.ins), len(outs), len(cm.out_shapes), len(sems), len(cm.sems)))
        aliases.update({len(ins) + i: len(outs) + o for i, o in cm.aliases.items()})
        ins, outs, sems = ins + cm.ins, outs + cm.out_shapes, sems + cm.sems

    def run(which):
        def fn(cin, cout, csem):
            for cm, (i0, ni, o0, no, s0, ns) in zip(comms, spans):
                getattr(cm, which)(cin[i0:i0 + ni], cout[o0:o0 + no], csem[s0:s0 + ns])
        return fn

    return _Comm(ins, outs, sems, run("start"), run("finish"), aliases)


def _rs_chips_comm(parts):
    n = len(parts)

    def copies(cin, cout, sems):
        send, recv, local = sems
        x, y, c = _mesh_pos()
        my_chip = 2 * x + y
        mine = [pltpu.make_async_copy(cin[t].at[my_chip], cout[t].at[my_chip], local.at[t]) for t in range(n)]
        sends, lands = [], []
        for t in range(n):
            for k, (px, py) in enumerate([(1 - x, y), (x, 1 - y), (1 - x, 1 - y)]):
                sems_k = dict(send_sem=send.at[3 * t + k], recv_sem=recv.at[3 * t + k], device_id=(px, py, c),
                              device_id_type=MESH)
                sends.append(pltpu.make_async_remote_copy(src_ref=cin[t].at[2 * px + py], dst_ref=cout[t].at[my_chip],
                                                          **sems_k))
                lands.append(pltpu.make_async_remote_copy(src_ref=cout[t].at[2 * px + py],
                                                          dst_ref=cout[t].at[2 * px + py], **sems_k))
        return mine, sends, lands

    def start(cin, cout, sems):
        mine, sends, _ = copies(cin, cout, sems)
        for cp in mine + sends:
            cp.start()

    def finish(cin, cout, sems):
        mine, sends, lands = copies(cin, cout, sems)
        for cp in lands:
            cp.wait_recv()
        for cp in sends:
            cp.wait_send()
        for cp in mine:
            cp.wait()

    return _Comm(parts, [jax.ShapeDtypeStruct(p.shape, p.dtype) for p in parts],
                 [pltpu.SemaphoreType.DMA((3 * n,)), pltpu.SemaphoreType.DMA((3 * n,)), pltpu.SemaphoreType.DMA((n,))],
                 start, finish)


def _row_tile(r, c, itemsize, budget=3 * 1024 * 1024):
    best = None
    for tr in range(16, r + 1, 16):
        if r % tr == 0 and tr * c * itemsize <= budget:
            best = tr
    return best or r


def _pair_sum(grad, land, *, name):
    _, r, c = grad.shape
    tr = _row_tile(r, c, 2)

    def body(g_ref, l_ref, o_ref):
        o_ref[...] = (g_ref[...].astype(F32) + l_ref[...].astype(F32)).astype(o_ref.dtype)

    return pl.pallas_call(
        body, name=name, grid=(4, r // tr),
        in_specs=[pl.BlockSpec((1, tr, c), lambda q, i: (2 * q + lax.axis_index("c"), i, 0)),
                  pl.BlockSpec((1, tr, c), lambda q, i: (q, i, 0))],
        out_specs=pl.BlockSpec((1, tr, c), lambda q, i: (q, i, 0)),
        out_shape=jax.ShapeDtypeStruct((4, r, c), grad.dtype),
        compiler_params=_cparams(("parallel", "parallel")),
    )(grad, land)


def _adamw_math(w, g, m, v):
    m = ADAM_B1 * m + (1.0 - ADAM_B1) * g
    v = ADAM_B2 * v + (1.0 - ADAM_B2) * jnp.square(g)
    m_hat = m / (1.0 - ADAM_B1 ** ADAM_STEP)
    v_hat = v / (1.0 - ADAM_B2 ** ADAM_STEP)
    delta = -ADAM_LR * (m_hat / (jnp.sqrt(v_hat) + ADAM_EPS) + ADAM_WD * w)
    return delta, m, v


def _adamw(parts, w, m, v, *, name):
    s, _, cp = parts.shape
    r, c = w.shape
    tr = _row_tile(r, cp, 4, budget=1024 * 1024)

    def body(p_ref, w_ref, m_ref, v_ref, g_ref, d_ref, nm_ref, nv_ref):
        g = p_ref[0].astype(F32)
        for i in range(1, s):
            g = g + p_ref[i].astype(F32)
        g = g[:, :c]
        delta, nm, nv = _adamw_math(w_ref[...], g, m_ref[...], v_ref[...])
        g_ref[...] = g
        d_ref[...] = delta
        nm_ref[...] = nm
        nv_ref[...] = nv

    blk = pl.BlockSpec((tr, c), lambda i: (i, 0))
    return pl.pallas_call(
        body, name=name, grid=(r // tr,),
        in_specs=[pl.BlockSpec((s, tr, cp), lambda i: (0, i, 0)), blk, blk, blk],
        out_specs=[blk] * 4, out_shape=[jax.ShapeDtypeStruct((r, c), F32)] * 4,
        compiler_params=_cparams(("parallel",)),
    )(parts, w, m, v)


def _w_in_pieces(d, nb, sources):
    segs = [(0, 4 * HW, False, 0), (4 * HW, 4 * HW + 2 * HEADS, True, 0),
            (4 * HW + 2 * HEADS, 7 * HW + 2 * HEADS, False, 4 * HW),
            (7 * HW + 2 * HEADS, 7 * HW + 3 * HEADS, True, 2 * HEADS),
            (7 * HW + 3 * HEADS, 7 * HW + 3 * HEADS + 2 * d, False, 7 * HW)]
    out = []
    for dev in range(N_DEV):
        lo, hi = dev * nb, (dev + 1) * nb
        for s0, s1, is_small, a0 in segs:
            p, q = max(lo, s0), min(hi, s1)
            if p >= q:
                continue
            a, b = a0 + p - s0, a0 + q - s0
            if is_small:
                out.append((dev, p - lo, q - lo, len(sources), a, b))
                continue
            for si, (start, width) in enumerate(sources):
                u, v = max(a, start), min(b, start + width)
                if u < v:
                    out.append((dev, p - lo + (u - a), p - lo + (v - a), si, u - start, v - start))
    return out


def _concat_cols(parts, *, name):
    t = parts[0].shape[0]
    n = len(parts)
    offs = [sum(p.shape[1] for p in parts[:i]) for i in range(n)]
    tm = min(128, t)

    def body(*refs):
        for i in range(n):
            refs[n][:, offs[i]:offs[i] + parts[i].shape[1]] = refs[i][...]

    return pl.pallas_call(
        body, name=name, grid=(t // tm,),
        in_specs=[pl.BlockSpec((tm, p.shape[1]), lambda i: (i, 0)) for p in parts],
        out_specs=pl.BlockSpec((tm, offs[-1] + parts[-1].shape[1]), lambda i: (i, 0)),
        out_shape=jax.ShapeDtypeStruct((t, offs[-1] + parts[-1].shape[1]), parts[0].dtype),
        compiler_params=_cparams(("parallel",)))(*parts)


def _w_in_to_aligned(g_in, *, name):
    _, d, nb = g_in.shape
    n_main = 7 * HW + 2 * d
    tr = min(128, d)
    pieces = _w_in_pieces(d, nb, [(0, n_main)])

    def body(g_ref, main_ref, small_ref):
        small_ref[...] = jnp.zeros_like(small_ref)
        for dev, s, e, src, a, b in pieces:
            dst = main_ref if src == 0 else small_ref
            dst[:, a:b] = g_ref[dev, :, s:e]

    return pl.pallas_call(
        body, name=name, grid=(d // tr,), in_specs=[pl.BlockSpec((N_DEV, tr, nb), lambda i: (0, i, 0))],
        out_specs=[pl.BlockSpec((tr, n_main), lambda i: (i, 0)), pl.BlockSpec((tr, LANES), lambda i: (i, 0))],
        out_shape=[jax.ShapeDtypeStruct((d, n_main), g_in.dtype), jax.ShapeDtypeStruct((d, LANES), g_in.dtype)],
        compiler_params=_cparams(("parallel",)),
    )(g_in)


def _w_in_grad_blocks(seg_grads, small_grad, sources, nb, *, name):
    d = small_grad.shape[0]
    tr = min(128, d)
    pieces = _w_in_pieces(d, nb, sources)
    ns = len(seg_grads)

    def body(*refs):
        o_ref = refs[ns + 1]
        for dev, s, e, src, a, b in pieces:
            o_ref[dev, :, s:e] = refs[src][:, a:b]

    return pl.pallas_call(
        body, name=name, grid=(d // tr,),
        in_specs=[pl.BlockSpec((tr, g.shape[1]), lambda i: (i, 0)) for g in seg_grads + [small_grad]],
        out_specs=pl.BlockSpec((N_DEV, tr, nb), lambda i: (0, i, 0)),
        out_shape=jax.ShapeDtypeStruct((N_DEV, d, nb), small_grad.dtype),
        compiler_params=_cparams(("parallel",)),
    )(*seg_grads, small_grad)


def _pad_cols(a, n):
    return a if a.shape[1] == n else jnp.concatenate([a, jnp.zeros((a.shape[0], n - a.shape[1]), a.dtype)], axis=1)


def _pad_rows(a, n):
    return a if a.shape[0] == n else jnp.concatenate([a, jnp.zeros((n - a.shape[0], a.shape[1]), a.dtype)], axis=0)


class _StaticPlan:
    def __init__(self, weights, cp):
        self.w, self.cp, self.grads = weights, cp, {}

    def comm_for(self, key):
        return None

    def done(self, key, res):
        pass

    def weight(self, name):
        return self.w[name]

    def grad(self, name, g):
        self.grads[name] = g

    def grad_w_in(self, g_main, g_small):
        self.grads["w_main"], self.grads["w_small"] = g_main, g_small


class _FsdpPlan:
    RIDES = {
        "in_proj": (("gather", (("conv", None), ("wa", None), ("wb", None), ("wout", None), ("wg", (0, 2)),
                                ("wd", (2, 4)))),),
        "gdn_fwd": (("gather", (("wg", (1, 2)),)), ("pass", ("wa", "wb", "wout"))),
        "fox_fwd": (("gather", (("wu", (0, 2)),)), ("pass", ("wg",))),
        "ffn_gate": (("gather", (("wu", (1, 2)), ("wd", (3, 4)))),),
        "ffn_up": (("gather", (("wd", (0, 2)),)),),
        "dw_ffn_gate": (("sibling", ("wd",)),),
        "d_hn_gate": (("chips", ("wd",)), ("sibling", ("wg",))),
        "d_hn_up": (("chips", ("wg",)),),
        "d_merged": (("sibling", ("wu",)),),
        "d_oa": (("sibling", ("wout",)),),
        "d_ob": (("sibling", ("wa",)),),
        "gdn_bwd": (("chips", ("wu",)), ("sibling", ("wb",))),
        "fox_bwd": (("chips", ("wout", "wa", "wb")),),
        "d_xn": (("chips", ("w_in", "conv")),),
    }
    PASS_GROUPS = (("conv",), ("wa", "wb", "wout"), ("wg",), ("wu",), ("wd",))

    def __init__(self, shards, d, cp, nb):
        self.shards, self.d, self.cp, self.nb = shards, d, cp, nb
        self.first, self.full = {}, {}
        self.blocks, self.queue, self.slots = {}, {}, {}
        self.flying = []

    def comm_for(self, key):
        comms, self.flying = [], []
        for kind, items in self.RIDES.get(key, ()):
            if kind == "gather":
                names = [n for n, _ in items]
                rows = [None if part is None else
                        (part[0] * (self.shards[n].shape[0] // part[1]), self.shards[n].shape[0] // part[1])
                        for n, part in items]
                comm = _ag_first_comm([self.shards[n] for n in names], rows, [self.first.get(n) for n in names])
            elif kind == "pass":
                names = list(items)
                comm = _ag_pass_comm([self.first[n] for n in names])
            elif kind == "sibling":
                names = [n for n in items if n in self.blocks]
                comm = _rs_sibling_comm([self.blocks[n] for n in names]) if names else None
            else:
                for n in items:
                    if n in self.blocks:
                        self.sibling_now(n)
                names = [n for n in items if n in self.queue]
                comm = _rs_chips_comm([self.queue.pop(n) for n in names]) if names else None
            if comm is not None:
                comms.append(comm)
                self.flying.append((kind, names, len(comm.out_shapes)))
        return _join_comms(comms) if comms else None

    def done(self, key, res):
        res = list(res)
        for kind, names, n_out in self.flying:
            outs, res = res[:n_out], res[n_out:]
            if kind == "gather":
                self.first.update(zip(names, outs))
            elif kind == "pass":
                self.full.update(zip(names, outs))
            elif kind == "sibling":
                for n, land in zip(names, outs):
                    self.queue[n] = _pair_sum(self.blocks.pop(n), land, name=f"pair_sum_{n}")
            else:
                self.slots.update(zip(names, outs))
        self.flying = []

    def weight(self, name):
        if name not in self.full:
            group = next(g for g in self.PASS_GROUPS if name in g)
            outs = _comm_call(_ag_pass_comm([self.first[n] for n in group]), name=f"all_gather_pass_{group[0]}")
            self.full.update(zip(group, outs))
        g = self.full[name]
        if name in ("wa", "wb", "conv"):
            return _cols_of_blocks(g)
        if name == "wout":
            return g.reshape(self.d, self.d)
        if name == "wd":
            return g.reshape(N_DEV * self.cp, self.d)
        return g

    def grad(self, name, g):
        if name == "wout":
            g = g.reshape(N_DEV, self.d // N_DEV, self.d)
        if name == "wd":
            g = g.reshape(N_DEV, self.cp, self.d)
        if name == "conv":
            g = _blocks_of_cols(g.astype(BF16))
        self.blocks[name] = g

    def grad_w_in(self, g_main, g_small):
        self.blocks["w_in"] = _w_in_grad_blocks([g_main], g_small, [(0, g_main.shape[1])], self.nb,
                                                name="w_in_grad_blocks")

    def sibling_now(self, name):
        blocks = self.blocks.pop(name)
        (land,) = _comm_call(_rs_sibling_comm([blocks]), name=f"grads_to_sibling_{name}")
        self.queue[name] = _pair_sum(blocks, land, name=f"pair_sum_{name}")

    def flush(self):
        for name in list(self.blocks):
            self.sibling_now(name)
        if self.queue:
            outs = _comm_call(_rs_chips_comm(list(self.queue.values())), name="grads_to_chips_tail")
            self.slots.update(zip(self.queue, outs))
            self.queue = {}


def _carried(plan, key, fn, *args, **kw):
    comm = plan.comm_for(key)
    if comm is None:
        return fn(*args, **kw)
    res, comm_res = fn(*args, comm=comm, **kw)
    plan.done(key, comm_res)
    return res


def _local_step(x, target, w_main, w_small, plan,
                norm_mix_w, norm_ffn_w, gdn_norm_w, fox_q_w, fox_k_w, a_row, b_row):
    t, d = x.shape
    cp = plan.cp
    fp = N_DEV * cp
    n_main = w_main.shape[1]
    off_gb = OFF_GA + d
    tm = 1024
    rt = 128

    (xn,) = _rowwise_fwd(_fn_norm, [(x, 0, d)], [norm_mix_w], [(d, BF16)], tm=rt, name="mix_norm")
    p_main = _carried(plan, "in_proj", _mm, xn, w_main, mode="nn", m=t, n=n_main, k=d, tm=tm, tn=512, tk=d,
                      out_dtype=BF16, name="in_proj")
    p_small = _mm(xn, w_small, mode="nn", m=t, n=LANES, k=d, tm=tm, tn=LANES, tk=d, out_dtype=F32, name="in_proj_small")
    (gates,) = _rowwise_fwd(_fn_gates, [(p_small, 0, LANES)], [a_row, b_row], [(LANES, F32)], tm=512, name="gates")
    conv_w = plan.weight("conv")
    qkv = _conv_fwd(p_main, conv_w, width=3 * HW, name="conv_fwd")
    o_gdn, s_all = _carried(plan, "gdn_fwd", _gdn_fwd, qkv, gates, name="gdn_fwd")
    gdn_rows = [(o_gdn, 0, HW), (p_main, OFF_ZA, HW)]
    (oa,) = _rowwise_fwd(_fn_gdn_out, gdn_rows, [gdn_norm_w], [(HW, BF16)], tm=512, inner=HEADS, name="gdn_out")
    wa = plan.weight("wa")
    ya = _mm(oa, wa, mode="nn", m=t, n=d, k=HW, tm=tm, tn=1024, tk=HW, out_dtype=BF16, name="branch_a")
    qk_rows = [(p_main, OFF_QB, HW), (p_main, OFF_KB, HW)]
    qn, kn = _rowwise_fwd(_fn_qknorm, qk_rows, [fox_q_w, fox_k_w], [(HW, BF16), (HW, BF16)], tm=512, inner=HEADS,
                          name="fox_qk_norm")
    blk = min(FOX_BLK, t)
    c4 = _fox_cumsum(gates, name="fox_cumsum").reshape(HEADS, t // blk, 1, blk)
    ob, ob32, lse = _carried(plan, "fox_fwd", _fox_fwd, qn, kn, p_main, c4, v_off=OFF_VB, name="fox_fwd")
    wb = plan.weight("wb")
    yb, merged = _mm(ob, wb, mode="nn", m=t, n=d, k=HW, tm=tm, tn=1024, tk=HW, out_dtype=BF16, name="branch_b",
                     tiles=[(p_main, OFF_GA), (p_main, off_gb), (ya, 0)], out_dtypes=[BF16, BF16],
                     epilogue=lambda p, ga, gb, a: (p, _fn_merge(ga, gb, a, _bf16_round(p))[0]))
    wout = plan.weight("wout")
    h = _mm(merged, wout, mode="nn", m=t, n=d, k=d, tm=tm, tn=512, tk=d, out_dtype=F32, add=x, name="out_proj")
    (hn,) = _rowwise_fwd(_fn_norm, [(h, 0, d)], [norm_ffn_w], [(d, BF16)], tm=rt, name="ffn_norm")
    wg = plan.weight("wg")
    gate = _carried(plan, "ffn_gate", _mm, hn, wg, mode="nn", m=t, n=fp, k=d, tm=512, tn=cp, tk=d, out_dtype=BF16,
                    b_blocked=True, name="ffn_gate")
    wu = plan.weight("wu")
    up, act = _carried(plan, "ffn_up", _mm, hn, wu, mode="nn", m=t, n=fp, k=d, tm=512, tn=cp, tk=d, out_dtype=BF16,
                       b_blocked=True, name="ffn_up", tiles=[(gate, 0)], out_dtypes=[BF16, BF16],
                       epilogue=lambda p, g: (p, _fn_swiglu(g, _bf16_round(p))[0]))
    wd = plan.weight("wd")
    y = _mm(act, wd, mode="nn", m=t, n=d, k=fp, tm=512, tn=256, tk=fp, out_dtype=F32, add=h, name="ffn_down")
    dy, dyb, loss_row = _loss_head(y, target, tm=rt, name="loss_head")

    dgate, dup = _mm(dyb, wd, mode="nt", m=t, n=fp, k=d, tm=tm, tn=512, tk=d, out_dtype=BF16, name="d_act",
                     tiles=[(gate, 0), (up, 0)], out_dtypes=[BF16, BF16],
                     epilogue=lambda p, g, u: jax.vjp(_fn_swiglu, g, u)[1]((_bf16_round(p),)))
    plan.grad("wd", _mm(act, dyb, mode="tn", m=fp, n=d, k=t, tm=512, tn=1024, tk=t, out_dtype=BF16, name="dw_ffn_down"))
    plan.grad("wg", _carried(plan, "dw_ffn_gate", _mm, hn, dgate, mode="tn", m=d, n=fp, k=t, tm=512, tn=cp, tk=t,
                             out_dtype=BF16, out_blocked=True, name="dw_ffn_gate"))
    dhn = _carried(plan, "d_hn_gate", _mm, dgate, wg, mode="nt", m=t, n=d, k=fp, tm=512, tn=256, tk=fp, out_dtype=F32,
                   b_blocked=True, name="d_hn_gate")
    dhn = _carried(plan, "d_hn_up", _mm, dup, wu, mode="nt", m=t, n=d, k=fp, tm=512, tn=256, tk=fp, out_dtype=F32,
                   add=dhn, b_blocked=True, name="d_hn_up")
    plan.grad("wu", _mm(hn, dup, mode="tn", m=d, n=fp, k=t, tm=512, tn=cp, tk=t, out_dtype=BF16, out_blocked=True,
                        name="dw_ffn_up"))
    dh, d_norm_ffn, dhb = _rowwise_bwd(_fn_norm, [(h, 0, d)], [norm_ffn_w], [dhn], [F32], tm=rt, name="d_ffn_norm",
                                       adds=[dy], bf16_copy_of=0)
    dga, dgb, dya, dyb2 = _carried(
        plan, "d_merged", _mm, dhb, wout, mode="nt", m=t, n=d, k=d, tm=512, tn=512, tk=d, out_dtype=BF16, name="d_merged",
        tiles=[(p_main, OFF_GA), (p_main, off_gb), (ya, 0), (yb, 0)], out_dtypes=[BF16] * 4,
        epilogue=lambda p, *gy: jax.vjp(_fn_merge, *gy)[1]((_bf16_round(p),)))
    plan.grad("wout", _mm(merged, dhb, mode="tn", m=d, n=d, k=t, tm=512, tn=512, tk=t, out_dtype=BF16, name="dw_out"))
    doa = _carried(plan, "d_oa", _mm, dya, wa, mode="nt", m=t, n=HW, k=d, tm=tm, tn=512, tk=d, out_dtype=BF16, name="d_oa")
    plan.grad("wa", _mm(oa, dya, mode="tn", m=HW, n=d, k=t, tm=1024, tn=d // N_DEV, tk=t, out_dtype=BF16,
                        out_blocked=True, name="dw_branch_a"))
    dob = _carried(plan, "d_ob", _mm, dyb2, wb, mode="nt", m=t, n=HW, k=d, tm=tm, tn=512, tk=d, out_dtype=BF16, name="d_ob")
    plan.grad("wb", _mm(ob, dyb2, mode="tn", m=HW, n=d, k=t, tm=1024, tn=d // N_DEV, tk=t, out_dtype=BF16,
                        out_blocked=True, name="dw_branch_b"))
    do_gdn, dza, d_gdn_norm = _rowwise_bwd(_fn_gdn_out, gdn_rows, [gdn_norm_w], [doa], [F32, BF16], tm=256,
                                           inner=HEADS, name="d_gdn_out")
    dqkv, dgates_gdn = _carried(plan, "gdn_bwd", _gdn_bwd, qkv, gates, s_all, do_gdn, name="gdn_bwd")
    dp_qkv, dconv = _conv_bwd(p_main, conv_w, dqkv, width=3 * HW, name="conv_bwd")
    plan.grad("conv", dconv)
    dqn, dkn, dvb, dc4, dcq = _carried(plan, "fox_bwd", _fox_bwd, qn, kn, p_main, c4, ob32, dob, lse, v_off=OFF_VB,
                                       name="fox_bwd")
    dqb, dkb, d_fox_q, d_fox_k = _rowwise_bwd(_fn_qknorm, qk_rows, [fox_q_w, fox_k_w], [dqn, dkn], [BF16, BF16],
                                              tm=256, inner=HEADS, name="d_fox_qk_norm")
    dgates = _fox_cumsum_bwd(dc4.reshape(HEADS, t) + dcq.reshape(HEADS, t), dgates_gdn, name="fox_cumsum_bwd")
    dsmall, d_a_row, d_b_row = _rowwise_bwd(_fn_gates, [(p_small, 0, LANES)], [a_row, b_row], [dgates], [F32],
                                            tm=512, name="d_gates")
    dp_main = _concat_cols([dp_qkv, dza, dqb, dkb, dvb, dga, dgb], name="d_p_main")
    plan.grad_w_in(_mm(xn, dp_main, mode="tn", m=d, n=n_main, k=t, tm=1024, tn=math.gcd(n_main, 1024), tk=t,
                       out_dtype=BF16, name="dw_in"),
                   _mm(xn, dsmall, mode="tn", m=d, n=LANES, k=t, tm=1024, tn=LANES, tk=t, out_dtype=BF16,
                       name="dw_in_small"))
    dxn = _mm(dsmall, w_small, mode="nt", m=t, n=d, k=LANES, tm=tm, tn=1024, tk=LANES, out_dtype=F32, name="d_xn_small")
    dxn = _carried(plan, "d_xn", _mm, dp_main, w_main, mode="nt", m=t, n=d, k=n_main, tm=tm, tn=1024,
                   tk=math.gcd(n_main, 2048),
                   out_dtype=F32, add=dxn, name="d_xn")
    grad_x, d_norm_mix = _rowwise_bwd(_fn_norm, [(x, 0, d)], [norm_mix_w], [dxn], [F32], tm=rt, name="d_mix_norm",
                                      adds=[dh])
    small = dict(norm_mix=d_norm_mix, norm_ffn=d_norm_ffn, gdn_norm=d_gdn_norm, fox_q=d_fox_q, fox_k=d_fox_k,
                 a_row=d_a_row, b_row=d_b_row)
    return loss_row[0, 0], grad_x, small


def _lane_row(pieces):
    row = jnp.zeros((1, LANES), F32)
    for off, p in pieces:
        row = lax.dynamic_update_slice(row, p.astype(F32), (0, off))
    return row


def _pack_small(norm_mix, norm_ffn, gdn_norm, fox_q, fox_k, a_log, dt_bias, b_f):
    rows = [norm_mix.reshape(-1, LANES), norm_ffn.reshape(-1, LANES), gdn_norm, fox_q, fox_k,
            _lane_row([(HEADS, a_log)]), _lane_row([(HEADS, dt_bias), (2 * HEADS, b_f)])]
    packed = jnp.concatenate(rows, axis=0)
    return _pad_rows(packed, -(-packed.shape[0] // 8) * 8)


def _unpack_small(p, d):
    nd = d // LANES
    r = 2 * nd
    return (p[0:nd].reshape(1, d), p[r + 3:r + 4, HEADS:2 * HEADS], p[r + 4:r + 5, HEADS:2 * HEADS], p[r:r + 1],
            p[r + 4:r + 5, 2 * HEADS:3 * HEADS], p[r + 1:r + 2], p[r + 2:r + 3], p[nd:r].reshape(1, d))


def _blocks_of_cols(a):
    r, c8 = a.shape
    return a.reshape(r, N_DEV, c8 // N_DEV).transpose(1, 0, 2)


def _cols_of_blocks(g):
    _, r, c = g.shape
    return g.transpose(1, 0, 2).reshape(r, N_DEV * c)


def kernel(x, norm_mix_w, w_in, conv_w, a_log, dt_bias, gdn_norm_w, fox_b_f, fox_q_norm_w, fox_k_norm_w, w_branch_a, w_branch_b, w_out, norm_ffn_w, w_ffn_gate, w_ffn_up, w_ffn_down, loss_target, m_norm_mix_w, m_w_in, m_conv_w, m_a_log, m_dt_bias, m_gdn_norm_w, m_fox_b_f, m_fox_q_norm_w, m_fox_k_norm_w, m_w_branch_a, m_w_branch_b, m_w_out, m_norm_ffn_w, m_w_ffn_gate, m_w_ffn_up, m_w_ffn_down, v_norm_mix_w, v_w_in, v_conv_w, v_a_log, v_dt_bias, v_gdn_norm_w, v_fox_b_f, v_fox_q_norm_w, v_fox_k_norm_w, v_w_branch_a, v_w_branch_b, v_w_out, v_norm_ffn_w, v_w_ffn_gate, v_w_ffn_up, v_w_ffn_down):
    d = x.shape[-1]
    cp = -(-w_ffn_down.shape[1] // LANES) * LANES
    nb = w_in.shape[2]

    g_in = _all_gather_relayed(w_in[0].astype(BF16), name="w_in_all_gather")
    w_main, w_small = _w_in_to_aligned(g_in, name="w_in_to_aligned")
    plan = _FsdpPlan(dict(conv=conv_w[0], wa=w_branch_a[0].astype(BF16), wb=w_branch_b[0].astype(BF16), wout=w_out[0].astype(BF16),
                          wg=_pad_cols(w_ffn_gate[0].astype(BF16), cp), wu=_pad_cols(w_ffn_up[0].astype(BF16), cp),
                          wd=_pad_rows(w_ffn_down[0].astype(BF16), cp)), d, cp, nb)
    a_row = _lane_row([(HEADS, a_log)])
    b_row = _lane_row([(HEADS, dt_bias), (2 * HEADS, fox_b_f)])

    loss_part, grad_x, gs = _local_step(
        x[0], loss_target[0], w_main, w_small, plan,
        norm_mix_w, norm_ffn_w, gdn_norm_w, fox_q_norm_w, fox_k_norm_w, a_row, b_row)
    loss = lax.psum(loss_part, ("x", "y", "c"))

    plan.flush()
    big = dict(w_in=("w_in", w_in, m_w_in, v_w_in), w_branch_a=("wa", w_branch_a, m_w_branch_a, v_w_branch_a),
               w_branch_b=("wb", w_branch_b, m_w_branch_b, v_w_branch_b), w_out=("wout", w_out, m_w_out, v_w_out),
               w_ffn_gate=("wg", w_ffn_gate, m_w_ffn_gate, v_w_ffn_gate), w_ffn_up=("wu", w_ffn_up, m_w_ffn_up, v_w_ffn_up),
               w_ffn_down=("wd", w_ffn_down, m_w_ffn_down, v_w_ffn_down), conv_w=("conv", conv_w, m_conv_w, v_conv_w))
    res = {}
    for nm, (key, w, m, v) in big.items():
        res[nm] = [o[None] for o in _adamw(plan.slots[key], w[0], m[0], v[0], name=f"adamw_{nm}")]

    g_small = _pack_small(gs["norm_mix"], gs["norm_ffn"], gs["gdn_norm"], gs["fox_q"], gs["fox_k"],
                          gs["a_row"][:, HEADS:2 * HEADS], gs["b_row"][:, HEADS:2 * HEADS],
                          gs["b_row"][:, 2 * HEADS:3 * HEADS])
    (g_small_all,) = _all_gather([g_small], name="small_grads_all_gather")
    w_small_p = _pack_small(norm_mix_w, norm_ffn_w, gdn_norm_w, fox_q_norm_w, fox_k_norm_w, a_log, dt_bias, fox_b_f)
    m_small_p = _pack_small(m_norm_mix_w, m_norm_ffn_w, m_gdn_norm_w, m_fox_q_norm_w, m_fox_k_norm_w, m_a_log,
                            m_dt_bias, m_fox_b_f)
    v_small_p = _pack_small(v_norm_mix_w, v_norm_ffn_w, v_gdn_norm_w, v_fox_q_norm_w, v_fox_k_norm_w, v_a_log,
                            v_dt_bias, v_fox_b_f)
    small_res = [_unpack_small(o, d) for o in _adamw(g_small_all, w_small_p, m_small_p, v_small_p, name="adamw_small")]

    def group(k):
        s = small_res[k]
        return [s[0], res["w_in"][k], res["conv_w"][k], s[1], s[2], s[3], s[4], s[5], s[6], res["w_branch_a"][k],
                res["w_branch_b"][k], res["w_out"][k], s[7], res["w_ffn_gate"][k], res["w_ffn_up"][k],
                res["w_ffn_down"][k]]

    return (loss, grad_x[None], *group(0), *group(1), *group(2), *group(3))
```

```python
import functools
import math

import jax
import jax.numpy as jnp
from jax import lax
from jax.experimental import pallas as pl
from jax.experimental.pallas import tpu as pltpu

F32 = jnp.float32
BF16 = jnp.bfloat16
HI = lax.Precision.HIGHEST
SOLVE_PRECISION = lax.Precision.HIGH
MESH = pl.DeviceIdType.MESH

EPS = 1e-6
HEADS = 16
DH = 128
HW = HEADS * DH
CHUNK = 64
CONV_K = 4
N_DEV = 8
LANES = 128
VMEM_LIMIT = 52 * 1024 * 1024

ADAM_LR = 0.001
ADAM_B1 = 0.9
ADAM_B2 = 0.999
ADAM_EPS = 1e-08
ADAM_WD = 0.01
ADAM_STEP = 10

OFF_QA, OFF_KA, OFF_VA, OFF_ZA, OFF_QB, OFF_KB, OFF_VB, OFF_GA = 0, HW, 2 * HW, 3 * HW, 4 * HW, 5 * HW, 6 * HW, 7 * HW


def _cparams(sem=None, vmem=VMEM_LIMIT):
    return pltpu.CompilerParams(dimension_semantics=sem, vmem_limit_bytes=vmem)


class _Comm:
    def __init__(self, ins, out_shapes, sems, start, finish, aliases=None):
        self.ins, self.out_shapes, self.sems = list(ins), list(out_shapes), list(sems)
        self.start, self.finish, self.aliases = start, finish, dict(aliases or {})


def _pcall(body, *, name, grid, in_specs, out_specs, out_shape, args, sem, scratch_shapes=(), comm=None):
    multi = isinstance(out_shape, (list, tuple))
    out_specs = list(out_specs) if multi else [out_specs]
    out_shape = list(out_shape) if multi else [out_shape]
    scratch_shapes = list(scratch_shapes)
    if comm is None:
        res = pl.pallas_call(body, name=name, grid=grid, in_specs=list(in_specs), out_specs=out_specs,
                             out_shape=out_shape, scratch_shapes=scratch_shapes, compiler_params=_cparams(sem))(*args)
        return res if multi else res[0]
    ni, no, ns = len(in_specs), len(out_specs), len(scratch_shapes)
    ci, co = len(comm.ins), len(comm.out_shapes)

    def wrapped(*refs):
        cin = refs[ni:ni + ci]
        outs = refs[ni + ci:ni + ci + no]
        cout = refs[ni + ci + no:ni + ci + no + co]
        scr = refs[ni + ci + no + co:ni + ci + no + co + ns]
        csem = refs[ni + ci + no + co + ns:]
        ids = [pl.program_id(ax) for ax in range(len(grid))]
        first = functools.reduce(jnp.logical_and, [i == 0 for i in ids])
        last = functools.reduce(jnp.logical_and, [i == g - 1 for i, g in zip(ids, grid)])

        @pl.when(first)
        def _():
            comm.start(cin, cout, csem)

        body(*refs[:ni], *outs, *scr)

        @pl.when(last)
        def _():
            comm.finish(cin, cout, csem)

    any_spec = pl.BlockSpec(memory_space=pl.ANY)
    res = pl.pallas_call(
        wrapped, name=name, grid=grid, in_specs=list(in_specs) + [any_spec] * ci,
        out_specs=out_specs + [any_spec] * co, out_shape=out_shape + comm.out_shapes,
        scratch_shapes=scratch_shapes + comm.sems,
        input_output_aliases={ni + i: no + o for i, o in comm.aliases.items()},
        compiler_params=_cparams(("arbitrary",) * len(grid)))(*args, *comm.ins)
    return (res[:no] if multi else res[0]), res[no:]


def _mm(a, b, *, mode, m, n, k, tm, tn, tk, out_dtype, name, a_off=(0, 0), b_off=(0, 0), add=None,
        b_blocked=False, out_blocked=False, comm=None, epilogue=None, tiles=(), out_dtypes=()):
    tm, tn, tk = min(tm, m), min(tn, n), min(tk, k)
    assert m % tm == 0 and n % tn == 0 and k % tk == 0, (name, m, n, k, tm, tn, tk)
    nk = k // tk
    if mode == "nn":
        a_blk, b_blk = (tm, tk), (tk, tn)
        ao, bo = (a_off[0] // tm, a_off[1] // tk), (b_off[0] // tk, b_off[1] // tn)
        a_map = lambda i, j, kk: (i + ao[0], kk + ao[1])
        b_map = lambda i, j, kk: (kk + bo[0], j + bo[1])
        dims = (((1,), (0,)), ((), ()))
        if b_blocked:
            assert b.shape == (n // tn, k, tn) and b_off == (0, 0), (name, b.shape)
            b_blk, b_map = (None, tk, tn), lambda i, j, kk: (j, kk, 0)
    elif mode == "nt":
        a_blk, b_blk = (tm, tk), (tn, tk)
        ao, bo = (a_off[0] // tm, a_off[1] // tk), (b_off[0] // tn, b_off[1] // tk)
        a_map = lambda i, j, kk: (i + ao[0], kk + ao[1])
        b_map = lambda i, j, kk: (j + bo[0], kk + bo[1])
        dims = (((1,), (1,)), ((), ()))
        if b_blocked and tk == k:
            kblocks, cblk = b.shape[0], b.shape[2]
            assert b.shape == (kblocks, n, cblk) and kblocks * cblk == k and b_off == (0, 0), (name, b.shape)
            b_blk, b_map = (kblocks, tn, cblk), lambda i, j, kk: (0, j, 0)
        elif b_blocked:
            assert b.shape == (nk, n, tk) and b_off == (0, 0), (name, b.shape)
            b_blk, b_map = (None, tn, tk), lambda i, j, kk: (kk, j, 0)
    else:
        assert not b_blocked
        a_blk, b_blk = (tk, tm), (tk, tn)
        ao, bo = (a_off[0] // tk, a_off[1] // tm), (b_off[0] // tk, b_off[1] // tn)
        a_map = lambda i, j, kk: (kk + ao[0], i + ao[1])
        b_map = lambda i, j, kk: (kk + bo[0], j + bo[1])
        dims = (((0,), (0,)), ((), ()))
    if not b_blocked:
        for off, blk in ((a_off, a_blk), (b_off, b_blk)):
            assert off[0] % blk[0] == 0 and off[1] % blk[1] == 0, (name, off, blk)
    has_add = add is not None
    n_tiles, n_outs = len(tiles), len(out_dtypes) if epilogue is not None else 1
    assert epilogue is None or (nk == 1 and not out_blocked and not has_add), name

    def body(*refs):
        a_ref, b_ref = refs[:2]
        c_ref = refs[2] if has_add else None
        tile_refs = refs[2 + has_add:2 + has_add + n_tiles]
        out_refs = refs[2 + has_add + n_tiles:2 + has_add + n_tiles + n_outs]
        o_ref, acc = out_refs[0], refs[-1]
        if len(b_blk) == 3 and b_blk[0] is not None:
            cblk = b_blk[2]
            p = sum(lax.dot_general(a_ref[:, q * cblk:(q + 1) * cblk].astype(BF16), b_ref[q].astype(BF16), dims,
                                    preferred_element_type=F32) for q in range(b_blk[0]))
        else:
            p = lax.dot_general(a_ref[...].astype(BF16), b_ref[...].astype(BF16), dims, preferred_element_type=F32)
        if epilogue is not None:
            for ref, val in zip(out_refs, epilogue(p, *[r[...].astype(F32) for r in tile_refs])):
                ref[...] = val.astype(ref.dtype)
        elif nk == 1:
            if has_add:
                p = p + c_ref[...].astype(F32)
            o_ref[...] = p.astype(o_ref.dtype)
        else:
            kk = pl.program_id(2)

            @pl.when(kk == 0)
            def _():
                acc[...] = p + c_ref[...].astype(F32) if has_add else p

            @pl.when(kk > 0)
            def _():
                acc[...] += p

            @pl.when(kk == nk - 1)
            def _():
                o_ref[...] = acc[...].astype(o_ref.dtype)

    in_specs = [pl.BlockSpec(a_blk, a_map), pl.BlockSpec(b_blk, b_map)]
    args = [a, b]
    if has_add:
        in_specs.append(pl.BlockSpec((tm, tn), lambda i, j, kk: (i, j)))
        args.append(add)
    for arr, off in tiles:
        assert off % tn == 0, (name, off, tn)
        in_specs.append(pl.BlockSpec((tm, tn), lambda i, j, kk, ob=off // tn: (i, j + ob)))
        args.append(arr)
    acc_shape = (tm, tn) if nk > 1 else (8, LANES)
    if out_blocked:
        out_spec = pl.BlockSpec((None, tm, tn), lambda i, j, kk: (j, i, 0))
        out_shape = jax.ShapeDtypeStruct((n // tn, m, tn), out_dtype)
    else:
        out_spec = pl.BlockSpec((tm, tn), lambda i, j, kk: (i, j))
        out_shape = jax.ShapeDtypeStruct((m, n), out_dtype)
    if epilogue is not None:
        out_spec = [out_spec] * n_outs
        out_shape = [jax.ShapeDtypeStruct((m, n), dt) for dt in out_dtypes]
    return _pcall(body, name=name, grid=(m // tm, n // tn, nk), in_specs=in_specs, out_specs=out_spec,
                  out_shape=out_shape, scratch_shapes=[pltpu.VMEM(acc_shape, F32)], args=args,
                  sem=("parallel", "parallel", "arbitrary"), comm=comm)


def _row_specs(rows, tm, ncol):
    specs = []
    for arr, off, width in rows:
        bw = width // ncol
        assert width % ncol == 0 and off % bw == 0, (off, width, ncol)
        ob = off // bw
        specs.append(pl.BlockSpec((tm, bw), lambda i, j, ob=ob: (i, j + ob)))
    return specs


def _col_group(ref, s, inner):
    w = ref.shape[1] // inner
    return slice(None), slice(s * w, (s + 1) * w)


def _rowwise_fwd(fn, rows, params, outs, *, tm, ncol=1, inner=1, name):
    t = rows[0][0].shape[0]
    tm = min(tm, t)
    nr, npar = len(rows), len(params)

    def body(*refs):
        par = [r[...].astype(F32) for r in refs[nr:nr + npar]]
        for s in range(inner):
            res = fn(*[r[_col_group(r, s, inner)].astype(F32) for r in refs[:nr]], *par)
            for o_ref, val in zip(refs[nr + npar:], res):
                o_ref[_col_group(o_ref, s, inner)] = val.astype(o_ref.dtype)

    in_specs = _row_specs(rows, tm, ncol) + [pl.BlockSpec(p.shape, lambda i, j: (0, 0)) for p in params]
    out_specs = [pl.BlockSpec((tm, w // ncol), lambda i, j: (i, j)) for w, _ in outs]
    out_shape = [jax.ShapeDtypeStruct((t, w), dt) for w, dt in outs]
    return pl.pallas_call(
        body, name=name, grid=(t // tm, ncol), in_specs=in_specs, out_specs=out_specs, out_shape=out_shape,
        compiler_params=_cparams(("parallel", "parallel")),
    )(*[r[0] for r in rows], *params)


def _rowwise_bwd(fn, rows, params, cts, grad_dtypes, *, tm, ncol=1, inner=1, name, adds=None, bf16_copy_of=None):
    t = rows[0][0].shape[0]
    tm = min(tm, t)
    nr, npar, nct = len(rows), len(params), len(cts)
    adds = adds or [None] * nr
    add_idx = [i for i, a in enumerate(adds) if a is not None]

    def body(*refs):
        par = [r[...].astype(F32) for r in refs[nr:nr + npar]]
        ct_refs = refs[nr + npar:nr + npar + nct]
        add_refs = refs[nr + npar + nct:nr + npar + nct + len(add_idx)]
        outs = refs[nr + npar + nct + len(add_idx):]
        extra = dict(zip(add_idx, add_refs))
        par_grads = [None] * npar
        for s in range(inner):
            ins = [r[_col_group(r, s, inner)].astype(F32) for r in refs[:nr]]
            _, vjp = jax.vjp(lambda *a: tuple(fn(*a)), *ins, *par)
            grads = vjp(tuple(r[_col_group(r, s, inner)].astype(F32) for r in ct_refs))
            for i in range(nr):
                g = grads[i]
                if i in extra:
                    g = g + extra[i][_col_group(extra[i], s, inner)].astype(F32)
                outs[i][_col_group(outs[i], s, inner)] = g.astype(outs[i].dtype)
                if i == bf16_copy_of:
                    outs[nr + npar][_col_group(outs[nr + npar], s, inner)] = g.astype(BF16)
            par_grads = [g if acc is None else acc + g for acc, g in zip(par_grads, grads[nr:])]
        first = jnp.logical_and(pl.program_id(0) == 0, pl.program_id(1) == 0)
        for pi in range(npar):
            o_ref = outs[nr + pi]
            g = par_grads[pi]

            @pl.when(first)
            def _(o_ref=o_ref, g=g):
                o_ref[...] = g

            @pl.when(jnp.logical_not(first))
            def _(o_ref=o_ref, g=g):
                o_ref[...] += g

    in_specs = (_row_specs(rows, tm, ncol)
                + [pl.BlockSpec(p.shape, lambda i, j: (0, 0)) for p in params]
                + [pl.BlockSpec((tm, c.shape[1] // ncol), lambda i, j: (i, j)) for c in cts]
                + [pl.BlockSpec((tm, adds[i].shape[1] // ncol), lambda i, j: (i, j)) for i in add_idx])
    out_specs = ([pl.BlockSpec((tm, w // ncol), lambda i, j: (i, j)) for _, _, w in rows]
                 + [pl.BlockSpec(p.shape, lambda i, j: (0, 0)) for p in params])
    out_shape = ([jax.ShapeDtypeStruct((t, w), dt) for (_, _, w), dt in zip(rows, grad_dtypes)]
                 + [jax.ShapeDtypeStruct(p.shape, F32) for p in params])
    if bf16_copy_of is not None:
        w = rows[bf16_copy_of][2]
        out_specs.append(pl.BlockSpec((tm, w // ncol), lambda i, j: (i, j)))
        out_shape.append(jax.ShapeDtypeStruct((t, w), BF16))
    return pl.pallas_call(
        body, name=name, grid=(t // tm, ncol), in_specs=in_specs, out_specs=out_specs, out_shape=out_shape,
        compiler_params=_cparams(("arbitrary", "arbitrary")),
    )(*[r[0] for r in rows], *params, *cts, *[adds[i] for i in add_idx])


def _rms(x, w):
    return x * lax.rsqrt(jnp.mean(x * x, axis=-1, keepdims=True) + EPS) * w


def _fn_norm(x, w):
    return (_rms(x, w),)


def _fn_gates(z, a_row, b_row):
    lane = lax.broadcasted_iota(jnp.int32, z.shape, 1)
    beta = jax.nn.sigmoid(z)
    g = -jnp.exp(a_row) * jax.nn.softplus(z + b_row)
    logf = jax.nn.log_sigmoid(z + b_row)
    return (jnp.where(lane < HEADS, beta, jnp.where(lane < 2 * HEADS, g, jnp.where(lane < 3 * HEADS, logf, 0.0))),)


def _fn_qknorm(q, k, qw, kw):
    return _rms(q, qw) * (DH ** -0.5 * LOG2E), _rms(k, kw)


def _fn_gdn_out(o, z, w):
    return (_rms(o, w) * jax.nn.silu(z),)


def _fn_merge(ga, gb, ya, yb):
    return (jax.nn.sigmoid(ga) * ya + jax.nn.sigmoid(gb) * yb,)


def _fn_swiglu(g, u):
    return (jax.nn.silu(g) * u,)


def _bf16_round(x):
    return x.astype(BF16).astype(F32)


def _loss_head(y, target, *, tm, name):
    t, d = y.shape
    tm = min(tm, t)

    def body(y_ref, t_ref, dyf_ref, dyb_ref, loss_ref):
        err = y_ref[...] - t_ref[...]
        dy = err * (1.0 / d)
        dyf_ref[...] = dy
        dyb_ref[...] = dy.astype(BF16)
        part = jnp.sum(err * err) * (0.5 / d)

        @pl.when(pl.program_id(0) == 0)
        def _():
            loss_ref[...] = jnp.zeros_like(loss_ref)

        loss_ref[...] += part

    blk = pl.BlockSpec((tm, d), lambda i: (i, 0))
    return pl.pallas_call(
        body, name=name, grid=(t // tm,), in_specs=[blk, blk],
        out_specs=[blk, blk, pl.BlockSpec((1, LANES), lambda i: (0, 0))],
        out_shape=[jax.ShapeDtypeStruct((t, d), F32), jax.ShapeDtypeStruct((t, d), BF16),
                   jax.ShapeDtypeStruct((1, LANES), F32)],
        compiler_params=_cparams(("arbitrary",)),
    )(y, target)


def _shift_down(x, s):
    if s == 0:
        return x
    row = lax.broadcasted_iota(jnp.int32, x.shape, 0)
    return jnp.where(row >= s, pltpu.roll(x, s, 0), 0.0)


def _shift_up(x, s):
    if s == 0:
        return x
    t = x.shape[0]
    row = lax.broadcasted_iota(jnp.int32, x.shape, 0)
    return jnp.where(row < t - s, pltpu.roll(x, t - s, 0), 0.0)


def _conv_pre(x, w):
    y = x * w[CONV_K - 1:CONV_K, :]
    for i in range(CONV_K - 1):
        y = y + _shift_down(x, CONV_K - 1 - i) * w[i:i + 1, :]
    return y


def _conv_fwd(p_main, conv_w, *, width, name):
    t = p_main.shape[0]
    tc = LANES

    def body(x_ref, w_ref, o_ref):
        y = _conv_pre(x_ref[...].astype(F32), w_ref[...])
        o_ref[...] = y * jax.nn.sigmoid(y)

    return pl.pallas_call(
        body, name=name, grid=(width // tc,),
        in_specs=[pl.BlockSpec((t, tc), lambda j: (0, j)), pl.BlockSpec((CONV_K, tc), lambda j: (0, j))],
        out_specs=pl.BlockSpec((t, tc), lambda j: (0, j)),
        out_shape=jax.ShapeDtypeStruct((t, width), F32),
        compiler_params=_cparams(("parallel",)),
    )(p_main, conv_w)


def _conv_bwd(p_main, conv_w, dy, *, width, name):
    t = p_main.shape[0]
    tc = LANES

    def body(x_ref, w_ref, dy_ref, dx_ref, dw_ref):
        x = x_ref[...].astype(F32)
        w = w_ref[...]
        pre = _conv_pre(x, w)
        sg = jax.nn.sigmoid(pre)
        dpre = dy_ref[...] * (sg * (1.0 + pre * (1.0 - sg)))
        dx = dpre * w[CONV_K - 1:CONV_K, :]
        dws = []
        for i in range(CONV_K - 1):
            s = CONV_K - 1 - i
            dx = dx + _shift_up(dpre, s) * w[i:i + 1, :]
            dws.append(jnp.sum(_shift_down(x, s) * dpre, axis=0, keepdims=True))
        dws.append(jnp.sum(x * dpre, axis=0, keepdims=True))
        dx_ref[...] = dx.astype(dx_ref.dtype)
        dw_ref[...] = jnp.concatenate(dws, axis=0)

    return pl.pallas_call(
        body, name=name, grid=(width // tc,),
        in_specs=[pl.BlockSpec((t, tc), lambda j: (0, j)), pl.BlockSpec((CONV_K, tc), lambda j: (0, j)),
                  pl.BlockSpec((t, tc), lambda j: (0, j))],
        out_specs=[pl.BlockSpec((t, tc), lambda j: (0, j)), pl.BlockSpec((CONV_K, tc), lambda j: (0, j))],
        out_shape=[jax.ShapeDtypeStruct((t, width), BF16), jax.ShapeDtypeStruct((CONV_K, width), F32)],
        compiler_params=_cparams(("parallel",)),
    )(p_main, conv_w, dy)


def _ordered_einsum(out, x, xs, y, ys):
    if out[1] in xs:
        return jnp.einsum(f"{xs},{ys}->{out}", x, y, preferred_element_type=F32)
    return jnp.einsum(f"{ys},{xs}->{out}", y, x, preferred_element_type=F32)


@functools.partial(jax.custom_vjp, nondiff_argnums=(2,))
def _bmm_bf16(a, b, spec):
    return jnp.einsum(spec, a.astype(BF16), b.astype(BF16), preferred_element_type=F32)


def _bmm_bf16_fwd(a, b, spec):
    a16, b16 = a.astype(BF16), b.astype(BF16)
    return jnp.einsum(spec, a16, b16, preferred_element_type=F32), (a16, b16)


def _bmm_bf16_bwd(spec, res, g):
    a16, b16 = res
    ins, out = spec.split("->")
    sa, sb = ins.split(",")
    g16 = g.astype(BF16)
    return _ordered_einsum(sa, g16, out, b16, sb), _ordered_einsum(sb, a16, sa, g16, out)


_bmm_bf16.defvjp(_bmm_bf16_fwd, _bmm_bf16_bwd)


def _bmm(a, b, spec, precision=None):
    if precision is None:
        return _bmm_bf16(a, b, spec)
    return jnp.einsum(spec, a, b, preferred_element_type=F32, precision=precision)


def _iota2(shape, dim):
    return lax.broadcasted_iota(jnp.int32, shape, dim)


@jax.custom_vjp
def _tri_inverse(a):
    return _tri_inverse_levels(a)


def _tri_inverse_fwd(a):
    t = _tri_inverse_levels(a)
    return t, t


def _tri_inverse_bwd(t, g):
    x = _bmm(t, g, "hji,hjk->hik", SOLVE_PRECISION)
    return (-_bmm(x, t, "hik,hjk->hij", SOLVE_PRECISION),)


_tri_inverse.defvjp(_tri_inverse_fwd, _tri_inverse_bwd)


def _tri_inverse_levels(a):
    c = a.shape[-1]
    r, m = _iota2((c, c), 0), _iota2((c, c), 1)
    eye = (r == m).astype(F32)
    inv = None
    b = 1
    while b < c:
        mask = jnp.logical_and(r // (2 * b) == m // (2 * b), jnp.logical_and(r % (2 * b) >= b, m % (2 * b) < b))
        off = jnp.where(mask[None], a, 0.0)
        if inv is None:
            inv = eye[None] - off
        else:
            inv = inv - _bmm(_bmm(inv, off, "hij,hjk->hik", SOLVE_PRECISION), inv, "hij,hjk->hik", SOLVE_PRECISION)
        b *= 2
    return inv


def _gdn_chunk(s, q3, k3, v3, b3, gc3):
    c = q3.shape[1]
    r, m = _iota2((c, c), 0), _iota2((c, c), 1)
    tril_incl = (r >= m)[None]
    tril_strict = (r > m)[None]
    eye = (r == m).astype(F32)[None]
    qn = q3 * lax.rsqrt(jnp.sum(q3 * q3, axis=-1, keepdims=True) + EPS) * (DH ** -0.5)
    kn = k3 * lax.rsqrt(jnp.sum(k3 * k3, axis=-1, keepdims=True) + EPS)
    ones = jnp.ones((q3.shape[0], c, c), F32)
    gc_row = _bmm(ones, gc3 * eye, "hij,hjk->hik", SOLVE_PRECISION)
    decay = jnp.where(tril_incl, jnp.exp(jnp.where(tril_incl, gc3 - gc_row, 0.0)), 0.0)
    a = jnp.where(tril_strict, _bmm(kn, kn, "hcd,hmd->hcm") * decay * b3, 0.0)
    tinv = _tri_inverse(a)
    egc = jnp.exp(gc3)
    u = _bmm(tinv, v3 * b3, "hij,hjk->hik", SOLVE_PRECISION)
    w = _bmm(tinv, kn * (b3 * egc), "hij,hjk->hik", SOLVE_PRECISION)
    qk = _bmm(qn, kn, "hcd,hmd->hcm") * decay
    v_new = u - _bmm(w, s, "hcd,hdv->hcv")
    o = _bmm(qn * egc, s, "hcd,hdv->hcv") + _bmm(qk, v_new, "hcm,hmv->hcv")
    row = _iota2((c, 1), 0)[None]
    g_last = jnp.sum(jnp.where(row == c - 1, gc3, 0.0), axis=1, keepdims=True)
    s_new = s * jnp.exp(g_last) + _bmm(kn * jnp.exp(g_last - gc3), v_new, "hcd,hcv->hdv")
    return s_new, o


GDN_HEAD_GROUP = 16


def _split_heads(ref, off, h0):
    return jnp.stack([ref[:, off + h * DH:off + (h + 1) * DH].astype(F32)
                      for h in range(h0, h0 + GDN_HEAD_GROUP)], axis=0)


def _store_heads(ref, x3, off, h0):
    for i in range(GDN_HEAD_GROUP):
        h = h0 + i
        ref[:, off + h * DH:off + (h + 1) * DH] = x3[i].astype(ref.dtype)


def _lane_cols(tile, lane0):
    lane = _iota2(tile.shape, 1)
    return jnp.stack([jnp.sum(jnp.where(lane == lane0 + i, tile, 0.0), axis=1, keepdims=True)
                      for i in range(GDN_HEAD_GROUP)], axis=0)


def _cols_to_lanes(cols3, lane0, shape):
    lane = _iota2(shape, 1)
    out = jnp.zeros(shape, F32)
    for i in range(GDN_HEAD_GROUP):
        out = out + jnp.where(lane == lane0 + i, cols3[i], 0.0)
    return out


def _chunk_cumsum_matrix():
    r, m = _iota2((CHUNK, CHUNK), 0), _iota2((CHUNK, CHUNK), 1)
    return (r >= m).astype(F32)


def _gdn_inputs(qkv_ref, gt, gcum, h0):
    return (_split_heads(qkv_ref, 0, h0), _split_heads(qkv_ref, HW, h0), _split_heads(qkv_ref, 2 * HW, h0),
            _lane_cols(gt, h0), _lane_cols(gcum, HEADS + h0))


def _gdn_fwd(qkv, gates, *, name, comm=None):
    t = qkv.shape[0]
    n = t // CHUNK

    def body(qkv_ref, gt_ref, o_ref, sall_ref, s_scr):
        @pl.when(pl.program_id(0) == 0)
        def _():
            s_scr[...] = jnp.zeros_like(s_scr)

        gt = gt_ref[...]
        gcum = jnp.dot(_chunk_cumsum_matrix(), gt, preferred_element_type=F32, precision=HI)
        for h0 in range(0, HEADS, GDN_HEAD_GROUP):
            grp = pl.ds(h0, GDN_HEAD_GROUP)
            s = s_scr[grp]
            sall_ref[0, grp] = s
            s_new, o3 = _gdn_chunk(s, *_gdn_inputs(qkv_ref, gt, gcum, h0))
            s_scr[grp] = s_new
            _store_heads(o_ref, o3, 0, h0)

    return _pcall(
        body, name=name, grid=(n,),
        in_specs=[pl.BlockSpec((CHUNK, 3 * HW), lambda i: (i, 0)), pl.BlockSpec((CHUNK, LANES), lambda i: (i, 0))],
        out_specs=[pl.BlockSpec((CHUNK, HW), lambda i: (i, 0)),
                   pl.BlockSpec((1, HEADS, DH, DH), lambda i: (i, 0, 0, 0))],
        out_shape=[jax.ShapeDtypeStruct((t, HW), F32), jax.ShapeDtypeStruct((n, HEADS, DH, DH), F32)],
        scratch_shapes=[pltpu.VMEM((HEADS, DH, DH), F32)], sem=("arbitrary",), args=(qkv, gates), comm=comm)


def _gdn_bwd(qkv, gates, s_all, do, *, name, comm=None):
    t = qkv.shape[0]
    n = t // CHUNK

    def body(qkv_ref, gt_ref, sall_ref, do_ref, dqkv_ref, dgt_ref, ds_scr):
        @pl.when(pl.program_id(0) == 0)
        def _():
            ds_scr[...] = jnp.zeros_like(ds_scr)

        gt = gt_ref[...]
        cum = _chunk_cumsum_matrix()
        gcum = jnp.dot(cum, gt, preferred_element_type=F32, precision=HI)
        shape = (CHUNK, LANES)
        dbeta = jnp.zeros(shape, F32)
        dgcum = jnp.zeros(shape, F32)
        for h0 in range(0, HEADS, GDN_HEAD_GROUP):
            grp = pl.ds(h0, GDN_HEAD_GROUP)
            _, vjp = jax.vjp(_gdn_chunk, sall_ref[0, grp], *_gdn_inputs(qkv_ref, gt, gcum, h0))
            ds, dq3, dk3, dv3, db3, dgc3 = vjp((ds_scr[grp], _split_heads(do_ref, 0, h0)))
            ds_scr[grp] = ds
            _store_heads(dqkv_ref, dq3, 0, h0)
            _store_heads(dqkv_ref, dk3, HW, h0)
            _store_heads(dqkv_ref, dv3, 2 * HW, h0)
            dbeta = dbeta + _cols_to_lanes(db3, h0, shape)
            dgcum = dgcum + _cols_to_lanes(dgc3, HEADS + h0, shape)
        dg = lax.dot_general(cum, dgcum, (((0,), (0,)), ((), ())), preferred_element_type=F32, precision=HI)
        dgt_ref[...] = dbeta + dg

    rev = lambda i: n - 1 - i
    return _pcall(
        body, name=name, grid=(n,),
        in_specs=[pl.BlockSpec((CHUNK, 3 * HW), lambda i: (rev(i), 0)), pl.BlockSpec((CHUNK, LANES), lambda i: (rev(i), 0)),
                  pl.BlockSpec((1, HEADS, DH, DH), lambda i: (rev(i), 0, 0, 0)),
                  pl.BlockSpec((CHUNK, HW), lambda i: (rev(i), 0))],
        out_specs=[pl.BlockSpec((CHUNK, 3 * HW), lambda i: (rev(i), 0)), pl.BlockSpec((CHUNK, LANES), lambda i: (rev(i), 0))],
        out_shape=[jax.ShapeDtypeStruct((t, 3 * HW), F32), jax.ShapeDtypeStruct((t, LANES), F32)],
        scratch_shapes=[pltpu.VMEM((HEADS, DH, DH), F32)], sem=("arbitrary",), args=(qkv, gates, s_all, do), comm=comm)


FOX_BLK = 1024
LOG2E = math.log2(math.e)
FOX_ROW_SPLIT = 1
NEG = -1e30


def _fox_cumsum(gates, *, name):
    t = gates.shape[0]
    blk = min(FOX_BLK, t)

    def body(g_ref, c_ref):
        r, m = _iota2((blk, blk), 0), _iota2((blk, blk), 1)
        upper = (r <= m).astype(F32)
        carry = jnp.zeros((HEADS, 1), F32)
        for b in range(t // blk):
            lf = g_ref[b * blk:(b + 1) * blk, :].T[2 * HEADS:3 * HEADS, :]
            c = jnp.dot(lf, upper, preferred_element_type=F32, precision=HI) + carry
            c_ref[:, b * blk:(b + 1) * blk] = c * LOG2E
            carry = carry + jnp.sum(lf, axis=1, keepdims=True)

    return pl.pallas_call(body, name=name, out_shape=jax.ShapeDtypeStruct((HEADS, t), F32),
                          compiler_params=_cparams())(gates)


def _fox_cumsum_bwd(dc, dgates_gdn, *, name):
    t = dc.shape[1]
    blk = min(FOX_BLK, t)

    def body(dc_ref, dg_ref, o_ref):
        r, m = _iota2((blk, blk), 0), _iota2((blk, blk), 1)
        lower = (r >= m).astype(F32)
        carry = jnp.zeros((HEADS, 1), F32)
        for b in reversed(range(t // blk)):
            d = dc_ref[:, b * blk:(b + 1) * blk]
            dlf = jnp.dot(d, lower, preferred_element_type=F32, precision=HI) + carry
            carry = carry + jnp.sum(d, axis=1, keepdims=True)
            tile = jnp.concatenate([jnp.zeros((2 * HEADS, blk), F32), dlf,
                                    jnp.zeros((LANES - 3 * HEADS, blk), F32)], axis=0)
            o_ref[b * blk:(b + 1) * blk, :] = tile.T + dg_ref[b * blk:(b + 1) * blk, :]

    return pl.pallas_call(body, name=name, out_shape=jax.ShapeDtypeStruct((t, LANES), F32),
                          compiler_params=_cparams())(dc, dgates_gdn)


def _fox_logits(q, k, c_row, row0=None):
    s = lax.dot_general(q, k, (((1,), (1,)), ((), ())), preferred_element_type=F32) - c_row
    if row0 is None:
        return s
    return jnp.where(row0 + _iota2(s.shape, 0) >= _iota2(s.shape, 1), s, NEG)


def _fox_fwd(qn, kn, p_main, c4, *, v_off, name, comm=None):
    t = qn.shape[0]
    blk = min(FOX_BLK, t)
    nb = t // blk
    vb = v_off // DH

    def body(q_ref, k_ref, v_ref, c_ref, o_ref, o32_ref, lse_ref):
        qi = pl.program_id(1)
        q = q_ref[...]

        def step(j, carry, diagonal=False):
            m, l, acc = carry
            rows = pl.ds(pl.multiple_of(j * blk, blk), blk)
            s = _fox_logits(q, k_ref[rows, :], c_ref[0, j], 0 if diagonal else None)
            m_new = jnp.maximum(m, jnp.max(s, axis=1, keepdims=True))
            p = jnp.exp2(s - m_new)
            scale = jnp.exp2(m - m_new)
            l = scale * l + jnp.sum(p, axis=1, keepdims=True)
            acc = scale * acc + jnp.dot(p.astype(BF16), v_ref[rows, :], preferred_element_type=F32)
            return m_new, l, acc

        init = (jnp.full((blk, 1), NEG, F32), jnp.zeros((blk, 1), F32), jnp.zeros((blk, DH), F32))
        m, l, acc = step(qi, lax.fori_loop(0, qi, step, init), diagonal=True)
        o = acc / l
        o_ref[...] = o.astype(o_ref.dtype)
        o32_ref[...] = o
        lse_ref[0] = m + jnp.log(l) * LOG2E

    return _pcall(
        body, name=name, grid=(HEADS, nb),
        in_specs=[pl.BlockSpec((blk, DH), lambda h, i: (i, h)), pl.BlockSpec((t, DH), lambda h, i: (0, h)),
                  pl.BlockSpec((t, DH), lambda h, i: (0, vb + h)), pl.BlockSpec((1, nb, 1, blk), lambda h, i: (h, 0, 0, 0))],
        out_specs=[pl.BlockSpec((blk, DH), lambda h, i: (i, h)), pl.BlockSpec((blk, DH), lambda h, i: (i, h)),
                   pl.BlockSpec((1, blk, 1), lambda h, i: (h, i, 0))],
        out_shape=[jax.ShapeDtypeStruct((t, HW), BF16), jax.ShapeDtypeStruct((t, HW), F32),
                   jax.ShapeDtypeStruct((HEADS, t, 1), F32)],
        sem=("parallel", "arbitrary"), args=(qn, kn, p_main, c4), comm=comm)


def _fox_bwd(qn, kn, p_main, c4, o32, do, lse, *, v_off, name, comm=None):
    t = qn.shape[0]
    blk = min(FOX_BLK, t)
    nb = t // blk
    vb = v_off // DH
    sub = blk // FOX_ROW_SPLIT
    tn_dims = (((0,), (0,)), ((), ()))
    nt_dims = (((1,), (1,)), ((), ()))

    def body(q_ref, k_ref, v_ref, c_ref, o_ref, do_ref, lse_ref, dq_ref, dk_ref, dv_ref, dc_ref, dcq_ref):
        kj = pl.program_id(1)

        @pl.when(kj == 0)
        def _():
            dq_ref[...] = jnp.zeros_like(dq_ref)
            dcq_ref[...] = jnp.zeros_like(dcq_ref)

        k = k_ref[...]
        v = v_ref[...]
        c_row = c_ref[0, 0]

        def step(i, carry, diagonal=False):
            dk, dv, dc = carry
            for u in range(FOX_ROW_SPLIT):
                rows = pl.ds(pl.multiple_of(i * blk + u * sub, sub), sub)
                q = q_ref[rows, :]
                dob = do_ref[rows, :]
                p = jnp.exp2(_fox_logits(q, k, c_row, u * sub if diagonal else None) - lse_ref[0, rows, :])
                pb = p.astype(BF16)
                dv = dv + lax.dot_general(pb, dob, tn_dims, preferred_element_type=F32)
                dp = lax.dot_general(dob, v, nt_dims, preferred_element_type=F32)
                delta = jnp.sum(dob.astype(F32) * o_ref[rows, :], axis=1, keepdims=True)
                ds = p * (dp - delta)
                dcq_ref[0, rows, :] += jnp.sum(ds, axis=1, keepdims=True)
                dsb = ds.astype(BF16)
                dq_ref[rows, :] += jnp.dot(dsb, k, preferred_element_type=F32) * (1.0 / LOG2E)
                dk = dk + lax.dot_general(dsb, q, tn_dims, preferred_element_type=F32) * (1.0 / LOG2E)
                dc = dc - jnp.sum(ds, axis=0, keepdims=True)
            return dk, dv, dc

        init = (jnp.zeros((blk, DH), F32), jnp.zeros((blk, DH), F32), jnp.zeros((1, blk), F32))
        dk, dv, dc = lax.fori_loop(kj + 1, nb, step, step(kj, init, diagonal=True))
        dk_ref[...] = dk
        dv_ref[...] = dv.astype(dv_ref.dtype)
        dc_ref[0, 0] = dc

    full = lambda h, j: (0, h)
    kvb = lambda h, j: (j, h)
    return _pcall(
        body, name=name, grid=(HEADS, nb), sem=("parallel", "arbitrary"), comm=comm,
        args=(qn, kn, p_main, c4, o32, do, lse),
        in_specs=[pl.BlockSpec((t, DH), full), pl.BlockSpec((blk, DH), kvb),
                  pl.BlockSpec((blk, DH), lambda h, j: (j, vb + h)), pl.BlockSpec((1, 1, 1, blk), lambda h, j: (h, j, 0, 0)),
                  pl.BlockSpec((t, DH), full), pl.BlockSpec((t, DH), full),
                  pl.BlockSpec((1, t, 1), lambda h, j: (h, 0, 0))],
        out_specs=[pl.BlockSpec((t, DH), full), pl.BlockSpec((blk, DH), kvb), pl.BlockSpec((blk, DH), kvb),
                   pl.BlockSpec((1, 1, 1, blk), lambda h, j: (h, j, 0, 0)), pl.BlockSpec((1, t, 1), lambda h, j: (h, 0, 0))],
        out_shape=[jax.ShapeDtypeStruct((t, HW), F32), jax.ShapeDtypeStruct((t, HW), F32),
                   jax.ShapeDtypeStruct((t, HW), BF16), jax.ShapeDtypeStruct((HEADS, nb, 1, blk), F32),
                   jax.ShapeDtypeStruct((HEADS, t, 1), F32)])


ANY = pl.BlockSpec(memory_space=pl.ANY)


def _mesh_pos():
    return lax.axis_index("x"), lax.axis_index("y"), lax.axis_index("c")


def _all_gather(blocks, *, name):
    n = len(blocks)

    def body(*refs):
        ins, outs = refs[:n], refs[n:2 * n]
        send, recv, local = refs[2 * n:]
        x, y, c = _mesh_pos()
        me, sibling = (x, y, c), (x, y, 1 - c)
        chips = [(1 - x, y), (x, 1 - y), (1 - x, 1 - y)]

        def copy(t, k, block, to, src=None):
            dst = outs[t].at[4 * block[0] + 2 * block[1] + block[2]]
            return pltpu.make_async_remote_copy(
                src_ref=dst if src is None else src, dst_ref=dst, send_sem=send.at[7 * t + k],
                recv_sem=recv.at[7 * t + k], device_id=to, device_id_type=MESH)

        mine = [pltpu.make_async_copy(ins[t], outs[t].at[4 * x + 2 * y + c], local.at[t]) for t in range(n)]
        for cp in mine:
            cp.start()
        first = []
        for t in range(n):
            first.append(copy(t, 0, me, sibling, src=ins[t]))
            first += [copy(t, 1 + j, me, (*chip, c), src=ins[t]) for j, chip in enumerate(chips)]
        for cp in first:
            cp.start()
        passed = []
        for j, chip in enumerate(chips):
            for t in range(n):
                copy(t, 1 + j, (*chip, c), me).wait_recv()
                fwd = copy(t, 4 + j, (*chip, c), sibling)
                fwd.start()
                passed.append(fwd)
        for t in range(n):
            copy(t, 0, sibling, me).wait_recv()
            for j, chip in enumerate(chips):
                copy(t, 4 + j, (*chip, 1 - c), me).wait_recv()
        for cp in first + passed:
            cp.wait_send()
        for cp in mine:
            cp.wait()

    return pl.pallas_call(
        body, name=name, in_specs=[ANY] * n, out_specs=[ANY] * n,
        out_shape=[jax.ShapeDtypeStruct((N_DEV,) + b.shape, b.dtype) for b in blocks],
        scratch_shapes=[pltpu.SemaphoreType.DMA((7 * n,)), pltpu.SemaphoreType.DMA((7 * n,)),
                        pltpu.SemaphoreType.DMA((n,))],
    )(*blocks)


def _all_gather_relayed(block, *, name):
    r = block.shape[0]
    half = r // 2
    assert half * 2 == r and half % 16 == 0, block.shape

    def body(in_ref, out_ref, send, recv, local):
        x, y, c = _mesh_pos()
        me, sibling, xn, yn, dg = (x, y, c), (x, y, 1 - c), (1 - x, y, c), (x, 1 - y, c), (1 - x, 1 - y, c)
        slot = lambda p: 4 * p[0] + 2 * p[1] + p[2]
        rows = {"a": pl.ds(0, half), "b": pl.ds(half, half)}

        def copy(k, src, dst, to):
            return pltpu.make_async_remote_copy(src_ref=src, dst_ref=dst, send_sem=send.at[k], recv_sem=recv.at[k],
                                                device_id=to, device_id_type=MESH)

        def part(p, h=None):
            ref = out_ref.at[slot(p)]
            return ref if h is None else ref.at[rows[h]]

        def landed(k, p, h=None):
            copy(k, part(p, h), part(p, h), me).wait_recv()

        mine = pltpu.make_async_copy(in_ref, part(me), local)
        mine.start()
        first = [copy(0, in_ref, part(me), sibling),
                 copy(1, in_ref.at[rows["a"]], part(me, "a"), xn), copy(2, in_ref.at[rows["b"]], part(me, "b"), xn),
                 copy(3, in_ref.at[rows["a"]], part(me, "a"), yn), copy(4, in_ref.at[rows["b"]], part(me, "b"), yn)]
        for cp in first:
            cp.start()
        landed(1, xn, "a")
        relay_a = copy(5, part(xn, "a"), part(xn, "a"), yn)
        relay_a.start()
        landed(4, yn, "b")
        relay_b = copy(6, part(yn, "b"), part(yn, "b"), xn)
        relay_b.start()
        landed(2, xn, "b")
        pass_x = copy(7, part(xn), part(xn), sibling)
        pass_x.start()
        landed(3, yn, "a")
        pass_y = copy(8, part(yn), part(yn), sibling)
        pass_y.start()
        landed(5, dg, "a")
        landed(6, dg, "b")
        pass_d = copy(9, part(dg), part(dg), sibling)
        pass_d.start()
        landed(0, sibling)
        for k, p in ((7, (1 - x, y, 1 - c)), (8, (x, 1 - y, 1 - c)), (9, (1 - x, 1 - y, 1 - c))):
            landed(k, p)
        for cp in first + [relay_a, relay_b, pass_x, pass_y, pass_d]:
            cp.wait_send()
        mine.wait()

    return pl.pallas_call(
        body, name=name, in_specs=[ANY], out_specs=ANY,
        out_shape=jax.ShapeDtypeStruct((N_DEV,) + block.shape, block.dtype),
        scratch_shapes=[pltpu.SemaphoreType.DMA((10,)), pltpu.SemaphoreType.DMA((10,)), pltpu.SemaphoreType.DMA],
    )(block)


def _comm_call(comm, *, name):
    ci, co = len(comm.ins), len(comm.out_shapes)

    def body(*refs):
        comm.start(refs[:ci], refs[ci:ci + co], refs[ci + co:])
        comm.finish(refs[:ci], refs[ci:ci + co], refs[ci + co:])

    return pl.pallas_call(body, name=name, in_specs=[ANY] * ci, out_specs=[ANY] * co, out_shape=comm.out_shapes,
                          scratch_shapes=comm.sems, input_output_aliases=comm.aliases)(*comm.ins)


def _ag_first_comm(shards, rows=None, into=None):
    n = len(shards)
    rows = rows or [None] * n
    into = into or [None] * n
    carried = [t for t in range(n) if into[t] is not None]

    def copies(cin, cout, sems):
        send, recv, local = sems
        x, y, c = _mesh_pos()
        peers = [(x, y, 1 - c), (1 - x, y, c), (x, 1 - y, c), (1 - x, 1 - y, c)]
        slot = lambda p: 4 * p[0] + 2 * p[1] + p[2]
        mine, out, inc = [], [], []
        for t in range(n):
            part = (lambda ref: ref) if rows[t] is None else (lambda ref, r=rows[t]: ref.at[pl.ds(r[0], r[1])])
            own = part(cout[t].at[slot((x, y, c))])
            mine.append(pltpu.make_async_copy(part(cin[t]), own, local.at[t]))
            for k, peer in enumerate(peers):
                sems_k = dict(send_sem=send.at[4 * t + k], recv_sem=recv.at[4 * t + k], device_id=peer,
                              device_id_type=MESH)
                theirs = part(cout[t].at[slot(peer)])
                out.append(pltpu.make_async_remote_copy(src_ref=part(cin[t]), dst_ref=own, **sems_k))
                inc.append(pltpu.make_async_remote_copy(src_ref=theirs, dst_ref=theirs, **sems_k))
        return mine, out, inc

    def start(cin, cout, sems):
        mine, out, _ = copies(cin, cout, sems)
        for cp in mine + out:
            cp.start()

    def finish(cin, cout, sems):
        mine, out, inc = copies(cin, cout, sems)
        for cp in inc:
            cp.wait_recv()
        for cp in out:
            cp.wait_send()
        for cp in mine:
            cp.wait()

    return _Comm(list(shards) + [into[t] for t in carried],
                 [jax.ShapeDtypeStruct((N_DEV,) + s.shape, s.dtype) for s in shards],
                 [pltpu.SemaphoreType.DMA((4 * n,)), pltpu.SemaphoreType.DMA((4 * n,)), pltpu.SemaphoreType.DMA((n,))],
                 start, finish, aliases={n + i: t for i, t in enumerate(carried)})


def _ag_pass_comm(gathered):
    n = len(gathered)

    def copies(cout, sems):
        send, recv = sems
        x, y, c = _mesh_pos()
        fwd, inc = [], []
        for t in range(n):
            for j, (px, py) in enumerate([(1 - x, y), (x, 1 - y), (1 - x, 1 - y)]):
                sems_j = dict(send_sem=send.at[3 * t + j], recv_sem=recv.at[3 * t + j], device_id=(x, y, 1 - c),
                              device_id_type=MESH)
                mine, theirs = cout[t].at[4 * px + 2 * py + c], cout[t].at[4 * px + 2 * py + 1 - c]
                fwd.append(pltpu.make_async_remote_copy(src_ref=mine, dst_ref=mine, **sems_j))
                inc.append(pltpu.make_async_remote_copy(src_ref=theirs, dst_ref=theirs, **sems_j))
        return fwd, inc

    def start(cin, cout, sems):
        for cp in copies(cout, sems)[0]:
            cp.start()

    def finish(cin, cout, sems):
        fwd, inc = copies(cout, sems)
        for cp in inc:
            cp.wait_recv()
        for cp in fwd:
            cp.wait_send()

    return _Comm(gathered, [jax.ShapeDtypeStruct(g.shape, g.dtype) for g in gathered],
                 [pltpu.SemaphoreType.DMA((3 * n,)), pltpu.SemaphoreType.DMA((3 * n,))], start, finish,
                 aliases={t: t for t in range(n)})


def _rs_sibling_comm(grads):
    n = len(grads)

    def copies(cin, cout, sems):
        send, recv = sems
        x, y, c = _mesh_pos()
        return [pltpu.make_async_remote_copy(
            src_ref=cin[t].at[2 * q + (1 - c)], dst_ref=cout[t].at[q], send_sem=send.at[4 * t + q],
            recv_sem=recv.at[4 * t + q], device_id=(x, y, 1 - c), device_id_type=MESH)
            for t in range(n) for q in range(4)]

    def start(cin, cout, sems):
        for cp in copies(cin, cout, sems):
            cp.start()

    def finish(cin, cout, sems):
        cps = copies(cin, cout, sems)
        for cp in cps:
            cp.wait_recv()
        for cp in cps:
            cp.wait_send()

    return _Comm(grads, [jax.ShapeDtypeStruct((4,) + g.shape[1:], g.dtype) for g in grads],
                 [pltpu.SemaphoreType.DMA((4 * n,)), pltpu.SemaphoreType.DMA((4 * n,))], start, finish)


def _join_comms(comms):
    ins, outs, sems, aliases, spans = [], [], [], {}, []
    for cm in comms:
        spans.append((len(ins), len(cm.ins), len(outs), len(cm.out_shapes), len(sems), len(cm.sems)))
        aliases.update({len(ins) + i: len(outs) + o for i, o in cm.aliases.items()})
        ins, outs, sems = ins + cm.ins, outs + cm.out_shapes, sems + cm.sems

    def run(which):
        def fn(cin, cout, csem):
            for cm, (i0, ni, o0, no, s0, ns) in zip(comms, spans):
                getattr(cm, which)(cin[i0:i0 + ni], cout[o0:o0 + no], csem[s0:s0 + ns])
        return fn

    return _Comm(ins, outs, sems, run("start"), run("finish"), aliases)


def _rs_chips_comm(parts):
    n = len(parts)

    def copies(cin, cout, sems):
        send, recv, local = sems
        x, y, c = _mesh_pos()
        my_chip = 2 * x + y
        mine = [pltpu.make_async_copy(cin[t].at[my_chip], cout[t].at[my_chip], local.at[t]) for t in range(n)]
        sends, lands = [], []
        for t in range(n):
            for k, (px, py) in enumerate([(1 - x, y), (x, 1 - y), (1 - x, 1 - y)]):
                sems_k = dict(send_sem=send.at[3 * t + k], recv_sem=recv.at[3 * t + k], device_id=(px, py, c),
                              device_id_type=MESH)
                sends.append(pltpu.make_async_remote_copy(src_ref=cin[t].at[2 * px + py], dst_ref=cout[t].at[my_chip],
                                                          **sems_k))
                lands.append(pltpu.make_async_remote_copy(src_ref=cout[t].at[2 * px + py],
                                                          dst_ref=cout[t].at[2 * px + py], **sems_k))
        return mine, sends, lands

    def start(cin, cout, sems):
        mine, sends, _ = copies(cin, cout, sems)
        for cp in mine + sends:
            cp.start()

    def finish(cin, cout, sems):
        mine, sends, lands = copies(cin, cout, sems)
        for cp in lands:
            cp.wait_recv()
        for cp in sends:
            cp.wait_send()
        for cp in mine:
            cp.wait()

    return _Comm(parts, [jax.ShapeDtypeStruct(p.shape, p.dtype) for p in parts],
                 [pltpu.SemaphoreType.DMA((3 * n,)), pltpu.SemaphoreType.DMA((3 * n,)), pltpu.SemaphoreType.DMA((n,))],
                 start, finish)


def _row_tile(r, c, itemsize, budget=3 * 1024 * 1024):
    best = None
    for tr in range(16, r + 1, 16):
        if r % tr == 0 and tr * c * itemsize <= budget:
            best = tr
    return best or r


def _pair_sum(grad, land, *, name):
    _, r, c = grad.shape
    tr = _row_tile(r, c, 2)

    def body(g_ref, l_ref, o_ref):
        o_ref[...] = (g_ref[...].astype(F32) + l_ref[...].astype(F32)).astype(o_ref.dtype)

    return pl.pallas_call(
        body, name=name, grid=(4, r // tr),
        in_specs=[pl.BlockSpec((1, tr, c), lambda q, i: (2 * q + lax.axis_index("c"), i, 0)),
                  pl.BlockSpec((1, tr, c), lambda q, i: (q, i, 0))],
        out_specs=pl.BlockSpec((1, tr, c), lambda q, i: (q, i, 0)),
        out_shape=jax.ShapeDtypeStruct((4, r, c), grad.dtype),
        compiler_params=_cparams(("parallel", "parallel")),
    )(grad, land)


def _adamw_math(w, g, m, v):
    m = ADAM_B1 * m + (1.0 - ADAM_B1) * g
    v = ADAM_B2 * v + (1.0 - ADAM_B2) * jnp.square(g)
    m_hat = m / (1.0 - ADAM_B1 ** ADAM_STEP)
    v_hat = v / (1.0 - ADAM_B2 ** ADAM_STEP)
    delta = -ADAM_LR * (m_hat / (jnp.sqrt(v_hat) + ADAM_EPS) + ADAM_WD * w)
    return delta, m, v


def _adamw(parts, w, m, v, *, name):
    s, _, cp = parts.shape
    r, c = w.shape
    tr = _row_tile(r, cp, 4, budget=1024 * 1024)

    def body(p_ref, w_ref, m_ref, v_ref, g_ref, d_ref, nm_ref, nv_ref):
        g = p_ref[0].astype(F32)
        for i in range(1, s):
            g = g + p_ref[i].astype(F32)
        g = g[:, :c]
        delta, nm, nv = _adamw_math(w_ref[...], g, m_ref[...], v_ref[...])
        g_ref[...] = g
        d_ref[...] = delta
        nm_ref[...] = nm
        nv_ref[...] = nv

    blk = pl.BlockSpec((tr, c), lambda i: (i, 0))
    return pl.pallas_call(
        body, name=name, grid=(r // tr,),
        in_specs=[pl.BlockSpec((s, tr, cp), lambda i: (0, i, 0)), blk, blk, blk],
        out_specs=[blk] * 4, out_shape=[jax.ShapeDtypeStruct((r, c), F32)] * 4,
        compiler_params=_cparams(("parallel",)),
    )(parts, w, m, v)


def _w_in_pieces(d, nb, sources):
    segs = [(0, 4 * HW, False, 0), (4 * HW, 4 * HW + 2 * HEADS, True, 0),
            (4 * HW + 2 * HEADS, 7 * HW + 2 * HEADS, False, 4 * HW),
            (7 * HW + 2 * HEADS, 7 * HW + 3 * HEADS, True, 2 * HEADS),
            (7 * HW + 3 * HEADS, 7 * HW + 3 * HEADS + 2 * d, False, 7 * HW)]
    out = []
    for dev in range(N_DEV):
        lo, hi = dev * nb, (dev + 1) * nb
        for s0, s1, is_small, a0 in segs:
            p, q = max(lo, s0), min(hi, s1)
            if p >= q:
                continue
            a, b = a0 + p - s0, a0 + q - s0
            if is_small:
                out.append((dev, p - lo, q - lo, len(sources), a, b))
                continue
            for si, (start, width) in enumerate(sources):
                u, v = max(a, start), min(b, start + width)
                if u < v:
                    out.append((dev, p - lo + (u - a), p - lo + (v - a), si, u - start, v - start))
    return out


def _concat_cols(parts, *, name):
    t = parts[0].shape[0]
    n = len(parts)
    offs = [sum(p.shape[1] for p in parts[:i]) for i in range(n)]
    tm = min(128, t)

    def body(*refs):
        for i in range(n):
            refs[n][:, offs[i]:offs[i] + parts[i].shape[1]] = refs[i][...]

    return pl.pallas_call(
        body, name=name, grid=(t // tm,),
        in_specs=[pl.BlockSpec((tm, p.shape[1]), lambda i: (i, 0)) for p in parts],
        out_specs=pl.BlockSpec((tm, offs[-1] + parts[-1].shape[1]), lambda i: (i, 0)),
        out_shape=jax.ShapeDtypeStruct((t, offs[-1] + parts[-1].shape[1]), parts[0].dtype),
        compiler_params=_cparams(("parallel",)))(*parts)


def _w_in_to_aligned(g_in, *, name):
    _, d, nb = g_in.shape
    n_main = 7 * HW + 2 * d
    tr = min(128, d)
    pieces = _w_in_pieces(d, nb, [(0, n_main)])

    def body(g_ref, main_ref, small_ref):
        small_ref[...] = jnp.zeros_like(small_ref)
        for dev, s, e, src, a, b in pieces:
            dst = main_ref if src == 0 else small_ref
            dst[:, a:b] = g_ref[dev, :, s:e]

    return pl.pallas_call(
        body, name=name, grid=(d // tr,), in_specs=[pl.BlockSpec((N_DEV, tr, nb), lambda i: (0, i, 0))],
        out_specs=[pl.BlockSpec((tr, n_main), lambda i: (i, 0)), pl.BlockSpec((tr, LANES), lambda i: (i, 0))],
        out_shape=[jax.ShapeDtypeStruct((d, n_main), g_in.dtype), jax.ShapeDtypeStruct((d, LANES), g_in.dtype)],
        compiler_params=_cparams(("parallel",)),
    )(g_in)


def _w_in_grad_blocks(seg_grads, small_grad, sources, nb, *, name):
    d = small_grad.shape[0]
    tr = min(128, d)
    pieces = _w_in_pieces(d, nb, sources)
    ns = len(seg_grads)

    def body(*refs):
        o_ref = refs[ns + 1]
        for dev, s, e, src, a, b in pieces:
            o_ref[dev, :, s:e] = refs[src][:, a:b]

    return pl.pallas_call(
        body, name=name, grid=(d // tr,),
        in_specs=[pl.BlockSpec((tr, g.shape[1]), lambda i: (i, 0)) for g in seg_grads + [small_grad]],
        out_specs=pl.BlockSpec((N_DEV, tr, nb), lambda i: (0, i, 0)),
        out_shape=jax.ShapeDtypeStruct((N_DEV, d, nb), small_grad.dtype),
        compiler_params=_cparams(("parallel",)),
    )(*seg_grads, small_grad)


def _pad_cols(a, n):
    return a if a.shape[1] == n else jnp.concatenate([a, jnp.zeros((a.shape[0], n - a.shape[1]), a.dtype)], axis=1)


def _pad_rows(a, n):
    return a if a.shape[0] == n else jnp.concatenate([a, jnp.zeros((n - a.shape[0], a.shape[1]), a.dtype)], axis=0)


class _StaticPlan:
    def __init__(self, weights, cp):
        self.w, self.cp, self.grads = weights, cp, {}

    def comm_for(self, key):
        return None

    def done(self, key, res):
        pass

    def weight(self, name):
        return self.w[name]

    def grad(self, name, g):
        self.grads[name] = g

    def grad_w_in(self, g_main, g_small):
        self.grads["w_main"], self.grads["w_small"] = g_main, g_small


class _FsdpPlan:
    RIDES = {
        "in_proj": (("gather", (("conv", None), ("wa", None), ("wb", None), ("wout", None), ("wg", (0, 2)),
                                ("wd", (2, 4)))),),
        "gdn_fwd": (("gather", (("wg", (1, 2)),)), ("pass", ("wa", "wb", "wout"))),
        "fox_fwd": (("gather", (("wu", (0, 2)),)), ("pass", ("wg",))),
        "ffn_gate": (("gather", (("wu", (1, 2)), ("wd", (3, 4)))),),
        "ffn_up": (("gather", (("wd", (0, 2)),)),),
        "dw_ffn_gate": (("sibling", ("wd",)),),
        "d_hn_gate": (("chips", ("wd",)), ("sibling", ("wg",))),
        "d_hn_up": (("chips", ("wg",)),),
        "d_merged": (("sibling", ("wu",)),),
        "d_oa": (("sibling", ("wout",)),),
        "d_ob": (("sibling", ("wa",)),),
        "gdn_bwd": (("chips", ("wu",)), ("sibling", ("wb",))),
        "fox_bwd": (("chips", ("wout", "wa", "wb")),),
        "d_xn": (("chips", ("w_in", "conv")),),
    }
    PASS_GROUPS = (("conv",), ("wa", "wb", "wout"), ("wg",), ("wu",), ("wd",))

    def __init__(self, shards, d, cp, nb):
        self.shards, self.d, self.cp, self.nb = shards, d, cp, nb
        self.first, self.full = {}, {}
        self.blocks, self.queue, self.slots = {}, {}, {}
        self.flying = []

    def comm_for(self, key):
        comms, self.flying = [], []
        for kind, items in self.RIDES.get(key, ()):
            if kind == "gather":
                names = [n for n, _ in items]
                rows = [None if part is None else
                        (part[0] * (self.shards[n].shape[0] // part[1]), self.shards[n].shape[0] // part[1])
                        for n, part in items]
                comm = _ag_first_comm([self.shards[n] for n in names], rows, [self.first.get(n) for n in names])
            elif kind == "pass":
                names = list(items)
                comm = _ag_pass_comm([self.first[n] for n in names])
            elif kind == "sibling":
                names = [n for n in items if n in self.blocks]
                comm = _rs_sibling_comm([self.blocks[n] for n in names]) if names else None
            else:
                for n in items:
                    if n in self.blocks:
                        self.sibling_now(n)
                names = [n for n in items if n in self.queue]
                comm = _rs_chips_comm([self.queue.pop(n) for n in names]) if names else None
            if comm is not None:
                comms.append(comm)
                self.flying.append((kind, names, len(comm.out_shapes)))
        return _join_comms(comms) if comms else None

    def done(self, key, res):
        res = list(res)
        for kind, names, n_out in self.flying:
            outs, res = res[:n_out], res[n_out:]
            if kind == "gather":
                self.first.update(zip(names, outs))
            elif kind == "pass":
                self.full.update(zip(names, outs))
            elif kind == "sibling":
                for n, land in zip(names, outs):
                    self.queue[n] = _pair_sum(self.blocks.pop(n), land, name=f"pair_sum_{n}")
            else:
                self.slots.update(zip(names, outs))
        self.flying = []

    def weight(self, name):
        if name not in self.full:
            group = next(g for g in self.PASS_GROUPS if name in g)
            outs = _comm_call(_ag_pass_comm([self.first[n] for n in group]), name=f"all_gather_pass_{group[0]}")
            self.full.update(zip(group, outs))
        g = self.full[name]
        if name in ("wa", "wb", "conv"):
            return _cols_of_blocks(g)
        if name == "wout":
            return g.reshape(self.d, self.d)
        if name == "wd":
            return g.reshape(N_DEV * self.cp, self.d)
        return g

    def grad(self, name, g):
        if name == "wout":
            g = g.reshape(N_DEV, self.d // N_DEV, self.d)
        if name == "wd":
            g = g.reshape(N_DEV, self.cp, self.d)
        if name == "conv":
            g = _blocks_of_cols(g.astype(BF16))
        self.blocks[name] = g

    def grad_w_in(self, g_main, g_small):
        self.blocks["w_in"] = _w_in_grad_blocks([g_main], g_small, [(0, g_main.shape[1])], self.nb,
                                                name="w_in_grad_blocks")

    def sibling_now(self, name):
        blocks = self.blocks.pop(name)
        (land,) = _comm_call(_rs_sibling_comm([blocks]), name=f"grads_to_sibling_{name}")
        self.queue[name] = _pair_sum(blocks, land, name=f"pair_sum_{name}")

    def flush(self):
        for name in list(self.blocks):
            self.sibling_now(name)
        if self.queue:
            outs = _comm_call(_rs_chips_comm(list(self.queue.values())), name="grads_to_chips_tail")
            self.slots.update(zip(self.queue, outs))
            self.queue = {}


def _carried(plan, key, fn, *args, **kw):
    comm = plan.comm_for(key)
    if comm is None:
        return fn(*args, **kw)
    res, comm_res = fn(*args, comm=comm, **kw)
    plan.done(key, comm_res)
    return res


def _local_step(x, target, w_main, w_small, plan,
                norm_mix_w, norm_ffn_w, gdn_norm_w, fox_q_w, fox_k_w, a_row, b_row):
    t, d = x.shape
    cp = plan.cp
    fp = N_DEV * cp
    n_main = w_main.shape[1]
    off_gb = OFF_GA + d
    tm = 1024
    rt = 128

    (xn,) = _rowwise_fwd(_fn_norm, [(x, 0, d)], [norm_mix_w], [(d, BF16)], tm=rt, name="mix_norm")
    p_main = _carried(plan, "in_proj", _mm, xn, w_main, mode="nn", m=t, n=n_main, k=d, tm=tm, tn=512, tk=d,
                      out_dtype=BF16, name="in_proj")
    p_small = _mm(xn, w_small, mode="nn", m=t, n=LANES, k=d, tm=tm, tn=LANES, tk=d, out_dtype=F32, name="in_proj_small")
    (gates,) = _rowwise_fwd(_fn_gates, [(p_small, 0, LANES)], [a_row, b_row], [(LANES, F32)], tm=512, name="gates")
    conv_w = plan.weight("conv")
    qkv = _conv_fwd(p_main, conv_w, width=3 * HW, name="conv_fwd")
    o_gdn, s_all = _carried(plan, "gdn_fwd", _gdn_fwd, qkv, gates, name="gdn_fwd")
    gdn_rows = [(o_gdn, 0, HW), (p_main, OFF_ZA, HW)]
    (oa,) = _rowwise_fwd(_fn_gdn_out, gdn_rows, [gdn_norm_w], [(HW, BF16)], tm=512, inner=HEADS, name="gdn_out")
    wa = plan.weight("wa")
    ya = _mm(oa, wa, mode="nn", m=t, n=d, k=HW, tm=tm, tn=1024, tk=HW, out_dtype=BF16, name="branch_a")
    qk_rows = [(p_main, OFF_QB, HW), (p_main, OFF_KB, HW)]
    qn, kn = _rowwise_fwd(_fn_qknorm, qk_rows, [fox_q_w, fox_k_w], [(HW, BF16), (HW, BF16)], tm=512, inner=HEADS,
                          name="fox_qk_norm")
    blk = min(FOX_BLK, t)
    c4 = _fox_cumsum(gates, name="fox_cumsum").reshape(HEADS, t // blk, 1, blk)
    ob, ob32, lse = _carried(plan, "fox_fwd", _fox_fwd, qn, kn, p_main, c4, v_off=OFF_VB, name="fox_fwd")
    wb = plan.weight("wb")
    yb, merged = _mm(ob, wb, mode="nn", m=t, n=d, k=HW, tm=tm, tn=1024, tk=HW, out_dtype=BF16, name="branch_b",
                     tiles=[(p_main, OFF_GA), (p_main, off_gb), (ya, 0)], out_dtypes=[BF16, BF16],
                     epilogue=lambda p, ga, gb, a: (p, _fn_merge(ga, gb, a, _bf16_round(p))[0]))
    wout = plan.weight("wout")
    h = _mm(merged, wout, mode="nn", m=t, n=d, k=d, tm=tm, tn=512, tk=d, out_dtype=F32, add=x, name="out_proj")
    (hn,) = _rowwise_fwd(_fn_norm, [(h, 0, d)], [norm_ffn_w], [(d, BF16)], tm=rt, name="ffn_norm")
    wg = plan.weight("wg")
    gate = _carried(plan, "ffn_gate", _mm, hn, wg, mode="nn", m=t, n=fp, k=d, tm=512, tn=cp, tk=d, out_dtype=BF16,
                    b_blocked=True, name="ffn_gate")
    wu = plan.weight("wu")
    up, act = _carried(plan, "ffn_up", _mm, hn, wu, mode="nn", m=t, n=fp, k=d, tm=512, tn=cp, tk=d, out_dtype=BF16,
                       b_blocked=True, name="ffn_up", tiles=[(gate, 0)], out_dtypes=[BF16, BF16],
                       epilogue=lambda p, g: (p, _fn_swiglu(g, _bf16_round(p))[0]))
    wd = plan.weight("wd")
    y = _mm(act, wd, mode="nn", m=t, n=d, k=fp, tm=512, tn=256, tk=fp, out_dtype=F32, add=h, name="ffn_down")
    dy, dyb, loss_row = _loss_head(y, target, tm=rt, name="loss_head")

    dgate, dup = _mm(dyb, wd, mode="nt", m=t, n=fp, k=d, tm=tm, tn=512, tk=d, out_dtype=BF16, name="d_act",
                     tiles=[(gate, 0), (up, 0)], out_dtypes=[BF16, BF16],
                     epilogue=lambda p, g, u: jax.vjp(_fn_swiglu, g, u)[1]((_bf16_round(p),)))
    plan.grad("wd", _mm(act, dyb, mode="tn", m=fp, n=d, k=t, tm=512, tn=1024, tk=t, out_dtype=BF16, name="dw_ffn_down"))
    plan.grad("wg", _carried(plan, "dw_ffn_gate", _mm, hn, dgate, mode="tn", m=d, n=fp, k=t, tm=512, tn=cp, tk=t,
                             out_dtype=BF16, out_blocked=True, name="dw_ffn_gate"))
    dhn = _carried(plan, "d_hn_gate", _mm, dgate, wg, mode="nt", m=t, n=d, k=fp, tm=512, tn=256, tk=fp, out_dtype=F32,
                   b_blocked=True, name="d_hn_gate")
    dhn = _carried(plan, "d_hn_up", _mm, dup, wu, mode="nt", m=t, n=d, k=fp, tm=512, tn=256, tk=fp, out_dtype=F32,
                   add=dhn, b_blocked=True, name="d_hn_up")
    plan.grad("wu", _mm(hn, dup, mode="tn", m=d, n=fp, k=t, tm=512, tn=cp, tk=t, out_dtype=BF16, out_blocked=True,
                        name="dw_ffn_up"))
    dh, d_norm_ffn, dhb = _rowwise_bwd(_fn_norm, [(h, 0, d)], [norm_ffn_w], [dhn], [F32], tm=rt, name="d_ffn_norm",
                                       adds=[dy], bf16_copy_of=0)
    dga, dgb, dya, dyb2 = _carried(
        plan, "d_merged", _mm, dhb, wout, mode="nt", m=t, n=d, k=d, tm=512, tn=512, tk=d, out_dtype=BF16, name="d_merged",
        tiles=[(p_main, OFF_GA), (p_main, off_gb), (ya, 0), (yb, 0)], out_dtypes=[BF16] * 4,
        epilogue=lambda p, *gy: jax.vjp(_fn_merge, *gy)[1]((_bf16_round(p),)))
    plan.grad("wout", _mm(merged, dhb, mode="tn", m=d, n=d, k=t, tm=512, tn=512, tk=t, out_dtype=BF16, name="dw_out"))
    doa = _carried(plan, "d_oa", _mm, dya, wa, mode="nt", m=t, n=HW, k=d, tm=tm, tn=512, tk=d, out_dtype=BF16, name="d_oa")
    plan.grad("wa", _mm(oa, dya, mode="tn", m=HW, n=d, k=t, tm=1024, tn=d // N_DEV, tk=t, out_dtype=BF16,
                        out_blocked=True, name="dw_branch_a"))
    dob = _carried(plan, "d_ob", _mm, dyb2, wb, mode="nt", m=t, n=HW, k=d, tm=tm, tn=512, tk=d, out_dtype=BF16, name="d_ob")
    plan.grad("wb", _mm(ob, dyb2, mode="tn", m=HW, n=d, k=t, tm=1024, tn=d // N_DEV, tk=t, out_dtype=BF16,
                        out_blocked=True, name="dw_branch_b"))
    do_gdn, dza, d_gdn_norm = _rowwise_bwd(_fn_gdn_out, gdn_rows, [gdn_norm_w], [doa], [F32, BF16], tm=256,
                                           inner=HEADS, name="d_gdn_out")
    dqkv, dgates_gdn = _carried(plan, "gdn_bwd", _gdn_bwd, qkv, gates, s_all, do_gdn, name="gdn_bwd")
    dp_qkv, dconv = _conv_bwd(p_main, conv_w, dqkv, width=3 * HW, name="conv_bwd")
    plan.grad("conv", dconv)
    dqn, dkn, dvb, dc4, dcq = _carried(plan, "fox_bwd", _fox_bwd, qn, kn, p_main, c4, ob32, dob, lse, v_off=OFF_VB,
                                       name="fox_bwd")
    dqb, dkb, d_fox_q, d_fox_k = _rowwise_bwd(_fn_qknorm, qk_rows, [fox_q_w, fox_k_w], [dqn, dkn], [BF16, BF16],
                                              tm=256, inner=HEADS, name="d_fox_qk_norm")
    dgates = _fox_cumsum_bwd(dc4.reshape(HEADS, t) + dcq.reshape(HEADS, t), dgates_gdn, name="fox_cumsum_bwd")
    dsmall, d_a_row, d_b_row = _rowwise_bwd(_fn_gates, [(p_small, 0, LANES)], [a_row, b_row], [dgates], [F32],
                                            tm=512, name="d_gates")
    dp_main = _concat_cols([dp_qkv, dza, dqb, dkb, dvb, dga, dgb], name="d_p_main")
    plan.grad_w_in(_mm(xn, dp_main, mode="tn", m=d, n=n_main, k=t, tm=1024, tn=math.gcd(n_main, 1024), tk=t,
                       out_dtype=BF16, name="dw_in"),
                   _mm(xn, dsmall, mode="tn", m=d, n=LANES, k=t, tm=1024, tn=LANES, tk=t, out_dtype=BF16,
                       name="dw_in_small"))
    dxn = _mm(dsmall, w_small, mode="nt", m=t, n=d, k=LANES, tm=tm, tn=1024, tk=LANES, out_dtype=F32, name="d_xn_small")
    dxn = _carried(plan, "d_xn", _mm, dp_main, w_main, mode="nt", m=t, n=d, k=n_main, tm=tm, tn=1024,
                   tk=math.gcd(n_main, 2048),
                   out_dtype=F32, add=dxn, name="d_xn")
    grad_x, d_norm_mix = _rowwise_bwd(_fn_norm, [(x, 0, d)], [norm_mix_w], [dxn], [F32], tm=rt, name="d_mix_norm",
                                      adds=[dh])
    small = dict(norm_mix=d_norm_mix, norm_ffn=d_norm_ffn, gdn_norm=d_gdn_norm, fox_q=d_fox_q, fox_k=d_fox_k,
                 a_row=d_a_row, b_row=d_b_row)
    return loss_row[0, 0], grad_x, small


def _lane_row(pieces):
    row = jnp.zeros((1, LANES), F32)
    for off, p in pieces:
        row = lax.dynamic_update_slice(row, p.astype(F32), (0, off))
    return row


def _pack_small(norm_mix, norm_ffn, gdn_norm, fox_q, fox_k, a_log, dt_bias, b_f):
    rows = [norm_mix.reshape(-1, LANES), norm_ffn.reshape(-1, LANES), gdn_norm, fox_q, fox_k,
            _lane_row([(HEADS, a_log)]), _lane_row([(HEADS, dt_bias), (2 * HEADS, b_f)])]
    packed = jnp.concatenate(rows, axis=0)
    return _pad_rows(packed, -(-packed.shape[0] // 8) * 8)


def _unpack_small(p, d):
    nd = d // LANES
    r = 2 * nd
    return (p[0:nd].reshape(1, d), p[r + 3:r + 4, HEADS:2 * HEADS], p[r + 4:r + 5, HEADS:2 * HEADS], p[r:r + 1],
            p[r + 4:r + 5, 2 * HEADS:3 * HEADS], p[r + 1:r + 2], p[r + 2:r + 3], p[nd:r].reshape(1, d))


def _blocks_of_cols(a):
    r, c8 = a.shape
    return a.reshape(r, N_DEV, c8 // N_DEV).transpose(1, 0, 2)


def _cols_of_blocks(g):
    _, r, c = g.shape
    return g.transpose(1, 0, 2).reshape(r, N_DEV * c)


def kernel(x, norm_mix_w, w_in, conv_w, a_log, dt_bias, gdn_norm_w, fox_b_f, fox_q_norm_w, fox_k_norm_w, w_branch_a, w_branch_b, w_out, norm_ffn_w, w_ffn_gate, w_ffn_up, w_ffn_down, loss_target, m_norm_mix_w, m_w_in, m_conv_w, m_a_log, m_dt_bias, m_gdn_norm_w, m_fox_b_f, m_fox_q_norm_w, m_fox_k_norm_w, m_w_branch_a, m_w_branch_b, m_w_out, m_norm_ffn_w, m_w_ffn_gate, m_w_ffn_up, m_w_ffn_down, v_norm_mix_w, v_w_in, v_conv_w, v_a_log, v_dt_bias, v_gdn_norm_w, v_fox_b_f, v_fox_q_norm_w, v_fox_k_norm_w, v_w_branch_a, v_w_branch_b, v_w_out, v_norm_ffn_w, v_w_ffn_gate, v_w_ffn_up, v_w_ffn_down):
    d = x.shape[-1]
    cp = -(-w_ffn_down.shape[1] // LANES) * LANES
    nb = w_in.shape[2]

    g_in = _all_gather_relayed(w_in[0].astype(BF16), name="w_in_all_gather")
    w_main, w_small = _w_in_to_aligned(g_in, name="w_in_to_aligned")
    plan = _FsdpPlan(dict(conv=conv_w[0], wa=w_branch_a[0].astype(BF16), wb=w_branch_b[0].astype(BF16), wout=w_out[0].astype(BF16),
                          wg=_pad_cols(w_ffn_gate[0].astype(BF16), cp), wu=_pad_cols(w_ffn_up[0].astype(BF16), cp),
                          wd=_pad_rows(w_ffn_down[0].astype(BF16), cp)), d, cp, nb)
    a_row = _lane_row([(HEADS, a_log)])
    b_row = _lane_row([(HEADS, dt_bias), (2 * HEADS, fox_b_f)])

    loss_part, grad_x, gs = _local_step(
        x[0], loss_target[0], w_main, w_small, plan,
        norm_mix_w, norm_ffn_w, gdn_norm_w, fox_q_norm_w, fox_k_norm_w, a_row, b_row)
    loss = lax.psum(loss_part, ("x", "y", "c"))

    plan.flush()
    big = dict(w_in=("w_in", w_in, m_w_in, v_w_in), w_branch_a=("wa", w_branch_a, m_w_branch_a, v_w_branch_a),
               w_branch_b=("wb", w_branch_b, m_w_branch_b, v_w_branch_b), w_out=("wout", w_out, m_w_out, v_w_out),
               w_ffn_gate=("wg", w_ffn_gate, m_w_ffn_gate, v_w_ffn_gate), w_ffn_up=("wu", w_ffn_up, m_w_ffn_up, v_w_ffn_up),
               w_ffn_down=("wd", w_ffn_down, m_w_ffn_down, v_w_ffn_down), conv_w=("conv", conv_w, m_conv_w, v_conv_w))
    res = {}
    for nm, (key, w, m, v) in big.items():
        res[nm] = [o[None] for o in _adamw(plan.slots[key], w[0], m[0], v[0], name=f"adamw_{nm}")]

    g_small = _pack_small(gs["norm_mix"], gs["norm_ffn"], gs["gdn_norm"], gs["fox_q"], gs["fox_k"],
                          gs["a_row"][:, HEADS:2 * HEADS], gs["b_row"][:, HEADS:2 * HEADS],
                          gs["b_row"][:, 2 * HEADS:3 * HEADS])
    (g_small_all,) = _all_gather([g_small], name="small_grads_all_gather")
    w_small_p = _pack_small(norm_mix_w, norm_ffn_w, gdn_norm_w, fox_q_norm_w, fox_k_norm_w, a_log, dt_bias, fox_b_f)
    m_small_p = _pack_small(m_norm_mix_w, m_norm_ffn_w, m_gdn_norm_w, m_fox_q_norm_w, m_fox_k_norm_w, m_a_log,
                            m_dt_bias, m_fox_b_f)
    v_small_p = _pack_small(v_norm_mix_w, v_norm_ffn_w, v_gdn_norm_w, v_fox_q_norm_w, v_fox_k_norm_w, v_a_log,
                            v_dt_bias, v_fox_b_f)
    small_res = [_unpack_small(o, d) for o in _adamw(g_small_all, w_small_p, m_small_p, v_small_p, name="adamw_small")]

    def group(k):
        s = small_res[k]
        return [s[0], res["w_in"][k], res["conv_w"][k], s[1], s[2], s[3], s[4], s[5], s[6], res["w_branch_a"][k],
                res["w_branch_b"][k], res["w_out"][k], s[7], res["w_ffn_gate"][k], res["w_ffn_up"][k],
                res["w_ffn_down"][k]]

    return (loss, grad_x[None], *group(0), *group(1), *group(2), *group(3))
```
